```python
import jax, jax.numpy as jnp
from jax import lax
import numpy as np

D_MODEL = 2048
BATCH = 2
SEQ = 4096
DEPTH = 1
DEC_BATCH = 16
DEC_SEQ = 2048
PAST_LEN = 128

GRID_W = 64
HEAD_DIM = 128
N_Q_HEADS = 16
N_KV_HEADS = 4
Q_GROUP = N_Q_HEADS // N_KV_HEADS
ATTN_W = N_Q_HEADS * HEAD_DIM
KV_W = N_KV_HEADS * HEAD_DIM
ROPE_THETA = 10000.0
ROPE_HALF = HEAD_DIM // 4
Q_BLOCK = 128
HG_HEADS = 8
HG_DK = 128
HG_DV = 128
HG_K_W = HG_HEADS * HG_DK
HG_V_W = HG_HEADS * HG_DV
HG_CHUNK = 64
N_EXPERTS = 32
TOP_K = 4
D_FF = D_MODEL
SWIGLU_LIMIT = 7.0
SWIGLU_ALPHA = 1.702
MOE_BLOCK = 256
NORM_EPS = 1e-5
IN_SPLIT_SIZES = (ATTN_W, KV_W, KV_W, HG_K_W, HG_K_W, HG_K_W, HG_V_W, HG_V_W, D_MODEL, D_MODEL)
IN_COLS = ATTN_W + 2 * KV_W + 3 * HG_K_W + 2 * HG_V_W + 2 * D_MODEL

kernel_name = 'hybrid_gqa_hgrn2_moe_encoder'


def _split_points():
    pts, acc = [], 0
    for s in IN_SPLIT_SIZES[:-1]:
        acc += s
        pts.append(acc)
    return pts


def rmsnorm(x, g):
    xf = x.astype(jnp.float32)
    y = xf * lax.rsqrt(jnp.mean(xf * xf, axis=-1, keepdims=True) + NORM_EPS)
    return (y * g.astype(jnp.float32)).astype(x.dtype)


def axial_rope_tables(seq_len):
    rows = seq_len // GRID_W
    row = jnp.repeat(jnp.arange(rows, dtype=jnp.float32), GRID_W)
    col = jnp.tile(jnp.arange(GRID_W, dtype=jnp.float32), rows)
    freqs = ROPE_THETA ** (-jnp.arange(ROPE_HALF, dtype=jnp.float32) / ROPE_HALF)
    ang_r = row[:, None] * freqs[None, :]
    ang_c = col[:, None] * freqs[None, :]
    return jnp.cos(ang_r), jnp.sin(ang_r), jnp.cos(ang_c), jnp.sin(ang_c)


def _rotate_half(x, cos, sin):
    c = cos[None, :, None, :].astype(x.dtype)
    s = sin[None, :, None, :].astype(x.dtype)
    x1, x2 = x[..., :ROPE_HALF], x[..., ROPE_HALF:]
    return jnp.concatenate([x1 * c - x2 * s, x1 * s + x2 * c], axis=-1)


def apply_axial_rope(x, tabs):
    cr, sr, cc, sc = tabs
    half = HEAD_DIM // 2
    return jnp.concatenate([_rotate_half(x[..., :half], cr, sr),
                            _rotate_half(x[..., half:], cc, sc)], axis=-1)


def gqa_attention(q, k, v):
    B, L = q.shape[0], q.shape[1]
    nb = L // Q_BLOCK
    qb = q.reshape(B, nb, Q_BLOCK, N_KV_HEADS, Q_GROUP, HEAD_DIM).transpose(1, 0, 2, 3, 4, 5)
    scale = HEAD_DIM ** -0.5

    def one_block(qblk):
        s = jnp.einsum('bqngd,bknd->bngqk', qblk, k, preferred_element_type=jnp.float32) * scale
        p = jax.nn.softmax(s, axis=-1).astype(v.dtype)
        return jnp.einsum('bngqk,bknd->bqngd', p, v)

    o = lax.map(one_block, qb)
    return o.transpose(1, 0, 2, 3, 4, 5).reshape(B, L, ATTN_W)


def hgrn2_direction(q, k, v, log_f):
    B, L, H = q.shape[0], q.shape[1], q.shape[2]
    n = L // HG_CHUNK

    def to_chunks(a):
        return a.reshape(B, n, HG_CHUNK, H, a.shape[-1]).transpose(1, 0, 3, 2, 4)

    mask = jnp.tril(jnp.ones((HG_CHUNK, HG_CHUNK), dtype=bool))[None, None, :, :, None]

    def step(S, xs):
        qc, kc, vc, gc = xs
        b = jnp.cumsum(gc, axis=2)
        diff = b[:, :, :, None, :] - b[:, :, None, :, :]
        decay = jnp.exp(jnp.where(mask, diff, -jnp.inf))
        scores = jnp.einsum('bhtsk,bhsk->bhts', decay * qc[:, :, :, None, :], kc)
        o = jnp.einsum('bhts,bhsv->bhtv', scores, vc) + jnp.einsum('bhtk,bhkv->bhtv', qc * jnp.exp(b), S)
        b_end = b[:, :, -1, :]
        S = jnp.exp(b_end)[..., None] * S + jnp.einsum(
            'bhsk,bhsv->bhkv', kc * jnp.exp(b_end[:, :, None, :] - b), vc)
        return S, o

    S0 = jnp.zeros((B, H, HG_DK, HG_DV), jnp.float32)
    _, o = lax.scan(step, S0, (to_chunks(q), to_chunks(k), to_chunks(v), to_chunks(log_f)))
    return o.transpose(1, 0, 3, 2, 4).reshape(B, L, H, HG_DV)


def hgrn2_branch(q_pre, ff_pre, fb_pre, i_pre, og_pre, lb_fwd, lb_bwd, out_norm):
    B, L = q_pre.shape[0], q_pre.shape[1]

    def heads(a, d):
        return a.reshape(B, L, HG_HEADS, d).astype(jnp.float32)

    q = jax.nn.silu(heads(q_pre, HG_DK))
    v = heads(i_pre, HG_DV)

    def forget(f_pre, lb):
        lb = lb.reshape(HG_HEADS, HG_DK).astype(jnp.float32)
        f = lb + (1.0 - lb) * jax.nn.sigmoid(heads(f_pre, HG_DK))
        return jnp.log(f), 1.0 - f

    lf_f, k_f = forget(ff_pre, lb_fwd)
    lf_b, k_b = forget(fb_pre, lb_bwd)
    flip = lambda a: jnp.flip(a, axis=1)
    o = hgrn2_direction(q, k_f, v, lf_f) + flip(hgrn2_direction(flip(q), flip(k_b), flip(v), flip(lf_b)))
    o = o * lax.rsqrt(jnp.mean(o * o, axis=-1, keepdims=True) + NORM_EPS)
    o = o * out_norm.reshape(HG_HEADS, HG_DV).astype(jnp.float32)
    o = o * jax.nn.silu(heads(og_pre, HG_DV))
    return o.reshape(B, L, HG_V_W).astype(q_pre.dtype)


def moe_ffn(h, w_r, b_r, w_gu, b_gu, w_dn, b_dn):
    Bsz, L, D = h.shape
    N = Bsz * L
    hf = h.reshape(N, D)
    logits = jnp.matmul(hf, w_r, preferred_element_type=jnp.float32) + b_r.astype(jnp.float32)
    top_v, top_i = lax.top_k(logits, TOP_K)
    gates = jax.nn.softmax(top_v, axis=-1)
    M = N * TOP_K
    e_flat = top_i.reshape(M).astype(jnp.int32)
    tok_flat = jnp.repeat(jnp.arange(N, dtype=jnp.int32), TOP_K)
    w_flat = gates.reshape(M)
    order = jnp.argsort(e_flat)
    e_sorted = e_flat[order]
    counts = jnp.bincount(e_flat, length=N_EXPERTS).astype(jnp.int32)
    padded = (counts + MOE_BLOCK - 1) // MOE_BLOCK * MOE_BLOCK
    pad_end = jnp.cumsum(padded)
    pad_start = pad_end - padded
    start = jnp.cumsum(counts) - counts
    dest = pad_start[e_sorted] + jnp.arange(M, dtype=jnp.int32) - start[e_sorted]
    P = M + N_EXPERTS * MOE_BLOCK
    n_blocks = P // MOE_BLOCK
    row_tok = jnp.zeros((P,), jnp.int32).at[dest].set(tok_flat[order])
    row_w = jnp.zeros((P,), jnp.float32).at[dest].set(w_flat[order])
    blk_start = jnp.arange(n_blocks, dtype=jnp.int32) * MOE_BLOCK
    blk_e = jnp.minimum(jnp.searchsorted(pad_end, blk_start, side='right'), N_EXPERTS - 1)

    def run_block(args):
        tok_b, w_b, e = args
        xb = hf[tok_b]
        gu = xb @ w_gu[e] + b_gu[e]
        glu = jnp.minimum(gu[:, :D_FF], SWIGLU_LIMIT)
        lin = jnp.clip(gu[:, D_FF:], -SWIGLU_LIMIT, SWIGLU_LIMIT)
        act = glu * jax.nn.sigmoid(SWIGLU_ALPHA * glu) * (lin + 1.0)
        y = act @ w_dn[e] + b_dn[e]
        return y.astype(jnp.float32) * w_b[:, None]

    ys = lax.map(run_block, (row_tok.reshape(n_blocks, MOE_BLOCK),
                             row_w.reshape(n_blocks, MOE_BLOCK), blk_e))
    out = jax.ops.segment_sum(ys.reshape(P, D), row_tok, num_segments=N)
    return out.reshape(Bsz, L, D).astype(h.dtype)


def trunk_layer(x, lb_fwd, lb_bwd, mix_norm, w_in, q_norm, k_norm, hg_out_norm,
                w_up_attn, w_up_hgrn, w_out, ffn_norm, w_router, b_router,
                w_gate_up, b_gate_up, w_down, b_down):
    B, L, _ = x.shape
    h = rmsnorm(x, mix_norm)
    z = h @ w_in
    q_a, k_a, v_a, q_h, ff_h, fb_h, i_h, og_h, g_a, g_b = jnp.split(z, _split_points(), axis=-1)
    tabs = axial_rope_tables(L)
    q = apply_axial_rope(rmsnorm(q_a.reshape(B, L, N_Q_HEADS, HEAD_DIM), q_norm), tabs)
    k = apply_axial_rope(rmsnorm(k_a.reshape(B, L, N_KV_HEADS, HEAD_DIM), k_norm), tabs)
    v = v_a.reshape(B, L, N_KV_HEADS, HEAD_DIM)
    y_a = gqa_attention(q, k, v) @ w_up_attn
    y_b = hgrn2_branch(q_h, ff_h, fb_h, i_h, og_h, lb_fwd, lb_bwd, hg_out_norm) @ w_up_hgrn
    merged = jax.nn.sigmoid(g_a) * y_a + jax.nn.sigmoid(g_b) * y_b
    x = x + merged @ w_out
    x = x + moe_ffn(rmsnorm(x, ffn_norm), w_router, b_router, w_gate_up, b_gate_up, w_down, b_down)
    return x


def encoder_forward(x, lb_all, mix_norm, w_in, q_norm, k_norm, hg_out_norm, w_up_attn,
                    w_up_hgrn, w_out, ffn_norm, w_router, b_router, w_gate_up, b_gate_up,
                    w_down, b_down, final_norm):
    for l in range(DEPTH):
        x = trunk_layer(x, lb_all[0, l], lb_all[1, l], mix_norm[l], w_in[l], q_norm[l], k_norm[l],
                        hg_out_norm[l], w_up_attn[l], w_up_hgrn[l], w_out[l], ffn_norm[l],
                        w_router[l], b_router[l], w_gate_up[l], b_gate_up[l], w_down[l], b_down[l])
    return rmsnorm(x, final_norm)


def setup_inputs(seed: int = 0) -> dict:
    key = jax.random.key(seed)
    ks = jax.random.split(key, 20)
    f32 = jnp.float32

    def nrm(k, shape, scale):
        return jax.random.normal(k, shape, f32) * scale

    def gain(k, shape):
        return 1.0 + 0.02 * jax.random.normal(k, shape, f32)

    return {
        'x_prompt': nrm(ks[0], (BATCH, SEQ, D_MODEL), 1.0),
        'x_sample': nrm(ks[1], (DEC_BATCH, DEC_SEQ, D_MODEL), 1.0),
        'mix_norm': gain(ks[2], (DEPTH, D_MODEL)),
        'w_in': nrm(ks[3], (DEPTH, D_MODEL, IN_COLS), D_MODEL ** -0.5),
        'q_norm': gain(ks[4], (DEPTH, HEAD_DIM)),
        'k_norm': gain(ks[5], (DEPTH, HEAD_DIM)),
        'hg_lb_logits': nrm(ks[6], (2, DEPTH + 1, HG_K_W), 0.5),
        'hg_out_norm': gain(ks[7], (DEPTH, HG_V_W)),
        'w_up_attn': nrm(ks[8], (DEPTH, ATTN_W, D_MODEL), ATTN_W ** -0.5),
        'w_up_hgrn': nrm(ks[9], (DEPTH, HG_V_W, D_MODEL), HG_V_W ** -0.5),
        'w_out': nrm(ks[10], (DEPTH, D_MODEL, D_MODEL), D_MODEL ** -0.5),
        'ffn_norm': gain(ks[11], (DEPTH, D_MODEL)),
        'w_router': nrm(ks[12], (DEPTH, D_MODEL, N_EXPERTS), D_MODEL ** -0.5),
        'b_router': nrm(ks[13], (DEPTH, N_EXPERTS), 0.01),
        'w_gate_up': nrm(ks[14], (DEPTH, N_EXPERTS, D_MODEL, 2 * D_FF), D_MODEL ** -0.5),
        'b_gate_up': nrm(ks[15], (DEPTH, N_EXPERTS, 2 * D_FF), 0.01),
        'w_down': nrm(ks[16], (DEPTH, N_EXPERTS, D_FF, D_MODEL), D_FF ** -0.5),
        'b_down': nrm(ks[17], (DEPTH, N_EXPERTS, D_MODEL), 0.01),
        'final_norm': gain(ks[18], (D_MODEL,)),
    }


def reference(x_prompt, x_sample, mix_norm, w_in, q_norm, k_norm, hg_lb_logits, hg_out_norm,
              w_up_attn, w_up_hgrn, w_out, ffn_norm, w_router, b_router, w_gate_up, b_gate_up,
              w_down, b_down, final_norm):
    lb_all = jnp.cumsum(jax.nn.softmax(hg_lb_logits.astype(jnp.float32), axis=1), axis=1)
    y_prompt = encoder_forward(x_prompt, lb_all, mix_norm, w_in, q_norm, k_norm, hg_out_norm,
                               w_up_attn, w_up_hgrn, w_out, ffn_norm, w_router, b_router,
                               w_gate_up, b_gate_up, w_down, b_down, final_norm)
    y_sample = encoder_forward(x_sample, lb_all, mix_norm, w_in, q_norm, k_norm, hg_out_norm,
                               w_up_attn, w_up_hgrn, w_out, ffn_norm, w_router, b_router,
                               w_gate_up, b_gate_up, w_down, b_down, final_norm)
    return (y_prompt, y_sample)
```

```python
import functools
import math

import jax
import jax.numpy as jnp
from jax import lax
from jax.experimental import pallas as pl
from jax.experimental.pallas import tpu as pltpu

GRID_W = 64
HEAD_DIM = 128
N_Q_HEADS = 16
N_KV_HEADS = 4
ROPE_THETA = 10000.0
ROPE_HALF = HEAD_DIM // 4
HG_HEADS = 8
HG_D = 128
HG_CHUNK = 64
N_EXPERTS = 32
TOP_K = 4
SWIGLU_LIMIT = 7.0
SWIGLU_ALPHA = 1.702
NORM_EPS = 1e-5

HG_SAFE_LOG_DECAY = -60.0

V7X_VMEM_BYTES = 64 * 1024 * 1024
VMEM_LIMIT_BYTES = V7X_VMEM_BYTES - 8 * 1024 * 1024
LANES = 128

BF16 = jnp.bfloat16
F32 = jnp.float32


def _params(*sem):
    return pltpu.CompilerParams(dimension_semantics=sem, vmem_limit_bytes=VMEM_LIMIT_BYTES)


def _tile(n, pref):
    t = min(n, pref)
    while n % t:
        t //= 2
    return t


def _sigmoid(x):
    return 1.0 / (1.0 + jnp.exp(-x))


def _rmsnorm(x, g):
    return x * lax.rsqrt(jnp.mean(x * x, axis=-1, keepdims=True) + NORM_EPS) * g


def _dot(a, b):
    return jnp.dot(a, b, preferred_element_type=F32)


def _dot_nt(a, b):
    return lax.dot_general(a, b, (((1,), (1,)), ((), ())), preferred_element_type=F32)


def _split3(x):
    hi = x.astype(BF16)
    r = x - hi.astype(F32)
    mid = r.astype(BF16)
    lo = (r - mid.astype(F32)).astype(BF16)
    return hi, mid, lo


def _dot_exact_lhs(m_bf16, x):
    hi, mid, lo = _split3(x)
    return _dot(m_bf16, hi) + _dot(m_bf16, mid) + _dot(m_bf16, lo)


def _qk_epilogue(z, hn_ref, cos_ref, sin_ref, *, scale):
    tm, tn = z.shape
    lane = lax.broadcasted_iota(jnp.int32, (tm, HEAD_DIM), 1)
    first = (lane % (2 * ROPE_HALF)) < ROPE_HALF
    outs = []
    for h in range(tn // HEAD_DIM):
        zh = z[:, h * HEAD_DIM:(h + 1) * HEAD_DIM]
        y = _rmsnorm(zh, hn_ref[...])
        partner = jnp.where(first, pltpu.roll(y, HEAD_DIM - ROPE_HALF, 1), pltpu.roll(y, ROPE_HALF, 1))
        outs.append((y * cos_ref[...] + partner * sin_ref[...]) * scale)
    return jnp.concatenate(outs, axis=1)


def _proj_kernel(x_ref, g_ref, w_ref, *rest, rope_scale):
    *extra, o_ref, h_scr = rest

    @pl.when(pl.program_id(1) == 0)
    def _():
        h_scr[...] = _rmsnorm(x_ref[...], g_ref[...]).astype(BF16)

    z = _dot(h_scr[...], w_ref[...])
    if rope_scale is not None:
        z = _qk_epilogue(z, *extra, scale=rope_scale)
    o_ref[...] = z.astype(o_ref.dtype)


def _norm_proj(x, gain, w, col_off, n_cols, out_dtype, *, rope=None):
    n, d = x.shape
    tm = _tile(n, 1024) if rope is None else rope[5]
    tn = _tile(math.gcd(n_cols, col_off), 512)
    in_specs = [
        pl.BlockSpec((tm, d), lambda i, j: (i, 0)),
        pl.BlockSpec((1, d), lambda i, j: (0, 0)),
        pl.BlockSpec((d, tn), lambda i, j: (0, col_off // tn + j)),
    ]
    args = [x, gain, w]
    rope_scale = None
    if rope is not None:
        head_gain, cos, sin, pos_tile, rope_scale, _ = rope
        in_specs += [
            pl.BlockSpec((1, HEAD_DIM), lambda i, j: (0, 0)),
            pl.BlockSpec((tm, HEAD_DIM), lambda i, j: (pos_tile(i, tm), 0)),
            pl.BlockSpec((tm, HEAD_DIM), lambda i, j: (pos_tile(i, tm), 0)),
        ]
        args += [head_gain, cos, sin]
    return pl.pallas_call(
        functools.partial(_proj_kernel, rope_scale=rope_scale),
        grid=(n // tm, n_cols // tn),
        in_specs=in_specs,
        out_specs=pl.BlockSpec((tm, tn), lambda i, j: (i, j)),
        out_shape=jax.ShapeDtypeStruct((n, n_cols), out_dtype),
        scratch_shapes=[pltpu.VMEM((tm, d), BF16)],
        compiler_params=_params("parallel", "arbitrary"),
        name="norm_proj",
    )(*args)


def _attn_kernel(q_ref, k_ref, v_ref, o_ref, m_scr, l_scr, acc_scr, *, tk, group):
    tq = q_ref.shape[0]
    seq = k_ref.shape[0]
    q = jnp.concatenate([q_ref[:, g * HEAD_DIM:(g + 1) * HEAD_DIM] for g in range(group)], axis=0)
    m_scr[...] = jnp.full(m_scr.shape, -1e30, F32)
    l_scr[...] = jnp.zeros(l_scr.shape, F32)
    acc_scr[...] = jnp.zeros(acc_scr.shape, F32)

    def body(c, carry):
        rows = pl.ds(pl.multiple_of(c * tk, tk), tk)
        s = _dot_nt(q, k_ref[rows, :])
        m_prev = m_scr[...]
        m_new = jnp.maximum(m_prev, jnp.max(s, axis=1, keepdims=True))
        alpha = jnp.exp(m_prev - m_new)
        p = jnp.exp(s - jnp.tile(m_new, (1, tk // LANES)))
        l_scr[...] = alpha * l_scr[...] + jnp.sum(p, axis=1, keepdims=True)
        acc_scr[...] = alpha * acc_scr[...] + _dot(p.astype(BF16), v_ref[rows, :])
        m_scr[...] = m_new
        return carry

    lax.fori_loop(0, seq // tk, body, 0)
    o = acc_scr[...] / l_scr[...]
    for g in range(group):
        o_ref[:, g * HEAD_DIM:(g + 1) * HEAD_DIM] = o[g * tq:(g + 1) * tq].astype(o_ref.dtype)


def _attention(q, k, v, row_off, batch, seq):
    group = N_Q_HEADS // N_KV_HEADS
    gw = group * HEAD_DIM
    tq = _tile(seq, 256)
    tk = _tile(seq, 512)
    assert row_off % seq == 0
    nq = seq // tq
    qmap = lambda b, n, i: (row_off // tq + b * nq + i, n)
    kvmap = lambda b, n, i: (row_off // seq + b, n)
    return pl.pallas_call(
        functools.partial(_attn_kernel, tk=tk, group=group),
        grid=(batch, N_KV_HEADS, nq),
        in_specs=[
            pl.BlockSpec((tq, gw), qmap),
            pl.BlockSpec((seq, HEAD_DIM), kvmap),
            pl.BlockSpec((seq, HEAD_DIM), kvmap),
        ],
        out_specs=pl.BlockSpec((tq, gw), lambda b, n, i: (b * nq + i, n)),
        out_shape=jax.ShapeDtypeStruct((batch * seq, N_Q_HEADS * HEAD_DIM), BF16),
        scratch_shapes=[
            pltpu.VMEM((group * tq, LANES), F32),
            pltpu.VMEM((group * tq, LANES), F32),
            pltpu.VMEM((group * tq, HEAD_DIM), F32),
        ],
        compiler_params=_params("parallel", "parallel", "arbitrary"),
        name="gqa_attention",
    )(q, k, v)


def _hgrn_direction(i, zq_ref, zf_ref, zi_ref, lb_ref, o_scr, s_scr, tmp_scr, tri, *, span, reverse):
    C = HG_CHUNK
    nc = span // C
    r0 = pl.multiple_of(i * span, span)
    rows = pl.ds(r0, span)
    qh = zq_ref[rows, :]
    q = qh * _sigmoid(qh)
    lb = lb_ref[...]
    f = lb + (1.0 - lb) * _sigmoid(zf_ref[rows, :])
    g = jnp.log(f)
    k = 1.0 - f
    v = zi_ref[rows, :]
    vb = v.astype(BF16)

    b = _dot_exact_lhs(tri.astype(BF16), g)
    end_row = 0 if reverse else C - 1
    ends = [b[c * C + end_row:c * C + end_row + 1, :] for c in range(nc)]
    b_end = jnp.concatenate([jnp.broadcast_to(e, (C, HG_D)) for e in ends], axis=0)
    qt = q * jnp.exp(b)
    qtb = qt.astype(BF16)
    kp = (k * jnp.exp(b_end - b)).astype(BF16)

    min_end = ends[0]
    for e in ends[1:]:
        min_end = jnp.minimum(min_end, e)
    safe = jnp.min(min_end) > HG_SAFE_LOG_DECAY

    @pl.when(safe)
    def _():
        kt = (k * jnp.exp(-b)).astype(BF16)
        a = jnp.where(tri, _dot_nt(qtb, kt), 0.0)
        o_scr[rows, :] = _dot(a.astype(BF16), vb)

    @pl.when(jnp.logical_not(safe))
    def _():
        b_scr, k_scr, v_scr = tmp_scr
        b_scr[...] = b
        k_scr[...] = k
        v_scr[...] = v
        rid = lax.broadcasted_iota(jnp.int32, (span, HG_D), 0)

        def pair(s, acc):
            cs = (s // C) * C
            if reverse:
                m = (rid <= s) & (rid >= cs)
            else:
                m = (rid >= s) & (rid < cs + C)
            w = jnp.where(m, jnp.exp(jnp.minimum(b - b_scr[pl.ds(s, 1), :], 0.0)), 0.0)
            r = jnp.sum(q * w * k_scr[pl.ds(s, 1), :], axis=1, keepdims=True)
            return acc + r * v_scr[pl.ds(s, 1), :]

        o_scr[rows, :] = lax.fori_loop(0, span, pair, jnp.zeros((span, HG_D), F32))

    st = s_scr[...]
    for c in (reversed(range(nc)) if reverse else range(nc)):
        sl = slice(c * C, (c + 1) * C)
        crow = pl.ds(r0 + c * C, C)
        o_scr[crow, :] = o_scr[crow, :] + _dot_nt(qtb[sl], st.astype(BF16))
        upd = lax.dot_general(vb[sl], kp[sl], (((0,), (0,)), ((), ())), preferred_element_type=F32)
        st = st * jnp.exp(ends[c]) + upd
    s_scr[...] = st


def _hgrn_kernel(zq_ref, zf_ref, zb_ref, zi_ref, zo_ref, lbf_ref, lbb_ref, on_ref, out_ref,
                 of_scr, ob_scr, sf_scr, sb_scr, b_scr, k_scr, v_scr, *, span):
    seq = zq_ref.shape[0]
    n_span = seq // span
    sf_scr[...] = jnp.zeros(sf_scr.shape, F32)
    sb_scr[...] = jnp.zeros(sb_scr.shape, F32)
    r = lax.broadcasted_iota(jnp.int32, (span, span), 0)
    c = lax.broadcasted_iota(jnp.int32, (span, span), 1)
    same = (r // HG_CHUNK) == (c // HG_CHUNK)
    tri_f = same & (c <= r)
    tri_b = same & (c >= r)
    tmp = (b_scr, k_scr, v_scr)

    def body(i, carry):
        _hgrn_direction(i, zq_ref, zf_ref, zi_ref, lbf_ref, of_scr, sf_scr, tmp, tri_f,
                        span=span, reverse=False)
        _hgrn_direction(n_span - 1 - i, zq_ref, zb_ref, zi_ref, lbb_ref, ob_scr, sb_scr, tmp, tri_b,
                        span=span, reverse=True)
        return carry

    lax.fori_loop(0, n_span, body, 0)

    def finish(i, carry):
        rows = pl.ds(pl.multiple_of(i * span, span), span)
        o = _rmsnorm(of_scr[rows, :] + ob_scr[rows, :], on_ref[...])
        og = zo_ref[rows, :]
        out_ref[rows, :] = (o * (og * _sigmoid(og))).astype(out_ref.dtype)
        return carry

    lax.fori_loop(0, n_span, finish, 0)


def _hgrn(zh, lb_f, lb_b, out_norm, row_off, batch, seq):
    span = _tile(seq, 256)
    assert row_off % seq == 0 and span % HG_CHUNK == 0
    zspec = lambda grp: pl.BlockSpec((seq, HG_D), lambda b, h: (row_off // seq + b, grp * HG_HEADS + h))
    hspec = pl.BlockSpec((1, HG_D), lambda b, h: (0, h))
    return pl.pallas_call(
        functools.partial(_hgrn_kernel, span=span),
        grid=(batch, HG_HEADS),
        in_specs=[zspec(0), zspec(1), zspec(2), zspec(3), zspec(4), hspec, hspec, hspec],
        out_specs=pl.BlockSpec((seq, HG_D), lambda b, h: (b, h)),
        out_shape=jax.ShapeDtypeStruct((batch * seq, HG_HEADS * HG_D), BF16),
        scratch_shapes=[
            pltpu.VMEM((seq, HG_D), F32), pltpu.VMEM((seq, HG_D), F32),
            pltpu.VMEM((HG_D, HG_D), F32), pltpu.VMEM((HG_D, HG_D), F32),
            pltpu.VMEM((span, HG_D), F32), pltpu.VMEM((span, HG_D), F32), pltpu.VMEM((span, HG_D), F32),
        ],
        compiler_params=_params("parallel", "parallel"),
        name="hgrn2",
    )(zh, zh, zh, zh, zh, lb_f, lb_b, out_norm)


def _merge_kernel(a_ref, h_ref, ga_ref, gb_ref, wa_ref, wh_ref, o_ref):
    ya = _dot(a_ref[...], wa_ref[...])
    yb = _dot(h_ref[...], wh_ref[...])
    merged = _sigmoid(ga_ref[...].astype(F32)) * ya + _sigmoid(gb_ref[...].astype(F32)) * yb
    o_ref[...] = merged.astype(o_ref.dtype)


def _merge(attn, hg, gates, w_up_attn, w_up_hgrn):
    n, wa = attn.shape
    wh = hg.shape[1]
    d = w_up_attn.shape[1]
    tm = _tile(n, 1024)
    tn = _tile(d, 512)
    nj = d // tn
    return pl.pallas_call(
        _merge_kernel,
        grid=(n // tm, nj),
        in_specs=[
            pl.BlockSpec((tm, wa), lambda i, j: (i, 0)),
            pl.BlockSpec((tm, wh), lambda i, j: (i, 0)),
            pl.BlockSpec((tm, tn), lambda i, j: (i, j)),
            pl.BlockSpec((tm, tn), lambda i, j: (i, nj + j)),
            pl.BlockSpec((wa, tn), lambda i, j: (0, j)),
            pl.BlockSpec((wh, tn), lambda i, j: (0, j)),
        ],
        out_specs=pl.BlockSpec((tm, tn), lambda i, j: (i, j)),
        out_shape=jax.ShapeDtypeStruct((n, d), BF16),
        compiler_params=_params("parallel", "arbitrary"),
        name="gated_merge",
    )(attn, hg, gates, gates, w_up_attn, w_up_hgrn)


def _pack_bf16_pairs(h):
    half = h.shape[1] // 2
    lo = pltpu.bitcast(h[:, :half].astype(BF16).astype(F32), jnp.uint32)
    hi = pltpu.bitcast(h[:, half:].astype(BF16).astype(F32), jnp.uint32)
    return (hi & jnp.uint32(0xFFFF0000)) | (lo >> 16)


def _unpack_bf16_pairs(u):
    lo = pltpu.bitcast(u << 16, F32).astype(BF16)
    hi = pltpu.bitcast(u & jnp.uint32(0xFFFF0000), F32).astype(BF16)
    return lo, hi


def _outproj_router_kernel(x_ref, m_ref, w_ref, g_ref, wr_ref, br_ref,
                           x1_ref, hp_ref, idx_ref, gate_ref):
    x1 = x_ref[...] + _dot(m_ref[...], w_ref[...])
    x1_ref[...] = x1
    h = _rmsnorm(x1, g_ref[...])
    hp_ref[...] = _pack_bf16_pairs(h)
    h_hi, h_mid, h_lo = _split3(h)
    w_hi, w_mid, w_lo = _split3(wr_ref[...])
    lg = (_dot_nt(w_hi, h_hi) + _dot_nt(w_hi, h_mid) + _dot_nt(w_mid, h_hi)
          + _dot_nt(w_mid, h_mid) + _dot_nt(w_hi, h_lo) + _dot_nt(w_lo, h_hi)) + br_ref[...]
    n_exp, tm = lg.shape
    eid = lax.broadcasted_iota(jnp.int32, (n_exp, tm), 0)
    vals = []
    for kk in range(TOP_K):
        m = jnp.max(lg, axis=0, keepdims=True)
        sel = jnp.min(jnp.where(lg == m, eid, n_exp), axis=0, keepdims=True)
        idx_ref[kk:kk + 1, :] = sel
        vals.append(m)
        lg = jnp.where(eid == sel, -jnp.inf, lg)
    ex = [jnp.exp(vv - vals[0]) for vv in vals]
    den = ex[0]
    for e in ex[1:]:
        den = den + e
    for kk in range(TOP_K):
        gate_ref[kk:kk + 1, :] = ex[kk] / den


def _outproj_router(x, merged, w_out, ffn_gain, w_router_t, b_router):
    n, d = x.shape
    tm = _tile(n, 512)
    n_exp = w_router_t.shape[0]
    return pl.pallas_call(
        _outproj_router_kernel,
        grid=(n // tm,),
        in_specs=[
            pl.BlockSpec((tm, d), lambda i: (i, 0)),
            pl.BlockSpec((tm, d), lambda i: (i, 0)),
            pl.BlockSpec((d, d), lambda i: (0, 0)),
            pl.BlockSpec((1, d), lambda i: (0, 0)),
            pl.BlockSpec((n_exp, d), lambda i: (0, 0)),
            pl.BlockSpec((n_exp, 1), lambda i: (0, 0)),
        ],
        out_specs=[
            pl.BlockSpec((tm, d), lambda i: (i, 0)),
            pl.BlockSpec((tm, d // 2), lambda i: (i, 0)),
            pl.BlockSpec((TOP_K, tm), lambda i: (0, i)),
            pl.BlockSpec((TOP_K, tm), lambda i: (0, i)),
        ],
        out_shape=[
            jax.ShapeDtypeStruct((n, d), F32),
            jax.ShapeDtypeStruct((n, d // 2), jnp.uint32),
            jax.ShapeDtypeStruct((TOP_K, n), jnp.int32),
            jax.ShapeDtypeStruct((TOP_K, n), F32),
        ],
        compiler_params=_params("parallel"),
        name="outproj_router",
    )(x, merged, w_out, ffn_gain, w_router_t, b_router)


def _route_kernel(idx_ref, dest_ref, cnt_ref, cnt_scr, base_scr, *, row_block):
    phase = pl.program_id(0)
    i = pl.program_id(1)
    n_exp = cnt_scr.shape[0]
    tt = idx_ref.shape[1]
    eid = lax.broadcasted_iota(jnp.int32, (n_exp, tt), 0)
    onehot = [(eid == idx_ref[kk:kk + 1, :]) for kk in range(TOP_K)]

    @pl.when((phase == 0) & (i == 0))
    def _():
        cnt_scr[...] = jnp.zeros(cnt_scr.shape, F32)

    @pl.when(phase == 0)
    def _():
        tot = onehot[0].astype(F32)
        for oh in onehot[1:]:
            tot = tot + oh.astype(F32)
        cnt_scr[...] = cnt_scr[...] + jnp.sum(tot, axis=1, keepdims=True)
        cnt_ref[...] = cnt_scr[...]

    @pl.when((phase == 1) & (i == 0))
    def _():
        cnt = cnt_scr[...].astype(jnp.int32)
        padded = ((cnt + (row_block - 1)) // row_block * row_block).astype(F32)
        er = lax.broadcasted_iota(jnp.int32, (n_exp, n_exp), 0)
        ec = lax.broadcasted_iota(jnp.int32, (n_exp, n_exp), 1)
        base_scr[...] = _dot_exact_lhs((ec < er).astype(BF16), padded)

    @pl.when(phase == 1)
    def _():
        tr = lax.broadcasted_iota(jnp.int32, (tt, tt), 0)
        tc = lax.broadcasted_iota(jnp.int32, (tt, tt), 1)
        before = (tr < tc).astype(BF16)
        run = base_scr[...][:, :1]
        for kk in range(TOP_K):
            oh = onehot[kk].astype(F32)
            rank = _dot(oh.astype(BF16), before) + run
            dest_ref[kk:kk + 1, :] = jnp.sum(oh * rank, axis=0, keepdims=True).astype(jnp.int32)
            run = run + jnp.sum(oh, axis=1, keepdims=True)
        base_scr[...] = jnp.broadcast_to(run, base_scr.shape)


def _route(idx, row_block):
    n = idx.shape[1]
    tt = _tile(n, 512)
    return pl.pallas_call(
        functools.partial(_route_kernel, row_block=row_block),
        grid=(2, n // tt),
        in_specs=[pl.BlockSpec((TOP_K, tt), lambda p, i: (0, i))],
        out_specs=[
            pl.BlockSpec((TOP_K, tt), lambda p, i: (0, i * p)),
            pl.BlockSpec((N_EXPERTS, LANES), lambda p, i: (0, 0)),
        ],
        out_shape=[
            jax.ShapeDtypeStruct((TOP_K, n), jnp.int32),
            jax.ShapeDtypeStruct((N_EXPERTS, LANES), F32),
        ],
        scratch_shapes=[pltpu.VMEM((N_EXPERTS, LANES), F32), pltpu.VMEM((N_EXPERTS, LANES), F32)],
        compiler_params=_params("arbitrary", "arbitrary"),
        name="route_offsets",
    )(idx)


def _dispatch_kernel(dest_ref, h_ref, xs_in_ref, xs_ref, sem):
    del xs_in_ref
    tt = dest_ref.shape[1]
    base = pl.program_id(0) * tt

    def row_copy(t, kk):
        return pltpu.make_async_copy(h_ref.at[pl.ds(base + t, 1)], xs_ref.at[pl.ds(dest_ref[kk, t], 1)], sem)

    def start(t, carry):
        for kk in range(TOP_K):
            row_copy(t, kk).start()
        return carry

    def wait(t, carry):
        for kk in range(TOP_K):
            row_copy(t, kk).wait()
        return carry

    lax.fori_loop(0, tt, start, 0)
    lax.fori_loop(0, tt, wait, 0)


def _dispatch(dest, hp, n_rows):
    n, w = hp.shape
    tt = _tile(n, 512)
    xs0 = jnp.zeros((n_rows, w), hp.dtype)
    return pl.pallas_call(
        _dispatch_kernel,
        grid=(n // tt,),
        in_specs=[
            pl.BlockSpec((TOP_K, tt), lambda i: (0, i), memory_space=pltpu.SMEM),
            pl.BlockSpec(memory_space=pl.ANY),
            pl.BlockSpec(memory_space=pl.ANY),
        ],
        out_specs=pl.BlockSpec(memory_space=pl.ANY),
        out_shape=jax.ShapeDtypeStruct((n_rows, w), hp.dtype),
        scratch_shapes=[pltpu.SemaphoreType.DMA(())],
        input_output_aliases={2: 0},
        compiler_params=_params("arbitrary"),
        name="dispatch_rows",
    )(dest, hp, xs0)


def _expert_kernel(meta_ref, xs_ref, wg_ref, wl_ref, bg_ref, bl_ref, wd_ref, bd_ref, y_ref, xlo_scr, xhi_scr):
    i = pl.program_id(0)
    j = pl.program_id(1)

    @pl.when(i < meta_ref[0])
    def _():
        @pl.when(j == 0)
        def _():
            lo, hi = _unpack_bf16_pairs(xs_ref[...])
            xlo_scr[...] = lo
            xhi_scr[...] = hi
            y_ref[...] = jnp.broadcast_to(bd_ref[...], y_ref.shape)

        half = xlo_scr.shape[1]
        xlo = xlo_scr[...]
        xhi = xhi_scr[...]
        glu = _dot(xlo, wg_ref[:half, :]) + _dot(xhi, wg_ref[half:, :]) + bg_ref[...]
        lin = _dot(xlo, wl_ref[:half, :]) + _dot(xhi, wl_ref[half:, :]) + bl_ref[...]
        glu = jnp.minimum(glu, SWIGLU_LIMIT)
        lin = jnp.clip(lin, -SWIGLU_LIMIT, SWIGLU_LIMIT)
        act = glu * _sigmoid(SWIGLU_ALPHA * glu) * (lin + 1.0)
        y_ref[...] = y_ref[...] + _dot(act.astype(BF16), wd_ref[...])


def _experts(meta, xs, w_gu, b_gu, w_dn, b_dn, row_block):
    n_rows, half = xs.shape
    d = 2 * half
    d_ff = w_dn.shape[1]
    tf = _tile(d_ff, 512)
    nf = d_ff // tf
    n_blocks = n_rows // row_block

    def jj(i, j, m):
        return jnp.where(i < m[0], j, nf - 1)

    grid_spec = pltpu.PrefetchScalarGridSpec(
        num_scalar_prefetch=1,
        grid=(n_blocks, nf),
        in_specs=[
            pl.BlockSpec((row_block, half), lambda i, j, m: (i, 0)),
            pl.BlockSpec((None, d, tf), lambda i, j, m: (m[1 + i], 0, jj(i, j, m))),
            pl.BlockSpec((None, d, tf), lambda i, j, m: (m[1 + i], 0, nf + jj(i, j, m))),
            pl.BlockSpec((None, 1, tf), lambda i, j, m: (m[1 + i], 0, jj(i, j, m))),
            pl.BlockSpec((None, 1, tf), lambda i, j, m: (m[1 + i], 0, nf + jj(i, j, m))),
            pl.BlockSpec((None, tf, d), lambda i, j, m: (m[1 + i], jj(i, j, m), 0)),
            pl.BlockSpec((None, 1, d), lambda i, j, m: (m[1 + i], 0, 0)),
        ],
        out_specs=pl.BlockSpec((row_block, d), lambda i, j, m: (i, 0)),
        scratch_shapes=[pltpu.VMEM((row_block, half), BF16), pltpu.VMEM((row_block, half), BF16)],
    )
    return pl.pallas_call(
        _expert_kernel,
        grid_spec=grid_spec,
        out_shape=jax.ShapeDtypeStruct((n_rows, d), F32),
        compiler_params=_params("arbitrary", "arbitrary"),
        name="expert_swiglu",
    )(meta, xs, w_gu, w_gu, b_gu, b_gu, w_dn, b_dn)


def _combine_kernel(dest_ref, gate_ref, x1_ref, fn_ref, ys_ref, o_ref, buf, sem):
    tt = dest_ref.shape[1]

    def row_copy(t, kk):
        return pltpu.make_async_copy(ys_ref.at[pl.ds(dest_ref[kk, t], 1)], buf.at[kk, pl.ds(t, 1)], sem)

    def start(t, carry):
        for kk in range(TOP_K):
            row_copy(t, kk).start()
        return carry

    def wait(t, carry):
        for kk in range(TOP_K):
            row_copy(t, kk).wait()
        return carry

    lax.fori_loop(0, tt, start, 0)
    lax.fori_loop(0, tt, wait, 0)
    acc = x1_ref[...]
    gates = gate_ref[...]
    for kk in range(TOP_K):
        acc = acc + buf[kk] * gates[:, kk:kk + 1]
    o_ref[...] = _rmsnorm(acc, fn_ref[...])


def _combine(dest, gates_t, x1, final_gain, ys):
    n, d = x1.shape
    tt = _tile(n, 256)
    return pl.pallas_call(
        _combine_kernel,
        grid=(n // tt,),
        in_specs=[
            pl.BlockSpec((TOP_K, tt), lambda i: (0, i), memory_space=pltpu.SMEM),
            pl.BlockSpec((tt, TOP_K), lambda i: (i, 0)),
            pl.BlockSpec((tt, d), lambda i: (i, 0)),
            pl.BlockSpec((1, d), lambda i: (0, 0)),
            pl.BlockSpec(memory_space=pl.ANY),
        ],
        out_specs=pl.BlockSpec((tt, d), lambda i: (i, 0)),
        out_shape=jax.ShapeDtypeStruct((n, d), F32),
        scratch_shapes=[pltpu.VMEM((TOP_K, tt, d), F32), pltpu.SemaphoreType.DMA(())],
        compiler_params=_params("arbitrary"),
        name="combine_rows",
    )(dest, gates_t, x1, final_gain, ys)


def _rope_tables(seq_len):
    rows = seq_len // GRID_W
    row = jnp.repeat(jnp.arange(rows, dtype=F32), GRID_W)
    col = jnp.tile(jnp.arange(GRID_W, dtype=F32), rows)
    freqs = ROPE_THETA ** (-jnp.arange(ROPE_HALF, dtype=F32) / ROPE_HALF)
    ang_r = row[:, None] * freqs[None, :]
    ang_c = col[:, None] * freqs[None, :]
    cos = jnp.concatenate([jnp.cos(ang_r), jnp.cos(ang_r), jnp.cos(ang_c), jnp.cos(ang_c)], axis=1)
    sin = jnp.concatenate([-jnp.sin(ang_r), jnp.sin(ang_r), -jnp.sin(ang_c), jnp.sin(ang_c)], axis=1)
    return cos, sin


def kernel(x_prompt, x_sample, mix_norm, w_in, q_norm, k_norm, hg_lb_logits, hg_out_norm, w_up_attn,
           w_up_hgrn, w_out, ffn_norm, w_router, b_router, w_gate_up, b_gate_up, w_down, b_down, final_norm):
    assert mix_norm.shape[0] == 1, "single trunk layer"
    bp, lp, d = x_prompt.shape
    bs, ls, _ = x_sample.shape
    n_p, n_s = bp * lp, bs * ls
    n = n_p + n_s
    streams = ((0, bp, lp), (n_p, bs, ls))
    attn_w = N_Q_HEADS * HEAD_DIM
    kv_w = N_KV_HEADS * HEAD_DIM
    hg_w = HG_HEADS * HG_D
    n_exp = w_router.shape[-1]
    row_block = 512

    x = jnp.concatenate([x_prompt.reshape(n_p, d), x_sample.reshape(n_s, d)], axis=0)
    lb = jnp.cumsum(jax.nn.softmax(hg_lb_logits.astype(F32), axis=1), axis=1)[:, 0]
    cos, sin = _rope_tables(max(lp, ls))
    w_in_b = w_in[0].astype(BF16)
    mix_g = mix_norm[0].reshape(1, d)

    tm_pos = _tile(min(lp, ls), 1024)
    assert lp % tm_pos == 0 and ls % tm_pos == 0 and n_p % tm_pos == 0

    def pos_tile(i, tm):
        return jnp.where(i < n_p // tm, i % (lp // tm), i % (ls // tm))

    q = _norm_proj(x, mix_g, w_in_b, 0, attn_w, BF16,
                   rope=(q_norm[0].reshape(1, HEAD_DIM), cos, sin, pos_tile, HEAD_DIM ** -0.5, tm_pos))
    k = _norm_proj(x, mix_g, w_in_b, attn_w, kv_w, BF16,
                   rope=(k_norm[0].reshape(1, HEAD_DIM), cos, sin, pos_tile, 1.0, tm_pos))
    v = _norm_proj(x, mix_g, w_in_b, attn_w + kv_w, kv_w, BF16)
    zh = _norm_proj(x, mix_g, w_in_b, attn_w + 2 * kv_w, 5 * hg_w, F32)
    gates = _norm_proj(x, mix_g, w_in_b, attn_w + 2 * kv_w + 5 * hg_w, 2 * d, BF16)

    attn = jnp.concatenate([_attention(q, k, v, off, b, l) for off, b, l in streams], axis=0)
    hg = jnp.concatenate([_hgrn(zh, lb[0:1], lb[1:2], hg_out_norm[0].reshape(1, hg_w), off, b, l)
                          for off, b, l in streams], axis=0)

    merged = _merge(attn, hg, gates, w_up_attn[0].astype(BF16), w_up_hgrn[0].astype(BF16))
    x1, hp, idx, gate = _outproj_router(x, merged, w_out[0].astype(BF16), ffn_norm[0].reshape(1, d),
                                        w_router[0].T, b_router[0].reshape(n_exp, 1))

    dest, counts = _route(idx, row_block)
    cnt = counts[:, 0].astype(jnp.int32)
    pad_end = jnp.cumsum((cnt + row_block - 1) // row_block * row_block)
    n_rows = n * TOP_K + n_exp * row_block
    n_blocks = n_rows // row_block
    blk_start = jnp.arange(n_blocks, dtype=jnp.int32) * row_block
    blk_e = jnp.minimum(jnp.sum(pad_end[None, :] <= blk_start[:, None], axis=1), n_exp - 1).astype(jnp.int32)
    meta = jnp.concatenate([(pad_end[-1:] // row_block).astype(jnp.int32), blk_e])

    xs = _dispatch(dest, hp, n_rows)
    ys = _experts(meta, xs, w_gate_up[0].astype(BF16), b_gate_up[0].reshape(n_exp, 1, -1),
                  w_down[0].astype(BF16), b_down[0].reshape(n_exp, 1, d), row_block)
    out = _combine(dest, gate.T, x1, final_norm.reshape(1, d), ys)
    return out[:n_p].reshape(bp, lp, d), out[n_p:].reshape(bs, ls, d)
```

```python
import functools

import jax
import jax.numpy as jnp
from jax import lax
from jax.experimental import pallas as pl
from jax.experimental.pallas import tpu as pltpu

GRID_W = 64
HEAD_DIM = 128
N_Q_HEADS = 16
N_KV_HEADS = 4
ROPE_THETA = 10000.0
ROPE_HALF = HEAD_DIM // 4
HG_HEADS = 8
HG_D = 128
HG_CHUNK = 64
N_EXPERTS = 32
TOP_K = 4
SWIGLU_LIMIT = 7.0
SWIGLU_ALPHA = 1.702
NORM_EPS = 1e-5

HG_SAFE_LOG_DECAY = -60.0

V7X_VMEM_BYTES = 64 * 1024 * 1024
VMEM_LIMIT_BYTES = V7X_VMEM_BYTES - 8 * 1024 * 1024
LANES = 128
MOE_ROW_BLOCK = 512

BF16 = jnp.bfloat16
F32 = jnp.float32


def _params(*sem):
    return pltpu.CompilerParams(dimension_semantics=sem, vmem_limit_bytes=VMEM_LIMIT_BYTES)


def _tile(n, pref):
    t = min(n, pref)
    while n % t:
        t //= 2
    return t


def _sigmoid(x):
    return 1.0 / (1.0 + jnp.exp(-x))


def _rmsnorm(x, g):
    return x * lax.rsqrt(jnp.mean(x * x, axis=-1, keepdims=True) + NORM_EPS) * g


def _dot(a, b):
    return jnp.dot(a, b, preferred_element_type=F32)


def _dot_nt(a, b):
    return lax.dot_general(a, b, (((1,), (1,)), ((), ())), preferred_element_type=F32)


def _split3(x):
    hi = x.astype(BF16)
    r = x - hi.astype(F32)
    mid = r.astype(BF16)
    lo = (r - mid.astype(F32)).astype(BF16)
    return hi, mid, lo


def _dot_exact_lhs(m_bf16, x):
    hi, mid, lo = _split3(x)
    return _dot(m_bf16, hi) + _dot(m_bf16, mid) + _dot(m_bf16, lo)


def _rope_head(zh, gain, cos, sin, first, scale):
    y = _rmsnorm(zh, gain)
    partner = jnp.where(first, pltpu.roll(y, HEAD_DIM - ROPE_HALF, 1), pltpu.roll(y, ROPE_HALF, 1))
    return (y * cos + partner * sin) * scale


def _proj_kernel(x_ref, g_ref, w_ref, qn_ref, kn_ref, cos_ref, sin_ref,
                 qkv_ref, zh_ref, gate_ref, h_scr, *, tile_kinds):
    j = pl.program_id(1)

    @pl.when(j == 0)
    def _():
        h_scr[...] = _rmsnorm(x_ref[...], g_ref[...]).astype(BF16)

    z = _dot(h_scr[...], w_ref[...])
    tm, tn = z.shape
    lane = lax.broadcasted_iota(jnp.int32, (tm, HEAD_DIM), 1)
    first = (lane % (2 * ROPE_HALF)) < ROPE_HALF

    def qkv_tile(kinds):
        outs = []
        for h, kind in enumerate(kinds):
            zh = z[:, h * HEAD_DIM:(h + 1) * HEAD_DIM]
            if kind == "q":
                zh = _rope_head(zh, qn_ref[...], cos_ref[...], sin_ref[...], first, HEAD_DIM ** -0.5)
            elif kind == "k":
                zh = _rope_head(zh, kn_ref[...], cos_ref[...], sin_ref[...], first, 1.0)
            outs.append(zh)
        return jnp.concatenate(outs, axis=1)

    for lo, hi, kind in tile_kinds:
        @pl.when((j >= lo) & (j < hi))
        def _(kind=kind):
            if kind == "zh":
                zh_ref[...] = z
            elif kind == "gate":
                gate_ref[...] = z.astype(gate_ref.dtype)
            else:
                qkv_ref[...] = qkv_tile(kind).astype(qkv_ref.dtype)


def _norm_proj(x, gain, w, q_gain, k_gain, cos, sin, seq):
    n, d = x.shape
    attn_w = N_Q_HEADS * HEAD_DIM
    kv_w = N_KV_HEADS * HEAD_DIM
    hg5 = 5 * HG_HEADS * HG_D
    qkv_w = attn_w + 2 * kv_w
    assert w.shape[1] == qkv_w + hg5 + 2 * d
    tm = _tile(seq, 1024)
    tn = 1024
    while attn_w % tn or (2 * kv_w) % tn or hg5 % tn or (2 * d) % tn:
        tn //= 2
    assert tn % HEAD_DIM == 0
    heads = ["q"] * N_Q_HEADS + ["k"] * N_KV_HEADS + ["v"] * N_KV_HEADS
    hpt = tn // HEAD_DIM
    n_qkv, n_zh, n_gate = qkv_w // tn, hg5 // tn, 2 * d // tn
    tile_kinds = []
    for t in range(n_qkv):
        kind = tuple(heads[t * hpt:(t + 1) * hpt])
        if tile_kinds and tile_kinds[-1][2] == kind:
            tile_kinds[-1] = (tile_kinds[-1][0], t + 1, kind)
        else:
            tile_kinds.append((t, t + 1, kind))
    tile_kinds += [(n_qkv, n_qkv + n_zh, "zh"), (n_qkv + n_zh, n_qkv + n_zh + n_gate, "gate")]
    n_pos = seq // tm
    return pl.pallas_call(
        functools.partial(_proj_kernel, tile_kinds=tile_kinds),
        grid=(n // tm, n_qkv + n_zh + n_gate),
        in_specs=[
            pl.BlockSpec((tm, d), lambda i, j: (i, 0)),
            pl.BlockSpec((1, d), lambda i, j: (0, 0)),
            pl.BlockSpec((d, tn), lambda i, j: (0, j)),
            pl.BlockSpec((1, HEAD_DIM), lambda i, j: (0, 0)),
            pl.BlockSpec((1, HEAD_DIM), lambda i, j: (0, 0)),
            pl.BlockSpec((tm, HEAD_DIM), lambda i, j: (i % n_pos, 0)),
            pl.BlockSpec((tm, HEAD_DIM), lambda i, j: (i % n_pos, 0)),
        ],
        out_specs=[
            pl.BlockSpec((tm, tn), lambda i, j: (i, jnp.minimum(j, n_qkv - 1))),
            pl.BlockSpec((tm, tn), lambda i, j: (i, jnp.clip(j - n_qkv, 0, n_zh - 1))),
            pl.BlockSpec((tm, tn), lambda i, j: (i, jnp.clip(j - n_qkv - n_zh, 0, n_gate - 1))),
        ],
        out_shape=[
            jax.ShapeDtypeStruct((n, qkv_w), BF16),
            jax.ShapeDtypeStruct((n, hg5), F32),
            jax.ShapeDtypeStruct((n, 2 * d), BF16),
        ],
        scratch_shapes=[pltpu.VMEM((tm, d), BF16)],
        compiler_params=_params("parallel", "arbitrary"),
        name="norm_proj",
    )(x, gain, w, q_gain, k_gain, cos, sin)


def _attn_kernel(q_ref, k_ref, v_ref, o_ref, m_scr, l_scr, acc_scr, *, tk, group):
    tq = q_ref.shape[0]
    seq = k_ref.shape[0]
    q = jnp.concatenate([q_ref[:, g * HEAD_DIM:(g + 1) * HEAD_DIM] for g in range(group)], axis=0)
    m_scr[...] = jnp.full(m_scr.shape, -1e30, F32)
    l_scr[...] = jnp.zeros(l_scr.shape, F32)
    acc_scr[...] = jnp.zeros(acc_scr.shape, F32)

    def body(c, carry):
        rows = pl.ds(pl.multiple_of(c * tk, tk), tk)
        s = _dot_nt(q, k_ref[rows, :])
        m_prev = m_scr[...]
        m_new = jnp.maximum(m_prev, jnp.max(s, axis=1, keepdims=True))
        alpha = jnp.exp(m_prev - m_new)
        p = jnp.exp(s - jnp.tile(m_new, (1, tk // LANES)))
        l_scr[...] = alpha * l_scr[...] + jnp.sum(p, axis=1, keepdims=True)
        acc_scr[...] = alpha * acc_scr[...] + _dot(p.astype(BF16), v_ref[rows, :])
        m_scr[...] = m_new
        return carry

    lax.fori_loop(0, seq // tk, body, 0)
    o = acc_scr[...] / l_scr[...]
    for g in range(group):
        o_ref[:, g * HEAD_DIM:(g + 1) * HEAD_DIM] = o[g * tq:(g + 1) * tq].astype(o_ref.dtype)


def _attention(qkv, batch, seq):
    group = N_Q_HEADS // N_KV_HEADS
    gw = group * HEAD_DIM
    tq = _tile(seq, 256)
    tk = _tile(seq, 512)
    nq = seq // tq
    return pl.pallas_call(
        functools.partial(_attn_kernel, tk=tk, group=group),
        grid=(batch, N_KV_HEADS, nq),
        in_specs=[
            pl.BlockSpec((tq, gw), lambda b, n, i: (b * nq + i, n)),
            pl.BlockSpec((seq, HEAD_DIM), lambda b, n, i: (b, N_Q_HEADS + n)),
            pl.BlockSpec((seq, HEAD_DIM), lambda b, n, i: (b, N_Q_HEADS + N_KV_HEADS + n)),
        ],
        out_specs=pl.BlockSpec((tq, gw), lambda b, n, i: (b * nq + i, n)),
        out_shape=jax.ShapeDtypeStruct((batch * seq, N_Q_HEADS * HEAD_DIM), BF16),
        scratch_shapes=[
            pltpu.VMEM((group * tq, LANES), F32),
            pltpu.VMEM((group * tq, LANES), F32),
            pltpu.VMEM((group * tq, HEAD_DIM), F32),
        ],
        compiler_params=_params("parallel", "parallel", "arbitrary"),
        name="gqa_attention",
    )(qkv, qkv, qkv)


def _hgrn_direction(i, zq_ref, zf_ref, zi_ref, lb_ref, o_scr, s_scr, tmp_scr, tri, *, span, reverse):
    C = HG_CHUNK
    nc = span // C
    r0 = pl.multiple_of(i * span, span)
    rows = pl.ds(r0, span)
    qh = zq_ref[rows, :]
    q = qh * _sigmoid(qh)
    lb = lb_ref[...]
    f = lb + (1.0 - lb) * _sigmoid(zf_ref[rows, :])
    g = jnp.log(f)
    k = 1.0 - f
    v = zi_ref[rows, :]
    vb = v.astype(BF16)

    b = _dot_exact_lhs(tri.astype(BF16), g)
    end_row = 0 if reverse else C - 1
    ends = [b[c * C + end_row:c * C + end_row + 1, :] for c in range(nc)]
    b_end = jnp.concatenate([jnp.broadcast_to(e, (C, HG_D)) for e in ends], axis=0)
    qt = q * jnp.exp(b)
    qtb = qt.astype(BF16)
    kp = (k * jnp.exp(b_end - b)).astype(BF16)

    min_end = ends[0]
    for e in ends[1:]:
        min_end = jnp.minimum(min_end, e)
    safe = jnp.min(min_end) > HG_SAFE_LOG_DECAY

    @pl.when(safe)
    def _():
        kt = (k * jnp.exp(-b)).astype(BF16)
        a = jnp.where(tri, _dot_nt(qtb, kt), 0.0)
        o_scr[rows, :] = _dot(a.astype(BF16), vb)

    @pl.when(jnp.logical_not(safe))
    def _():
        b_scr, k_scr, v_scr = tmp_scr
        b_scr[...] = b
        k_scr[...] = k
        v_scr[...] = v
        rid = lax.broadcasted_iota(jnp.int32, (span, HG_D), 0)

        def pair(s, acc):
            cs = (s // C) * C
            if reverse:
                m = (rid <= s) & (rid >= cs)
            else:
                m = (rid >= s) & (rid < cs + C)
            w = jnp.where(m, jnp.exp(jnp.minimum(b - b_scr[pl.ds(s, 1), :], 0.0)), 0.0)
            r = jnp.sum(q * w * k_scr[pl.ds(s, 1), :], axis=1, keepdims=True)
            return acc + r * v_scr[pl.ds(s, 1), :]

        o_scr[rows, :] = lax.fori_loop(0, span, pair, jnp.zeros((span, HG_D), F32))

    st = s_scr[...]
    for c in (reversed(range(nc)) if reverse else range(nc)):
        sl = slice(c * C, (c + 1) * C)
        crow = pl.ds(r0 + c * C, C)
        o_scr[crow, :] = o_scr[crow, :] + _dot_nt(qtb[sl], st.astype(BF16))
        upd = lax.dot_general(vb[sl], kp[sl], (((0,), (0,)), ((), ())), preferred_element_type=F32)
        st = st * jnp.exp(ends[c]) + upd
    s_scr[...] = st


def _hgrn_kernel(zq_ref, zf_ref, zb_ref, zi_ref, zo_ref, lbf_ref, lbb_ref, on_ref, out_ref,
                 of_scr, ob_scr, sf_scr, sb_scr, b_scr, k_scr, v_scr, *, span):
    seq = zq_ref.shape[0]
    n_span = seq // span
    sf_scr[...] = jnp.zeros(sf_scr.shape, F32)
    sb_scr[...] = jnp.zeros(sb_scr.shape, F32)
    r = lax.broadcasted_iota(jnp.int32, (span, span), 0)
    c = lax.broadcasted_iota(jnp.int32, (span, span), 1)
    same = (r // HG_CHUNK) == (c // HG_CHUNK)
    tri_f = same & (c <= r)
    tri_b = same & (c >= r)
    tmp = (b_scr, k_scr, v_scr)

    def body(i, carry):
        _hgrn_direction(i, zq_ref, zf_ref, zi_ref, lbf_ref, of_scr, sf_scr, tmp, tri_f,
                        span=span, reverse=False)
        _hgrn_direction(n_span - 1 - i, zq_ref, zb_ref, zi_ref, lbb_ref, ob_scr, sb_scr, tmp, tri_b,
                        span=span, reverse=True)
        return carry

    lax.fori_loop(0, n_span, body, 0)

    def finish(i, carry):
        rows = pl.ds(pl.multiple_of(i * span, span), span)
        o = _rmsnorm(of_scr[rows, :] + ob_scr[rows, :], on_ref[...])
        og = zo_ref[rows, :]
        out_ref[rows, :] = (o * (og * _sigmoid(og))).astype(out_ref.dtype)
        return carry

    lax.fori_loop(0, n_span, finish, 0)


def _hgrn(zh, lb_f, lb_b, out_norm, batch, seq):
    span = _tile(seq, 256)
    assert span % HG_CHUNK == 0
    zspec = lambda grp: pl.BlockSpec((seq, HG_D), lambda b, h: (b, grp * HG_HEADS + h))
    hspec = pl.BlockSpec((1, HG_D), lambda b, h: (0, h))
    return pl.pallas_call(
        functools.partial(_hgrn_kernel, span=span),
        grid=(batch, HG_HEADS),
        in_specs=[zspec(0), zspec(1), zspec(2), zspec(3), zspec(4), hspec, hspec, hspec],
        out_specs=pl.BlockSpec((seq, HG_D), lambda b, h: (b, h)),
        out_shape=jax.ShapeDtypeStruct((batch * seq, HG_HEADS * HG_D), BF16),
        scratch_shapes=[
            pltpu.VMEM((seq, HG_D), F32), pltpu.VMEM((seq, HG_D), F32),
            pltpu.VMEM((HG_D, HG_D), F32), pltpu.VMEM((HG_D, HG_D), F32),
            pltpu.VMEM((span, HG_D), F32), pltpu.VMEM((span, HG_D), F32), pltpu.VMEM((span, HG_D), F32),
        ],
        compiler_params=_params("parallel", "parallel"),
        name="hgrn2",
    )(zh, zh, zh, zh, zh, lb_f, lb_b, out_norm)


def _merge_kernel(a_ref, h_ref, ga_ref, gb_ref, wa_ref, wh_ref, o_ref):
    ya = _dot(a_ref[...], wa_ref[...])
    yb = _dot(h_ref[...], wh_ref[...])
    merged = _sigmoid(ga_ref[...].astype(F32)) * ya + _sigmoid(gb_ref[...].astype(F32)) * yb
    o_ref[...] = merged.astype(o_ref.dtype)


def _merge(attn, hg, gates, w_up_attn, w_up_hgrn):
    n, wa = attn.shape
    wh = hg.shape[1]
    d = w_up_attn.shape[1]
    tm = _tile(n, 1024)
    tn = _tile(d, 512)
    nj = d // tn
    return pl.pallas_call(
        _merge_kernel,
        grid=(n // tm, nj),
        in_specs=[
            pl.BlockSpec((tm, wa), lambda i, j: (i, 0)),
            pl.BlockSpec((tm, wh), lambda i, j: (i, 0)),
            pl.BlockSpec((tm, tn), lambda i, j: (i, j)),
            pl.BlockSpec((tm, tn), lambda i, j: (i, nj + j)),
            pl.BlockSpec((wa, tn), lambda i, j: (0, j)),
            pl.BlockSpec((wh, tn), lambda i, j: (0, j)),
        ],
        out_specs=pl.BlockSpec((tm, tn), lambda i, j: (i, j)),
        out_shape=jax.ShapeDtypeStruct((n, d), BF16),
        compiler_params=_params("parallel", "arbitrary"),
        name="gated_merge",
    )(attn, hg, gates, gates, w_up_attn, w_up_hgrn)


def _pack_bf16_pairs(h):
    half = h.shape[1] // 2
    lo = pltpu.bitcast(h[:, :half].astype(BF16).astype(F32), jnp.uint32)
    hi = pltpu.bitcast(h[:, half:].astype(BF16).astype(F32), jnp.uint32)
    return (hi & jnp.uint32(0xFFFF0000)) | (lo >> 16)


def _unpack_bf16_pairs(u):
    lo = pltpu.bitcast(u << 16, F32).astype(BF16)
    hi = pltpu.bitcast(u & jnp.uint32(0xFFFF0000), F32).astype(BF16)
    return lo, hi


def _outproj_router_kernel(x_ref, m_ref, w_ref, g_ref, wr_ref, br_ref, hp_in_ref,
                           x1_ref, hp_ref, idx_ref, gate_ref):
    del hp_in_ref
    x1 = x_ref[...] + _dot(m_ref[...], w_ref[...])
    x1_ref[...] = x1
    h = _rmsnorm(x1, g_ref[...])
    hp_ref[...] = _pack_bf16_pairs(h)
    h_hi, h_mid, h_lo = _split3(h)
    w_hi, w_mid, w_lo = _split3(wr_ref[...])
    lg = (_dot_nt(w_hi, h_hi) + _dot_nt(w_hi, h_mid) + _dot_nt(w_mid, h_hi)
          + _dot_nt(w_mid, h_mid) + _dot_nt(w_hi, h_lo) + _dot_nt(w_lo, h_hi)) + br_ref[...]
    n_exp, tm = lg.shape
    eid = lax.broadcasted_iota(jnp.int32, (n_exp, tm), 0)
    vals = []
    for kk in range(TOP_K):
        m = jnp.max(lg, axis=0, keepdims=True)
        sel = jnp.min(jnp.where(lg == m, eid, n_exp), axis=0, keepdims=True)
        idx_ref[kk:kk + 1, :] = sel
        vals.append(m)
        lg = jnp.where(eid == sel, -jnp.inf, lg)
    ex = [jnp.exp(vv - vals[0]) for vv in vals]
    den = ex[0]
    for e in ex[1:]:
        den = den + e
    for kk in range(TOP_K):
        gate_ref[kk:kk + 1, :] = ex[kk] / den


def _outproj_router(x, merged, w_out, ffn_gain, w_router_t, b_router, hp_prev, row_off, n_total):
    n, d = x.shape
    tm = _tile(n, 512)
    assert row_off % tm == 0
    n_exp = w_router_t.shape[0]
    if hp_prev is None:
        hp_prev = jnp.zeros((8, LANES), jnp.uint32)
        aliases = {}
    else:
        aliases = {6: 1}
    return pl.pallas_call(
        _outproj_router_kernel,
        grid=(n // tm,),
        in_specs=[
            pl.BlockSpec((tm, d), lambda i: (i, 0)),
            pl.BlockSpec((tm, d), lambda i: (i, 0)),
            pl.BlockSpec((d, d), lambda i: (0, 0)),
            pl.BlockSpec((1, d), lambda i: (0, 0)),
            pl.BlockSpec((n_exp, d), lambda i: (0, 0)),
            pl.BlockSpec((n_exp, 1), lambda i: (0, 0)),
            pl.BlockSpec(memory_space=pl.ANY),
        ],
        out_specs=[
            pl.BlockSpec((tm, d), lambda i: (i, 0)),
            pl.BlockSpec((tm, d // 2), lambda i: (row_off // tm + i, 0)),
            pl.BlockSpec((TOP_K, tm), lambda i: (0, i)),
            pl.BlockSpec((TOP_K, tm), lambda i: (0, i)),
        ],
        out_shape=[
            jax.ShapeDtypeStruct((n, d), F32),
            jax.ShapeDtypeStruct((n_total, d // 2), jnp.uint32),
            jax.ShapeDtypeStruct((TOP_K, n), jnp.int32),
            jax.ShapeDtypeStruct((TOP_K, n), F32),
        ],
        input_output_aliases=aliases,
        compiler_params=_params("parallel"),
        name="outproj_router",
    )(x, merged, w_out, ffn_gain, w_router_t, b_router, hp_prev)


def _route_kernel(idx_ref, dest_ref, cnt_ref, cnt_scr, base_scr, *, row_block):
    phase = pl.program_id(0)
    i = pl.program_id(1)
    n_exp = cnt_scr.shape[0]
    tt = idx_ref.shape[1]
    eid = lax.broadcasted_iota(jnp.int32, (n_exp, tt), 0)
    onehot = [(eid == idx_ref[kk:kk + 1, :]) for kk in range(TOP_K)]

    @pl.when((phase == 0) & (i == 0))
    def _():
        cnt_scr[...] = jnp.zeros(cnt_scr.shape, F32)

    @pl.when(phase == 0)
    def _():
        tot = onehot[0].astype(F32)
        for oh in onehot[1:]:
            tot = tot + oh.astype(F32)
        cnt_scr[...] = cnt_scr[...] + jnp.sum(tot, axis=1, keepdims=True)
        cnt_ref[...] = cnt_scr[...]

    @pl.when((phase == 1) & (i == 0))
    def _():
        cnt = cnt_scr[...].astype(jnp.int32)
        padded = ((cnt + (row_block - 1)) // row_block * row_block).astype(F32)
        er = lax.broadcasted_iota(jnp.int32, (n_exp, n_exp), 0)
        ec = lax.broadcasted_iota(jnp.int32, (n_exp, n_exp), 1)
        base_scr[...] = _dot_exact_lhs((ec < er).astype(BF16), padded)

    @pl.when(phase == 1)
    def _():
        tr = lax.broadcasted_iota(jnp.int32, (tt, tt), 0)
        tc = lax.broadcasted_iota(jnp.int32, (tt, tt), 1)
        before = (tr < tc).astype(BF16)
        run = base_scr[...][:, :1]
        for kk in range(TOP_K):
            oh = onehot[kk].astype(F32)
            rank = _dot(oh.astype(BF16), before) + run
            dest_ref[kk:kk + 1, :] = jnp.sum(oh * rank, axis=0, keepdims=True).astype(jnp.int32)
            run = run + jnp.sum(oh, axis=1, keepdims=True)
        base_scr[...] = jnp.broadcast_to(run, base_scr.shape)


def _route(idx, row_block):
    n = idx.shape[1]
    tt = _tile(n, 512)
    return pl.pallas_call(
        functools.partial(_route_kernel, row_block=row_block),
        grid=(2, n // tt),
        in_specs=[pl.BlockSpec((TOP_K, tt), lambda p, i: (0, i))],
        out_specs=[
            pl.BlockSpec((TOP_K, tt), lambda p, i: (0, i * p)),
            pl.BlockSpec((N_EXPERTS, LANES), lambda p, i: (0, 0)),
        ],
        out_shape=[
            jax.ShapeDtypeStruct((TOP_K, n), jnp.int32),
            jax.ShapeDtypeStruct((N_EXPERTS, LANES), F32),
        ],
        scratch_shapes=[pltpu.VMEM((N_EXPERTS, LANES), F32), pltpu.VMEM((N_EXPERTS, LANES), F32)],
        compiler_params=_params("arbitrary", "arbitrary"),
        name="route_offsets",
    )(idx)


def _dispatch_kernel(seg_ref, dest_ref, h_ref, xs_ref, zero_scr, sem, zsem, *, row_block):
    tt = dest_ref.shape[1]
    n_exp = seg_ref.shape[1]

    @pl.when(pl.program_id(0) == 0)
    def _():
        zero_scr[...] = jnp.zeros(zero_scr.shape, zero_scr.dtype)

        def zero_copy(e):
            start = pl.multiple_of(seg_ref[1, e] - row_block, row_block)
            return pltpu.make_async_copy(zero_scr, xs_ref.at[pl.ds(start, row_block)], zsem)

        for e in range(n_exp):
            @pl.when(seg_ref[1, e] > seg_ref[0, e])
            def _(e=e):
                zero_copy(e).start()
        for e in range(n_exp):
            @pl.when(seg_ref[1, e] > seg_ref[0, e])
            def _(e=e):
                zero_copy(e).wait()

    def row_copy(t, kk):
        return pltpu.make_async_copy(h_ref.at[pl.ds(t, 1)], xs_ref.at[pl.ds(dest_ref[kk, t], 1)], sem)

    def start(t, carry):
        for kk in range(TOP_K):
            row_copy(t, kk).start()
        return carry

    def wait(t, carry):
        for kk in range(TOP_K):
            row_copy(t, kk).wait()
        return carry

    lax.fori_loop(0, tt, start, 0, unroll=4)
    lax.fori_loop(0, tt, wait, 0, unroll=4)


def _dispatch(seg, dest, hp, n_rows, row_block):
    n, w = hp.shape
    tt = _tile(n, 512)
    grid_spec = pltpu.PrefetchScalarGridSpec(
        num_scalar_prefetch=1,
        grid=(n // tt,),
        in_specs=[
            pl.BlockSpec((TOP_K, tt), lambda i, s: (0, i), memory_space=pltpu.SMEM),
            pl.BlockSpec((tt, w), lambda i, s: (i, 0)),
        ],
        out_specs=pl.BlockSpec(memory_space=pl.ANY),
        scratch_shapes=[pltpu.VMEM((row_block, w), hp.dtype), pltpu.SemaphoreType.DMA(()),
                        pltpu.SemaphoreType.DMA(())],
    )
    return pl.pallas_call(
        functools.partial(_dispatch_kernel, row_block=row_block),
        grid_spec=grid_spec,
        out_shape=jax.ShapeDtypeStruct((n_rows, w), hp.dtype),
        compiler_params=_params("arbitrary"),
        name="dispatch_rows",
    )(seg, dest, hp)


def _expert_kernel(meta_ref, xs_ref, wg_ref, wl_ref, bg_ref, bl_ref, wd_ref, bd_ref, y_ref, xlo_scr, xhi_scr):
    i = pl.program_id(0)
    j = pl.program_id(1)

    @pl.when(i < meta_ref[0])
    def _():
        @pl.when(j == 0)
        def _():
            lo, hi = _unpack_bf16_pairs(xs_ref[...])
            xlo_scr[...] = lo
            xhi_scr[...] = hi
            y_ref[...] = jnp.broadcast_to(bd_ref[...], y_ref.shape)

        half = xlo_scr.shape[1]
        xlo = xlo_scr[...]
        xhi = xhi_scr[...]
        glu = _dot(xlo, wg_ref[:half, :]) + _dot(xhi, wg_ref[half:, :]) + bg_ref[...]
        lin = _dot(xlo, wl_ref[:half, :]) + _dot(xhi, wl_ref[half:, :]) + bl_ref[...]
        glu = jnp.minimum(glu, SWIGLU_LIMIT)
        lin = jnp.clip(lin, -SWIGLU_LIMIT, SWIGLU_LIMIT)
        act = glu * _sigmoid(SWIGLU_ALPHA * glu) * (lin + 1.0)
        y_ref[...] = y_ref[...] + _dot(act.astype(BF16), wd_ref[...])


def _experts(meta, xs, w_gu, b_gu, w_dn, b_dn, row_block):
    n_rows, half = xs.shape
    d = 2 * half
    d_ff = w_dn.shape[1]
    tf = _tile(d_ff, 512)
    nf = d_ff // tf
    n_blocks = n_rows // row_block

    def jj(i, j, m):
        return jnp.where(i < m[0], j, nf - 1)

    grid_spec = pltpu.PrefetchScalarGridSpec(
        num_scalar_prefetch=1,
        grid=(n_blocks, nf),
        in_specs=[
            pl.BlockSpec((row_block, half), lambda i, j, m: (i, 0)),
            pl.BlockSpec((None, d, tf), lambda i, j, m: (m[1 + i], 0, jj(i, j, m))),
            pl.BlockSpec((None, d, tf), lambda i, j, m: (m[1 + i], 0, nf + jj(i, j, m))),
            pl.BlockSpec((None, 1, tf), lambda i, j, m: (m[1 + i], 0, jj(i, j, m))),
            pl.BlockSpec((None, 1, tf), lambda i, j, m: (m[1 + i], 0, nf + jj(i, j, m))),
            pl.BlockSpec((None, tf, d), lambda i, j, m: (m[1 + i], jj(i, j, m), 0)),
            pl.BlockSpec((None, 1, d), lambda i, j, m: (m[1 + i], 0, 0)),
        ],
        out_specs=pl.BlockSpec((row_block, d), lambda i, j, m: (i, 0)),
        scratch_shapes=[pltpu.VMEM((row_block, half), BF16), pltpu.VMEM((row_block, half), BF16)],
    )
    return pl.pallas_call(
        _expert_kernel,
        grid_spec=grid_spec,
        out_shape=jax.ShapeDtypeStruct((n_rows, d), F32),
        compiler_params=_params("arbitrary", "arbitrary"),
        name="expert_swiglu",
    )(meta, xs, w_gu, w_gu, b_gu, b_gu, w_dn, b_dn)


def _combine_kernel(dest_ref, gate_ref, x1_ref, fn_ref, ys_ref, o_ref, buf, sem):
    tt = dest_ref.shape[1]

    def row_copy(t, kk):
        return pltpu.make_async_copy(ys_ref.at[pl.ds(dest_ref[kk, t], 1)], buf.at[kk, pl.ds(t, 1)], sem)

    def start(t, carry):
        for kk in range(TOP_K):
            row_copy(t, kk).start()
        return carry

    def wait(t, carry):
        for kk in range(TOP_K):
            row_copy(t, kk).wait()
        return carry

    lax.fori_loop(0, tt, start, 0, unroll=4)
    lax.fori_loop(0, tt, wait, 0, unroll=4)
    acc = x1_ref[...]
    gates = gate_ref[...]
    for kk in range(TOP_K):
        acc = acc + buf[kk] * gates[:, kk:kk + 1]
    o_ref[...] = _rmsnorm(acc, fn_ref[...])


def _combine(dest, gates_t, x1, final_gain, ys, row_off):
    n, d = x1.shape
    tt = _tile(n, 256)
    assert row_off % tt == 0
    off = row_off // tt
    return pl.pallas_call(
        _combine_kernel,
        grid=(n // tt,),
        in_specs=[
            pl.BlockSpec((TOP_K, tt), lambda i: (0, off + i), memory_space=pltpu.SMEM),
            pl.BlockSpec((tt, TOP_K), lambda i: (off + i, 0)),
            pl.BlockSpec((tt, d), lambda i: (i, 0)),
            pl.BlockSpec((1, d), lambda i: (0, 0)),
            pl.BlockSpec(memory_space=pl.ANY),
        ],
        out_specs=pl.BlockSpec((tt, d), lambda i: (i, 0)),
        out_shape=jax.ShapeDtypeStruct((n, d), F32),
        scratch_shapes=[pltpu.VMEM((TOP_K, tt, d), F32), pltpu.SemaphoreType.DMA(())],
        compiler_params=_params("arbitrary"),
        name="combine_rows",
    )(dest, gates_t, x1, final_gain, ys)


def _rope_tables(seq_len):
    rows = seq_len // GRID_W
    row = jnp.repeat(jnp.arange(rows, dtype=F32), GRID_W)
    col = jnp.tile(jnp.arange(GRID_W, dtype=F32), rows)
    freqs = ROPE_THETA ** (-jnp.arange(ROPE_HALF, dtype=F32) / ROPE_HALF)
    ang_r = row[:, None] * freqs[None, :]
    ang_c = col[:, None] * freqs[None, :]
    cos = jnp.concatenate([jnp.cos(ang_r), jnp.cos(ang_r), jnp.cos(ang_c), jnp.cos(ang_c)], axis=1)
    sin = jnp.concatenate([-jnp.sin(ang_r), jnp.sin(ang_r), -jnp.sin(ang_c), jnp.sin(ang_c)], axis=1)
    return cos, sin


def kernel(x_prompt, x_sample, mix_norm, w_in, q_norm, k_norm, hg_lb_logits, hg_out_norm, w_up_attn,
           w_up_hgrn, w_out, ffn_norm, w_router, b_router, w_gate_up, b_gate_up, w_down, b_down, final_norm):
    assert mix_norm.shape[0] == 1, "single trunk layer"
    d = x_prompt.shape[-1]
    hg_w = HG_HEADS * HG_D
    n_exp = w_router.shape[-1]
    row_block = MOE_ROW_BLOCK
    streams = [(x.reshape(-1, d), x.shape[0], x.shape[1]) for x in (x_prompt, x_sample)]
    n_total = sum(x.shape[0] for x, _, _ in streams)

    lb = jnp.cumsum(jax.nn.softmax(hg_lb_logits.astype(F32), axis=1), axis=1)[:, 0]
    w_in_b = w_in[0].astype(BF16)
    w_ua_b = w_up_attn[0].astype(BF16)
    w_uh_b = w_up_hgrn[0].astype(BF16)
    w_out_b = w_out[0].astype(BF16)
    w_r_t = w_router[0].T
    mix_g = mix_norm[0].reshape(1, d)

    x1s, idxs, gate_ts = [], [], []
    hp = None
    row_off = 0
    for x, batch, seq in streams:
        cos, sin = _rope_tables(seq)
        qkv, zh, gates = _norm_proj(x, mix_g, w_in_b, q_norm[0].reshape(1, HEAD_DIM),
                                    k_norm[0].reshape(1, HEAD_DIM), cos, sin, seq)
        attn = _attention(qkv, batch, seq)
        hg = _hgrn(zh, lb[0:1], lb[1:2], hg_out_norm[0].reshape(1, hg_w), batch, seq)
        merged = _merge(attn, hg, gates, w_ua_b, w_uh_b)
        x1, hp, idx, gate = _outproj_router(x, merged, w_out_b, ffn_norm[0].reshape(1, d), w_r_t,
                                            b_router[0].reshape(n_exp, 1), hp, row_off, n_total)
        x1s.append(x1)
        idxs.append(idx)
        gate_ts.append(gate.T)
        row_off += x.shape[0]

    idx = jnp.concatenate(idxs, axis=1)
    gate_t = jnp.concatenate(gate_ts, axis=0)
    dest, counts = _route(idx, row_block)
    cnt = counts[:, 0].astype(jnp.int32)
    padded = (cnt + row_block - 1) // row_block * row_block
    pad_end = jnp.cumsum(padded)
    n_rows = n_total * TOP_K + n_exp * row_block
    n_blocks = n_rows // row_block
    blk_start = jnp.arange(n_blocks, dtype=jnp.int32) * row_block
    blk_e = jnp.minimum(jnp.sum(pad_end[None, :] <= blk_start[:, None], axis=1), n_exp - 1).astype(jnp.int32)
    meta = jnp.concatenate([(pad_end[-1:] // row_block).astype(jnp.int32), blk_e])
    seg = jnp.stack([pad_end - padded, pad_end]).astype(jnp.int32)

    xs = _dispatch(seg, dest, hp, n_rows, row_block)
    ys = _experts(meta, xs, w_gate_up[0].astype(BF16), b_gate_up[0].reshape(n_exp, 1, -1),
                  w_down[0].astype(BF16), b_down[0].reshape(n_exp, 1, d), row_block)

    outs = []
    row_off = 0
    for (x, batch, seq), x1 in zip(streams, x1s):
        out = _combine(dest, gate_t, x1, final_norm.reshape(1, d), ys, row_off)
        outs.append(out.reshape(batch, seq, d))
        row_off += x.shape[0]
    return tuple(outs)
```

```python
import functools

import jax
import jax.numpy as jnp
from jax import lax
from jax.experimental import pallas as pl
from jax.experimental.pallas import tpu as pltpu

GRID_W = 64
HEAD_DIM = 128
N_Q_HEADS = 16
N_KV_HEADS = 4
ROPE_THETA = 10000.0
ROPE_HALF = HEAD_DIM // 4
HG_HEADS = 8
HG_D = 128
HG_CHUNK = 64
N_EXPERTS = 32
TOP_K = 4
SWIGLU_LIMIT = 7.0
SWIGLU_ALPHA = 1.702
NORM_EPS = 1e-5

HG_SAFE_LOG_DECAY = -60.0

LOG2_E = 1.4426950408889634
ATTN_FIXED_SHIFT_LIMIT = 60.0

V7X_VMEM_BYTES = 64 * 1024 * 1024
VMEM_LIMIT_BYTES = V7X_VMEM_BYTES - 8 * 1024 * 1024
LANES = 128
MOE_ROW_BLOCK = 512

BF16 = jnp.bfloat16
F32 = jnp.float32


def _params(*sem):
    return pltpu.CompilerParams(dimension_semantics=sem, vmem_limit_bytes=VMEM_LIMIT_BYTES)


def _tile(n, pref):
    t = min(n, pref)
    while n % t:
        t //= 2
    return t


def _sigmoid(x):
    return 1.0 / (1.0 + jnp.exp(-x))


def _rmsnorm(x, g):
    return x * lax.rsqrt(jnp.mean(x * x, axis=-1, keepdims=True) + NORM_EPS) * g


def _dot(a, b):
    return jnp.dot(a, b, preferred_element_type=F32)


def _dot_nt(a, b):
    return lax.dot_general(a, b, (((1,), (1,)), ((), ())), preferred_element_type=F32)


def _split3(x):
    hi = x.astype(BF16)
    r = x - hi.astype(F32)
    mid = r.astype(BF16)
    lo = (r - mid.astype(F32)).astype(BF16)
    return hi, mid, lo


def _dot_exact_lhs(m_bf16, x):
    hi, mid, lo = _split3(x)
    return _dot(m_bf16, hi) + _dot(m_bf16, mid) + _dot(m_bf16, lo)


def _rope_head(zh, gain, cos, sin, first, scale):
    y = _rmsnorm(zh, gain)
    partner = jnp.where(first, pltpu.roll(y, HEAD_DIM - ROPE_HALF, 1), pltpu.roll(y, ROPE_HALF, 1))
    return (y * cos + partner * sin) * scale


def _proj_kernel(x_ref, g_ref, w_ref, qn_ref, kn_ref, cos_ref, sin_ref,
                 qkv_ref, zh_ref, gate_ref, h_scr, *, tile_kinds):
    j = pl.program_id(1)

    @pl.when(j == 0)
    def _():
        h_scr[...] = _rmsnorm(x_ref[...], g_ref[...]).astype(BF16)

    tm = h_scr.shape[0]
    lane = lax.broadcasted_iota(jnp.int32, (tm, HEAD_DIM), 1)
    first = (lane % (2 * ROPE_HALF)) < ROPE_HALF

    def qkv_tile(kinds):
        z = _dot(h_scr[...], w_ref[...])
        outs = []
        for h, kind in enumerate(kinds):
            zh = z[:, h * HEAD_DIM:(h + 1) * HEAD_DIM]
            if kind == "q":
                zh = _rope_head(zh, qn_ref[...], cos_ref[...], sin_ref[...], first, LOG2_E * HEAD_DIM ** -0.5)
            elif kind == "k":
                zh = _rope_head(zh, kn_ref[...], cos_ref[...], sin_ref[...], first, 1.0)
            outs.append(zh)
        return jnp.concatenate(outs, axis=1)

    for lo, hi, kind in tile_kinds:
        @pl.when((j >= lo) & (j < hi))
        def _(kind=kind):
            if kind == "zh":
                zh_ref[...] = _dot(h_scr[...], w_ref[...])
            elif kind == "gate":
                gate_ref[...] = _dot(h_scr[...], w_ref[...]).astype(gate_ref.dtype)
            else:
                qkv_ref[...] = qkv_tile(kind).astype(qkv_ref.dtype)


def _norm_proj(x, gain, w, q_gain, k_gain, cos, sin, seq):
    n, d = x.shape
    attn_w = N_Q_HEADS * HEAD_DIM
    kv_w = N_KV_HEADS * HEAD_DIM
    hg5 = 5 * HG_HEADS * HG_D
    qkv_w = attn_w + 2 * kv_w
    assert w.shape[1] == qkv_w + hg5 + 2 * d
    tm = _tile(seq, 1024)
    tn = 1024
    while attn_w % tn or (2 * kv_w) % tn or hg5 % tn or (2 * d) % tn:
        tn //= 2
    assert tn % HEAD_DIM == 0
    heads = ["q"] * N_Q_HEADS + ["k"] * N_KV_HEADS + ["v"] * N_KV_HEADS
    hpt = tn // HEAD_DIM
    n_qkv, n_zh, n_gate = qkv_w // tn, hg5 // tn, 2 * d // tn
    tile_kinds = []
    for t in range(n_qkv):
        kind = tuple(heads[t * hpt:(t + 1) * hpt])
        if tile_kinds and tile_kinds[-1][2] == kind:
            tile_kinds[-1] = (tile_kinds[-1][0], t + 1, kind)
        else:
            tile_kinds.append((t, t + 1, kind))
    tile_kinds += [(n_qkv, n_qkv + n_zh, "zh"), (n_qkv + n_zh, n_qkv + n_zh + n_gate, "gate")]
    n_pos = seq // tm
    return pl.pallas_call(
        functools.partial(_proj_kernel, tile_kinds=tile_kinds),
        grid=(n // tm, n_qkv + n_zh + n_gate),
        in_specs=[
            pl.BlockSpec((tm, d), lambda i, j: (i, 0)),
            pl.BlockSpec((1, d), lambda i, j: (0, 0)),
            pl.BlockSpec((d, tn), lambda i, j: (0, j)),
            pl.BlockSpec((1, HEAD_DIM), lambda i, j: (0, 0)),
            pl.BlockSpec((1, HEAD_DIM), lambda i, j: (0, 0)),
            pl.BlockSpec((tm, HEAD_DIM), lambda i, j: (i % n_pos, 0)),
            pl.BlockSpec((tm, HEAD_DIM), lambda i, j: (i % n_pos, 0)),
        ],
        out_specs=[
            pl.BlockSpec((tm, tn), lambda i, j: (i, jnp.minimum(j, n_qkv - 1))),
            pl.BlockSpec((tm, tn), lambda i, j: (i, jnp.clip(j - n_qkv, 0, n_zh - 1))),
            pl.BlockSpec((tm, tn), lambda i, j: (i, jnp.clip(j - n_qkv - n_zh, 0, n_gate - 1))),
        ],
        out_shape=[
            jax.ShapeDtypeStruct((n, qkv_w), BF16),
            jax.ShapeDtypeStruct((n, hg5), F32),
            jax.ShapeDtypeStruct((n, 2 * d), BF16),
        ],
        scratch_shapes=[pltpu.VMEM((tm, d), BF16)],
        compiler_params=_params("parallel", "arbitrary"),
        name="norm_proj",
    )(x, gain, w, q_gain, k_gain, cos, sin)


def _attn_kernel(bound_ref, q_ref, k_ref, v_ref, o_ref, acc_scr, m_scr, *, tk, group):
    tq = q_ref.shape[0]
    seq = k_ref.shape[0]
    q = jnp.concatenate([q_ref[:, g * HEAD_DIM:(g + 1) * HEAD_DIM] for g in range(group)], axis=0)
    ones = jnp.ones((tk, HEAD_DIM), BF16)
    bound = bound_ref[0]
    acc_scr[...] = jnp.zeros(acc_scr.shape, F32)

    def chunk(c):
        rows = pl.ds(pl.multiple_of(c * tk, tk), tk)
        s = _dot_nt(q, k_ref[rows, :])
        return s, jnp.concatenate([v_ref[rows, :], ones], axis=1)

    @pl.when(bound <= ATTN_FIXED_SHIFT_LIMIT)
    def _():
        def body(c, carry):
            s, v1 = chunk(c)
            acc_scr[...] += _dot(jnp.exp2(s - bound).astype(BF16), v1)
            return carry
        lax.fori_loop(0, seq // tk, body, 0, unroll=True)

    @pl.when(bound > ATTN_FIXED_SHIFT_LIMIT)
    def _():
        m_scr[...] = jnp.full(m_scr.shape, -1e30, F32)

        def body(c, carry):
            s, v1 = chunk(c)
            m_prev = m_scr[...]
            m_new = jnp.maximum(m_prev, jnp.max(s, axis=1, keepdims=True))
            alpha = jnp.exp2(m_prev - m_new)
            p = jnp.exp2(s - jnp.tile(m_new, (1, tk // LANES)))
            acc_scr[...] = jnp.tile(alpha, (1, 2)) * acc_scr[...] + _dot(p.astype(BF16), v1)
            m_scr[...] = m_new
            return carry
        lax.fori_loop(0, seq // tk, body, 0)

    acc = acc_scr[...]
    o = acc[:, :HEAD_DIM] / acc[:, HEAD_DIM:]
    for g in range(group):
        o_ref[:, g * HEAD_DIM:(g + 1) * HEAD_DIM] = o[g * tq:(g + 1) * tq].astype(o_ref.dtype)


def _attention(qkv, score_bound, batch, seq):
    group = N_Q_HEADS // N_KV_HEADS
    gw = group * HEAD_DIM
    tq = _tile(seq, 256)
    tk = _tile(seq, 512)
    nq = seq // tq
    return pl.pallas_call(
        functools.partial(_attn_kernel, tk=tk, group=group),
        grid=(batch, N_KV_HEADS, nq),
        in_specs=[
            pl.BlockSpec(memory_space=pltpu.SMEM),
            pl.BlockSpec((tq, gw), lambda b, n, i: (b * nq + i, n)),
            pl.BlockSpec((seq, HEAD_DIM), lambda b, n, i: (b, N_Q_HEADS + n)),
            pl.BlockSpec((seq, HEAD_DIM), lambda b, n, i: (b, N_Q_HEADS + N_KV_HEADS + n)),
        ],
        out_specs=pl.BlockSpec((tq, gw), lambda b, n, i: (b * nq + i, n)),
        out_shape=jax.ShapeDtypeStruct((batch * seq, N_Q_HEADS * HEAD_DIM), BF16),
        scratch_shapes=[
            pltpu.VMEM((group * tq, 2 * HEAD_DIM), F32),
            pltpu.VMEM((group * tq, LANES), F32),
        ],
        compiler_params=_params("parallel", "parallel", "arbitrary"),
        name="gqa_attention",
    )(score_bound, qkv, qkv, qkv)


def _hgrn_span(i, q_scr, b_scr, k_scr, zi_ref, vt_scr, o_scr, s_scr, tri, *, span, reverse, pairwise):
    C = HG_CHUNK
    nc = span // C
    r0 = pl.multiple_of(i * span, span)
    rows = pl.ds(r0, span)
    q = q_scr[rows, :]
    b = b_scr[rows, :]
    k = k_scr[rows, :]
    v = zi_ref[rows, :]
    vb = v.astype(BF16)
    end_row = 0 if reverse else C - 1
    ends = [b[c * C + end_row:c * C + end_row + 1, :] for c in range(nc)]
    b_end = jnp.concatenate([jnp.broadcast_to(e, (C, HG_D)) for e in ends], axis=0)
    qtb = (q * jnp.exp(b)).astype(BF16)
    kp = k * jnp.exp(b_end - b)

    if pairwise:
        rid = lax.broadcasted_iota(jnp.int32, (span, HG_D), 0)

        def pair(s, acc):
            cs = (s // C) * C
            if reverse:
                m = (rid <= s) & (rid >= cs)
            else:
                m = (rid >= s) & (rid < cs + C)
            w = jnp.where(m, jnp.exp(jnp.minimum(b - b_scr[pl.ds(r0 + s, 1), :], 0.0)), 0.0)
            r = jnp.sum(q * w * k_scr[pl.ds(r0 + s, 1), :], axis=1, keepdims=True)
            return acc + r * zi_ref[pl.ds(r0 + s, 1), :]

        o_intra = lax.fori_loop(0, span, pair, jnp.zeros((span, HG_D), F32))
    else:
        ktb = (k * jnp.exp(-b)).astype(BF16)
        a = jnp.where(tri, _dot_nt(qtb, ktb), 0.0)
        o_intra = _dot(a.astype(BF16), vb)

    chunk_of_row = lax.broadcasted_iota(jnp.int32, (span, HG_D), 0) // C
    kp_blocks = jnp.concatenate([jnp.where(chunk_of_row == c, kp, 0.0) for c in range(nc)], axis=1)
    upd = _dot(vt_scr[:, rows], kp_blocks.astype(BF16))

    st = s_scr[...]
    states = [None] * nc
    for c in (reversed(range(nc)) if reverse else range(nc)):
        states[c] = st.astype(BF16)
        st = st * jnp.exp(ends[c]) + upd[:, c * HG_D:(c + 1) * HG_D]
    s_scr[...] = st
    o_inter = _dot_nt(qtb, jnp.concatenate(states, axis=0))
    o_scr[rows, :] = o_intra + jnp.concatenate(
        [o_inter[c * C:(c + 1) * C, c * HG_D:(c + 1) * HG_D] for c in range(nc)], axis=0)


def _hgrn_kernel(zq_ref, zf_ref, zb_ref, zi_ref, zo_ref, lbf_ref, lbb_ref, on_ref, out_ref,
                 of_scr, ob_scr, sf_scr, sb_scr, q_scr, bf_scr, bb_scr, kf_scr, kb_scr, vt_scr, *, span):
    seq = zq_ref.shape[0]
    n_span = seq // span
    C = HG_CHUNK
    nc = span // C
    sf_scr[...] = jnp.zeros(sf_scr.shape, F32)
    sb_scr[...] = jnp.zeros(sb_scr.shape, F32)
    r = lax.broadcasted_iota(jnp.int32, (span, span), 0)
    c = lax.broadcasted_iota(jnp.int32, (span, span), 1)
    same = (r // C) == (c // C)
    tri_f = same & (c <= r)
    tri_b = same & (c >= r)

    def prepare(i, min_end):
        rows = pl.ds(pl.multiple_of(i * span, span), span)
        qh = zq_ref[rows, :]
        q_scr[rows, :] = qh * _sigmoid(qh)
        vt_scr[:, rows] = zi_ref[rows, :].T.astype(BF16)
        for z_ref, lb_ref, tri, b_scr, k_scr, end_row in (
                (zf_ref, lbf_ref, tri_f, bf_scr, kf_scr, C - 1), (zb_ref, lbb_ref, tri_b, bb_scr, kb_scr, 0)):
            lb = lb_ref[...]
            f = lb + (1.0 - lb) * _sigmoid(z_ref[rows, :])
            b = _dot_exact_lhs(tri.astype(BF16), jnp.log(f))
            b_scr[rows, :] = b
            k_scr[rows, :] = 1.0 - f
            for cc in range(nc):
                min_end = jnp.minimum(min_end, b[cc * C + end_row:cc * C + end_row + 1, :])
        return min_end

    min_end = lax.fori_loop(0, n_span, prepare, jnp.zeros((1, HG_D), F32))
    safe = jnp.min(min_end) > HG_SAFE_LOG_DECAY

    def scan(pairwise):
        def body(i, carry):
            _hgrn_span(i, q_scr, bf_scr, kf_scr, zi_ref, vt_scr, of_scr, sf_scr, tri_f,
                       span=span, reverse=False, pairwise=pairwise)
            _hgrn_span(n_span - 1 - i, q_scr, bb_scr, kb_scr, zi_ref, vt_scr, ob_scr, sb_scr, tri_b,
                       span=span, reverse=True, pairwise=pairwise)
            return carry
        lax.fori_loop(0, n_span, body, 0)

    @pl.when(safe)
    def _():
        scan(False)

    @pl.when(jnp.logical_not(safe))
    def _():
        scan(True)

    def finish(i, carry):
        rows = pl.ds(pl.multiple_of(i * span, span), span)
        o = _rmsnorm(of_scr[rows, :] + ob_scr[rows, :], on_ref[...])
        og = zo_ref[rows, :]
        out_ref[rows, :] = (o * (og * _sigmoid(og))).astype(out_ref.dtype)
        return carry

    lax.fori_loop(0, n_span, finish, 0)


def _hgrn(zh, lb_f, lb_b, out_norm, batch, seq):
    span = _tile(seq, 256)
    assert span % HG_CHUNK == 0
    zspec = lambda grp: pl.BlockSpec((seq, HG_D), lambda b, h: (b, grp * HG_HEADS + h))
    hspec = pl.BlockSpec((1, HG_D), lambda b, h: (0, h))
    seq_buf = pltpu.VMEM((seq, HG_D), F32)
    return pl.pallas_call(
        functools.partial(_hgrn_kernel, span=span),
        grid=(batch, HG_HEADS),
        in_specs=[zspec(0), zspec(1), zspec(2), zspec(3), zspec(4), hspec, hspec, hspec],
        out_specs=pl.BlockSpec((seq, HG_D), lambda b, h: (b, h)),
        out_shape=jax.ShapeDtypeStruct((batch * seq, HG_HEADS * HG_D), BF16),
        scratch_shapes=[
            seq_buf, seq_buf,
            pltpu.VMEM((HG_D, HG_D), F32), pltpu.VMEM((HG_D, HG_D), F32),
            seq_buf, seq_buf, seq_buf, seq_buf, seq_buf,
            pltpu.VMEM((HG_D, seq), BF16),
        ],
        compiler_params=_params("parallel", "parallel"),
        name="hgrn2",
    )(zh, zh, zh, zh, zh, lb_f, lb_b, out_norm)


def _merge_kernel(a_ref, h_ref, ga_ref, gb_ref, wa_ref, wh_ref, o_ref):
    ya = _dot(a_ref[...], wa_ref[...])
    yb = _dot(h_ref[...], wh_ref[...])
    merged = _sigmoid(ga_ref[...].astype(F32)) * ya + _sigmoid(gb_ref[...].astype(F32)) * yb
    o_ref[...] = merged.astype(o_ref.dtype)


def _merge(attn, hg, gates, w_up_attn, w_up_hgrn):
    n, wa = attn.shape
    wh = hg.shape[1]
    d = w_up_attn.shape[1]
    tm = _tile(n, 1024)
    tn = _tile(d, 512)
    nj = d // tn
    return pl.pallas_call(
        _merge_kernel,
        grid=(n // tm, nj),
        in_specs=[
            pl.BlockSpec((tm, wa), lambda i, j: (i, 0)),
            pl.BlockSpec((tm, wh), lambda i, j: (i, 0)),
            pl.BlockSpec((tm, tn), lambda i, j: (i, j)),
            pl.BlockSpec((tm, tn), lambda i, j: (i, nj + j)),
            pl.BlockSpec((wa, tn), lambda i, j: (0, j)),
            pl.BlockSpec((wh, tn), lambda i, j: (0, j)),
        ],
        out_specs=pl.BlockSpec((tm, tn), lambda i, j: (i, j)),
        out_shape=jax.ShapeDtypeStruct((n, d), BF16),
        compiler_params=_params("parallel", "arbitrary"),
        name="gated_merge",
    )(attn, hg, gates, gates, w_up_attn, w_up_hgrn)


def _pack_bf16_pairs(h):
    half = h.shape[1] // 2
    lo = pltpu.bitcast(h[:, :half].astype(BF16).astype(F32), jnp.uint32)
    hi = pltpu.bitcast(h[:, half:].astype(BF16).astype(F32), jnp.uint32)
    return (hi & jnp.uint32(0xFFFF0000)) | (lo >> 16)


def _unpack_bf16_pairs(u):
    lo = pltpu.bitcast(u << 16, F32).astype(BF16)
    hi = pltpu.bitcast(u & jnp.uint32(0xFFFF0000), F32).astype(BF16)
    return lo, hi


def _outproj_router_kernel(x_ref, m_ref, w_ref, g_ref, wr_ref, br_ref, hp_in_ref,
                           x1_ref, hp_ref, idx_ref, gate_ref):
    del hp_in_ref
    x1 = x_ref[...] + _dot(m_ref[...], w_ref[...])
    x1_ref[...] = x1
    h = _rmsnorm(x1, g_ref[...])
    hp_ref[...] = _pack_bf16_pairs(h)
    h_hi, h_mid, h_lo = _split3(h)
    w_hi, w_mid, w_lo = _split3(wr_ref[...])
    lg = (_dot_nt(w_hi, h_hi) + _dot_nt(w_hi, h_mid) + _dot_nt(w_mid, h_hi)
          + _dot_nt(w_mid, h_mid) + _dot_nt(w_hi, h_lo) + _dot_nt(w_lo, h_hi)) + br_ref[...]
    n_exp, tm = lg.shape
    eid = lax.broadcasted_iota(jnp.int32, (n_exp, tm), 0)
    vals = []
    for kk in range(TOP_K):
        m = jnp.max(lg, axis=0, keepdims=True)
        sel = jnp.min(jnp.where(lg == m, eid, n_exp), axis=0, keepdims=True)
        idx_ref[kk:kk + 1, :] = sel
        vals.append(m)
        lg = jnp.where(eid == sel, -jnp.inf, lg)
    ex = [jnp.exp(vv - vals[0]) for vv in vals]
    den = ex[0]
    for e in ex[1:]:
        den = den + e
    for kk in range(TOP_K):
        gate_ref[kk:kk + 1, :] = ex[kk] / den


def _outproj_router(x, merged, w_out, ffn_gain, w_router_t, b_router, hp_prev, row_off, n_total):
    n, d = x.shape
    tm = _tile(n, 512)
    assert row_off % tm == 0
    n_exp = w_router_t.shape[0]
    if hp_prev is None:
        hp_prev = jnp.zeros((8, LANES), jnp.uint32)
        aliases = {}
    else:
        aliases = {6: 1}
    return pl.pallas_call(
        _outproj_router_kernel,
        grid=(n // tm,),
        in_specs=[
            pl.BlockSpec((tm, d), lambda i: (i, 0)),
            pl.BlockSpec((tm, d), lambda i: (i, 0)),
            pl.BlockSpec((d, d), lambda i: (0, 0)),
            pl.BlockSpec((1, d), lambda i: (0, 0)),
            pl.BlockSpec((n_exp, d), lambda i: (0, 0)),
            pl.BlockSpec((n_exp, 1), lambda i: (0, 0)),
            pl.BlockSpec(memory_space=pl.ANY),
        ],
        out_specs=[
            pl.BlockSpec((tm, d), lambda i: (i, 0)),
            pl.BlockSpec((tm, d // 2), lambda i: (row_off // tm + i, 0)),
            pl.BlockSpec((TOP_K, tm), lambda i: (0, i)),
            pl.BlockSpec((TOP_K, tm), lambda i: (0, i)),
        ],
        out_shape=[
            jax.ShapeDtypeStruct((n, d), F32),
            jax.ShapeDtypeStruct((n_total, d // 2), jnp.uint32),
            jax.ShapeDtypeStruct((TOP_K, n), jnp.int32),
            jax.ShapeDtypeStruct((TOP_K, n), F32),
        ],
        input_output_aliases=aliases,
        compiler_params=_params("parallel"),
        name="outproj_router",
    )(x, merged, w_out, ffn_gain, w_router_t, b_router, hp_prev)


def _route_kernel(idx_ref, dest_ref, cnt_ref, cnt_scr, base_scr, *, row_block):
    phase = pl.program_id(0)
    i = pl.program_id(1)
    n_exp = cnt_scr.shape[0]
    tt = idx_ref.shape[1]
    eid = lax.broadcasted_iota(jnp.int32, (n_exp, tt), 0)
    onehot = [(eid == idx_ref[kk:kk + 1, :]) for kk in range(TOP_K)]

    @pl.when((phase == 0) & (i == 0))
    def _():
        cnt_scr[...] = jnp.zeros(cnt_scr.shape, F32)

    @pl.when(phase == 0)
    def _():
        tot = onehot[0].astype(F32)
        for oh in onehot[1:]:
            tot = tot + oh.astype(F32)
        cnt_scr[...] = cnt_scr[...] + jnp.sum(tot, axis=1, keepdims=True)
        cnt_ref[...] = cnt_scr[...]

    @pl.when((phase == 1) & (i == 0))
    def _():
        cnt = cnt_scr[...].astype(jnp.int32)
        padded = ((cnt + (row_block - 1)) // row_block * row_block).astype(F32)
        er = lax.broadcasted_iota(jnp.int32, (n_exp, n_exp), 0)
        ec = lax.broadcasted_iota(jnp.int32, (n_exp, n_exp), 1)
        base_scr[...] = _dot_exact_lhs((ec < er).astype(BF16), padded)

    @pl.when(phase == 1)
    def _():
        tr = lax.broadcasted_iota(jnp.int32, (tt, tt), 0)
        tc = lax.broadcasted_iota(jnp.int32, (tt, tt), 1)
        before = (tr < tc).astype(BF16)
        run = base_scr[...][:, :1]
        for kk in range(TOP_K):
            oh = onehot[kk].astype(F32)
            rank = _dot(oh.astype(BF16), before) + run
            dest_ref[kk:kk + 1, :] = jnp.sum(oh * rank, axis=0, keepdims=True).astype(jnp.int32)
            run = run + jnp.sum(oh, axis=1, keepdims=True)
        base_scr[...] = jnp.broadcast_to(run, base_scr.shape)


def _route(idx, row_block):
    n = idx.shape[1]
    tt = _tile(n, 512)
    return pl.pallas_call(
        functools.partial(_route_kernel, row_block=row_block),
        grid=(2, n // tt),
        in_specs=[pl.BlockSpec((TOP_K, tt), lambda p, i: (0, i))],
        out_specs=[
            pl.BlockSpec((TOP_K, tt), lambda p, i: (0, i * p)),
            pl.BlockSpec((N_EXPERTS, LANES), lambda p, i: (0, 0)),
        ],
        out_shape=[
            jax.ShapeDtypeStruct((TOP_K, n), jnp.int32),
            jax.ShapeDtypeStruct((N_EXPERTS, LANES), F32),
        ],
        scratch_shapes=[pltpu.VMEM((N_EXPERTS, LANES), F32), pltpu.VMEM((N_EXPERTS, LANES), F32)],
        compiler_params=_params("arbitrary", "arbitrary"),
        name="route_offsets",
    )(idx)


def _dispatch_kernel(seg_ref, dest_ref, h_ref, xs_ref, zero_scr, sem, zsem, *, row_block):
    tt = dest_ref.shape[1]
    n_exp = seg_ref.shape[1]

    @pl.when(pl.program_id(0) == 0)
    def _():
        zero_scr[...] = jnp.zeros(zero_scr.shape, zero_scr.dtype)

        def zero_copy(e):
            start = pl.multiple_of(seg_ref[1, e] - row_block, row_block)
            return pltpu.make_async_copy(zero_scr, xs_ref.at[pl.ds(start, row_block)], zsem)

        for e in range(n_exp):
            @pl.when(seg_ref[1, e] > seg_ref[0, e])
            def _(e=e):
                zero_copy(e).start()
        for e in range(n_exp):
            @pl.when(seg_ref[1, e] > seg_ref[0, e])
            def _(e=e):
                zero_copy(e).wait()

    def row_copy(t, kk):
        return pltpu.make_async_copy(h_ref.at[pl.ds(t, 1)], xs_ref.at[pl.ds(dest_ref[kk, t], 1)], sem)

    def start(t, carry):
        for kk in range(TOP_K):
            row_copy(t, kk).start()
        return carry

    def wait(t, carry):
        for kk in range(TOP_K):
            row_copy(t, kk).wait()
        return carry

    lax.fori_loop(0, tt, start, 0, unroll=4)
    lax.fori_loop(0, tt, wait, 0, unroll=4)


def _dispatch(seg, dest, hp, n_rows, row_block):
    n, w = hp.shape
    tt = _tile(n, 512)
    grid_spec = pltpu.PrefetchScalarGridSpec(
        num_scalar_prefetch=1,
        grid=(n // tt,),
        in_specs=[
            pl.BlockSpec((TOP_K, tt), lambda i, s: (0, i), memory_space=pltpu.SMEM),
            pl.BlockSpec((tt, w), lambda i, s: (i, 0)),
        ],
        out_specs=pl.BlockSpec(memory_space=pl.ANY),
        scratch_shapes=[pltpu.VMEM((row_block, w), hp.dtype), pltpu.SemaphoreType.DMA(()),
                        pltpu.SemaphoreType.DMA(())],
    )
    return pl.pallas_call(
        functools.partial(_dispatch_kernel, row_block=row_block),
        grid_spec=grid_spec,
        out_shape=jax.ShapeDtypeStruct((n_rows, w), hp.dtype),
        compiler_params=_params("arbitrary"),
        name="dispatch_rows",
    )(seg, dest, hp)


def _expert_kernel(meta_ref, xs_ref, wg_ref, wl_ref, bg_ref, bl_ref, wd_ref, bd_ref, y_ref, xlo_scr, xhi_scr):
    i = pl.program_id(0)
    j = pl.program_id(1)

    @pl.when(i < meta_ref[0])
    def _():
        @pl.when(j == 0)
        def _():
            lo, hi = _unpack_bf16_pairs(xs_ref[...])
            xlo_scr[...] = lo
            xhi_scr[...] = hi
            y_ref[...] = jnp.broadcast_to(bd_ref[...], y_ref.shape)

        half = xlo_scr.shape[1]
        xlo = xlo_scr[...]
        xhi = xhi_scr[...]
        glu = _dot(xlo, wg_ref[:half, :]) + _dot(xhi, wg_ref[half:, :]) + bg_ref[...]
        lin = _dot(xlo, wl_ref[:half, :]) + _dot(xhi, wl_ref[half:, :]) + bl_ref[...]
        glu = jnp.minimum(glu, SWIGLU_LIMIT)
        lin = jnp.clip(lin, -SWIGLU_LIMIT, SWIGLU_LIMIT)
        act = glu * _sigmoid(SWIGLU_ALPHA * glu) * (lin + 1.0)
        y_ref[...] = y_ref[...] + _dot(act.astype(BF16), wd_ref[...])


def _experts(meta, xs, w_gu, b_gu, w_dn, b_dn, row_block):
    n_rows, half = xs.shape
    d = 2 * half
    d_ff = w_dn.shape[1]
    tf = _tile(d_ff, 512)
    nf = d_ff // tf
    n_blocks = n_rows // row_block

    def jj(i, j, m):
        return jnp.where(i < m[0], j, nf - 1)

    grid_spec = pltpu.PrefetchScalarGridSpec(
        num_scalar_prefetch=1,
        grid=(n_blocks, nf),
        in_specs=[
            pl.BlockSpec((row_block, half), lambda i, j, m: (i, 0)),
            pl.BlockSpec((None, d, tf), lambda i, j, m: (m[1 + i], 0, jj(i, j, m))),
            pl.BlockSpec((None, d, tf), lambda i, j, m: (m[1 + i], 0, nf + jj(i, j, m))),
            pl.BlockSpec((None, 1, tf), lambda i, j, m: (m[1 + i], 0, jj(i, j, m))),
            pl.BlockSpec((None, 1, tf), lambda i, j, m: (m[1 + i], 0, nf + jj(i, j, m))),
            pl.BlockSpec((None, tf, d), lambda i, j, m: (m[1 + i], jj(i, j, m), 0)),
            pl.BlockSpec((None, 1, d), lambda i, j, m: (m[1 + i], 0, 0)),
        ],
        out_specs=pl.BlockSpec((row_block, d), lambda i, j, m: (i, 0)),
        scratch_shapes=[pltpu.VMEM((row_block, half), BF16), pltpu.VMEM((row_block, half), BF16)],
    )
    return pl.pallas_call(
        _expert_kernel,
        grid_spec=grid_spec,
        out_shape=jax.ShapeDtypeStruct((n_rows, d), F32),
        compiler_params=_params("arbitrary", "arbitrary"),
        name="expert_swiglu",
    )(meta, xs, w_gu, w_gu, b_gu, b_gu, w_dn, b_dn)


def _combine_kernel(dest_ref, gate_ref, x1_ref, fn_ref, ys_ref, o_ref, buf, sem):
    tt = dest_ref.shape[1]

    def row_copy(t, kk):
        return pltpu.make_async_copy(ys_ref.at[pl.ds(dest_ref[kk, t], 1)], buf.at[kk, pl.ds(t, 1)], sem)

    def start(t, carry):
        for kk in range(TOP_K):
            row_copy(t, kk).start()
        return carry

    def wait(t, carry):
        for kk in range(TOP_K):
            row_copy(t, kk).wait()
        return carry

    lax.fori_loop(0, tt, start, 0, unroll=4)
    lax.fori_loop(0, tt, wait, 0, unroll=4)
    acc = x1_ref[...]
    gates = gate_ref[...]
    for kk in range(TOP_K):
        acc = acc + buf[kk] * gates[:, kk:kk + 1]
    o_ref[...] = _rmsnorm(acc, fn_ref[...])


def _combine(dest, gates_t, x1, final_gain, ys, row_off):
    n, d = x1.shape
    tt = _tile(n, 256)
    assert row_off % tt == 0
    off = row_off // tt
    return pl.pallas_call(
        _combine_kernel,
        grid=(n // tt,),
        in_specs=[
            pl.BlockSpec((TOP_K, tt), lambda i: (0, off + i), memory_space=pltpu.SMEM),
            pl.BlockSpec((tt, TOP_K), lambda i: (off + i, 0)),
            pl.BlockSpec((tt, d), lambda i: (i, 0)),
            pl.BlockSpec((1, d), lambda i: (0, 0)),
            pl.BlockSpec(memory_space=pl.ANY),
        ],
        out_specs=pl.BlockSpec((tt, d), lambda i: (i, 0)),
        out_shape=jax.ShapeDtypeStruct((n, d), F32),
        scratch_shapes=[pltpu.VMEM((TOP_K, tt, d), F32), pltpu.SemaphoreType.DMA(())],
        compiler_params=_params("arbitrary"),
        name="combine_rows",
    )(dest, gates_t, x1, final_gain, ys)


def _rope_tables(seq_len):
    rows = seq_len // GRID_W
    row = jnp.repeat(jnp.arange(rows, dtype=F32), GRID_W)
    col = jnp.tile(jnp.arange(GRID_W, dtype=F32), rows)
    freqs = ROPE_THETA ** (-jnp.arange(ROPE_HALF, dtype=F32) / ROPE_HALF)
    ang_r = row[:, None] * freqs[None, :]
    ang_c = col[:, None] * freqs[None, :]
    cos = jnp.concatenate([jnp.cos(ang_r), jnp.cos(ang_r), jnp.cos(ang_c), jnp.cos(ang_c)], axis=1)
    sin = jnp.concatenate([-jnp.sin(ang_r), jnp.sin(ang_r), -jnp.sin(ang_c), jnp.sin(ang_c)], axis=1)
    return cos, sin


def kernel(x_prompt, x_sample, mix_norm, w_in, q_norm, k_norm, hg_lb_logits, hg_out_norm, w_up_attn,
           w_up_hgrn, w_out, ffn_norm, w_router, b_router, w_gate_up, b_gate_up, w_down, b_down, final_norm):
    assert mix_norm.shape[0] == 1, "single trunk layer"
    d = x_prompt.shape[-1]
    hg_w = HG_HEADS * HG_D
    n_exp = w_router.shape[-1]
    row_block = MOE_ROW_BLOCK
    streams = [(x.reshape(-1, d), x.shape[0], x.shape[1]) for x in (x_prompt, x_sample)]
    n_total = sum(x.shape[0] for x, _, _ in streams)

    lb = jnp.cumsum(jax.nn.softmax(hg_lb_logits.astype(F32), axis=1), axis=1)[:, 0]
    w_in_b = w_in[0].astype(BF16)
    w_ua_b = w_up_attn[0].astype(BF16)
    w_uh_b = w_up_hgrn[0].astype(BF16)
    w_out_b = w_out[0].astype(BF16)
    w_r_t = w_router[0].T
    mix_g = mix_norm[0].reshape(1, d)
    score_bound = (1.02 * LOG2_E * HEAD_DIM ** 0.5 * jnp.max(jnp.abs(q_norm[0])) * jnp.max(jnp.abs(k_norm[0])))
    score_bound = score_bound.astype(F32).reshape(1)

    x1s, idxs, gate_ts = [], [], []
    hp = None
    row_off = 0
    for x, batch, seq in streams:
        cos, sin = _rope_tables(seq)
        qkv, zh, gates = _norm_proj(x, mix_g, w_in_b, q_norm[0].reshape(1, HEAD_DIM),
                                    k_norm[0].reshape(1, HEAD_DIM), cos, sin, seq)
        attn = _attention(qkv, score_bound, batch, seq)
        hg = _hgrn(zh, lb[0:1], lb[1:2], hg_out_norm[0].reshape(1, hg_w), batch, seq)
        merged = _merge(attn, hg, gates, w_ua_b, w_uh_b)
        x1, hp, idx, gate = _outproj_router(x, merged, w_out_b, ffn_norm[0].reshape(1, d), w_r_t,
                                            b_router[0].reshape(n_exp, 1), hp, row_off, n_total)
        x1s.append(x1)
        idxs.append(idx)
        gate_ts.append(gate.T)
        row_off += x.shape[0]

    idx = jnp.concatenate(idxs, axis=1)
    gate_t = jnp.concatenate(gate_ts, axis=0)
    dest, counts = _route(idx, row_block)
    cnt = counts[:, 0].astype(jnp.int32)
    padded = (cnt + row_block - 1) // row_block * row_block
    pad_end = jnp.cumsum(padded)
    n_rows = n_total * TOP_K + n_exp * row_block
    n_blocks = n_rows // row_block
    blk_start = jnp.arange(n_blocks, dtype=jnp.int32) * row_block
    blk_e = jnp.minimum(jnp.sum(pad_end[None, :] <= blk_start[:, None], axis=1), n_exp - 1).astype(jnp.int32)
    meta = jnp.concatenate([(pad_end[-1:] // row_block).astype(jnp.int32), blk_e])
    seg = jnp.stack([pad_end - padded, pad_end]).astype(jnp.int32)

    xs = _dispatch(seg, dest, hp, n_rows, row_block)
    ys = _experts(meta, xs, w_gate_up[0].astype(BF16), b_gate_up[0].reshape(n_exp, 1, -1),
                  w_down[0].astype(BF16), b_down[0].reshape(n_exp, 1, d), row_block)

    outs = []
    row_off = 0
    for (x, batch, seq), x1 in zip(streams, x1s):
        out = _combine(dest, gate_t, x1, final_norm.reshape(1, d), ys, row_off)
        outs.append(out.reshape(batch, seq, d))
        row_off += x.shape[0]
    return tuple(outs)
```

```python
import functools

import jax
import jax.numpy as jnp
from jax import lax
from jax.experimental import pallas as pl
from jax.experimental.pallas import tpu as pltpu

GRID_W = 64
HEAD_DIM = 128
N_Q_HEADS = 16
N_KV_HEADS = 4
ROPE_THETA = 10000.0
ROPE_HALF = HEAD_DIM // 4
HG_HEADS = 8
HG_D = 128
HG_CHUNK = 64
N_EXPERTS = 32
TOP_K = 4
SWIGLU_LIMIT = 7.0
SWIGLU_ALPHA = 1.702
NORM_EPS = 1e-5

HG_SAFE_LOG_DECAY = -60.0

LOG2_E = 1.4426950408889634
ATTN_FIXED_SHIFT_LIMIT = 60.0

V7X_VMEM_BYTES = 64 * 1024 * 1024
VMEM_LIMIT_BYTES = V7X_VMEM_BYTES - 8 * 1024 * 1024
LANES = 128
MOE_ROW_BLOCK = 512

BF16 = jnp.bfloat16
F32 = jnp.float32


def _params(*sem):
    return pltpu.CompilerParams(dimension_semantics=sem, vmem_limit_bytes=VMEM_LIMIT_BYTES)


def _tile(n, pref):
    t = min(n, pref)
    while n % t:
        t //= 2
    return t


def _sigmoid(x):
    return 1.0 / (1.0 + jnp.exp(-x))


def _rmsnorm(x, g):
    return x * lax.rsqrt(jnp.mean(x * x, axis=-1, keepdims=True) + NORM_EPS) * g


def _dot(a, b):
    return jnp.dot(a, b, preferred_element_type=F32)


def _dot_nt(a, b):
    return lax.dot_general(a, b, (((1,), (1,)), ((), ())), preferred_element_type=F32)


def _split3(x):
    hi = x.astype(BF16)
    r = x - hi.astype(F32)
    mid = r.astype(BF16)
    lo = (r - mid.astype(F32)).astype(BF16)
    return hi, mid, lo


def _dot_exact_lhs(m_bf16, x):
    hi, mid, lo = _split3(x)
    return _dot(m_bf16, hi) + _dot(m_bf16, mid) + _dot(m_bf16, lo)


def _rope_head(zh, gain, cos, sin, first, scale):
    y = _rmsnorm(zh, gain)
    partner = jnp.where(first, pltpu.roll(y, HEAD_DIM - ROPE_HALF, 1), pltpu.roll(y, ROPE_HALF, 1))
    return (y * cos + partner * sin) * scale


def _proj_kernel(x_ref, g_ref, w_ref, qn_ref, kn_ref, cos_ref, sin_ref,
                 qkv_ref, zh_ref, gate_ref, h_scr, *, tile_kinds):
    j = pl.program_id(1)

    @pl.when(j == 0)
    def _():
        h_scr[...] = _rmsnorm(x_ref[...], g_ref[...]).astype(BF16)

    z = _dot(h_scr[...], w_ref[...])
    tm = z.shape[0]
    lane = lax.broadcasted_iota(jnp.int32, (tm, HEAD_DIM), 1)
    first = (lane % (2 * ROPE_HALF)) < ROPE_HALF

    def qkv_tile(kinds):
        outs = []
        for h, kind in enumerate(kinds):
            zh = z[:, h * HEAD_DIM:(h + 1) * HEAD_DIM]
            if kind == "q":
                zh = _rope_head(zh, qn_ref[...], cos_ref[...], sin_ref[...], first, LOG2_E * HEAD_DIM ** -0.5)
            elif kind == "k":
                zh = _rope_head(zh, kn_ref[...], cos_ref[...], sin_ref[...], first, 1.0)
            outs.append(zh)
        return jnp.concatenate(outs, axis=1)

    for lo, hi, kind in tile_kinds:
        @pl.when((j >= lo) & (j < hi))
        def _(kind=kind):
            if kind == "zh":
                zh_ref[...] = z
            elif kind == "gate":
                gate_ref[...] = z.astype(gate_ref.dtype)
            else:
                qkv_ref[...] = qkv_tile(kind).astype(qkv_ref.dtype)


def _norm_proj(x, gain, w, q_gain, k_gain, cos, sin, seq):
    n, d = x.shape
    attn_w = N_Q_HEADS * HEAD_DIM
    kv_w = N_KV_HEADS * HEAD_DIM
    hg5 = 5 * HG_HEADS * HG_D
    qkv_w = attn_w + 2 * kv_w
    assert w.shape[1] == qkv_w + hg5 + 2 * d
    tm = _tile(seq, 1024)
    tn = 1024
    while attn_w % tn or (2 * kv_w) % tn or hg5 % tn or (2 * d) % tn:
        tn //= 2
    assert tn % HEAD_DIM == 0
    heads = ["q"] * N_Q_HEADS + ["k"] * N_KV_HEADS + ["v"] * N_KV_HEADS
    hpt = tn // HEAD_DIM
    n_qkv, n_zh, n_gate = qkv_w // tn, hg5 // tn, 2 * d // tn
    tile_kinds = []
    for t in range(n_qkv):
        kind = tuple(heads[t * hpt:(t + 1) * hpt])
        if tile_kinds and tile_kinds[-1][2] == kind:
            tile_kinds[-1] = (tile_kinds[-1][0], t + 1, kind)
        else:
            tile_kinds.append((t, t + 1, kind))
    tile_kinds += [(n_qkv, n_qkv + n_zh, "zh"), (n_qkv + n_zh, n_qkv + n_zh + n_gate, "gate")]
    n_pos = seq // tm
    return pl.pallas_call(
        functools.partial(_proj_kernel, tile_kinds=tile_kinds),
        grid=(n // tm, n_qkv + n_zh + n_gate),
        in_specs=[
            pl.BlockSpec((tm, d), lambda i, j: (i, 0)),
            pl.BlockSpec((1, d), lambda i, j: (0, 0)),
            pl.BlockSpec((d, tn), lambda i, j: (0, j)),
            pl.BlockSpec((1, HEAD_DIM), lambda i, j: (0, 0)),
            pl.BlockSpec((1, HEAD_DIM), lambda i, j: (0, 0)),
            pl.BlockSpec((tm, HEAD_DIM), lambda i, j: (i % n_pos, 0)),
            pl.BlockSpec((tm, HEAD_DIM), lambda i, j: (i % n_pos, 0)),
        ],
        out_specs=[
            pl.BlockSpec((tm, tn), lambda i, j: (i, jnp.minimum(j, n_qkv - 1))),
            pl.BlockSpec((tm, tn), lambda i, j: (i, jnp.clip(j - n_qkv, 0, n_zh - 1))),
            pl.BlockSpec((tm, tn), lambda i, j: (i, jnp.clip(j - n_qkv - n_zh, 0, n_gate - 1))),
        ],
        out_shape=[
            jax.ShapeDtypeStruct((n, qkv_w), BF16),
            jax.ShapeDtypeStruct((n, hg5), F32),
            jax.ShapeDtypeStruct((n, 2 * d), BF16),
        ],
        scratch_shapes=[pltpu.VMEM((tm, d), BF16)],
        compiler_params=_params("parallel", "arbitrary"),
        name="norm_proj",
    )(x, gain, w, q_gain, k_gain, cos, sin)


def _attn_kernel(bound_ref, q_ref, k_ref, v_ref, o_ref, acc_scr, m_scr, *, tk, group):
    tq = q_ref.shape[0]
    seq = k_ref.shape[0]
    q = jnp.concatenate([q_ref[:, g * HEAD_DIM:(g + 1) * HEAD_DIM] for g in range(group)], axis=0)
    ones = jnp.ones((tk, HEAD_DIM), BF16)
    bound = bound_ref[0]
    acc_scr[...] = jnp.zeros(acc_scr.shape, F32)

    def chunk(c):
        rows = pl.ds(pl.multiple_of(c * tk, tk), tk)
        s = _dot_nt(q, k_ref[rows, :])
        return s, jnp.concatenate([v_ref[rows, :], ones], axis=1)

    @pl.when(bound <= ATTN_FIXED_SHIFT_LIMIT)
    def _():
        def body(c, carry):
            s, v1 = chunk(c)
            acc_scr[...] += _dot(jnp.exp2(s - bound).astype(BF16), v1)
            return carry
        lax.fori_loop(0, seq // tk, body, 0, unroll=True)

    @pl.when(bound > ATTN_FIXED_SHIFT_LIMIT)
    def _():
        m_scr[...] = jnp.full(m_scr.shape, -1e30, F32)

        def body(c, carry):
            s, v1 = chunk(c)
            m_prev = m_scr[...]
            m_new = jnp.maximum(m_prev, jnp.max(s, axis=1, keepdims=True))
            alpha = jnp.exp2(m_prev - m_new)
            p = jnp.exp2(s - jnp.tile(m_new, (1, tk // LANES)))
            acc_scr[...] = jnp.tile(alpha, (1, 2)) * acc_scr[...] + _dot(p.astype(BF16), v1)
            m_scr[...] = m_new
            return carry
        lax.fori_loop(0, seq // tk, body, 0)

    acc = acc_scr[...]
    o = acc[:, :HEAD_DIM] / acc[:, HEAD_DIM:]
    for g in range(group):
        o_ref[:, g * HEAD_DIM:(g + 1) * HEAD_DIM] = o[g * tq:(g + 1) * tq].astype(o_ref.dtype)


def _attention(qkv, score_bound, batch, seq):
    group = N_Q_HEADS // N_KV_HEADS
    gw = group * HEAD_DIM
    tq = _tile(seq, 256)
    tk = _tile(seq, 512)
    nq = seq // tq
    return pl.pallas_call(
        functools.partial(_attn_kernel, tk=tk, group=group),
        grid=(batch, N_KV_HEADS, nq),
        in_specs=[
            pl.BlockSpec(memory_space=pltpu.SMEM),
            pl.BlockSpec((tq, gw), lambda b, n, i: (b * nq + i, n)),
            pl.BlockSpec((seq, HEAD_DIM), lambda b, n, i: (b, N_Q_HEADS + n)),
            pl.BlockSpec((seq, HEAD_DIM), lambda b, n, i: (b, N_Q_HEADS + N_KV_HEADS + n)),
        ],
        out_specs=pl.BlockSpec((tq, gw), lambda b, n, i: (b * nq + i, n)),
        out_shape=jax.ShapeDtypeStruct((batch * seq, N_Q_HEADS * HEAD_DIM), BF16),
        scratch_shapes=[
            pltpu.VMEM((group * tq, 2 * HEAD_DIM), F32),
            pltpu.VMEM((group * tq, LANES), F32),
        ],
        compiler_params=_params("parallel", "parallel", "arbitrary"),
        name="gqa_attention",
    )(score_bound, qkv, qkv, qkv)


def _hgrn_span(i, q_scr, b_scr, k_scr, zi_ref, vt_scr, o_scr, s_scr, tri, *, span, reverse, pairwise):
    C = HG_CHUNK
    nc = span // C
    r0 = pl.multiple_of(i * span, span)
    rows = pl.ds(r0, span)
    q = q_scr[rows, :]
    b = b_scr[rows, :]
    k = k_scr[rows, :]
    v = zi_ref[rows, :]
    vb = v.astype(BF16)
    end_row = 0 if reverse else C - 1
    ends = [b[c * C + end_row:c * C + end_row + 1, :] for c in range(nc)]
    b_end = jnp.concatenate([jnp.broadcast_to(e, (C, HG_D)) for e in ends], axis=0)
    qtb = (q * jnp.exp(b)).astype(BF16)
    kp = k * jnp.exp(b_end - b)

    if pairwise:
        rid = lax.broadcasted_iota(jnp.int32, (span, HG_D), 0)

        def pair(s, acc):
            cs = (s // C) * C
            if reverse:
                m = (rid <= s) & (rid >= cs)
            else:
                m = (rid >= s) & (rid < cs + C)
            w = jnp.where(m, jnp.exp(jnp.minimum(b - b_scr[pl.ds(r0 + s, 1), :], 0.0)), 0.0)
            r = jnp.sum(q * w * k_scr[pl.ds(r0 + s, 1), :], axis=1, keepdims=True)
            return acc + r * zi_ref[pl.ds(r0 + s, 1), :]

        o_intra = lax.fori_loop(0, span, pair, jnp.zeros((span, HG_D), F32))
    else:
        ktb = (k * jnp.exp(-b)).astype(BF16)
        a = jnp.where(tri, _dot_nt(qtb, ktb), 0.0)
        o_intra = _dot(a.astype(BF16), vb)

    chunk_of_row = lax.broadcasted_iota(jnp.int32, (span, HG_D), 0) // C
    kp_blocks = jnp.concatenate([jnp.where(chunk_of_row == c, kp, 0.0) for c in range(nc)], axis=1)
    upd = _dot(vt_scr[:, rows], kp_blocks.astype(BF16))

    st = s_scr[...]
    states = [None] * nc
    for c in (reversed(range(nc)) if reverse else range(nc)):
        states[c] = st.astype(BF16)
        st = st * jnp.exp(ends[c]) + upd[:, c * HG_D:(c + 1) * HG_D]
    s_scr[...] = st
    o_inter = _dot_nt(qtb, jnp.concatenate(states, axis=0))
    o_scr[rows, :] = o_intra + jnp.concatenate(
        [o_inter[c * C:(c + 1) * C, c * HG_D:(c + 1) * HG_D] for c in range(nc)], axis=0)


def _hgrn_kernel(zq_ref, zf_ref, zb_ref, zi_ref, zo_ref, lbf_ref, lbb_ref, on_ref, out_ref,
                 of_scr, ob_scr, sf_scr, sb_scr, q_scr, bf_scr, bb_scr, kf_scr, kb_scr, vt_scr, *, span):
    seq = zq_ref.shape[0]
    n_span = seq // span
    C = HG_CHUNK
    nc = span // C
    sf_scr[...] = jnp.zeros(sf_scr.shape, F32)
    sb_scr[...] = jnp.zeros(sb_scr.shape, F32)
    r = lax.broadcasted_iota(jnp.int32, (span, span), 0)
    c = lax.broadcasted_iota(jnp.int32, (span, span), 1)
    same = (r // C) == (c // C)
    tri_f = same & (c <= r)
    tri_b = same & (c >= r)

    def prepare(i, min_end):
        rows = pl.ds(pl.multiple_of(i * span, span), span)
        qh = zq_ref[rows, :]
        q_scr[rows, :] = qh * _sigmoid(qh)
        vt_scr[:, rows] = zi_ref[rows, :].T.astype(BF16)
        for z_ref, lb_ref, tri, b_scr, k_scr, end_row in (
                (zf_ref, lbf_ref, tri_f, bf_scr, kf_scr, C - 1), (zb_ref, lbb_ref, tri_b, bb_scr, kb_scr, 0)):
            lb = lb_ref[...]
            f = lb + (1.0 - lb) * _sigmoid(z_ref[rows, :])
            b = _dot_exact_lhs(tri.astype(BF16), jnp.log(f))
            b_scr[rows, :] = b
            k_scr[rows, :] = 1.0 - f
            for cc in range(nc):
                min_end = jnp.minimum(min_end, b[cc * C + end_row:cc * C + end_row + 1, :])
        return min_end

    min_end = lax.fori_loop(0, n_span, prepare, jnp.zeros((1, HG_D), F32))
    safe = jnp.min(min_end) > HG_SAFE_LOG_DECAY

    def scan(pairwise):
        def body(i, carry):
            _hgrn_span(i, q_scr, bf_scr, kf_scr, zi_ref, vt_scr, of_scr, sf_scr, tri_f,
                       span=span, reverse=False, pairwise=pairwise)
            _hgrn_span(n_span - 1 - i, q_scr, bb_scr, kb_scr, zi_ref, vt_scr, ob_scr, sb_scr, tri_b,
                       span=span, reverse=True, pairwise=pairwise)
            return carry
        lax.fori_loop(0, n_span, body, 0)

    @pl.when(safe)
    def _():
        scan(False)

    @pl.when(jnp.logical_not(safe))
    def _():
        scan(True)

    def finish(i, carry):
        rows = pl.ds(pl.multiple_of(i * span, span), span)
        o = _rmsnorm(of_scr[rows, :] + ob_scr[rows, :], on_ref[...])
        og = zo_ref[rows, :]
        out_ref[rows, :] = (o * (og * _sigmoid(og))).astype(out_ref.dtype)
        return carry

    lax.fori_loop(0, n_span, finish, 0)


def _hgrn(zh, lb_f, lb_b, out_norm, batch, seq):
    span = _tile(seq, 256)
    assert span % HG_CHUNK == 0
    zspec = lambda grp: pl.BlockSpec((seq, HG_D), lambda b, h: (b, grp * HG_HEADS + h))
    hspec = pl.BlockSpec((1, HG_D), lambda b, h: (0, h))
    seq_buf = pltpu.VMEM((seq, HG_D), F32)
    return pl.pallas_call(
        functools.partial(_hgrn_kernel, span=span),
        grid=(batch, HG_HEADS),
        in_specs=[zspec(0), zspec(1), zspec(2), zspec(3), zspec(4), hspec, hspec, hspec],
        out_specs=pl.BlockSpec((seq, HG_D), lambda b, h: (b, h)),
        out_shape=jax.ShapeDtypeStruct((batch * seq, HG_HEADS * HG_D), BF16),
        scratch_shapes=[
            seq_buf, seq_buf,
            pltpu.VMEM((HG_D, HG_D), F32), pltpu.VMEM((HG_D, HG_D), F32),
            seq_buf, seq_buf, seq_buf, seq_buf, seq_buf,
            pltpu.VMEM((HG_D, seq), BF16),
        ],
        compiler_params=_params("parallel", "parallel"),
        name="hgrn2",
    )(zh, zh, zh, zh, zh, lb_f, lb_b, out_norm)


def _merge_kernel(a_ref, h_ref, ga_ref, gb_ref, wa_ref, wh_ref, o_ref):
    ya = _dot(a_ref[...], wa_ref[...])
    yb = _dot(h_ref[...], wh_ref[...])
    merged = _sigmoid(ga_ref[...].astype(F32)) * ya + _sigmoid(gb_ref[...].astype(F32)) * yb
    o_ref[...] = merged.astype(o_ref.dtype)


def _merge(attn, hg, gates, w_up_attn, w_up_hgrn):
    n, wa = attn.shape
    wh = hg.shape[1]
    d = w_up_attn.shape[1]
    tm = _tile(n, 1024)
    tn = _tile(d, 512)
    nj = d // tn
    return pl.pallas_call(
        _merge_kernel,
        grid=(n // tm, nj),
        in_specs=[
            pl.BlockSpec((tm, wa), lambda i, j: (i, 0)),
            pl.BlockSpec((tm, wh), lambda i, j: (i, 0)),
            pl.BlockSpec((tm, tn), lambda i, j: (i, j)),
            pl.BlockSpec((tm, tn), lambda i, j: (i, nj + j)),
            pl.BlockSpec((wa, tn), lambda i, j: (0, j)),
            pl.BlockSpec((wh, tn), lambda i, j: (0, j)),
        ],
        out_specs=pl.BlockSpec((tm, tn), lambda i, j: (i, j)),
        out_shape=jax.ShapeDtypeStruct((n, d), BF16),
        compiler_params=_params("parallel", "arbitrary"),
        name="gated_merge",
    )(attn, hg, gates, gates, w_up_attn, w_up_hgrn)


def _pack_bf16_pairs(h):
    half = h.shape[1] // 2
    lo = pltpu.bitcast(h[:, :half].astype(BF16).astype(F32), jnp.uint32)
    hi = pltpu.bitcast(h[:, half:].astype(BF16).astype(F32), jnp.uint32)
    return (hi & jnp.uint32(0xFFFF0000)) | (lo >> 16)


def _unpack_bf16_pairs(u):
    lo = pltpu.bitcast(u << 16, F32).astype(BF16)
    hi = pltpu.bitcast(u & jnp.uint32(0xFFFF0000), F32).astype(BF16)
    return lo, hi


def _outproj_router_kernel(x_ref, m_ref, w_ref, g_ref, wr_ref, br_ref, hp_in_ref,
                           x1_ref, hp_ref, idx_ref, gate_ref):
    del hp_in_ref
    x1 = x_ref[...] + _dot(m_ref[...], w_ref[...])
    x1_ref[...] = x1
    h = _rmsnorm(x1, g_ref[...])
    hp_ref[...] = _pack_bf16_pairs(h)
    h_hi, h_mid, _ = _split3(h)
    w_hi, w_mid, _ = _split3(wr_ref[...])
    lg = (_dot_nt(w_hi, h_hi) + _dot_nt(w_hi, h_mid) + _dot_nt(w_mid, h_hi)) + br_ref[...]
    n_exp, tm = lg.shape
    eid = lax.broadcasted_iota(jnp.int32, (n_exp, tm), 0)
    vals = []
    for kk in range(TOP_K):
        m = jnp.max(lg, axis=0, keepdims=True)
        sel = jnp.min(jnp.where(lg == m, eid, n_exp), axis=0, keepdims=True)
        idx_ref[kk:kk + 1, :] = sel
        vals.append(m)
        lg = jnp.where(eid == sel, -jnp.inf, lg)
    ex = [jnp.exp(vv - vals[0]) for vv in vals]
    den = ex[0]
    for e in ex[1:]:
        den = den + e
    for kk in range(TOP_K):
        gate_ref[kk:kk + 1, :] = ex[kk] / den


def _outproj_router(x, merged, w_out, ffn_gain, w_router_t, b_router, hp_prev, row_off, n_total):
    n, d = x.shape
    tm = _tile(n, 512)
    assert row_off % tm == 0
    n_exp = w_router_t.shape[0]
    if hp_prev is None:
        hp_prev = jnp.zeros((8, LANES), jnp.uint32)
        aliases = {}
    else:
        aliases = {6: 1}
    return pl.pallas_call(
        _outproj_router_kernel,
        grid=(n // tm,),
        in_specs=[
            pl.BlockSpec((tm, d), lambda i: (i, 0)),
            pl.BlockSpec((tm, d), lambda i: (i, 0)),
            pl.BlockSpec((d, d), lambda i: (0, 0)),
            pl.BlockSpec((1, d), lambda i: (0, 0)),
            pl.BlockSpec((n_exp, d), lambda i: (0, 0)),
            pl.BlockSpec((n_exp, 1), lambda i: (0, 0)),
            pl.BlockSpec(memory_space=pl.ANY),
        ],
        out_specs=[
            pl.BlockSpec((tm, d), lambda i: (i, 0)),
            pl.BlockSpec((tm, d // 2), lambda i: (row_off // tm + i, 0)),
            pl.BlockSpec((TOP_K, tm), lambda i: (0, i)),
            pl.BlockSpec((TOP_K, tm), lambda i: (0, i)),
        ],
        out_shape=[
            jax.ShapeDtypeStruct((n, d), F32),
            jax.ShapeDtypeStruct((n_total, d // 2), jnp.uint32),
            jax.ShapeDtypeStruct((TOP_K, n), jnp.int32),
            jax.ShapeDtypeStruct((TOP_K, n), F32),
        ],
        input_output_aliases=aliases,
        compiler_params=_params("parallel"),
        name="outproj_router",
    )(x, merged, w_out, ffn_gain, w_router_t, b_router, hp_prev)


def _route_kernel(idx_ref, dest_ref, cnt_ref, cnt_scr, base_scr, *, row_block):
    phase = pl.program_id(0)
    i = pl.program_id(1)
    n_exp = cnt_scr.shape[0]
    tt = idx_ref.shape[1]
    eid = lax.broadcasted_iota(jnp.int32, (n_exp, tt), 0)
    onehot = [(eid == idx_ref[kk:kk + 1, :]) for kk in range(TOP_K)]

    @pl.when((phase == 0) & (i == 0))
    def _():
        cnt_scr[...] = jnp.zeros(cnt_scr.shape, F32)

    @pl.when(phase == 0)
    def _():
        tot = onehot[0].astype(F32)
        for oh in onehot[1:]:
            tot = tot + oh.astype(F32)
        cnt_scr[...] = cnt_scr[...] + jnp.sum(tot, axis=1, keepdims=True)
        cnt_ref[...] = cnt_scr[...]

    @pl.when((phase == 1) & (i == 0))
    def _():
        cnt = cnt_scr[...].astype(jnp.int32)
        padded = ((cnt + (row_block - 1)) // row_block * row_block).astype(F32)
        er = lax.broadcasted_iota(jnp.int32, (n_exp, n_exp), 0)
        ec = lax.broadcasted_iota(jnp.int32, (n_exp, n_exp), 1)
        base_scr[...] = _dot_exact_lhs((ec < er).astype(BF16), padded)

    @pl.when(phase == 1)
    def _():
        tr = lax.broadcasted_iota(jnp.int32, (tt, tt), 0)
        tc = lax.broadcasted_iota(jnp.int32, (tt, tt), 1)
        before = (tr < tc).astype(BF16)
        run = base_scr[...][:, :1]
        for kk in range(TOP_K):
            oh = onehot[kk].astype(F32)
            rank = _dot(oh.astype(BF16), before) + run
            dest_ref[kk:kk + 1, :] = jnp.sum(oh * rank, axis=0, keepdims=True).astype(jnp.int32)
            run = run + jnp.sum(oh, axis=1, keepdims=True)
        base_scr[...] = jnp.broadcast_to(run, base_scr.shape)


def _route(idx, row_block):
    n = idx.shape[1]
    tt = _tile(n, 512)
    return pl.pallas_call(
        functools.partial(_route_kernel, row_block=row_block),
        grid=(2, n // tt),
        in_specs=[pl.BlockSpec((TOP_K, tt), lambda p, i: (0, i))],
        out_specs=[
            pl.BlockSpec((TOP_K, tt), lambda p, i: (0, i * p)),
            pl.BlockSpec((N_EXPERTS, LANES), lambda p, i: (0, 0)),
        ],
        out_shape=[
            jax.ShapeDtypeStruct((TOP_K, n), jnp.int32),
            jax.ShapeDtypeStruct((N_EXPERTS, LANES), F32),
        ],
        scratch_shapes=[pltpu.VMEM((N_EXPERTS, LANES), F32), pltpu.VMEM((N_EXPERTS, LANES), F32)],
        compiler_params=_params("arbitrary", "arbitrary"),
        name="route_offsets",
    )(idx)


def _dispatch_kernel(seg_ref, dest_ref, h_ref, xs_ref, zero_scr, sem, zsem, *, row_block):
    tt = dest_ref.shape[1]
    n_exp = seg_ref.shape[1]

    @pl.when(pl.program_id(0) == 0)
    def _():
        zero_scr[...] = jnp.zeros(zero_scr.shape, zero_scr.dtype)

        def zero_copy(e):
            start = pl.multiple_of(seg_ref[1, e] - row_block, row_block)
            return pltpu.make_async_copy(zero_scr, xs_ref.at[pl.ds(start, row_block)], zsem)

        for e in range(n_exp):
            @pl.when(seg_ref[1, e] > seg_ref[0, e])
            def _(e=e):
                zero_copy(e).start()
        for e in range(n_exp):
            @pl.when(seg_ref[1, e] > seg_ref[0, e])
            def _(e=e):
                zero_copy(e).wait()

    def row_copy(t, kk):
        return pltpu.make_async_copy(h_ref.at[pl.ds(t, 1)], xs_ref.at[pl.ds(dest_ref[kk, t], 1)], sem)

    def start(t, carry):
        for kk in range(TOP_K):
            row_copy(t, kk).start()
        return carry

    lax.fori_loop(0, tt, start, 0, unroll=8)
    pltpu.make_async_copy(xs_ref.at[pl.ds(0, TOP_K * tt)], xs_ref.at[pl.ds(0, TOP_K * tt)], sem).wait()


def _dispatch(seg, dest, hp, n_rows, row_block):
    n, w = hp.shape
    tt = _tile(n, 512)
    grid_spec = pltpu.PrefetchScalarGridSpec(
        num_scalar_prefetch=1,
        grid=(n // tt,),
        in_specs=[
            pl.BlockSpec((TOP_K, tt), lambda i, s: (0, i), memory_space=pltpu.SMEM),
            pl.BlockSpec((tt, w), lambda i, s: (i, 0)),
        ],
        out_specs=pl.BlockSpec(memory_space=pl.ANY),
        scratch_shapes=[pltpu.VMEM((row_block, w), hp.dtype), pltpu.SemaphoreType.DMA(()),
                        pltpu.SemaphoreType.DMA(())],
    )
    return pl.pallas_call(
        functools.partial(_dispatch_kernel, row_block=row_block),
        grid_spec=grid_spec,
        out_shape=jax.ShapeDtypeStruct((n_rows, w), hp.dtype),
        compiler_params=_params("arbitrary"),
        name="dispatch_rows",
    )(seg, dest, hp)


def _expert_kernel(meta_ref, xs_ref, wg_ref, wl_ref, bg_ref, bl_ref, wd_ref, bd_ref, y_ref, xlo_scr, xhi_scr):
    i = pl.program_id(0)
    j = pl.program_id(1)

    @pl.when(i < meta_ref[0])
    def _():
        @pl.when(j == 0)
        def _():
            lo, hi = _unpack_bf16_pairs(xs_ref[...])
            xlo_scr[...] = lo
            xhi_scr[...] = hi
            y_ref[...] = jnp.broadcast_to(bd_ref[...], y_ref.shape)

        half = xlo_scr.shape[1]
        xlo = xlo_scr[...]
        xhi = xhi_scr[...]
        glu = _dot(xlo, wg_ref[:half, :]) + _dot(xhi, wg_ref[half:, :]) + bg_ref[...]
        lin = _dot(xlo, wl_ref[:half, :]) + _dot(xhi, wl_ref[half:, :]) + bl_ref[...]
        glu = jnp.minimum(glu, SWIGLU_LIMIT)
        lin = jnp.clip(lin, -SWIGLU_LIMIT, SWIGLU_LIMIT)
        act = glu * _sigmoid(SWIGLU_ALPHA * glu) * (lin + 1.0)
        y_ref[...] = y_ref[...] + _dot(act.astype(BF16), wd_ref[...])


def _experts(meta, xs, w_gu, b_gu, w_dn, b_dn, row_block):
    n_rows, half = xs.shape
    d = 2 * half
    d_ff = w_dn.shape[1]
    tf = _tile(d_ff, 1024)
    nf = d_ff // tf
    n_blocks = n_rows // row_block

    def jj(i, j, m):
        return jnp.where(i < m[0], j, nf - 1)

    grid_spec = pltpu.PrefetchScalarGridSpec(
        num_scalar_prefetch=1,
        grid=(n_blocks, nf),
        in_specs=[
            pl.BlockSpec((row_block, half), lambda i, j, m: (i, 0)),
            pl.BlockSpec((None, d, tf), lambda i, j, m: (m[1 + i], 0, jj(i, j, m))),
            pl.BlockSpec((None, d, tf), lambda i, j, m: (m[1 + i], 0, nf + jj(i, j, m))),
            pl.BlockSpec((None, 1, tf), lambda i, j, m: (m[1 + i], 0, jj(i, j, m))),
            pl.BlockSpec((None, 1, tf), lambda i, j, m: (m[1 + i], 0, nf + jj(i, j, m))),
            pl.BlockSpec((None, tf, d), lambda i, j, m: (m[1 + i], jj(i, j, m), 0)),
            pl.BlockSpec((None, 1, d), lambda i, j, m: (m[1 + i], 0, 0)),
        ],
        out_specs=pl.BlockSpec((row_block, d), lambda i, j, m: (i, 0)),
        scratch_shapes=[pltpu.VMEM((row_block, half), BF16), pltpu.VMEM((row_block, half), BF16)],
    )
    return pl.pallas_call(
        _expert_kernel,
        grid_spec=grid_spec,
        out_shape=jax.ShapeDtypeStruct((n_rows, d), F32),
        compiler_params=_params("arbitrary", "arbitrary"),
        name="expert_swiglu",
    )(meta, xs, w_gu, w_gu, b_gu, b_gu, w_dn, b_dn)


def _combine_kernel(dest_ref, dest_next_ref, gate_ref, x1_ref, fn_ref, ys_ref, o_ref, buf, sem):
    tt = dest_ref.shape[1]
    i = pl.program_id(0)
    slot = i % 2

    def gather(d_ref, s):
        def start(t, carry):
            for kk in range(TOP_K):
                pltpu.make_async_copy(ys_ref.at[pl.ds(d_ref[kk, t], 1)],
                                      buf.at[s, pl.ds(kk * tt + t, 1)], sem.at[s]).start()
            return carry
        lax.fori_loop(0, tt, start, 0, unroll=8)

    @pl.when(i == 0)
    def _():
        gather(dest_ref, 0)

    @pl.when(i + 1 < pl.num_programs(0))
    def _():
        gather(dest_next_ref, 1 - slot)

    pltpu.make_async_copy(ys_ref.at[pl.ds(0, TOP_K * tt)], buf.at[slot], sem.at[slot]).wait()
    acc = x1_ref[...]
    gates = gate_ref[...]
    for kk in range(TOP_K):
        acc = acc + buf[slot, pl.ds(kk * tt, tt), :] * gates[:, kk:kk + 1]
    o_ref[...] = _rmsnorm(acc, fn_ref[...])


def _combine(dest, gates_t, x1, final_gain, ys, row_off):
    n, d = x1.shape
    tt = _tile(n, 256)
    assert row_off % tt == 0
    off = row_off // tt
    nt = n // tt
    return pl.pallas_call(
        _combine_kernel,
        grid=(nt,),
        in_specs=[
            pl.BlockSpec((TOP_K, tt), lambda i: (0, off + i), memory_space=pltpu.SMEM),
            pl.BlockSpec((TOP_K, tt), lambda i: (0, off + jnp.minimum(i + 1, nt - 1)), memory_space=pltpu.SMEM),
            pl.BlockSpec((tt, TOP_K), lambda i: (off + i, 0)),
            pl.BlockSpec((tt, d), lambda i: (i, 0)),
            pl.BlockSpec((1, d), lambda i: (0, 0)),
            pl.BlockSpec(memory_space=pl.ANY),
        ],
        out_specs=pl.BlockSpec((tt, d), lambda i: (i, 0)),
        out_shape=jax.ShapeDtypeStruct((n, d), F32),
        scratch_shapes=[pltpu.VMEM((2, TOP_K * tt, d), F32), pltpu.SemaphoreType.DMA((2,))],
        compiler_params=_params("arbitrary"),
        name="combine_rows",
    )(dest, dest, gates_t, x1, final_gain, ys)


def _rope_tables(seq_len):
    rows = seq_len // GRID_W
    row = jnp.repeat(jnp.arange(rows, dtype=F32), GRID_W)
    col = jnp.tile(jnp.arange(GRID_W, dtype=F32), rows)
    freqs = ROPE_THETA ** (-jnp.arange(ROPE_HALF, dtype=F32) / ROPE_HALF)
    ang_r = row[:, None] * freqs[None, :]
    ang_c = col[:, None] * freqs[None, :]
    cos = jnp.concatenate([jnp.cos(ang_r), jnp.cos(ang_r), jnp.cos(ang_c), jnp.cos(ang_c)], axis=1)
    sin = jnp.concatenate([-jnp.sin(ang_r), jnp.sin(ang_r), -jnp.sin(ang_c), jnp.sin(ang_c)], axis=1)
    return cos, sin


def kernel(x_prompt, x_sample, mix_norm, w_in, q_norm, k_norm, hg_lb_logits, hg_out_norm, w_up_attn,
           w_up_hgrn, w_out, ffn_norm, w_router, b_router, w_gate_up, b_gate_up, w_down, b_down, final_norm):
    assert mix_norm.shape[0] == 1, "single trunk layer"
    d = x_prompt.shape[-1]
    hg_w = HG_HEADS * HG_D
    n_exp = w_router.shape[-1]
    row_block = MOE_ROW_BLOCK
    streams = [(x.reshape(-1, d), x.shape[0], x.shape[1]) for x in (x_prompt, x_sample)]
    n_total = sum(x.shape[0] for x, _, _ in streams)

    lb = jnp.cumsum(jax.nn.softmax(hg_lb_logits.astype(F32), axis=1), axis=1)[:, 0]
    w_in_b = w_in[0].astype(BF16)
    w_ua_b = w_up_attn[0].astype(BF16)
    w_uh_b = w_up_hgrn[0].astype(BF16)
    w_out_b = w_out[0].astype(BF16)
    w_r_t = w_router[0].T
    mix_g = mix_norm[0].reshape(1, d)
    score_bound = (1.02 * LOG2_E * HEAD_DIM ** 0.5 * jnp.max(jnp.abs(q_norm[0])) * jnp.max(jnp.abs(k_norm[0])))
    score_bound = score_bound.astype(F32).reshape(1)

    x1s, idxs, gate_ts = [], [], []
    hp = None
    row_off = 0
    for x, batch, seq in streams:
        cos, sin = _rope_tables(seq)
        qkv, zh, gates = _norm_proj(x, mix_g, w_in_b, q_norm[0].reshape(1, HEAD_DIM),
                                    k_norm[0].reshape(1, HEAD_DIM), cos, sin, seq)
        attn = _attention(qkv, score_bound, batch, seq)
        hg = _hgrn(zh, lb[0:1], lb[1:2], hg_out_norm[0].reshape(1, hg_w), batch, seq)
        merged = _merge(attn, hg, gates, w_ua_b, w_uh_b)
        x1, hp, idx, gate = _outproj_router(x, merged, w_out_b, ffn_norm[0].reshape(1, d), w_r_t,
                                            b_router[0].reshape(n_exp, 1), hp, row_off, n_total)
        x1s.append(x1)
        idxs.append(idx)
        gate_ts.append(gate.T)
        row_off += x.shape[0]

    idx = jnp.concatenate(idxs, axis=1)
    gate_t = jnp.concatenate(gate_ts, axis=0)
    dest, counts = _route(idx, row_block)
    cnt = counts[:, 0].astype(jnp.int32)
    padded = (cnt + row_block - 1) // row_block * row_block
    pad_end = jnp.cumsum(padded)
    n_rows = n_total * TOP_K + n_exp * row_block
    n_blocks = n_rows // row_block
    blk_start = jnp.arange(n_blocks, dtype=jnp.int32) * row_block
    blk_e = jnp.minimum(jnp.sum(pad_end[None, :] <= blk_start[:, None], axis=1), n_exp - 1).astype(jnp.int32)
    meta = jnp.concatenate([(pad_end[-1:] // row_block).astype(jnp.int32), blk_e])
    seg = jnp.stack([pad_end - padded, pad_end]).astype(jnp.int32)

    xs = _dispatch(seg, dest, hp, n_rows, row_block)
    ys = _experts(meta, xs, w_gate_up[0].astype(BF16), b_gate_up[0].reshape(n_exp, 1, -1),
                  w_down[0].astype(BF16), b_down[0].reshape(n_exp, 1, d), row_block)

    outs = []
    row_off = 0
    for (x, batch, seq), x1 in zip(streams, x1s):
        out = _combine(dest, gate_t, x1, final_norm.reshape(1, d), ys, row_off)
        outs.append(out.reshape(batch, seq, d))
        row_off += x.shape[0]
    return tuple(outs)
```

```python
import functools

import jax
import jax.numpy as jnp
from jax import lax
from jax.experimental import pallas as pl
from jax.experimental.pallas import tpu as pltpu

GRID_W = 64
HEAD_DIM = 128
N_Q_HEADS = 16
N_KV_HEADS = 4
ROPE_THETA = 10000.0
ROPE_HALF = HEAD_DIM // 4
HG_HEADS = 8
HG_D = 128
HG_CHUNK = 64
N_EXPERTS = 32
TOP_K = 4
SWIGLU_LIMIT = 7.0
SWIGLU_ALPHA = 1.702
NORM_EPS = 1e-5

HG_SAFE_LOG_DECAY = -60.0

LOG2_E = 1.4426950408889634
ATTN_FIXED_SHIFT_LIMIT = 60.0

V7X_VMEM_BYTES = 64 * 1024 * 1024
VMEM_LIMIT_BYTES = V7X_VMEM_BYTES - 8 * 1024 * 1024
LANES = 128
MOE_ROW_BLOCK = 512

BF16 = jnp.bfloat16
F32 = jnp.float32


def _params(*sem):
    return pltpu.CompilerParams(dimension_semantics=sem, vmem_limit_bytes=VMEM_LIMIT_BYTES)


def _tile(n, pref):
    t = min(n, pref)
    while n % t:
        t //= 2
    return t


def _sigmoid(x):
    return 1.0 / (1.0 + jnp.exp(-x))


def _rmsnorm(x, g):
    return x * lax.rsqrt(jnp.mean(x * x, axis=-1, keepdims=True) + NORM_EPS) * g


def _dot(a, b):
    return jnp.dot(a, b, preferred_element_type=F32)


def _dot_nt(a, b):
    return lax.dot_general(a, b, (((1,), (1,)), ((), ())), preferred_element_type=F32)


def _split3(x):
    hi = x.astype(BF16)
    r = x - hi.astype(F32)
    mid = r.astype(BF16)
    lo = (r - mid.astype(F32)).astype(BF16)
    return hi, mid, lo


def _dot_exact_lhs(m_bf16, x):
    hi, mid, lo = _split3(x)
    return _dot(m_bf16, hi) + _dot(m_bf16, mid) + _dot(m_bf16, lo)


def _rope_head_pair(zp, gain, cos, sin, ones_blk, perm_blk, scale):
    ss = _dot((zp * zp).astype(BF16), ones_blk)
    y = zp * lax.rsqrt(ss * (1.0 / HEAD_DIM) + NORM_EPS) * gain
    y_hi = y.astype(BF16)
    y_lo = (y - y_hi.astype(F32)).astype(BF16)
    partner = _dot(y_hi, perm_blk) + _dot(y_lo, perm_blk)
    return (y * cos + partner * sin) * scale


def _proj_kernel(x_ref, g_ref, w_ref, qn_ref, kn_ref, cos_ref, sin_ref, ones_ref, perm_ref,
                 qkv_ref, zh_ref, gate_ref, h_scr, *, tile_kinds):
    j = pl.program_id(1)

    @pl.when(j == 0)
    def _():
        h_scr[...] = _rmsnorm(x_ref[...], g_ref[...]).astype(BF16)

    z = _dot(h_scr[...], w_ref[...])

    def qkv_tile(kinds):
        two = lambda r: jnp.concatenate([r[...], r[...]], axis=1)
        outs = []
        for h in range(0, len(kinds), 2):
            kind = kinds[h]
            assert kinds[h + 1] == kind
            zp = z[:, h * HEAD_DIM:(h + 2) * HEAD_DIM]
            if kind == "q":
                zp = _rope_head_pair(zp, two(qn_ref), two(cos_ref), two(sin_ref), ones_ref[...], perm_ref[...],
                                     LOG2_E * HEAD_DIM ** -0.5)
            elif kind == "k":
                zp = _rope_head_pair(zp, two(kn_ref), two(cos_ref), two(sin_ref), ones_ref[...], perm_ref[...], 1.0)
            outs.append(zp)
        return jnp.concatenate(outs, axis=1)

    for lo, hi, kind in tile_kinds:
        @pl.when((j >= lo) & (j < hi))
        def _(kind=kind):
            if kind == "zh":
                zh_ref[...] = z
            elif kind == "gate":
                gate_ref[...] = z.astype(gate_ref.dtype)
            else:
                qkv_ref[...] = qkv_tile(kind).astype(qkv_ref.dtype)


def _norm_proj(x, gain, w, q_gain, k_gain, cos, sin, seq):
    n, d = x.shape
    attn_w = N_Q_HEADS * HEAD_DIM
    kv_w = N_KV_HEADS * HEAD_DIM
    hg5 = 5 * HG_HEADS * HG_D
    qkv_w = attn_w + 2 * kv_w
    assert w.shape[1] == qkv_w + hg5 + 2 * d
    tm = _tile(seq, 1024)
    tn = 1024
    while attn_w % tn or (2 * kv_w) % tn or hg5 % tn or (2 * d) % tn:
        tn //= 2
    assert tn % HEAD_DIM == 0
    heads = ["q"] * N_Q_HEADS + ["k"] * N_KV_HEADS + ["v"] * N_KV_HEADS
    hpt = tn // HEAD_DIM
    n_qkv, n_zh, n_gate = qkv_w // tn, hg5 // tn, 2 * d // tn
    tile_kinds = []
    for t in range(n_qkv):
        kind = tuple(heads[t * hpt:(t + 1) * hpt])
        if tile_kinds and tile_kinds[-1][2] == kind:
            tile_kinds[-1] = (tile_kinds[-1][0], t + 1, kind)
        else:
            tile_kinds.append((t, t + 1, kind))
    tile_kinds += [(n_qkv, n_qkv + n_zh, "zh"), (n_qkv + n_zh, n_qkv + n_zh + n_gate, "gate")]
    n_pos = seq // tm
    r = jnp.arange(2 * HEAD_DIM)
    same_head = (r[:, None] // HEAD_DIM) == (r[None, :] // HEAD_DIM)
    ones_blk = same_head.astype(BF16)
    partner_of = jnp.where((r % (2 * ROPE_HALF)) < ROPE_HALF, r + ROPE_HALF, r - ROPE_HALF)
    perm_blk = (r[:, None] == partner_of[None, :]).astype(BF16)
    const_spec = pl.BlockSpec((2 * HEAD_DIM, 2 * HEAD_DIM), lambda i, j: (0, 0))
    return pl.pallas_call(
        functools.partial(_proj_kernel, tile_kinds=tile_kinds),
        grid=(n // tm, n_qkv + n_zh + n_gate),
        in_specs=[
            pl.BlockSpec((tm, d), lambda i, j: (i, 0)),
            pl.BlockSpec((1, d), lambda i, j: (0, 0)),
            pl.BlockSpec((d, tn), lambda i, j: (0, j)),
            pl.BlockSpec((1, HEAD_DIM), lambda i, j: (0, 0)),
            pl.BlockSpec((1, HEAD_DIM), lambda i, j: (0, 0)),
            pl.BlockSpec((tm, HEAD_DIM), lambda i, j: (i % n_pos, 0)),
            pl.BlockSpec((tm, HEAD_DIM), lambda i, j: (i % n_pos, 0)),
            const_spec,
            const_spec,
        ],
        out_specs=[
            pl.BlockSpec((tm, tn), lambda i, j: (i, jnp.minimum(j, n_qkv - 1))),
            pl.BlockSpec((tm, tn), lambda i, j: (i, jnp.clip(j - n_qkv, 0, n_zh - 1))),
            pl.BlockSpec((tm, tn), lambda i, j: (i, jnp.clip(j - n_qkv - n_zh, 0, n_gate - 1))),
        ],
        out_shape=[
            jax.ShapeDtypeStruct((n, qkv_w), BF16),
            jax.ShapeDtypeStruct((n, hg5), F32),
            jax.ShapeDtypeStruct((n, 2 * d), BF16),
        ],
        scratch_shapes=[pltpu.VMEM((tm, d), BF16)],
        compiler_params=_params("parallel", "arbitrary"),
        name="norm_proj",
    )(x, gain, w, q_gain, k_gain, cos, sin, ones_blk, perm_blk)


def _attn_kernel(bound_ref, q_ref, k_ref, v_ref, o_ref, acc_scr, m_scr, *, tk, group):
    tq = q_ref.shape[0]
    seq = k_ref.shape[0]
    q = jnp.concatenate([q_ref[:, g * HEAD_DIM:(g + 1) * HEAD_DIM] for g in range(group)], axis=0)
    ones = jnp.ones((tk, HEAD_DIM), BF16)
    bound = bound_ref[0]
    acc_scr[...] = jnp.zeros(acc_scr.shape, F32)

    def chunk(c):
        rows = pl.ds(pl.multiple_of(c * tk, tk), tk)
        s = _dot_nt(q, k_ref[rows, :])
        return s, jnp.concatenate([v_ref[rows, :], ones], axis=1)

    @pl.when(bound <= ATTN_FIXED_SHIFT_LIMIT)
    def _():
        def body(c, carry):
            s, v1 = chunk(c)
            acc_scr[...] += _dot(jnp.exp2(s - bound).astype(BF16), v1)
            return carry
        lax.fori_loop(0, seq // tk, body, 0, unroll=True)

    @pl.when(bound > ATTN_FIXED_SHIFT_LIMIT)
    def _():
        m_scr[...] = jnp.full(m_scr.shape, -1e30, F32)

        def body(c, carry):
            s, v1 = chunk(c)
            m_prev = m_scr[...]
            m_new = jnp.maximum(m_prev, jnp.max(s, axis=1, keepdims=True))
            alpha = jnp.exp2(m_prev - m_new)
            p = jnp.exp2(s - jnp.tile(m_new, (1, tk // LANES)))
            acc_scr[...] = jnp.tile(alpha, (1, 2)) * acc_scr[...] + _dot(p.astype(BF16), v1)
            m_scr[...] = m_new
            return carry
        lax.fori_loop(0, seq // tk, body, 0)

    acc = acc_scr[...]
    o = acc[:, :HEAD_DIM] / acc[:, HEAD_DIM:]
    for g in range(group):
        o_ref[:, g * HEAD_DIM:(g + 1) * HEAD_DIM] = o[g * tq:(g + 1) * tq].astype(o_ref.dtype)


def _attention(qkv, score_bound, batch, seq):
    group = N_Q_HEADS // N_KV_HEADS
    gw = group * HEAD_DIM
    tq = _tile(seq, 512)
    tk = _tile(seq, 512)
    nq = seq // tq
    return pl.pallas_call(
        functools.partial(_attn_kernel, tk=tk, group=group),
        grid=(batch, N_KV_HEADS, nq),
        in_specs=[
            pl.BlockSpec(memory_space=pltpu.SMEM),
            pl.BlockSpec((tq, gw), lambda b, n, i: (b * nq + i, n)),
            pl.BlockSpec((seq, HEAD_DIM), lambda b, n, i: (b, N_Q_HEADS + n)),
            pl.BlockSpec((seq, HEAD_DIM), lambda b, n, i: (b, N_Q_HEADS + N_KV_HEADS + n)),
        ],
        out_specs=pl.BlockSpec((tq, gw), lambda b, n, i: (b * nq + i, n)),
        out_shape=jax.ShapeDtypeStruct((batch * seq, N_Q_HEADS * HEAD_DIM), BF16),
        scratch_shapes=[
            pltpu.VMEM((group * tq, 2 * HEAD_DIM), F32),
            pltpu.VMEM((group * tq, LANES), F32),
        ],
        compiler_params=_params("parallel", "parallel", "arbitrary"),
        name="gqa_attention",
    )(score_bound, qkv, qkv, qkv)


def _hgrn_span(i, q_scr, b_scr, k_scr, zi_ref, vt_scr, o_scr, s_scr, tri, *, span, reverse, pairwise):
    C = HG_CHUNK
    nc = span // C
    r0 = pl.multiple_of(i * span, span)
    rows = pl.ds(r0, span)
    q = q_scr[rows, :]
    b = b_scr[rows, :]
    k = k_scr[rows, :]
    v = zi_ref[rows, :]
    vb = v.astype(BF16)
    end_row = 0 if reverse else C - 1
    ends = [b[c * C + end_row:c * C + end_row + 1, :] for c in range(nc)]
    b_end = jnp.concatenate([jnp.broadcast_to(e, (C, HG_D)) for e in ends], axis=0)
    qtb = (q * jnp.exp(b)).astype(BF16)
    kp = k * jnp.exp(b_end - b)

    if pairwise:
        rid = lax.broadcasted_iota(jnp.int32, (span, HG_D), 0)

        def pair(s, acc):
            cs = (s // C) * C
            if reverse:
                m = (rid <= s) & (rid >= cs)
            else:
                m = (rid >= s) & (rid < cs + C)
            w = jnp.where(m, jnp.exp(jnp.minimum(b - b_scr[pl.ds(r0 + s, 1), :], 0.0)), 0.0)
            r = jnp.sum(q * w * k_scr[pl.ds(r0 + s, 1), :], axis=1, keepdims=True)
            return acc + r * zi_ref[pl.ds(r0 + s, 1), :]

        o_intra = lax.fori_loop(0, span, pair, jnp.zeros((span, HG_D), F32))
    else:
        ktb = (k * jnp.exp(-b)).astype(BF16)
        a = jnp.where(tri, _dot_nt(qtb, ktb), 0.0)
        o_intra = _dot(a.astype(BF16), vb)

    chunk_of_row = lax.broadcasted_iota(jnp.int32, (span, HG_D), 0) // C
    kp_blocks = jnp.concatenate([jnp.where(chunk_of_row == c, kp, 0.0) for c in range(nc)], axis=1)
    upd = _dot(vt_scr[:, rows], kp_blocks.astype(BF16))

    st = s_scr[...]
    states = [None] * nc
    for c in (reversed(range(nc)) if reverse else range(nc)):
        states[c] = st.astype(BF16)
        st = st * jnp.exp(ends[c]) + upd[:, c * HG_D:(c + 1) * HG_D]
    s_scr[...] = st
    o_inter = _dot_nt(qtb, jnp.concatenate(states, axis=0))
    o_scr[rows, :] = o_intra + jnp.concatenate(
        [o_inter[c * C:(c + 1) * C, c * HG_D:(c + 1) * HG_D] for c in range(nc)], axis=0)


def _hgrn_kernel(zq_ref, zf_ref, zb_ref, zi_ref, zo_ref, lbf_ref, lbb_ref, on_ref, out_ref,
                 of_scr, ob_scr, sf_scr, sb_scr, q_scr, bf_scr, bb_scr, kf_scr, kb_scr, vt_scr, *, span):
    seq = zq_ref.shape[0]
    n_span = seq // span
    C = HG_CHUNK
    nc = span // C
    sf_scr[...] = jnp.zeros(sf_scr.shape, F32)
    sb_scr[...] = jnp.zeros(sb_scr.shape, F32)
    r = lax.broadcasted_iota(jnp.int32, (span, span), 0)
    c = lax.broadcasted_iota(jnp.int32, (span, span), 1)
    same = (r // C) == (c // C)
    tri_f = same & (c <= r)
    tri_b = same & (c >= r)

    def prepare(i, min_end):
        rows = pl.ds(pl.multiple_of(i * span, span), span)
        qh = zq_ref[rows, :]
        q_scr[rows, :] = qh * _sigmoid(qh)
        vt_scr[:, rows] = zi_ref[rows, :].T.astype(BF16)
        for z_ref, lb_ref, tri, b_scr, k_scr, end_row in (
                (zf_ref, lbf_ref, tri_f, bf_scr, kf_scr, C - 1), (zb_ref, lbb_ref, tri_b, bb_scr, kb_scr, 0)):
            lb = lb_ref[...]
            f = lb + (1.0 - lb) * _sigmoid(z_ref[rows, :])
            b = _dot_exact_lhs(tri.astype(BF16), jnp.log(f))
            b_scr[rows, :] = b
            k_scr[rows, :] = 1.0 - f
            for cc in range(nc):
                min_end = jnp.minimum(min_end, b[cc * C + end_row:cc * C + end_row + 1, :])
        return min_end

    min_end = lax.fori_loop(0, n_span, prepare, jnp.zeros((1, HG_D), F32))
    safe = jnp.min(min_end) > HG_SAFE_LOG_DECAY

    def scan(pairwise):
        def body(i, carry):
            _hgrn_span(i, q_scr, bf_scr, kf_scr, zi_ref, vt_scr, of_scr, sf_scr, tri_f,
                       span=span, reverse=False, pairwise=pairwise)
            _hgrn_span(n_span - 1 - i, q_scr, bb_scr, kb_scr, zi_ref, vt_scr, ob_scr, sb_scr, tri_b,
                       span=span, reverse=True, pairwise=pairwise)
            return carry
        lax.fori_loop(0, n_span, body, 0)

    @pl.when(safe)
    def _():
        scan(False)

    @pl.when(jnp.logical_not(safe))
    def _():
        scan(True)

    def finish(i, carry):
        rows = pl.ds(pl.multiple_of(i * span, span), span)
        o = _rmsnorm(of_scr[rows, :] + ob_scr[rows, :], on_ref[...])
        og = zo_ref[rows, :]
        out_ref[rows, :] = (o * (og * _sigmoid(og))).astype(out_ref.dtype)
        return carry

    lax.fori_loop(0, n_span, finish, 0)


def _hgrn(zh, lb_f, lb_b, out_norm, batch, seq):
    span = _tile(seq, 256)
    assert span % HG_CHUNK == 0
    zspec = lambda grp: pl.BlockSpec((seq, HG_D), lambda b, h: (b, grp * HG_HEADS + h))
    hspec = pl.BlockSpec((1, HG_D), lambda b, h: (0, h))
    seq_buf = pltpu.VMEM((seq, HG_D), F32)
    return pl.pallas_call(
        functools.partial(_hgrn_kernel, span=span),
        grid=(batch, HG_HEADS),
        in_specs=[zspec(0), zspec(1), zspec(2), zspec(3), zspec(4), hspec, hspec, hspec],
        out_specs=pl.BlockSpec((seq, HG_D), lambda b, h: (b, h)),
        out_shape=jax.ShapeDtypeStruct((batch * seq, HG_HEADS * HG_D), BF16),
        scratch_shapes=[
            seq_buf, seq_buf,
            pltpu.VMEM((HG_D, HG_D), F32), pltpu.VMEM((HG_D, HG_D), F32),
            seq_buf, seq_buf, seq_buf, seq_buf, seq_buf,
            pltpu.VMEM((HG_D, seq), BF16),
        ],
        compiler_params=_params("parallel", "parallel"),
        name="hgrn2",
    )(zh, zh, zh, zh, zh, lb_f, lb_b, out_norm)


def _merge_kernel(a_ref, h_ref, ga_ref, gb_ref, wa_ref, wh_ref, o_ref):
    ya = _dot(a_ref[...], wa_ref[...])
    yb = _dot(h_ref[...], wh_ref[...])
    merged = _sigmoid(ga_ref[...].astype(F32)) * ya + _sigmoid(gb_ref[...].astype(F32)) * yb
    o_ref[...] = merged.astype(o_ref.dtype)


def _merge(attn, hg, gates, w_up_attn, w_up_hgrn):
    n, wa = attn.shape
    wh = hg.shape[1]
    d = w_up_attn.shape[1]
    tm = _tile(n, 1024)
    tn = _tile(d, 512)
    nj = d // tn
    return pl.pallas_call(
        _merge_kernel,
        grid=(n // tm, nj),
        in_specs=[
            pl.BlockSpec((tm, wa), lambda i, j: (i, 0)),
            pl.BlockSpec((tm, wh), lambda i, j: (i, 0)),
            pl.BlockSpec((tm, tn), lambda i, j: (i, j)),
            pl.BlockSpec((tm, tn), lambda i, j: (i, nj + j)),
            pl.BlockSpec((wa, tn), lambda i, j: (0, j)),
            pl.BlockSpec((wh, tn), lambda i, j: (0, j)),
        ],
        out_specs=pl.BlockSpec((tm, tn), lambda i, j: (i, j)),
        out_shape=jax.ShapeDtypeStruct((n, d), BF16),
        compiler_params=_params("parallel", "arbitrary"),
        name="gated_merge",
    )(attn, hg, gates, gates, w_up_attn, w_up_hgrn)


def _pack_bf16_pairs(h):
    half = h.shape[1] // 2
    lo = pltpu.bitcast(h[:, :half].astype(BF16).astype(F32), jnp.uint32)
    hi = pltpu.bitcast(h[:, half:].astype(BF16).astype(F32), jnp.uint32)
    return (hi & jnp.uint32(0xFFFF0000)) | (lo >> 16)


def _unpack_bf16_pairs(u):
    lo = pltpu.bitcast(u << 16, F32).astype(BF16)
    hi = pltpu.bitcast(u & jnp.uint32(0xFFFF0000), F32).astype(BF16)
    return lo, hi


def _outproj_router_kernel(x_ref, m_ref, w_ref, g_ref, wr_ref, br_ref, hp_in_ref,
                           x1_ref, hp_ref, idx_ref, gate_ref):
    del hp_in_ref
    x1 = x_ref[...] + _dot(m_ref[...], w_ref[...])
    x1_ref[...] = x1
    h = _rmsnorm(x1, g_ref[...])
    hp_ref[...] = _pack_bf16_pairs(h)
    h_hi, h_mid, _ = _split3(h)
    w_hi, w_mid, _ = _split3(wr_ref[...])
    lg = (_dot_nt(w_hi, h_hi) + _dot_nt(w_hi, h_mid) + _dot_nt(w_mid, h_hi)) + br_ref[...]
    n_exp, tm = lg.shape
    eid = lax.broadcasted_iota(jnp.int32, (n_exp, tm), 0)
    vals = []
    for kk in range(TOP_K):
        m = jnp.max(lg, axis=0, keepdims=True)
        sel = jnp.min(jnp.where(lg == m, eid, n_exp), axis=0, keepdims=True)
        idx_ref[kk:kk + 1, :] = sel
        vals.append(m)
        lg = jnp.where(eid == sel, -jnp.inf, lg)
    ex = [jnp.exp(vv - vals[0]) for vv in vals]
    den = ex[0]
    for e in ex[1:]:
        den = den + e
    for kk in range(TOP_K):
        gate_ref[kk:kk + 1, :] = ex[kk] / den


def _outproj_router(x, merged, w_out, ffn_gain, w_router_t, b_router, hp_prev, row_off, n_total):
    n, d = x.shape
    tm = _tile(n, 512)
    assert row_off % tm == 0
    n_exp = w_router_t.shape[0]
    if hp_prev is None:
        hp_prev = jnp.zeros((8, LANES), jnp.uint32)
        aliases = {}
    else:
        aliases = {6: 1}
    return pl.pallas_call(
        _outproj_router_kernel,
        grid=(n // tm,),
        in_specs=[
            pl.BlockSpec((tm, d), lambda i: (i, 0)),
            pl.BlockSpec((tm, d), lambda i: (i, 0)),
            pl.BlockSpec((d, d), lambda i: (0, 0)),
            pl.BlockSpec((1, d), lambda i: (0, 0)),
            pl.BlockSpec((n_exp, d), lambda i: (0, 0)),
            pl.BlockSpec((n_exp, 1), lambda i: (0, 0)),
            pl.BlockSpec(memory_space=pl.ANY),
        ],
        out_specs=[
            pl.BlockSpec((tm, d), lambda i: (i, 0)),
            pl.BlockSpec((tm, d // 2), lambda i: (row_off // tm + i, 0)),
            pl.BlockSpec((TOP_K, tm), lambda i: (0, i)),
            pl.BlockSpec((TOP_K, tm), lambda i: (0, i)),
        ],
        out_shape=[
            jax.ShapeDtypeStruct((n, d), F32),
            jax.ShapeDtypeStruct((n_total, d // 2), jnp.uint32),
            jax.ShapeDtypeStruct((TOP_K, n), jnp.int32),
            jax.ShapeDtypeStruct((TOP_K, n), F32),
        ],
        input_output_aliases=aliases,
        compiler_params=_params("parallel"),
        name="outproj_router",
    )(x, merged, w_out, ffn_gain, w_router_t, b_router, hp_prev)


def _route_kernel(idx_ref, dest_ref, cnt_ref, cnt_scr, base_scr, *, row_block):
    phase = pl.program_id(0)
    i = pl.program_id(1)
    n_exp = cnt_scr.shape[0]
    tt = idx_ref.shape[1]
    eid = lax.broadcasted_iota(jnp.int32, (n_exp, tt), 0)
    onehot = [(eid == idx_ref[kk:kk + 1, :]) for kk in range(TOP_K)]

    @pl.when((phase == 0) & (i == 0))
    def _():
        cnt_scr[...] = jnp.zeros(cnt_scr.shape, F32)

    @pl.when(phase == 0)
    def _():
        tot = onehot[0].astype(F32)
        for oh in onehot[1:]:
            tot = tot + oh.astype(F32)
        cnt_scr[...] = cnt_scr[...] + jnp.sum(tot, axis=1, keepdims=True)
        cnt_ref[...] = cnt_scr[...]

    @pl.when((phase == 1) & (i == 0))
    def _():
        cnt = cnt_scr[...].astype(jnp.int32)
        padded = ((cnt + (row_block - 1)) // row_block * row_block).astype(F32)
        er = lax.broadcasted_iota(jnp.int32, (n_exp, n_exp), 0)
        ec = lax.broadcasted_iota(jnp.int32, (n_exp, n_exp), 1)
        base_scr[...] = _dot_exact_lhs((ec < er).astype(BF16), padded)

    @pl.when(phase == 1)
    def _():
        tr = lax.broadcasted_iota(jnp.int32, (tt, tt), 0)
        tc = lax.broadcasted_iota(jnp.int32, (tt, tt), 1)
        before = (tr < tc).astype(BF16)
        run = base_scr[...][:, :1]
        for kk in range(TOP_K):
            oh = onehot[kk].astype(F32)
            rank = _dot(oh.astype(BF16), before) + run
            dest_ref[kk:kk + 1, :] = jnp.sum(oh * rank, axis=0, keepdims=True).astype(jnp.int32)
            run = run + jnp.sum(oh, axis=1, keepdims=True)
        base_scr[...] = jnp.broadcast_to(run, base_scr.shape)


def _route(idx, row_block):
    n = idx.shape[1]
    tt = _tile(n, 512)
    return pl.pallas_call(
        functools.partial(_route_kernel, row_block=row_block),
        grid=(2, n // tt),
        in_specs=[pl.BlockSpec((TOP_K, tt), lambda p, i: (0, i))],
        out_specs=[
            pl.BlockSpec((TOP_K, tt), lambda p, i: (0, i * p)),
            pl.BlockSpec((N_EXPERTS, LANES), lambda p, i: (0, 0)),
        ],
        out_shape=[
            jax.ShapeDtypeStruct((TOP_K, n), jnp.int32),
            jax.ShapeDtypeStruct((N_EXPERTS, LANES), F32),
        ],
        scratch_shapes=[pltpu.VMEM((N_EXPERTS, LANES), F32), pltpu.VMEM((N_EXPERTS, LANES), F32)],
        compiler_params=_params("arbitrary", "arbitrary"),
        name="route_offsets",
    )(idx)


def _dispatch_kernel(seg_ref, dest_ref, h_ref, xs_ref, zero_scr, sem, zsem, *, row_block):
    tt = dest_ref.shape[1]
    n_exp = seg_ref.shape[1]

    @pl.when(pl.program_id(0) == 0)
    def _():
        zero_scr[...] = jnp.zeros(zero_scr.shape, zero_scr.dtype)

        def zero_copy(e):
            start = pl.multiple_of(seg_ref[1, e] - row_block, row_block)
            return pltpu.make_async_copy(zero_scr, xs_ref.at[pl.ds(start, row_block)], zsem)

        for e in range(n_exp):
            @pl.when(seg_ref[1, e] > seg_ref[0, e])
            def _(e=e):
                zero_copy(e).start()
        for e in range(n_exp):
            @pl.when(seg_ref[1, e] > seg_ref[0, e])
            def _(e=e):
                zero_copy(e).wait()

    def row_copy(t, kk):
        return pltpu.make_async_copy(h_ref.at[pl.ds(t, 1)], xs_ref.at[pl.ds(dest_ref[kk, t], 1)], sem)

    def start(t, carry):
        for kk in range(TOP_K):
            row_copy(t, kk).start()
        return carry

    lax.fori_loop(0, tt, start, 0, unroll=8)
    pltpu.make_async_copy(xs_ref.at[pl.ds(0, TOP_K * tt)], xs_ref.at[pl.ds(0, TOP_K * tt)], sem).wait()


def _dispatch(seg, dest, hp, n_rows, row_block):
    n, w = hp.shape
    tt = _tile(n, 512)
    grid_spec = pltpu.PrefetchScalarGridSpec(
        num_scalar_prefetch=1,
        grid=(n // tt,),
        in_specs=[
            pl.BlockSpec((TOP_K, tt), lambda i, s: (0, i), memory_space=pltpu.SMEM),
            pl.BlockSpec((tt, w), lambda i, s: (i, 0)),
        ],
        out_specs=pl.BlockSpec(memory_space=pl.ANY),
        scratch_shapes=[pltpu.VMEM((row_block, w), hp.dtype), pltpu.SemaphoreType.DMA(()),
                        pltpu.SemaphoreType.DMA(())],
    )
    return pl.pallas_call(
        functools.partial(_dispatch_kernel, row_block=row_block),
        grid_spec=grid_spec,
        out_shape=jax.ShapeDtypeStruct((n_rows, w), hp.dtype),
        compiler_params=_params("arbitrary"),
        name="dispatch_rows",
    )(seg, dest, hp)


def _expert_kernel(meta_ref, xs_ref, wg_ref, wl_ref, bg_ref, bl_ref, wd_ref, bd_ref, ys_ref,
                   xlo_scr, xhi_scr, y_ref):
    i = pl.program_id(0)
    j = pl.program_id(1)

    @pl.when(i < meta_ref[0])
    def _():
        @pl.when(j == 0)
        def _():
            lo, hi = _unpack_bf16_pairs(xs_ref[...])
            xlo_scr[...] = lo
            xhi_scr[...] = hi
            y_ref[...] = jnp.broadcast_to(bd_ref[...], y_ref.shape)

        half = xlo_scr.shape[1]
        xlo = xlo_scr[...]
        xhi = xhi_scr[...]
        glu = _dot(xlo, wg_ref[:half, :]) + _dot(xhi, wg_ref[half:, :]) + bg_ref[...]
        lin = _dot(xlo, wl_ref[:half, :]) + _dot(xhi, wl_ref[half:, :]) + bl_ref[...]
        glu = jnp.minimum(glu, SWIGLU_LIMIT)
        lin = jnp.clip(lin, -SWIGLU_LIMIT, SWIGLU_LIMIT)
        act = glu * _sigmoid(SWIGLU_ALPHA * glu) * (lin + 1.0)
        y_ref[...] = y_ref[...] + _dot(act.astype(BF16), wd_ref[...])

        @pl.when(j == pl.num_programs(1) - 1)
        def _():
            ys_ref[...] = _pack_bf16_pairs(y_ref[...])


def _experts(meta, xs, w_gu, b_gu, w_dn, b_dn, row_block):
    n_rows, half = xs.shape
    d = 2 * half
    d_ff = w_dn.shape[1]
    tf = _tile(d_ff, 1024)
    nf = d_ff // tf
    n_blocks = n_rows // row_block

    def jj(i, j, m):
        return jnp.where(i < m[0], j, nf - 1)

    grid_spec = pltpu.PrefetchScalarGridSpec(
        num_scalar_prefetch=1,
        grid=(n_blocks, nf),
        in_specs=[
            pl.BlockSpec((row_block, half), lambda i, j, m: (i, 0)),
            pl.BlockSpec((None, d, tf), lambda i, j, m: (m[1 + i], 0, jj(i, j, m))),
            pl.BlockSpec((None, d, tf), lambda i, j, m: (m[1 + i], 0, nf + jj(i, j, m))),
            pl.BlockSpec((None, 1, tf), lambda i, j, m: (m[1 + i], 0, jj(i, j, m))),
            pl.BlockSpec((None, 1, tf), lambda i, j, m: (m[1 + i], 0, nf + jj(i, j, m))),
            pl.BlockSpec((None, tf, d), lambda i, j, m: (m[1 + i], jj(i, j, m), 0)),
            pl.BlockSpec((None, 1, d), lambda i, j, m: (m[1 + i], 0, 0)),
        ],
        out_specs=pl.BlockSpec((row_block, half), lambda i, j, m: (i, 0)),
        scratch_shapes=[pltpu.VMEM((row_block, half), BF16), pltpu.VMEM((row_block, half), BF16),
                        pltpu.VMEM((row_block, d), F32)],
    )
    return pl.pallas_call(
        _expert_kernel,
        grid_spec=grid_spec,
        out_shape=jax.ShapeDtypeStruct((n_rows, half), jnp.uint32),
        compiler_params=_params("arbitrary", "arbitrary"),
        name="expert_swiglu",
    )(meta, xs, w_gu, w_gu, b_gu, b_gu, w_dn, b_dn)


def _combine_kernel(dest_ref, dest_next_ref, gate_ref, x1_ref, fn_ref, ys_ref, o_ref, buf, sem):
    tt = dest_ref.shape[1]
    i = pl.program_id(0)
    slot = i % 2

    def gather(d_ref, s):
        def start(t, carry):
            for kk in range(TOP_K):
                pltpu.make_async_copy(ys_ref.at[pl.ds(d_ref[kk, t], 1)],
                                      buf.at[s, pl.ds(kk * tt + t, 1)], sem.at[s]).start()
            return carry
        lax.fori_loop(0, tt, start, 0, unroll=8)

    @pl.when(i == 0)
    def _():
        gather(dest_ref, 0)

    @pl.when(i + 1 < pl.num_programs(0))
    def _():
        gather(dest_next_ref, 1 - slot)

    pltpu.make_async_copy(ys_ref.at[pl.ds(0, TOP_K * tt)], buf.at[slot], sem.at[slot]).wait()
    half = buf.shape[2]
    gates = gate_ref[...]
    acc_lo = x1_ref[:, :half]
    acc_hi = x1_ref[:, half:]
    for kk in range(TOP_K):
        u = buf[slot, pl.ds(kk * tt, tt), :]
        g = gates[:, kk:kk + 1]
        acc_lo = acc_lo + pltpu.bitcast(u << 16, F32) * g
        acc_hi = acc_hi + pltpu.bitcast(u & jnp.uint32(0xFFFF0000), F32) * g
    ms = (jnp.sum(acc_lo * acc_lo, axis=-1, keepdims=True)
          + jnp.sum(acc_hi * acc_hi, axis=-1, keepdims=True)) / (2 * half)
    inv = lax.rsqrt(ms + NORM_EPS)
    o_ref[:, :half] = acc_lo * inv * fn_ref[:, :half]
    o_ref[:, half:] = acc_hi * inv * fn_ref[:, half:]


def _combine(dest, gates_t, x1, final_gain, ys, row_off):
    n, d = x1.shape
    tt = _tile(n, 256)
    assert row_off % tt == 0
    off = row_off // tt
    nt = n // tt
    return pl.pallas_call(
        _combine_kernel,
        grid=(nt,),
        in_specs=[
            pl.BlockSpec((TOP_K, tt), lambda i: (0, off + i), memory_space=pltpu.SMEM),
            pl.BlockSpec((TOP_K, tt), lambda i: (0, off + jnp.minimum(i + 1, nt - 1)), memory_space=pltpu.SMEM),
            pl.BlockSpec((tt, TOP_K), lambda i: (off + i, 0)),
            pl.BlockSpec((tt, d), lambda i: (i, 0)),
            pl.BlockSpec((1, d), lambda i: (0, 0)),
            pl.BlockSpec(memory_space=pl.ANY),
        ],
        out_specs=pl.BlockSpec((tt, d), lambda i: (i, 0)),
        out_shape=jax.ShapeDtypeStruct((n, d), F32),
        scratch_shapes=[pltpu.VMEM((2, TOP_K * tt, d // 2), jnp.uint32), pltpu.SemaphoreType.DMA((2,))],
        compiler_params=_params("arbitrary"),
        name="combine_rows",
    )(dest, dest, gates_t, x1, final_gain, ys)


def _rope_tables(seq_len):
    rows = seq_len // GRID_W
    row = jnp.repeat(jnp.arange(rows, dtype=F32), GRID_W)
    col = jnp.tile(jnp.arange(GRID_W, dtype=F32), rows)
    freqs = ROPE_THETA ** (-jnp.arange(ROPE_HALF, dtype=F32) / ROPE_HALF)
    ang_r = row[:, None] * freqs[None, :]
    ang_c = col[:, None] * freqs[None, :]
    cos = jnp.concatenate([jnp.cos(ang_r), jnp.cos(ang_r), jnp.cos(ang_c), jnp.cos(ang_c)], axis=1)
    sin = jnp.concatenate([-jnp.sin(ang_r), jnp.sin(ang_r), -jnp.sin(ang_c), jnp.sin(ang_c)], axis=1)
    return cos, sin


def kernel(x_prompt, x_sample, mix_norm, w_in, q_norm, k_norm, hg_lb_logits, hg_out_norm, w_up_attn,
           w_up_hgrn, w_out, ffn_norm, w_router, b_router, w_gate_up, b_gate_up, w_down, b_down, final_norm):
    assert mix_norm.shape[0] == 1, "single trunk layer"
    d = x_prompt.shape[-1]
    hg_w = HG_HEADS * HG_D
    n_exp = w_router.shape[-1]
    row_block = MOE_ROW_BLOCK
    streams = [(x.reshape(-1, d), x.shape[0], x.shape[1]) for x in (x_prompt, x_sample)]
    n_total = sum(x.shape[0] for x, _, _ in streams)

    lb = jnp.cumsum(jax.nn.softmax(hg_lb_logits.astype(F32), axis=1), axis=1)[:, 0]
    w_in_b = w_in[0].astype(BF16)
    w_ua_b = w_up_attn[0].astype(BF16)
    w_uh_b = w_up_hgrn[0].astype(BF16)
    w_out_b = w_out[0].astype(BF16)
    w_r_t = w_router[0].T
    mix_g = mix_norm[0].reshape(1, d)
    score_bound = (1.02 * LOG2_E * HEAD_DIM ** 0.5 * jnp.max(jnp.abs(q_norm[0])) * jnp.max(jnp.abs(k_norm[0])))
    score_bound = score_bound.astype(F32).reshape(1)

    x1s, idxs, gate_ts = [], [], []
    hp = None
    row_off = 0
    for x, batch, seq in streams:
        cos, sin = _rope_tables(seq)
        qkv, zh, gates = _norm_proj(x, mix_g, w_in_b, q_norm[0].reshape(1, HEAD_DIM),
                                    k_norm[0].reshape(1, HEAD_DIM), cos, sin, seq)
        attn = _attention(qkv, score_bound, batch, seq)
        hg = _hgrn(zh, lb[0:1], lb[1:2], hg_out_norm[0].reshape(1, hg_w), batch, seq)
        merged = _merge(attn, hg, gates, w_ua_b, w_uh_b)
        x1, hp, idx, gate = _outproj_router(x, merged, w_out_b, ffn_norm[0].reshape(1, d), w_r_t,
                                            b_router[0].reshape(n_exp, 1), hp, row_off, n_total)
        x1s.append(x1)
        idxs.append(idx)
        gate_ts.append(gate.T)
        row_off += x.shape[0]

    idx = jnp.concatenate(idxs, axis=1)
    gate_t = jnp.concatenate(gate_ts, axis=0)
    dest, counts = _route(idx, row_block)
    cnt = counts[:, 0].astype(jnp.int32)
    padded = (cnt + row_block - 1) // row_block * row_block
    pad_end = jnp.cumsum(padded)
    n_rows = n_total * TOP_K + n_exp * row_block
    n_blocks = n_rows // row_block
    blk_start = jnp.arange(n_blocks, dtype=jnp.int32) * row_block
    blk_e = jnp.minimum(jnp.sum(pad_end[None, :] <= blk_start[:, None], axis=1), n_exp - 1).astype(jnp.int32)
    meta = jnp.concatenate([(pad_end[-1:] // row_block).astype(jnp.int32), blk_e])
    seg = jnp.stack([pad_end - padded, pad_end]).astype(jnp.int32)

    xs = _dispatch(seg, dest, hp, n_rows, row_block)
    ys = _experts(meta, xs, w_gate_up[0].astype(BF16), b_gate_up[0].reshape(n_exp, 1, -1),
                  w_down[0].astype(BF16), b_down[0].reshape(n_exp, 1, d), row_block)

    outs = []
    row_off = 0
    for (x, batch, seq), x1 in zip(streams, x1s):
        out = _combine(dest, gate_t, x1, final_norm.reshape(1, d), ys, row_off)
        outs.append(out.reshape(batch, seq, d))
        row_off += x.shape[0]
    return tuple(outs)
```

```python
import functools

import jax
import jax.numpy as jnp
from jax import lax
from jax.experimental import pallas as pl
from jax.experimental.pallas import tpu as pltpu

GRID_W = 64
HEAD_DIM = 128
N_Q_HEADS = 16
N_KV_HEADS = 4
ROPE_THETA = 10000.0
ROPE_HALF = HEAD_DIM // 4
HG_HEADS = 8
HG_D = 128
HG_CHUNK = 64
N_EXPERTS = 32
TOP_K = 4
SWIGLU_LIMIT = 7.0
SWIGLU_ALPHA = 1.702
NORM_EPS = 1e-5

HG_SAFE_LOG_DECAY = -60.0

LOG2_E = 1.4426950408889634
ATTN_FIXED_SHIFT_LIMIT = 60.0

V7X_VMEM_BYTES = 64 * 1024 * 1024
VMEM_LIMIT_BYTES = V7X_VMEM_BYTES - 8 * 1024 * 1024
LANES = 128
MOE_ROW_BLOCK = 512

BF16 = jnp.bfloat16
F32 = jnp.float32


def _params(*sem):
    return pltpu.CompilerParams(dimension_semantics=sem, vmem_limit_bytes=VMEM_LIMIT_BYTES)


def _tile(n, pref):
    t = min(n, pref)
    while n % t:
        t //= 2
    return t


def _sigmoid(x):
    return 1.0 / (1.0 + jnp.exp(-x))


def _rmsnorm(x, g):
    return x * lax.rsqrt(jnp.mean(x * x, axis=-1, keepdims=True) + NORM_EPS) * g


def _dot(a, b):
    return jnp.dot(a, b, preferred_element_type=F32)


def _dot_nt(a, b):
    return lax.dot_general(a, b, (((1,), (1,)), ((), ())), preferred_element_type=F32)


def _split3(x):
    hi = x.astype(BF16)
    r = x - hi.astype(F32)
    mid = r.astype(BF16)
    lo = (r - mid.astype(F32)).astype(BF16)
    return hi, mid, lo


def _dot_exact_lhs(m_bf16, x, pieces=3):
    parts = _split3(x)[:pieces]
    acc = _dot(m_bf16, parts[0])
    for p in parts[1:]:
        acc = acc + _dot(m_bf16, p)
    return acc


def _rope_head_pair(zp, gain, cos, sin, ones_blk, perm_blk, scale):
    ss = _dot((zp * zp).astype(BF16), ones_blk)
    y = zp * lax.rsqrt(ss * (1.0 / HEAD_DIM) + NORM_EPS) * gain
    y_hi = y.astype(BF16)
    y_lo = (y - y_hi.astype(F32)).astype(BF16)
    partner = _dot(y_hi, perm_blk) + _dot(y_lo, perm_blk)
    return (y * cos + partner * sin) * scale


def _proj_kernel(x_ref, g_ref, w_ref, qn_ref, kn_ref, cos_ref, sin_ref, ones_ref, perm_ref,
                 qkv_ref, zh_ref, gate_ref, h_scr, *, tile_kinds):
    j = pl.program_id(1)

    @pl.when(j == 0)
    def _():
        h_scr[...] = _rmsnorm(x_ref[...], g_ref[...]).astype(BF16)

    z = _dot(h_scr[...], w_ref[...])

    def qkv_tile(kinds):
        two = lambda r: jnp.concatenate([r[...], r[...]], axis=1)
        outs = []
        for h in range(0, len(kinds), 2):
            kind = kinds[h]
            assert kinds[h + 1] == kind
            zp = z[:, h * HEAD_DIM:(h + 2) * HEAD_DIM]
            if kind == "q":
                zp = _rope_head_pair(zp, two(qn_ref), two(cos_ref), two(sin_ref), ones_ref[...], perm_ref[...],
                                     LOG2_E * HEAD_DIM ** -0.5)
            elif kind == "k":
                zp = _rope_head_pair(zp, two(kn_ref), two(cos_ref), two(sin_ref), ones_ref[...], perm_ref[...], 1.0)
            outs.append(zp)
        return jnp.concatenate(outs, axis=1)

    for lo, hi, kind in tile_kinds:
        @pl.when((j >= lo) & (j < hi))
        def _(kind=kind):
            if kind == "zh":
                zh_ref[...] = z
            elif kind == "gate":
                gate_ref[...] = z.astype(gate_ref.dtype)
            else:
                qkv_ref[...] = qkv_tile(kind).astype(qkv_ref.dtype)


def _norm_proj(x, gain, w, q_gain, k_gain, cos, sin, seq):
    n, d = x.shape
    attn_w = N_Q_HEADS * HEAD_DIM
    kv_w = N_KV_HEADS * HEAD_DIM
    hg5 = 5 * HG_HEADS * HG_D
    qkv_w = attn_w + 2 * kv_w
    assert w.shape[1] == qkv_w + hg5 + 2 * d
    tm = _tile(seq, 1024)
    tn = 1024
    while attn_w % tn or (2 * kv_w) % tn or hg5 % tn or (2 * d) % tn:
        tn //= 2
    assert tn % HEAD_DIM == 0
    heads = ["q"] * N_Q_HEADS + ["k"] * N_KV_HEADS + ["v"] * N_KV_HEADS
    hpt = tn // HEAD_DIM
    n_qkv, n_zh, n_gate = qkv_w // tn, hg5 // tn, 2 * d // tn
    tile_kinds = []
    for t in range(n_qkv):
        kind = tuple(heads[t * hpt:(t + 1) * hpt])
        if tile_kinds and tile_kinds[-1][2] == kind:
            tile_kinds[-1] = (tile_kinds[-1][0], t + 1, kind)
        else:
            tile_kinds.append((t, t + 1, kind))
    tile_kinds += [(n_qkv, n_qkv + n_zh, "zh"), (n_qkv + n_zh, n_qkv + n_zh + n_gate, "gate")]
    n_pos = seq // tm
    r = jnp.arange(2 * HEAD_DIM)
    same_head = (r[:, None] // HEAD_DIM) == (r[None, :] // HEAD_DIM)
    ones_blk = same_head.astype(BF16)
    partner_of = jnp.where((r % (2 * ROPE_HALF)) < ROPE_HALF, r + ROPE_HALF, r - ROPE_HALF)
    perm_blk = (r[:, None] == partner_of[None, :]).astype(BF16)
    const_spec = pl.BlockSpec((2 * HEAD_DIM, 2 * HEAD_DIM), lambda i, j: (0, 0))
    return pl.pallas_call(
        functools.partial(_proj_kernel, tile_kinds=tile_kinds),
        grid=(n // tm, n_qkv + n_zh + n_gate),
        in_specs=[
            pl.BlockSpec((tm, d), lambda i, j: (i, 0)),
            pl.BlockSpec((1, d), lambda i, j: (0, 0)),
            pl.BlockSpec((d, tn), lambda i, j: (0, j)),
            pl.BlockSpec((1, HEAD_DIM), lambda i, j: (0, 0)),
            pl.BlockSpec((1, HEAD_DIM), lambda i, j: (0, 0)),
            pl.BlockSpec((tm, HEAD_DIM), lambda i, j: (i % n_pos, 0)),
            pl.BlockSpec((tm, HEAD_DIM), lambda i, j: (i % n_pos, 0)),
            const_spec,
            const_spec,
        ],
        out_specs=[
            pl.BlockSpec((tm, tn), lambda i, j: (i, jnp.minimum(j, n_qkv - 1))),
            pl.BlockSpec((tm, tn), lambda i, j: (i, jnp.clip(j - n_qkv, 0, n_zh - 1))),
            pl.BlockSpec((tm, tn), lambda i, j: (i, jnp.clip(j - n_qkv - n_zh, 0, n_gate - 1))),
        ],
        out_shape=[
            jax.ShapeDtypeStruct((n, qkv_w), BF16),
            jax.ShapeDtypeStruct((n, hg5), F32),
            jax.ShapeDtypeStruct((n, 2 * d), BF16),
        ],
        scratch_shapes=[pltpu.VMEM((tm, d), BF16)],
        compiler_params=_params("parallel", "arbitrary"),
        name="norm_proj",
    )(x, gain, w, q_gain, k_gain, cos, sin, ones_blk, perm_blk)


def _attn_kernel(bound_ref, q_ref, k_ref, v_ref, o_ref, acc_scr, m_scr, *, tk, group):
    tq = q_ref.shape[0]
    seq = k_ref.shape[0]
    q = jnp.concatenate([q_ref[:, g * HEAD_DIM:(g + 1) * HEAD_DIM] for g in range(group)], axis=0)
    ones = jnp.ones((tk, HEAD_DIM), BF16)
    bound = bound_ref[0]
    acc_scr[...] = jnp.zeros(acc_scr.shape, F32)

    def chunk(c):
        rows = pl.ds(pl.multiple_of(c * tk, tk), tk)
        s = _dot_nt(q, k_ref[rows, :])
        return s, jnp.concatenate([v_ref[rows, :], ones], axis=1)

    @pl.when(bound <= ATTN_FIXED_SHIFT_LIMIT)
    def _():
        def body(c, carry):
            s, v1 = chunk(c)
            acc_scr[...] += _dot(jnp.exp2(s - bound).astype(BF16), v1)
            return carry
        lax.fori_loop(0, seq // tk, body, 0, unroll=True)

    @pl.when(bound > ATTN_FIXED_SHIFT_LIMIT)
    def _():
        m_scr[...] = jnp.full(m_scr.shape, -1e30, F32)

        def body(c, carry):
            s, v1 = chunk(c)
            m_prev = m_scr[...]
            m_new = jnp.maximum(m_prev, jnp.max(s, axis=1, keepdims=True))
            alpha = jnp.exp2(m_prev - m_new)
            p = jnp.exp2(s - jnp.tile(m_new, (1, tk // LANES)))
            acc_scr[...] = jnp.tile(alpha, (1, 2)) * acc_scr[...] + _dot(p.astype(BF16), v1)
            m_scr[...] = m_new
            return carry
        lax.fori_loop(0, seq // tk, body, 0)

    acc = acc_scr[...]
    o = acc[:, :HEAD_DIM] / acc[:, HEAD_DIM:]
    for g in range(group):
        o_ref[:, g * HEAD_DIM:(g + 1) * HEAD_DIM] = o[g * tq:(g + 1) * tq].astype(o_ref.dtype)


def _attention(qkv, score_bound, batch, seq):
    group = N_Q_HEADS // N_KV_HEADS
    gw = group * HEAD_DIM
    tq = _tile(seq, 512)
    tk = _tile(seq, 512)
    nq = seq // tq
    return pl.pallas_call(
        functools.partial(_attn_kernel, tk=tk, group=group),
        grid=(batch, N_KV_HEADS, nq),
        in_specs=[
            pl.BlockSpec(memory_space=pltpu.SMEM),
            pl.BlockSpec((tq, gw), lambda b, n, i: (b * nq + i, n)),
            pl.BlockSpec((seq, HEAD_DIM), lambda b, n, i: (b, N_Q_HEADS + n)),
            pl.BlockSpec((seq, HEAD_DIM), lambda b, n, i: (b, N_Q_HEADS + N_KV_HEADS + n)),
        ],
        out_specs=pl.BlockSpec((tq, gw), lambda b, n, i: (b * nq + i, n)),
        out_shape=jax.ShapeDtypeStruct((batch * seq, N_Q_HEADS * HEAD_DIM), BF16),
        scratch_shapes=[
            pltpu.VMEM((group * tq, 2 * HEAD_DIM), F32),
            pltpu.VMEM((group * tq, LANES), F32),
        ],
        compiler_params=_params("parallel", "parallel", "arbitrary"),
        name="gqa_attention",
    )(score_bound, qkv, qkv, qkv)


def _hgrn_span(i, q_scr, b_scr, k_scr, zi_ref, vt_scr, o_scr, s_scr, tri, *, span, reverse, pairwise):
    C = HG_CHUNK
    nc = span // C
    r0 = pl.multiple_of(i * span, span)
    rows = pl.ds(r0, span)
    q = q_scr[rows, :]
    b = b_scr[rows, :]
    k = k_scr[rows, :]
    v = zi_ref[rows, :]
    vb = v.astype(BF16)
    end_row = 0 if reverse else C - 1
    ends = [b[c * C + end_row:c * C + end_row + 1, :] for c in range(nc)]
    b_end = jnp.concatenate([jnp.broadcast_to(e, (C, HG_D)) for e in ends], axis=0)
    qtb = (q * jnp.exp(b)).astype(BF16)
    kp = k * jnp.exp(b_end - b)

    if pairwise:
        rid = lax.broadcasted_iota(jnp.int32, (span, HG_D), 0)

        def pair(s, acc):
            cs = (s // C) * C
            if reverse:
                m = (rid <= s) & (rid >= cs)
            else:
                m = (rid >= s) & (rid < cs + C)
            w = jnp.where(m, jnp.exp(jnp.minimum(b - b_scr[pl.ds(r0 + s, 1), :], 0.0)), 0.0)
            r = jnp.sum(q * w * k_scr[pl.ds(r0 + s, 1), :], axis=1, keepdims=True)
            return acc + r * zi_ref[pl.ds(r0 + s, 1), :]

        o_intra = lax.fori_loop(0, span, pair, jnp.zeros((span, HG_D), F32))
    else:
        ktb = (k * jnp.exp(-b)).astype(BF16)
        a = jnp.where(tri, _dot_nt(qtb, ktb), 0.0)
        o_intra = _dot(a.astype(BF16), vb)

    chunk_of_row = lax.broadcasted_iota(jnp.int32, (span, HG_D), 0) // C
    kp_blocks = jnp.concatenate([jnp.where(chunk_of_row == c, kp, 0.0) for c in range(nc)], axis=1)
    upd = _dot(vt_scr[:, rows], kp_blocks.astype(BF16))

    st = s_scr[...]
    states = [None] * nc
    for c in (reversed(range(nc)) if reverse else range(nc)):
        states[c] = st.astype(BF16)
        st = st * jnp.exp(ends[c]) + upd[:, c * HG_D:(c + 1) * HG_D]
    s_scr[...] = st
    o_inter = _dot_nt(qtb, jnp.concatenate(states, axis=0))
    o_scr[rows, :] = o_intra + jnp.concatenate(
        [o_inter[c * C:(c + 1) * C, c * HG_D:(c + 1) * HG_D] for c in range(nc)], axis=0)


def _hgrn_kernel(zq_ref, zf_ref, zb_ref, zi_ref, zo_ref, lbf_ref, lbb_ref, on_ref, out_ref,
                 of_scr, ob_scr, sf_scr, sb_scr, q_scr, bf_scr, bb_scr, kf_scr, kb_scr, vt_scr, *, span):
    seq = zq_ref.shape[0]
    n_span = seq // span
    C = HG_CHUNK
    nc = span // C
    sf_scr[...] = jnp.zeros(sf_scr.shape, F32)
    sb_scr[...] = jnp.zeros(sb_scr.shape, F32)
    r = lax.broadcasted_iota(jnp.int32, (span, span), 0)
    c = lax.broadcasted_iota(jnp.int32, (span, span), 1)
    same = (r // C) == (c // C)
    tri_f = same & (c <= r)
    tri_b = same & (c >= r)
    tri_f16 = tri_f.astype(BF16)
    tri_b16 = tri_b.astype(BF16)

    def prepare(i, min_end):
        rows = pl.ds(pl.multiple_of(i * span, span), span)
        qh = zq_ref[rows, :]
        q_scr[rows, :] = qh * _sigmoid(qh)
        vt_scr[:, rows] = zi_ref[rows, :].T.astype(BF16)
        for z_ref, lb_ref, tri, b_scr, k_scr, end_row in (
                (zf_ref, lbf_ref, tri_f16, bf_scr, kf_scr, C - 1), (zb_ref, lbb_ref, tri_b16, bb_scr, kb_scr, 0)):
            lb = lb_ref[...]
            f = lb + (1.0 - lb) * _sigmoid(z_ref[rows, :])
            b = _dot_exact_lhs(tri, jnp.log(f), pieces=2)
            b_scr[rows, :] = b
            k_scr[rows, :] = 1.0 - f
            for cc in range(nc):
                min_end = jnp.minimum(min_end, b[cc * C + end_row:cc * C + end_row + 1, :])
        return min_end

    min_end = lax.fori_loop(0, n_span, prepare, jnp.zeros((1, HG_D), F32), unroll=2)
    safe = jnp.min(min_end) > HG_SAFE_LOG_DECAY

    def scan(pairwise):
        def body(i, carry):
            _hgrn_span(i, q_scr, bf_scr, kf_scr, zi_ref, vt_scr, of_scr, sf_scr, tri_f,
                       span=span, reverse=False, pairwise=pairwise)
            _hgrn_span(n_span - 1 - i, q_scr, bb_scr, kb_scr, zi_ref, vt_scr, ob_scr, sb_scr, tri_b,
                       span=span, reverse=True, pairwise=pairwise)
            return carry
        lax.fori_loop(0, n_span, body, 0, unroll=1 if pairwise else 2)

    @pl.when(safe)
    def _():
        scan(False)

    @pl.when(jnp.logical_not(safe))
    def _():
        scan(True)

    def finish(i, carry):
        rows = pl.ds(pl.multiple_of(i * span, span), span)
        o = _rmsnorm(of_scr[rows, :] + ob_scr[rows, :], on_ref[...])
        og = zo_ref[rows, :]
        out_ref[rows, :] = (o * (og * _sigmoid(og))).astype(out_ref.dtype)
        return carry

    lax.fori_loop(0, n_span, finish, 0)


def _hgrn(zh, lb_f, lb_b, out_norm, batch, seq):
    span = _tile(seq, 256)
    assert span % HG_CHUNK == 0
    zspec = lambda grp: pl.BlockSpec((seq, HG_D), lambda b, h: (b, grp * HG_HEADS + h))
    hspec = pl.BlockSpec((1, HG_D), lambda b, h: (0, h))
    seq_buf = pltpu.VMEM((seq, HG_D), F32)
    return pl.pallas_call(
        functools.partial(_hgrn_kernel, span=span),
        grid=(batch, HG_HEADS),
        in_specs=[zspec(0), zspec(1), zspec(2), zspec(3), zspec(4), hspec, hspec, hspec],
        out_specs=pl.BlockSpec((seq, HG_D), lambda b, h: (b, h)),
        out_shape=jax.ShapeDtypeStruct((batch * seq, HG_HEADS * HG_D), BF16),
        scratch_shapes=[
            seq_buf, seq_buf,
            pltpu.VMEM((HG_D, HG_D), F32), pltpu.VMEM((HG_D, HG_D), F32),
            seq_buf, seq_buf, seq_buf, seq_buf, seq_buf,
            pltpu.VMEM((HG_D, seq), BF16),
        ],
        compiler_params=_params("parallel", "parallel"),
        name="hgrn2",
    )(zh, zh, zh, zh, zh, lb_f, lb_b, out_norm)


def _merge_kernel(a_ref, h_ref, ga_ref, gb_ref, wa_ref, wh_ref, o_ref):
    ya = _dot(a_ref[...], wa_ref[...])
    yb = _dot(h_ref[...], wh_ref[...])
    merged = _sigmoid(ga_ref[...].astype(F32)) * ya + _sigmoid(gb_ref[...].astype(F32)) * yb
    o_ref[...] = merged.astype(o_ref.dtype)


def _merge(attn, hg, gates, w_up_attn, w_up_hgrn):
    n, wa = attn.shape
    wh = hg.shape[1]
    d = w_up_attn.shape[1]
    tm = _tile(n, 1024)
    tn = _tile(d, 512)
    nj = d // tn
    return pl.pallas_call(
        _merge_kernel,
        grid=(n // tm, nj),
        in_specs=[
            pl.BlockSpec((tm, wa), lambda i, j: (i, 0)),
            pl.BlockSpec((tm, wh), lambda i, j: (i, 0)),
            pl.BlockSpec((tm, tn), lambda i, j: (i, j)),
            pl.BlockSpec((tm, tn), lambda i, j: (i, nj + j)),
            pl.BlockSpec((wa, tn), lambda i, j: (0, j)),
            pl.BlockSpec((wh, tn), lambda i, j: (0, j)),
        ],
        out_specs=pl.BlockSpec((tm, tn), lambda i, j: (i, j)),
        out_shape=jax.ShapeDtypeStruct((n, d), BF16),
        compiler_params=_params("parallel", "arbitrary"),
        name="gated_merge",
    )(attn, hg, gates, gates, w_up_attn, w_up_hgrn)


def _pack_bf16_pairs(h):
    half = h.shape[1] // 2
    lo = pltpu.bitcast(h[:, :half].astype(BF16).astype(F32), jnp.uint32)
    hi = pltpu.bitcast(h[:, half:].astype(BF16).astype(F32), jnp.uint32)
    return (hi & jnp.uint32(0xFFFF0000)) | (lo >> 16)


def _unpack_bf16_pairs(u):
    lo = pltpu.bitcast(u << 16, F32).astype(BF16)
    hi = pltpu.bitcast(u & jnp.uint32(0xFFFF0000), F32).astype(BF16)
    return lo, hi


def _outproj_router_kernel(x_ref, m_ref, w_ref, g_ref, wr_ref, br_ref, hp_in_ref,
                           x1_ref, hp_ref, idx_ref, gate_ref):
    del hp_in_ref
    x1 = x_ref[...] + _dot(m_ref[...], w_ref[...])
    x1_ref[...] = x1
    h = _rmsnorm(x1, g_ref[...])
    hp_ref[...] = _pack_bf16_pairs(h)
    h_hi, h_mid, _ = _split3(h)
    w_hi, w_mid, _ = _split3(wr_ref[...])
    lg = (_dot_nt(w_hi, h_hi) + _dot_nt(w_hi, h_mid) + _dot_nt(w_mid, h_hi)) + br_ref[...]
    n_exp, tm = lg.shape
    eid = lax.broadcasted_iota(jnp.int32, (n_exp, tm), 0)
    vals = []
    for kk in range(TOP_K):
        m = jnp.max(lg, axis=0, keepdims=True)
        sel = jnp.min(jnp.where(lg == m, eid, n_exp), axis=0, keepdims=True)
        idx_ref[kk:kk + 1, :] = sel
        vals.append(m)
        lg = jnp.where(eid == sel, -jnp.inf, lg)
    ex = [jnp.exp(vv - vals[0]) for vv in vals]
    den = ex[0]
    for e in ex[1:]:
        den = den + e
    for kk in range(TOP_K):
        gate_ref[kk:kk + 1, :] = ex[kk] / den


def _outproj_router(x, merged, w_out, ffn_gain, w_router_t, b_router, hp_prev, row_off, n_total):
    n, d = x.shape
    tm = _tile(n, 512)
    assert row_off % tm == 0
    n_exp = w_router_t.shape[0]
    if hp_prev is None:
        hp_prev = jnp.zeros((8, LANES), jnp.uint32)
        aliases = {}
    else:
        aliases = {6: 1}
    return pl.pallas_call(
        _outproj_router_kernel,
        grid=(n // tm,),
        in_specs=[
            pl.BlockSpec((tm, d), lambda i: (i, 0)),
            pl.BlockSpec((tm, d), lambda i: (i, 0)),
            pl.BlockSpec((d, d), lambda i: (0, 0)),
            pl.BlockSpec((1, d), lambda i: (0, 0)),
            pl.BlockSpec((n_exp, d), lambda i: (0, 0)),
            pl.BlockSpec((n_exp, 1), lambda i: (0, 0)),
            pl.BlockSpec(memory_space=pl.ANY),
        ],
        out_specs=[
            pl.BlockSpec((tm, d), lambda i: (i, 0)),
            pl.BlockSpec((tm, d // 2), lambda i: (row_off // tm + i, 0)),
            pl.BlockSpec((TOP_K, tm), lambda i: (0, i)),
            pl.BlockSpec((TOP_K, tm), lambda i: (0, i)),
        ],
        out_shape=[
            jax.ShapeDtypeStruct((n, d), F32),
            jax.ShapeDtypeStruct((n_total, d // 2), jnp.uint32),
            jax.ShapeDtypeStruct((TOP_K, n), jnp.int32),
            jax.ShapeDtypeStruct((TOP_K, n), F32),
        ],
        input_output_aliases=aliases,
        compiler_params=_params("parallel"),
        name="outproj_router",
    )(x, merged, w_out, ffn_gain, w_router_t, b_router, hp_prev)


def _route_kernel(idx_ref, dest_ref, cnt_ref, cnt_scr, base_scr, *, row_block):
    phase = pl.program_id(0)
    i = pl.program_id(1)
    n_exp = cnt_scr.shape[0]
    tt = idx_ref.shape[1]
    eid = lax.broadcasted_iota(jnp.int32, (n_exp, tt), 0)
    onehot = [(eid == idx_ref[kk:kk + 1, :]) for kk in range(TOP_K)]

    @pl.when((phase == 0) & (i == 0))
    def _():
        cnt_scr[...] = jnp.zeros(cnt_scr.shape, F32)

    @pl.when(phase == 0)
    def _():
        tot = onehot[0].astype(F32)
        for oh in onehot[1:]:
            tot = tot + oh.astype(F32)
        cnt_scr[...] = cnt_scr[...] + jnp.sum(tot, axis=1, keepdims=True)
        cnt_ref[...] = cnt_scr[...]

    @pl.when((phase == 1) & (i == 0))
    def _():
        cnt = cnt_scr[...].astype(jnp.int32)
        padded = ((cnt + (row_block - 1)) // row_block * row_block).astype(F32)
        er = lax.broadcasted_iota(jnp.int32, (n_exp, n_exp), 0)
        ec = lax.broadcasted_iota(jnp.int32, (n_exp, n_exp), 1)
        base_scr[...] = _dot_exact_lhs((ec < er).astype(BF16), padded)

    @pl.when(phase == 1)
    def _():
        tr = lax.broadcasted_iota(jnp.int32, (tt, tt), 0)
        tc = lax.broadcasted_iota(jnp.int32, (tt, tt), 1)
        before = (tr < tc).astype(BF16)
        run = base_scr[...][:, :1]
        for kk in range(TOP_K):
            oh = onehot[kk].astype(F32)
            rank = _dot(oh.astype(BF16), before) + run
            dest_ref[kk:kk + 1, :] = jnp.sum(oh * rank, axis=0, keepdims=True).astype(jnp.int32)
            run = run + jnp.sum(oh, axis=1, keepdims=True)
        base_scr[...] = jnp.broadcast_to(run, base_scr.shape)


def _route(idx, row_block):
    n = idx.shape[1]
    tt = _tile(n, 512)
    return pl.pallas_call(
        functools.partial(_route_kernel, row_block=row_block),
        grid=(2, n // tt),
        in_specs=[pl.BlockSpec((TOP_K, tt), lambda p, i: (0, i))],
        out_specs=[
            pl.BlockSpec((TOP_K, tt), lambda p, i: (0, i * p)),
            pl.BlockSpec((N_EXPERTS, LANES), lambda p, i: (0, 0)),
        ],
        out_shape=[
            jax.ShapeDtypeStruct((TOP_K, n), jnp.int32),
            jax.ShapeDtypeStruct((N_EXPERTS, LANES), F32),
        ],
        scratch_shapes=[pltpu.VMEM((N_EXPERTS, LANES), F32), pltpu.VMEM((N_EXPERTS, LANES), F32)],
        compiler_params=_params("arbitrary", "arbitrary"),
        name="route_offsets",
    )(idx)


def _dispatch_kernel(seg_ref, dest_ref, h_ref, xs_ref, zero_scr, sem, zsem, *, row_block):
    tt = dest_ref.shape[1]
    n_exp = seg_ref.shape[1]

    @pl.when(pl.program_id(0) == 0)
    def _():
        zero_scr[...] = jnp.zeros(zero_scr.shape, zero_scr.dtype)

        def zero_copy(e):
            start = pl.multiple_of(seg_ref[1, e] - row_block, row_block)
            return pltpu.make_async_copy(zero_scr, xs_ref.at[pl.ds(start, row_block)], zsem)

        for e in range(n_exp):
            @pl.when(seg_ref[1, e] > seg_ref[0, e])
            def _(e=e):
                zero_copy(e).start()
        for e in range(n_exp):
            @pl.when(seg_ref[1, e] > seg_ref[0, e])
            def _(e=e):
                zero_copy(e).wait()

    def row_copy(t, kk):
        return pltpu.make_async_copy(h_ref.at[pl.ds(t, 1)], xs_ref.at[pl.ds(dest_ref[kk, t], 1)], sem)

    def start(t, carry):
        for kk in range(TOP_K):
            row_copy(t, kk).start()
        return carry

    lax.fori_loop(0, tt, start, 0, unroll=8)
    pltpu.make_async_copy(xs_ref.at[pl.ds(0, TOP_K * tt)], xs_ref.at[pl.ds(0, TOP_K * tt)], sem).wait()


def _dispatch(seg, dest, hp, n_rows, row_block):
    n, w = hp.shape
    tt = _tile(n, 512)
    grid_spec = pltpu.PrefetchScalarGridSpec(
        num_scalar_prefetch=1,
        grid=(n // tt,),
        in_specs=[
            pl.BlockSpec((TOP_K, tt), lambda i, s: (0, i), memory_space=pltpu.SMEM),
            pl.BlockSpec((tt, w), lambda i, s: (i, 0)),
        ],
        out_specs=pl.BlockSpec(memory_space=pl.ANY),
        scratch_shapes=[pltpu.VMEM((row_block, w), hp.dtype), pltpu.SemaphoreType.DMA(()),
                        pltpu.SemaphoreType.DMA(())],
    )
    return pl.pallas_call(
        functools.partial(_dispatch_kernel, row_block=row_block),
        grid_spec=grid_spec,
        out_shape=jax.ShapeDtypeStruct((n_rows, w), hp.dtype),
        compiler_params=_params("arbitrary"),
        name="dispatch_rows",
    )(seg, dest, hp)


def _expert_kernel(meta_ref, xs_ref, wg_ref, wl_ref, bg_ref, bl_ref, wd_ref, bd_ref, ys_ref,
                   x_scr, y_ref):
    i = pl.program_id(0)
    j = pl.program_id(1)

    @pl.when(i < meta_ref[0])
    def _():
        @pl.when(j == 0)
        def _():
            half = xs_ref.shape[1]
            lo, hi = _unpack_bf16_pairs(xs_ref[...])
            x_scr[:, :half] = lo
            x_scr[:, half:] = hi
            y_ref[...] = jnp.broadcast_to(bd_ref[...], y_ref.shape)

        x = x_scr[...]
        glu = _dot(x, wg_ref[...]) + bg_ref[...]
        lin = _dot(x, wl_ref[...]) + bl_ref[...]
        glu = jnp.minimum(glu, SWIGLU_LIMIT)
        lin = jnp.clip(lin, -SWIGLU_LIMIT, SWIGLU_LIMIT)
        act = glu * _sigmoid(SWIGLU_ALPHA * glu) * (lin + 1.0)
        y_ref[...] = y_ref[...] + _dot(act.astype(BF16), wd_ref[...])

        @pl.when(j == pl.num_programs(1) - 1)
        def _():
            ys_ref[...] = _pack_bf16_pairs(y_ref[...])


def _experts(meta, xs, w_gu, b_gu, w_dn, b_dn, row_block):
    n_rows, half = xs.shape
    d = 2 * half
    d_ff = w_dn.shape[1]
    tf = _tile(d_ff, 1024)
    nf = d_ff // tf
    n_blocks = n_rows // row_block

    def jj(i, j, m):
        return jnp.where(i < m[0], j, nf - 1)

    grid_spec = pltpu.PrefetchScalarGridSpec(
        num_scalar_prefetch=1,
        grid=(n_blocks, nf),
        in_specs=[
            pl.BlockSpec((row_block, half), lambda i, j, m: (i, 0)),
            pl.BlockSpec((None, d, tf), lambda i, j, m: (m[1 + i], 0, jj(i, j, m))),
            pl.BlockSpec((None, d, tf), lambda i, j, m: (m[1 + i], 0, nf + jj(i, j, m))),
            pl.BlockSpec((None, 1, tf), lambda i, j, m: (m[1 + i], 0, jj(i, j, m))),
            pl.BlockSpec((None, 1, tf), lambda i, j, m: (m[1 + i], 0, nf + jj(i, j, m))),
            pl.BlockSpec((None, tf, d), lambda i, j, m: (m[1 + i], jj(i, j, m), 0)),
            pl.BlockSpec((None, 1, d), lambda i, j, m: (m[1 + i], 0, 0)),
        ],
        out_specs=pl.BlockSpec((row_block, half), lambda i, j, m: (i, 0)),
        scratch_shapes=[pltpu.VMEM((row_block, d), BF16), pltpu.VMEM((row_block, d), F32)],
    )
    return pl.pallas_call(
        _expert_kernel,
        grid_spec=grid_spec,
        out_shape=jax.ShapeDtypeStruct((n_rows, half), jnp.uint32),
        compiler_params=_params("arbitrary", "arbitrary"),
        name="expert_swiglu",
    )(meta, xs, w_gu, w_gu, b_gu, b_gu, w_dn, b_dn)


def _combine_kernel(dest_ref, dest_next_ref, gate_ref, x1_ref, fn_ref, ys_ref, o_ref, buf, sem):
    tt = dest_ref.shape[1]
    i = pl.program_id(0)
    slot = i % 2

    def gather(d_ref, s):
        def start(t, carry):
            for kk in range(TOP_K):
                pltpu.make_async_copy(ys_ref.at[pl.ds(d_ref[kk, t], 1)],
                                      buf.at[s, pl.ds(kk * tt + t, 1)], sem.at[s]).start()
            return carry
        lax.fori_loop(0, tt, start, 0, unroll=8)

    @pl.when(i == 0)
    def _():
        gather(dest_ref, 0)

    @pl.when(i + 1 < pl.num_programs(0))
    def _():
        gather(dest_next_ref, 1 - slot)

    pltpu.make_async_copy(ys_ref.at[pl.ds(0, TOP_K * tt)], buf.at[slot], sem.at[slot]).wait()
    half = buf.shape[2]
    gates = gate_ref[...]
    acc_lo = x1_ref[:, :half]
    acc_hi = x1_ref[:, half:]
    for kk in range(TOP_K):
        u = buf[slot, pl.ds(kk * tt, tt), :]
        g = gates[:, kk:kk + 1]
        acc_lo = acc_lo + pltpu.bitcast(u << 16, F32) * g
        acc_hi = acc_hi + pltpu.bitcast(u & jnp.uint32(0xFFFF0000), F32) * g
    ms = (jnp.sum(acc_lo * acc_lo, axis=-1, keepdims=True)
          + jnp.sum(acc_hi * acc_hi, axis=-1, keepdims=True)) / (2 * half)
    inv = lax.rsqrt(ms + NORM_EPS)
    o_ref[:, :half] = acc_lo * inv * fn_ref[:, :half]
    o_ref[:, half:] = acc_hi * inv * fn_ref[:, half:]


def _combine(dest, gates_t, x1, final_gain, ys, row_off):
    n, d = x1.shape
    tt = _tile(n, 256)
    assert row_off % tt == 0
    off = row_off // tt
    nt = n // tt
    return pl.pallas_call(
        _combine_kernel,
        grid=(nt,),
        in_specs=[
            pl.BlockSpec((TOP_K, tt), lambda i: (0, off + i), memory_space=pltpu.SMEM),
            pl.BlockSpec((TOP_K, tt), lambda i: (0, off + jnp.minimum(i + 1, nt - 1)), memory_space=pltpu.SMEM),
            pl.BlockSpec((tt, TOP_K), lambda i: (off + i, 0)),
            pl.BlockSpec((tt, d), lambda i: (i, 0)),
            pl.BlockSpec((1, d), lambda i: (0, 0)),
            pl.BlockSpec(memory_space=pl.ANY),
        ],
        out_specs=pl.BlockSpec((tt, d), lambda i: (i, 0)),
        out_shape=jax.ShapeDtypeStruct((n, d), F32),
        scratch_shapes=[pltpu.VMEM((2, TOP_K * tt, d // 2), jnp.uint32), pltpu.SemaphoreType.DMA((2,))],
        compiler_params=_params("arbitrary"),
        name="combine_rows",
    )(dest, dest, gates_t, x1, final_gain, ys)


def _rope_tables(seq_len):
    rows = seq_len // GRID_W
    row = jnp.repeat(jnp.arange(rows, dtype=F32), GRID_W)
    col = jnp.tile(jnp.arange(GRID_W, dtype=F32), rows)
    freqs = ROPE_THETA ** (-jnp.arange(ROPE_HALF, dtype=F32) / ROPE_HALF)
    ang_r = row[:, None] * freqs[None, :]
    ang_c = col[:, None] * freqs[None, :]
    cos = jnp.concatenate([jnp.cos(ang_r), jnp.cos(ang_r), jnp.cos(ang_c), jnp.cos(ang_c)], axis=1)
    sin = jnp.concatenate([-jnp.sin(ang_r), jnp.sin(ang_r), -jnp.sin(ang_c), jnp.sin(ang_c)], axis=1)
    return cos, sin


def kernel(x_prompt, x_sample, mix_norm, w_in, q_norm, k_norm, hg_lb_logits, hg_out_norm, w_up_attn,
           w_up_hgrn, w_out, ffn_norm, w_router, b_router, w_gate_up, b_gate_up, w_down, b_down, final_norm):
    assert mix_norm.shape[0] == 1, "single trunk layer"
    d = x_prompt.shape[-1]
    hg_w = HG_HEADS * HG_D
    n_exp = w_router.shape[-1]
    row_block = MOE_ROW_BLOCK
    streams = [(x.reshape(-1, d), x.shape[0], x.shape[1]) for x in (x_prompt, x_sample)]
    n_total = sum(x.shape[0] for x, _, _ in streams)

    lb = jnp.cumsum(jax.nn.softmax(hg_lb_logits.astype(F32), axis=1), axis=1)[:, 0]
    w_in_b = w_in[0].astype(BF16)
    w_ua_b = w_up_attn[0].astype(BF16)
    w_uh_b = w_up_hgrn[0].astype(BF16)
    w_out_b = w_out[0].astype(BF16)
    w_r_t = w_router[0].T
    mix_g = mix_norm[0].reshape(1, d)
    score_bound = (1.02 * LOG2_E * HEAD_DIM ** 0.5 * jnp.max(jnp.abs(q_norm[0])) * jnp.max(jnp.abs(k_norm[0])))
    score_bound = score_bound.astype(F32).reshape(1)

    x1s, idxs, gate_ts = [], [], []
    hp = None
    row_off = 0
    for x, batch, seq in streams:
        cos, sin = _rope_tables(seq)
        qkv, zh, gates = _norm_proj(x, mix_g, w_in_b, q_norm[0].reshape(1, HEAD_DIM),
                                    k_norm[0].reshape(1, HEAD_DIM), cos, sin, seq)
        attn = _attention(qkv, score_bound, batch, seq)
        hg = _hgrn(zh, lb[0:1], lb[1:2], hg_out_norm[0].reshape(1, hg_w), batch, seq)
        merged = _merge(attn, hg, gates, w_ua_b, w_uh_b)
        x1, hp, idx, gate = _outproj_router(x, merged, w_out_b, ffn_norm[0].reshape(1, d), w_r_t,
                                            b_router[0].reshape(n_exp, 1), hp, row_off, n_total)
        x1s.append(x1)
        idxs.append(idx)
        gate_ts.append(gate.T)
        row_off += x.shape[0]

    idx = jnp.concatenate(idxs, axis=1)
    gate_t = jnp.concatenate(gate_ts, axis=0)
    dest, counts = _route(idx, row_block)
    cnt = counts[:, 0].astype(jnp.int32)
    padded = (cnt + row_block - 1) // row_block * row_block
    pad_end = jnp.cumsum(padded)
    n_rows = n_total * TOP_K + n_exp * row_block
    n_blocks = n_rows // row_block
    blk_start = jnp.arange(n_blocks, dtype=jnp.int32) * row_block
    blk_e = jnp.minimum(jnp.sum(pad_end[None, :] <= blk_start[:, None], axis=1), n_exp - 1).astype(jnp.int32)
    meta = jnp.concatenate([(pad_end[-1:] // row_block).astype(jnp.int32), blk_e])
    seg = jnp.stack([pad_end - padded, pad_end]).astype(jnp.int32)

    xs = _dispatch(seg, dest, hp, n_rows, row_block)
    ys = _experts(meta, xs, w_gate_up[0].astype(BF16), b_gate_up[0].reshape(n_exp, 1, -1),
                  w_down[0].astype(BF16), b_down[0].reshape(n_exp, 1, d), row_block)

    outs = []
    row_off = 0
    for (x, batch, seq), x1 in zip(streams, x1s):
        out = _combine(dest, gate_t, x1, final_norm.reshape(1, d), ys, row_off)
        outs.append(out.reshape(batch, seq, d))
        row_off += x.shape[0]
    return tuple(outs)
```

```python
import functools

import jax
import jax.numpy as jnp
from jax import lax
from jax.experimental import pallas as pl
from jax.experimental.pallas import tpu as pltpu

GRID_W = 64
HEAD_DIM = 128
N_Q_HEADS = 16
N_KV_HEADS = 4
ROPE_THETA = 10000.0
ROPE_HALF = HEAD_DIM // 4
HG_HEADS = 8
HG_D = 128
HG_CHUNK = 64
N_EXPERTS = 32
TOP_K = 4
SWIGLU_LIMIT = 7.0
SWIGLU_ALPHA = 1.702
NORM_EPS = 1e-5

HG_SAFE_LOG_DECAY = -60.0

LOG2_E = 1.4426950408889634
ATTN_FIXED_SHIFT_LIMIT = 60.0

V7X_VMEM_BYTES = 64 * 1024 * 1024
VMEM_LIMIT_BYTES = V7X_VMEM_BYTES - 8 * 1024 * 1024
LANES = 128
MOE_ROW_BLOCK = 512

BF16 = jnp.bfloat16
F32 = jnp.float32


def _params(*sem):
    return pltpu.CompilerParams(dimension_semantics=sem, vmem_limit_bytes=VMEM_LIMIT_BYTES)


def _tile(n, pref):
    t = min(n, pref)
    while n % t:
        t //= 2
    return t


def _sigmoid(x):
    return 1.0 / (1.0 + jnp.exp(-x))


def _rmsnorm(x, g):
    return x * lax.rsqrt(jnp.mean(x * x, axis=-1, keepdims=True) + NORM_EPS) * g


def _dot(a, b):
    return jnp.dot(a, b, preferred_element_type=F32)


def _dot_nt(a, b):
    return lax.dot_general(a, b, (((1,), (1,)), ((), ())), preferred_element_type=F32)


def _split3(x):
    hi = x.astype(BF16)
    r = x - hi.astype(F32)
    mid = r.astype(BF16)
    lo = (r - mid.astype(F32)).astype(BF16)
    return hi, mid, lo


def _dot_exact_lhs(m_bf16, x, pieces=3):
    parts = _split3(x)[:pieces]
    acc = _dot(m_bf16, parts[0])
    for p in parts[1:]:
        acc = acc + _dot(m_bf16, p)
    return acc


def _rope_head_pair(zp, gain, cos, sin, ones_blk, perm_blk, scale):
    ss = _dot((zp * zp).astype(BF16), ones_blk)
    y = zp * lax.rsqrt(ss * (1.0 / HEAD_DIM) + NORM_EPS) * gain
    y_hi = y.astype(BF16)
    y_lo = (y - y_hi.astype(F32)).astype(BF16)
    partner = _dot(y_hi, perm_blk) + _dot(y_lo, perm_blk)
    return (y * cos + partner * sin) * scale


def _proj_kernel(x_ref, g_ref, w_ref, qn_ref, kn_ref, cos_ref, sin_ref, ones_ref, perm_ref,
                 qkv_ref, zh_ref, gate_ref, h_scr, *, tile_kinds):
    j = pl.program_id(1)

    @pl.when(j == 0)
    def _():
        h_scr[...] = _rmsnorm(x_ref[...], g_ref[...]).astype(BF16)

    z = _dot(h_scr[...], w_ref[...])

    def qkv_tile(kinds):
        two = lambda r: jnp.concatenate([r[...], r[...]], axis=1)
        outs = []
        for h in range(0, len(kinds), 2):
            kind = kinds[h]
            assert kinds[h + 1] == kind
            zp = z[:, h * HEAD_DIM:(h + 2) * HEAD_DIM]
            if kind == "q":
                zp = _rope_head_pair(zp, two(qn_ref), two(cos_ref), two(sin_ref), ones_ref[...], perm_ref[...],
                                     LOG2_E * HEAD_DIM ** -0.5)
            elif kind == "k":
                zp = _rope_head_pair(zp, two(kn_ref), two(cos_ref), two(sin_ref), ones_ref[...], perm_ref[...], 1.0)
            outs.append(zp)
        return jnp.concatenate(outs, axis=1)

    for lo, hi, kind in tile_kinds:
        @pl.when((j >= lo) & (j < hi))
        def _(kind=kind):
            if kind == "zh":
                zh_ref[...] = z
            elif kind == "gate":
                gate_ref[...] = z.astype(gate_ref.dtype)
            else:
                qkv_ref[...] = qkv_tile(kind).astype(qkv_ref.dtype)


def _norm_proj(x, gain, w, q_gain, k_gain, cos, sin, seq):
    n, d = x.shape
    attn_w = N_Q_HEADS * HEAD_DIM
    kv_w = N_KV_HEADS * HEAD_DIM
    hg5 = 5 * HG_HEADS * HG_D
    qkv_w = attn_w + 2 * kv_w
    assert w.shape[1] == qkv_w + hg5 + 2 * d
    tm = _tile(seq, 1024)
    tn = 1024
    while attn_w % tn or (2 * kv_w) % tn or hg5 % tn or (2 * d) % tn:
        tn //= 2
    assert tn % HEAD_DIM == 0
    heads = ["q"] * N_Q_HEADS + ["k"] * N_KV_HEADS + ["v"] * N_KV_HEADS
    hpt = tn // HEAD_DIM
    n_qkv, n_zh, n_gate = qkv_w // tn, hg5 // tn, 2 * d // tn
    tile_kinds = []
    for t in range(n_qkv):
        kind = tuple(heads[t * hpt:(t + 1) * hpt])
        if tile_kinds and tile_kinds[-1][2] == kind:
            tile_kinds[-1] = (tile_kinds[-1][0], t + 1, kind)
        else:
            tile_kinds.append((t, t + 1, kind))
    tile_kinds += [(n_qkv, n_qkv + n_zh, "zh"), (n_qkv + n_zh, n_qkv + n_zh + n_gate, "gate")]
    n_pos = seq // tm
    r = jnp.arange(2 * HEAD_DIM)
    same_head = (r[:, None] // HEAD_DIM) == (r[None, :] // HEAD_DIM)
    ones_blk = same_head.astype(BF16)
    partner_of = jnp.where((r % (2 * ROPE_HALF)) < ROPE_HALF, r + ROPE_HALF, r - ROPE_HALF)
    perm_blk = (r[:, None] == partner_of[None, :]).astype(BF16)
    const_spec = pl.BlockSpec((2 * HEAD_DIM, 2 * HEAD_DIM), lambda i, j: (0, 0))
    return pl.pallas_call(
        functools.partial(_proj_kernel, tile_kinds=tile_kinds),
        grid=(n // tm, n_qkv + n_zh + n_gate),
        in_specs=[
            pl.BlockSpec((tm, d), lambda i, j: (i, 0)),
            pl.BlockSpec((1, d), lambda i, j: (0, 0)),
            pl.BlockSpec((d, tn), lambda i, j: (0, j)),
            pl.BlockSpec((1, HEAD_DIM), lambda i, j: (0, 0)),
            pl.BlockSpec((1, HEAD_DIM), lambda i, j: (0, 0)),
            pl.BlockSpec((tm, HEAD_DIM), lambda i, j: (i % n_pos, 0)),
            pl.BlockSpec((tm, HEAD_DIM), lambda i, j: (i % n_pos, 0)),
            const_spec,
            const_spec,
        ],
        out_specs=[
            pl.BlockSpec((tm, tn), lambda i, j: (i, jnp.minimum(j, n_qkv - 1))),
            pl.BlockSpec((tm, tn), lambda i, j: (i, jnp.clip(j - n_qkv, 0, n_zh - 1))),
            pl.BlockSpec((tm, tn), lambda i, j: (i, jnp.clip(j - n_qkv - n_zh, 0, n_gate - 1))),
        ],
        out_shape=[
            jax.ShapeDtypeStruct((n, qkv_w), BF16),
            jax.ShapeDtypeStruct((n, hg5), F32),
            jax.ShapeDtypeStruct((n, 2 * d), BF16),
        ],
        scratch_shapes=[pltpu.VMEM((tm, d), BF16)],
        compiler_params=_params("parallel", "arbitrary"),
        name="norm_proj",
    )(x, gain, w, q_gain, k_gain, cos, sin, ones_blk, perm_blk)


def _attn_kernel(bound_ref, q_ref, k_ref, v_ref, wf_ref, wb_in_ref, o_ref, wb_ref, acc_scr, m_scr, *, tk, group):
    del wb_in_ref
    tq = q_ref.shape[0]
    seq = k_ref.shape[0]
    n_chunks = seq // tk
    slab = wf_ref.shape[0] // n_chunks
    q = jnp.concatenate([q_ref[:, g * HEAD_DIM:(g + 1) * HEAD_DIM] for g in range(group)], axis=0)
    ones = jnp.ones((tk, HEAD_DIM), BF16)
    bound = bound_ref[0]
    acc_scr[...] = jnp.zeros(acc_scr.shape, F32)

    def chunk(c):
        rows = pl.ds(pl.multiple_of(c * tk, tk), tk)
        s = _dot_nt(q, k_ref[rows, :])
        return s, jnp.concatenate([v_ref[rows, :], ones], axis=1)

    @pl.when(bound <= ATTN_FIXED_SHIFT_LIMIT)
    def _():
        def body(c, carry):
            s, v1 = chunk(c)
            acc_scr[...] += _dot(jnp.exp2(s - bound).astype(BF16), v1)
            wrows = pl.ds(pl.multiple_of(c * slab, slab), slab)
            wb_ref[wrows, :] = wf_ref[wrows, :].astype(BF16)
            return carry
        lax.fori_loop(0, n_chunks, body, 0, unroll=True)

    @pl.when(bound > ATTN_FIXED_SHIFT_LIMIT)
    def _():
        wb_ref[...] = wf_ref[...].astype(BF16)
        m_scr[...] = jnp.full(m_scr.shape, -1e30, F32)

        def body(c, carry):
            s, v1 = chunk(c)
            m_prev = m_scr[...]
            m_new = jnp.maximum(m_prev, jnp.max(s, axis=1, keepdims=True))
            alpha = jnp.exp2(m_prev - m_new)
            p = jnp.exp2(s - jnp.tile(m_new, (1, tk // LANES)))
            acc_scr[...] = jnp.tile(alpha, (1, 2)) * acc_scr[...] + _dot(p.astype(BF16), v1)
            m_scr[...] = m_new
            return carry
        lax.fori_loop(0, seq // tk, body, 0)

    acc = acc_scr[...]
    o = acc[:, :HEAD_DIM] / acc[:, HEAD_DIM:]
    for g in range(group):
        o_ref[:, g * HEAD_DIM:(g + 1) * HEAD_DIM] = o[g * tq:(g + 1) * tq].astype(o_ref.dtype)


def _attention(qkv, score_bound, batch, seq, w_f32, col_lo, col_hi, w_bf16_prev):
    group = N_Q_HEADS // N_KV_HEADS
    gw = group * HEAD_DIM
    tq = _tile(seq, 512)
    tk = _tile(seq, 512)
    nq = seq // tq
    n_steps = batch * N_KV_HEADS * nq
    w_rows, w_cols = w_f32.shape
    slab = w_rows // n_steps
    width = col_hi - col_lo
    assert slab * n_steps == w_rows and slab % (16 * (seq // tk)) == 0, (w_rows, n_steps, seq // tk)
    assert col_lo % width == 0 and width % LANES == 0
    wmap = lambda b, n, i: ((b * N_KV_HEADS + n) * nq + i, col_lo // width)
    if w_bf16_prev is None:
        w_bf16_prev = jnp.zeros((8, LANES), BF16)
        aliases = {}
    else:
        aliases = {5: 1}
    w_cols_block = width
    return pl.pallas_call(
        functools.partial(_attn_kernel, tk=tk, group=group),
        grid=(batch, N_KV_HEADS, nq),
        in_specs=[
            pl.BlockSpec(memory_space=pltpu.SMEM),
            pl.BlockSpec((tq, gw), lambda b, n, i: (b * nq + i, n)),
            pl.BlockSpec((seq, HEAD_DIM), lambda b, n, i: (b, N_Q_HEADS + n)),
            pl.BlockSpec((seq, HEAD_DIM), lambda b, n, i: (b, N_Q_HEADS + N_KV_HEADS + n)),
            pl.BlockSpec((slab, w_cols_block), wmap),
            pl.BlockSpec(memory_space=pl.ANY),
        ],
        out_specs=[pl.BlockSpec((tq, gw), lambda b, n, i: (b * nq + i, n)),
                   pl.BlockSpec((slab, w_cols_block), wmap)],
        out_shape=[jax.ShapeDtypeStruct((batch * seq, N_Q_HEADS * HEAD_DIM), BF16),
                   jax.ShapeDtypeStruct((w_rows, w_cols), BF16)],
        scratch_shapes=[
            pltpu.VMEM((group * tq, 2 * HEAD_DIM), F32),
            pltpu.VMEM((group * tq, LANES), F32),
        ],
        input_output_aliases=aliases,
        compiler_params=_params("parallel", "parallel", "arbitrary"),
        name="gqa_attention",
    )(score_bound, qkv, qkv, qkv, w_f32, w_bf16_prev)


def _hgrn_span(i, q_scr, b_scr, k_scr, zi_ref, vt_scr, o_scr, s_scr, tri, *, span, reverse, pairwise):
    C = HG_CHUNK
    nc = span // C
    r0 = pl.multiple_of(i * span, span)
    rows = pl.ds(r0, span)
    q = q_scr[rows, :]
    b = b_scr[rows, :]
    k = k_scr[rows, :]
    v = zi_ref[rows, :]
    vb = v.astype(BF16)
    end_row = 0 if reverse else C - 1
    ends = [b[c * C + end_row:c * C + end_row + 1, :] for c in range(nc)]
    b_end = jnp.concatenate([jnp.broadcast_to(e, (C, HG_D)) for e in ends], axis=0)
    qtb = (q * jnp.exp(b)).astype(BF16)
    kp = k * jnp.exp(b_end - b)

    if pairwise:
        rid = lax.broadcasted_iota(jnp.int32, (span, HG_D), 0)

        def pair(s, acc):
            cs = (s // C) * C
            if reverse:
                m = (rid <= s) & (rid >= cs)
            else:
                m = (rid >= s) & (rid < cs + C)
            w = jnp.where(m, jnp.exp(jnp.minimum(b - b_scr[pl.ds(r0 + s, 1), :], 0.0)), 0.0)
            r = jnp.sum(q * w * k_scr[pl.ds(r0 + s, 1), :], axis=1, keepdims=True)
            return acc + r * zi_ref[pl.ds(r0 + s, 1), :]

        o_intra = lax.fori_loop(0, span, pair, jnp.zeros((span, HG_D), F32))
    else:
        ktb = (k * jnp.exp(-b)).astype(BF16)
        a = jnp.where(tri, _dot_nt(qtb, ktb), 0.0)
        o_intra = _dot(a.astype(BF16), vb)

    chunk_of_row = lax.broadcasted_iota(jnp.int32, (span, HG_D), 0) // C
    kp_blocks = jnp.concatenate([jnp.where(chunk_of_row == c, kp, 0.0) for c in range(nc)], axis=1)
    upd = _dot(vt_scr[:, rows], kp_blocks.astype(BF16))

    st = s_scr[...]
    states = [None] * nc
    for c in (reversed(range(nc)) if reverse else range(nc)):
        states[c] = st.astype(BF16)
        st = st * jnp.exp(ends[c]) + upd[:, c * HG_D:(c + 1) * HG_D]
    s_scr[...] = st
    o_inter = _dot_nt(qtb, jnp.concatenate(states, axis=0))
    o_scr[rows, :] = o_intra + jnp.concatenate(
        [o_inter[c * C:(c + 1) * C, c * HG_D:(c + 1) * HG_D] for c in range(nc)], axis=0)


def _hgrn_kernel(zq_ref, zf_ref, zb_ref, zi_ref, zo_ref, lbf_ref, lbb_ref, on_ref, out_ref,
                 of_scr, ob_scr, sf_scr, sb_scr, q_scr, bf_scr, bb_scr, kf_scr, kb_scr, vt_scr, *, span):
    seq = zq_ref.shape[0]
    n_span = seq // span
    C = HG_CHUNK
    nc = span // C
    sf_scr[...] = jnp.zeros(sf_scr.shape, F32)
    sb_scr[...] = jnp.zeros(sb_scr.shape, F32)
    r = lax.broadcasted_iota(jnp.int32, (span, span), 0)
    c = lax.broadcasted_iota(jnp.int32, (span, span), 1)
    same = (r // C) == (c // C)
    tri_f = same & (c <= r)
    tri_b = same & (c >= r)
    tri_f16 = tri_f.astype(BF16)
    tri_b16 = tri_b.astype(BF16)

    def prepare(i, min_end):
        rows = pl.ds(pl.multiple_of(i * span, span), span)
        qh = zq_ref[rows, :]
        q_scr[rows, :] = qh * _sigmoid(qh)
        vt_scr[:, rows] = zi_ref[rows, :].T.astype(BF16)
        for z_ref, lb_ref, tri, b_scr, k_scr, end_row in (
                (zf_ref, lbf_ref, tri_f16, bf_scr, kf_scr, C - 1), (zb_ref, lbb_ref, tri_b16, bb_scr, kb_scr, 0)):
            lb = lb_ref[...]
            f = lb + (1.0 - lb) * _sigmoid(z_ref[rows, :])
            b = _dot_exact_lhs(tri, jnp.log(f), pieces=2)
            b_scr[rows, :] = b
            k_scr[rows, :] = 1.0 - f
            for cc in range(nc):
                min_end = jnp.minimum(min_end, b[cc * C + end_row:cc * C + end_row + 1, :])
        return min_end

    min_end = lax.fori_loop(0, n_span, prepare, jnp.zeros((1, HG_D), F32), unroll=2)
    safe = jnp.min(min_end) > HG_SAFE_LOG_DECAY

    def scan(pairwise):
        def body(i, carry):
            _hgrn_span(i, q_scr, bf_scr, kf_scr, zi_ref, vt_scr, of_scr, sf_scr, tri_f,
                       span=span, reverse=False, pairwise=pairwise)
            _hgrn_span(n_span - 1 - i, q_scr, bb_scr, kb_scr, zi_ref, vt_scr, ob_scr, sb_scr, tri_b,
                       span=span, reverse=True, pairwise=pairwise)
            return carry
        lax.fori_loop(0, n_span, body, 0, unroll=1 if pairwise else 2)

    @pl.when(safe)
    def _():
        scan(False)

    @pl.when(jnp.logical_not(safe))
    def _():
        scan(True)

    def finish(i, carry):
        rows = pl.ds(pl.multiple_of(i * span, span), span)
        o = _rmsnorm(of_scr[rows, :] + ob_scr[rows, :], on_ref[...])
        og = zo_ref[rows, :]
        out_ref[rows, :] = (o * (og * _sigmoid(og))).astype(out_ref.dtype)
        return carry

    lax.fori_loop(0, n_span, finish, 0)


def _hgrn(zh, lb_f, lb_b, out_norm, batch, seq):
    span = _tile(seq, 256)
    assert span % HG_CHUNK == 0
    zspec = lambda grp: pl.BlockSpec((seq, HG_D), lambda b, h: (b, grp * HG_HEADS + h))
    hspec = pl.BlockSpec((1, HG_D), lambda b, h: (0, h))
    seq_buf = pltpu.VMEM((seq, HG_D), F32)
    return pl.pallas_call(
        functools.partial(_hgrn_kernel, span=span),
        grid=(batch, HG_HEADS),
        in_specs=[zspec(0), zspec(1), zspec(2), zspec(3), zspec(4), hspec, hspec, hspec],
        out_specs=pl.BlockSpec((seq, HG_D), lambda b, h: (b, h)),
        out_shape=jax.ShapeDtypeStruct((batch * seq, HG_HEADS * HG_D), BF16),
        scratch_shapes=[
            seq_buf, seq_buf,
            pltpu.VMEM((HG_D, HG_D), F32), pltpu.VMEM((HG_D, HG_D), F32),
            seq_buf, seq_buf, seq_buf, seq_buf, seq_buf,
            pltpu.VMEM((HG_D, seq), BF16),
        ],
        compiler_params=_params("parallel", "parallel"),
        name="hgrn2",
    )(zh, zh, zh, zh, zh, lb_f, lb_b, out_norm)


def _merge_kernel(a_ref, h_ref, ga_ref, gb_ref, wa_ref, wh_ref, o_ref):
    ya = _dot(a_ref[...], wa_ref[...])
    yb = _dot(h_ref[...], wh_ref[...])
    merged = _sigmoid(ga_ref[...].astype(F32)) * ya + _sigmoid(gb_ref[...].astype(F32)) * yb
    o_ref[...] = merged.astype(o_ref.dtype)


def _merge(attn, hg, gates, w_up_attn, w_up_hgrn):
    n, wa = attn.shape
    wh = hg.shape[1]
    d = w_up_attn.shape[1]
    tm = _tile(n, 1024)
    tn = _tile(d, 512)
    nj = d // tn
    return pl.pallas_call(
        _merge_kernel,
        grid=(n // tm, nj),
        in_specs=[
            pl.BlockSpec((tm, wa), lambda i, j: (i, 0)),
            pl.BlockSpec((tm, wh), lambda i, j: (i, 0)),
            pl.BlockSpec((tm, tn), lambda i, j: (i, j)),
            pl.BlockSpec((tm, tn), lambda i, j: (i, nj + j)),
            pl.BlockSpec((wa, tn), lambda i, j: (0, j)),
            pl.BlockSpec((wh, tn), lambda i, j: (0, j)),
        ],
        out_specs=pl.BlockSpec((tm, tn), lambda i, j: (i, j)),
        out_shape=jax.ShapeDtypeStruct((n, d), BF16),
        compiler_params=_params("parallel", "arbitrary"),
        name="gated_merge",
    )(attn, hg, gates, gates, w_up_attn, w_up_hgrn)


def _pack_bf16_pairs(h):
    half = h.shape[1] // 2
    lo = pltpu.bitcast(h[:, :half].astype(BF16).astype(F32), jnp.uint32)
    hi = pltpu.bitcast(h[:, half:].astype(BF16).astype(F32), jnp.uint32)
    return (hi & jnp.uint32(0xFFFF0000)) | (lo >> 16)


def _unpack_bf16_pairs(u):
    lo = pltpu.bitcast(u << 16, F32).astype(BF16)
    hi = pltpu.bitcast(u & jnp.uint32(0xFFFF0000), F32).astype(BF16)
    return lo, hi


def _outproj_router_kernel(x_ref, m_ref, w_ref, g_ref, wr_ref, br_ref, hp_in_ref,
                           x1_ref, hp_ref, idx_ref, gate_ref):
    del hp_in_ref
    x1 = x_ref[...] + _dot(m_ref[...], w_ref[...])
    x1_ref[...] = x1
    h = _rmsnorm(x1, g_ref[...])
    hp_ref[...] = _pack_bf16_pairs(h)
    h_hi, h_mid, _ = _split3(h)
    w_hi, w_mid, _ = _split3(wr_ref[...])
    lg = (_dot_nt(w_hi, h_hi) + _dot_nt(w_hi, h_mid) + _dot_nt(w_mid, h_hi)) + br_ref[...]
    n_exp, tm = lg.shape
    eid = lax.broadcasted_iota(jnp.int32, (n_exp, tm), 0)
    vals = []
    for kk in range(TOP_K):
        m = jnp.max(lg, axis=0, keepdims=True)
        sel = jnp.min(jnp.where(lg == m, eid, n_exp), axis=0, keepdims=True)
        idx_ref[kk:kk + 1, :] = sel
        vals.append(m)
        lg = jnp.where(eid == sel, -jnp.inf, lg)
    ex = [jnp.exp(vv - vals[0]) for vv in vals]
    den = ex[0]
    for e in ex[1:]:
        den = den + e
    for kk in range(TOP_K):
        gate_ref[kk:kk + 1, :] = ex[kk] / den


def _outproj_router(x, merged, w_out, ffn_gain, w_router_t, b_router, hp_prev, row_off, n_total):
    n, d = x.shape
    tm = _tile(n, 512)
    assert row_off % tm == 0
    n_exp = w_router_t.shape[0]
    if hp_prev is None:
        hp_prev = jnp.zeros((8, LANES), jnp.uint32)
        aliases = {}
    else:
        aliases = {6: 1}
    return pl.pallas_call(
        _outproj_router_kernel,
        grid=(n // tm,),
        in_specs=[
            pl.BlockSpec((tm, d), lambda i: (i, 0)),
            pl.BlockSpec((tm, d), lambda i: (i, 0)),
            pl.BlockSpec((d, d), lambda i: (0, 0)),
            pl.BlockSpec((1, d), lambda i: (0, 0)),
            pl.BlockSpec((n_exp, d), lambda i: (0, 0)),
            pl.BlockSpec((n_exp, 1), lambda i: (0, 0)),
            pl.BlockSpec(memory_space=pl.ANY),
        ],
        out_specs=[
            pl.BlockSpec((tm, d), lambda i: (i, 0)),
            pl.BlockSpec((tm, d // 2), lambda i: (row_off // tm + i, 0)),
            pl.BlockSpec((TOP_K, tm), lambda i: (0, i)),
            pl.BlockSpec((TOP_K, tm), lambda i: (0, i)),
        ],
        out_shape=[
            jax.ShapeDtypeStruct((n, d), F32),
            jax.ShapeDtypeStruct((n_total, d // 2), jnp.uint32),
            jax.ShapeDtypeStruct((TOP_K, n), jnp.int32),
            jax.ShapeDtypeStruct((TOP_K, n), F32),
        ],
        input_output_aliases=aliases,
        compiler_params=_params("parallel"),
        name="outproj_router",
    )(x, merged, w_out, ffn_gain, w_router_t, b_router, hp_prev)


def _route_kernel(idx_ref, dest_ref, cnt_ref, cnt_scr, base_scr, *, row_block):
    phase = pl.program_id(0)
    i = pl.program_id(1)
    n_exp = cnt_scr.shape[0]
    tt = idx_ref.shape[1]
    eid = lax.broadcasted_iota(jnp.int32, (n_exp, tt), 0)
    onehot = [(eid == idx_ref[kk:kk + 1, :]) for kk in range(TOP_K)]

    @pl.when((phase == 0) & (i == 0))
    def _():
        cnt_scr[...] = jnp.zeros(cnt_scr.shape, F32)

    @pl.when(phase == 0)
    def _():
        tot = onehot[0].astype(F32)
        for oh in onehot[1:]:
            tot = tot + oh.astype(F32)
        cnt_scr[...] = cnt_scr[...] + jnp.sum(tot, axis=1, keepdims=True)
        cnt_ref[...] = cnt_scr[...]

    @pl.when((phase == 1) & (i == 0))
    def _():
        cnt = cnt_scr[...].astype(jnp.int32)
        padded = ((cnt + (row_block - 1)) // row_block * row_block).astype(F32)
        er = lax.broadcasted_iota(jnp.int32, (n_exp, n_exp), 0)
        ec = lax.broadcasted_iota(jnp.int32, (n_exp, n_exp), 1)
        base_scr[...] = _dot_exact_lhs((ec < er).astype(BF16), padded)

    @pl.when(phase == 1)
    def _():
        tr = lax.broadcasted_iota(jnp.int32, (tt, tt), 0)
        tc = lax.broadcasted_iota(jnp.int32, (tt, tt), 1)
        before = (tr < tc).astype(BF16)
        run = base_scr[...][:, :1]
        for kk in range(TOP_K):
            oh = onehot[kk].astype(F32)
            rank = _dot(oh.astype(BF16), before) + run
            dest_ref[kk:kk + 1, :] = jnp.sum(oh * rank, axis=0, keepdims=True).astype(jnp.int32)
            run = run + jnp.sum(oh, axis=1, keepdims=True)
        base_scr[...] = jnp.broadcast_to(run, base_scr.shape)


def _route(idx, row_block):
    n = idx.shape[1]
    tt = _tile(n, 512)
    return pl.pallas_call(
        functools.partial(_route_kernel, row_block=row_block),
        grid=(2, n // tt),
        in_specs=[pl.BlockSpec((TOP_K, tt), lambda p, i: (0, i))],
        out_specs=[
            pl.BlockSpec((TOP_K, tt), lambda p, i: (0, i * p)),
            pl.BlockSpec((N_EXPERTS, LANES), lambda p, i: (0, 0)),
        ],
        out_shape=[
            jax.ShapeDtypeStruct((TOP_K, n), jnp.int32),
            jax.ShapeDtypeStruct((N_EXPERTS, LANES), F32),
        ],
        scratch_shapes=[pltpu.VMEM((N_EXPERTS, LANES), F32), pltpu.VMEM((N_EXPERTS, LANES), F32)],
        compiler_params=_params("arbitrary", "arbitrary"),
        name="route_offsets",
    )(idx)


def _dispatch_kernel(seg_ref, dest_ref, h_ref, xs_ref, zero_scr, sem, zsem, *, row_block):
    tt = dest_ref.shape[1]
    n_exp = seg_ref.shape[1]

    @pl.when(pl.program_id(0) == 0)
    def _():
        zero_scr[...] = jnp.zeros(zero_scr.shape, zero_scr.dtype)

        def zero_copy(e):
            start = pl.multiple_of(seg_ref[1, e] - row_block, row_block)
            return pltpu.make_async_copy(zero_scr, xs_ref.at[pl.ds(start, row_block)], zsem)

        for e in range(n_exp):
            @pl.when(seg_ref[1, e] > seg_ref[0, e])
            def _(e=e):
                zero_copy(e).start()
        for e in range(n_exp):
            @pl.when(seg_ref[1, e] > seg_ref[0, e])
            def _(e=e):
                zero_copy(e).wait()

    def row_copy(t, kk):
        return pltpu.make_async_copy(h_ref.at[pl.ds(t, 1)], xs_ref.at[pl.ds(dest_ref[kk, t], 1)], sem)

    def start(t, carry):
        for kk in range(TOP_K):
            row_copy(t, kk).start()
        return carry

    lax.fori_loop(0, tt, start, 0, unroll=8)
    pltpu.make_async_copy(xs_ref.at[pl.ds(0, TOP_K * tt)], xs_ref.at[pl.ds(0, TOP_K * tt)], sem).wait()


def _dispatch(seg, dest, hp, n_rows, row_block):
    n, w = hp.shape
    tt = _tile(n, 512)
    grid_spec = pltpu.PrefetchScalarGridSpec(
        num_scalar_prefetch=1,
        grid=(n // tt,),
        in_specs=[
            pl.BlockSpec((TOP_K, tt), lambda i, s: (0, i), memory_space=pltpu.SMEM),
            pl.BlockSpec((tt, w), lambda i, s: (i, 0)),
        ],
        out_specs=pl.BlockSpec(memory_space=pl.ANY),
        scratch_shapes=[pltpu.VMEM((row_block, w), hp.dtype), pltpu.SemaphoreType.DMA(()),
                        pltpu.SemaphoreType.DMA(())],
    )
    return pl.pallas_call(
        functools.partial(_dispatch_kernel, row_block=row_block),
        grid_spec=grid_spec,
        out_shape=jax.ShapeDtypeStruct((n_rows, w), hp.dtype),
        compiler_params=_params("arbitrary"),
        name="dispatch_rows",
    )(seg, dest, hp)


def _expert_kernel(meta_ref, xs_ref, wg_ref, wl_ref, bg_ref, bl_ref, wd_ref, bd_ref, ys_ref,
                   x_scr, y_ref):
    i = pl.program_id(0)
    j = pl.program_id(1)

    @pl.when(i < meta_ref[0])
    def _():
        @pl.when(j == 0)
        def _():
            half = xs_ref.shape[1]
            lo, hi = _unpack_bf16_pairs(xs_ref[...])
            x_scr[:, :half] = lo
            x_scr[:, half:] = hi
            y_ref[...] = jnp.broadcast_to(bd_ref[...], y_ref.shape)

        x = x_scr[...]
        glu = _dot(x, wg_ref[...]) + bg_ref[...]
        lin = _dot(x, wl_ref[...]) + bl_ref[...]
        glu = jnp.minimum(glu, SWIGLU_LIMIT)
        lin = jnp.clip(lin, -SWIGLU_LIMIT, SWIGLU_LIMIT)
        act = glu * _sigmoid(SWIGLU_ALPHA * glu) * (lin + 1.0)
        y_ref[...] = y_ref[...] + _dot(act.astype(BF16), wd_ref[...].astype(BF16))

        @pl.when(j == pl.num_programs(1) - 1)
        def _():
            ys_ref[...] = _pack_bf16_pairs(y_ref[...])


def _experts(meta, xs, w_gu, b_gu, w_dn, b_dn, row_block):
    n_rows, half = xs.shape
    d = 2 * half
    d_ff = w_dn.shape[1]
    tf = _tile(d_ff, 1024)
    nf = d_ff // tf
    n_blocks = n_rows // row_block

    def jj(i, j, m):
        return jnp.where(i < m[0], j, nf - 1)

    grid_spec = pltpu.PrefetchScalarGridSpec(
        num_scalar_prefetch=1,
        grid=(n_blocks, nf),
        in_specs=[
            pl.BlockSpec((row_block, half), lambda i, j, m: (i, 0)),
            pl.BlockSpec((None, d, tf), lambda i, j, m: (m[1 + i], 0, jj(i, j, m))),
            pl.BlockSpec((None, d, tf), lambda i, j, m: (m[1 + i], 0, nf + jj(i, j, m))),
            pl.BlockSpec((None, 1, tf), lambda i, j, m: (m[1 + i], 0, jj(i, j, m))),
            pl.BlockSpec((None, 1, tf), lambda i, j, m: (m[1 + i], 0, nf + jj(i, j, m))),
            pl.BlockSpec((None, tf, d), lambda i, j, m: (m[1 + i], jj(i, j, m), 0)),
            pl.BlockSpec((None, 1, d), lambda i, j, m: (m[1 + i], 0, 0)),
        ],
        out_specs=pl.BlockSpec((row_block, half), lambda i, j, m: (i, 0)),
        scratch_shapes=[pltpu.VMEM((row_block, d), BF16), pltpu.VMEM((row_block, d), F32)],
    )
    return pl.pallas_call(
        _expert_kernel,
        grid_spec=grid_spec,
        out_shape=jax.ShapeDtypeStruct((n_rows, half), jnp.uint32),
        compiler_params=_params("arbitrary", "arbitrary"),
        name="expert_swiglu",
    )(meta, xs, w_gu, w_gu, b_gu, b_gu, w_dn, b_dn)


def _combine_kernel(dest_ref, dest_next_ref, gate_ref, x1_ref, fn_ref, ys_ref, o_ref, buf, sem):
    tt = dest_ref.shape[1]
    i = pl.program_id(0)
    slot = i % 2

    def gather(d_ref, s):
        def start(t, carry):
            for kk in range(TOP_K):
                pltpu.make_async_copy(ys_ref.at[pl.ds(d_ref[kk, t], 1)],
                                      buf.at[s, pl.ds(kk * tt + t, 1)], sem.at[s]).start()
            return carry
        lax.fori_loop(0, tt, start, 0, unroll=8)

    @pl.when(i == 0)
    def _():
        gather(dest_ref, 0)

    @pl.when(i + 1 < pl.num_programs(0))
    def _():
        gather(dest_next_ref, 1 - slot)

    pltpu.make_async_copy(ys_ref.at[pl.ds(0, TOP_K * tt)], buf.at[slot], sem.at[slot]).wait()
    half = buf.shape[2]
    gates = gate_ref[...]
    acc_lo = x1_ref[:, :half]
    acc_hi = x1_ref[:, half:]
    for kk in range(TOP_K):
        u = buf[slot, pl.ds(kk * tt, tt), :]
        g = gates[:, kk:kk + 1]
        acc_lo = acc_lo + pltpu.bitcast(u << 16, F32) * g
        acc_hi = acc_hi + pltpu.bitcast(u & jnp.uint32(0xFFFF0000), F32) * g
    ms = (jnp.sum(acc_lo * acc_lo, axis=-1, keepdims=True)
          + jnp.sum(acc_hi * acc_hi, axis=-1, keepdims=True)) / (2 * half)
    inv = lax.rsqrt(ms + NORM_EPS)
    o_ref[:, :half] = acc_lo * inv * fn_ref[:, :half]
    o_ref[:, half:] = acc_hi * inv * fn_ref[:, half:]


def _combine(dest, gates_t, x1, final_gain, ys, row_off):
    n, d = x1.shape
    tt = _tile(n, 256)
    assert row_off % tt == 0
    off = row_off // tt
    nt = n // tt
    return pl.pallas_call(
        _combine_kernel,
        grid=(nt,),
        in_specs=[
            pl.BlockSpec((TOP_K, tt), lambda i: (0, off + i), memory_space=pltpu.SMEM),
            pl.BlockSpec((TOP_K, tt), lambda i: (0, off + jnp.minimum(i + 1, nt - 1)), memory_space=pltpu.SMEM),
            pl.BlockSpec((tt, TOP_K), lambda i: (off + i, 0)),
            pl.BlockSpec((tt, d), lambda i: (i, 0)),
            pl.BlockSpec((1, d), lambda i: (0, 0)),
            pl.BlockSpec(memory_space=pl.ANY),
        ],
        out_specs=pl.BlockSpec((tt, d), lambda i: (i, 0)),
        out_shape=jax.ShapeDtypeStruct((n, d), F32),
        scratch_shapes=[pltpu.VMEM((2, TOP_K * tt, d // 2), jnp.uint32), pltpu.SemaphoreType.DMA((2,))],
        compiler_params=_params("arbitrary"),
        name="combine_rows",
    )(dest, dest, gates_t, x1, final_gain, ys)


def _rope_tables(seq_len):
    rows = seq_len // GRID_W
    row = jnp.repeat(jnp.arange(rows, dtype=F32), GRID_W)
    col = jnp.tile(jnp.arange(GRID_W, dtype=F32), rows)
    freqs = ROPE_THETA ** (-jnp.arange(ROPE_HALF, dtype=F32) / ROPE_HALF)
    ang_r = row[:, None] * freqs[None, :]
    ang_c = col[:, None] * freqs[None, :]
    cos = jnp.concatenate([jnp.cos(ang_r), jnp.cos(ang_r), jnp.cos(ang_c), jnp.cos(ang_c)], axis=1)
    sin = jnp.concatenate([-jnp.sin(ang_r), jnp.sin(ang_r), -jnp.sin(ang_c), jnp.sin(ang_c)], axis=1)
    return cos, sin


def kernel(x_prompt, x_sample, mix_norm, w_in, q_norm, k_norm, hg_lb_logits, hg_out_norm, w_up_attn,
           w_up_hgrn, w_out, ffn_norm, w_router, b_router, w_gate_up, b_gate_up, w_down, b_down, final_norm):
    assert mix_norm.shape[0] == 1, "single trunk layer"
    d = x_prompt.shape[-1]
    hg_w = HG_HEADS * HG_D
    n_exp = w_router.shape[-1]
    row_block = MOE_ROW_BLOCK
    streams = [(x.reshape(-1, d), x.shape[0], x.shape[1]) for x in (x_prompt, x_sample)]
    n_total = sum(x.shape[0] for x, _, _ in streams)

    lb = jnp.cumsum(jax.nn.softmax(hg_lb_logits.astype(F32), axis=1), axis=1)[:, 0]
    w_in_b = w_in[0].astype(BF16)
    w_ua_b = w_up_attn[0].astype(BF16)
    w_uh_b = w_up_hgrn[0].astype(BF16)
    w_out_b = w_out[0].astype(BF16)
    w_r_t = w_router[0].T
    mix_g = mix_norm[0].reshape(1, d)
    score_bound = (1.02 * LOG2_E * HEAD_DIM ** 0.5 * jnp.max(jnp.abs(q_norm[0])) * jnp.max(jnp.abs(k_norm[0])))
    score_bound = score_bound.astype(F32).reshape(1)

    w_gu2d = w_gate_up[0].reshape(-1, w_gate_up.shape[-1])
    quarter = w_gu2d.shape[1] // 4
    cast_cols = [(3 * quarter, 4 * quarter), (0, 3 * quarter)]
    w_gu_b = None

    x1s, idxs, gate_ts = [], [], []
    hp = None
    row_off = 0
    for (x, batch, seq), (col_lo, col_hi) in zip(streams, cast_cols):
        cos, sin = _rope_tables(seq)
        qkv, zh, gates = _norm_proj(x, mix_g, w_in_b, q_norm[0].reshape(1, HEAD_DIM),
                                    k_norm[0].reshape(1, HEAD_DIM), cos, sin, seq)
        attn, w_gu_b = _attention(qkv, score_bound, batch, seq, w_gu2d, col_lo, col_hi, w_gu_b)
        hg = _hgrn(zh, lb[0:1], lb[1:2], hg_out_norm[0].reshape(1, hg_w), batch, seq)
        merged = _merge(attn, hg, gates, w_ua_b, w_uh_b)
        x1, hp, idx, gate = _outproj_router(x, merged, w_out_b, ffn_norm[0].reshape(1, d), w_r_t,
                                            b_router[0].reshape(n_exp, 1), hp, row_off, n_total)
        x1s.append(x1)
        idxs.append(idx)
        gate_ts.append(gate.T)
        row_off += x.shape[0]

    idx = jnp.concatenate(idxs, axis=1)
    gate_t = jnp.concatenate(gate_ts, axis=0)
    dest, counts = _route(idx, row_block)
    cnt = counts[:, 0].astype(jnp.int32)
    padded = (cnt + row_block - 1) // row_block * row_block
    pad_end = jnp.cumsum(padded)
    n_rows = n_total * TOP_K + n_exp * row_block
    n_blocks = n_rows // row_block
    blk_start = jnp.arange(n_blocks, dtype=jnp.int32) * row_block
    blk_e = jnp.minimum(jnp.sum(pad_end[None, :] <= blk_start[:, None], axis=1), n_exp - 1).astype(jnp.int32)
    meta = jnp.concatenate([(pad_end[-1:] // row_block).astype(jnp.int32), blk_e])
    seg = jnp.stack([pad_end - padded, pad_end]).astype(jnp.int32)

    xs = _dispatch(seg, dest, hp, n_rows, row_block)
    ys = _experts(meta, xs, w_gu_b.reshape(w_gate_up.shape[1:]), b_gate_up[0].reshape(n_exp, 1, -1),
                  w_down[0], b_down[0].reshape(n_exp, 1, d), row_block)

    outs = []
    row_off = 0
    for (x, batch, seq), x1 in zip(streams, x1s):
        out = _combine(dest, gate_t, x1, final_norm.reshape(1, d), ys, row_off)
        outs.append(out.reshape(batch, seq, d))
        row_off += x.shape[0]
    return tuple(outs)
```

```python
import functools

import jax
import jax.numpy as jnp
from jax import lax
from jax.experimental import pallas as pl
from jax.experimental.pallas import tpu as pltpu

GRID_W = 64
HEAD_DIM = 128
N_Q_HEADS = 16
N_KV_HEADS = 4
ROPE_THETA = 10000.0
ROPE_HALF = HEAD_DIM // 4
HG_HEADS = 8
HG_D = 128
HG_CHUNK = 64
N_EXPERTS = 32
TOP_K = 4
SWIGLU_LIMIT = 7.0
SWIGLU_ALPHA = 1.702
NORM_EPS = 1e-5

HG_SAFE_LOG_DECAY = -60.0

LOG2_E = 1.4426950408889634
ATTN_FIXED_SHIFT_LIMIT = 60.0

V7X_VMEM_BYTES = 64 * 1024 * 1024
VMEM_LIMIT_BYTES = V7X_VMEM_BYTES - 8 * 1024 * 1024
LANES = 128
MOE_ROW_BLOCK = 512

BF16 = jnp.bfloat16
F32 = jnp.float32


def _params(*sem):
    return pltpu.CompilerParams(dimension_semantics=sem, vmem_limit_bytes=VMEM_LIMIT_BYTES)


def _tile(n, pref):
    t = min(n, pref)
    while n % t:
        t //= 2
    return t


def _sigmoid(x):
    return 1.0 / (1.0 + jnp.exp(-x))


def _rmsnorm(x, g):
    return x * lax.rsqrt(jnp.mean(x * x, axis=-1, keepdims=True) + NORM_EPS) * g


def _dot(a, b):
    return jnp.dot(a, b, preferred_element_type=F32)


def _dot_nt(a, b):
    return lax.dot_general(a, b, (((1,), (1,)), ((), ())), preferred_element_type=F32)


def _split3(x):
    hi = x.astype(BF16)
    r = x - hi.astype(F32)
    mid = r.astype(BF16)
    lo = (r - mid.astype(F32)).astype(BF16)
    return hi, mid, lo


def _dot_exact_lhs(m_bf16, x, pieces=3):
    parts = _split3(x)[:pieces]
    acc = _dot(m_bf16, parts[0])
    for p in parts[1:]:
        acc = acc + _dot(m_bf16, p)
    return acc


def _rope_head_pair(zp, gain, cos, sin, ones_blk, perm_blk, scale):
    ss = _dot((zp * zp).astype(BF16), ones_blk)
    y = zp * lax.rsqrt(ss * (1.0 / HEAD_DIM) + NORM_EPS) * gain
    y_hi = y.astype(BF16)
    y_lo = (y - y_hi.astype(F32)).astype(BF16)
    partner = _dot(y_hi, perm_blk) + _dot(y_lo, perm_blk)
    return (y * cos + partner * sin) * scale


def _proj_kernel(x_ref, g_ref, w_ref, qn_ref, kn_ref, cos_ref, sin_ref, ones_ref, perm_ref,
                 qkv_ref, zh_ref, gate_ref, h_scr, *, tile_kinds):
    j = pl.program_id(1)

    @pl.when(j == 0)
    def _():
        h_scr[...] = _rmsnorm(x_ref[...], g_ref[...]).astype(BF16)

    z = _dot(h_scr[...], w_ref[...])

    def qkv_tile(kinds):
        two = lambda r: jnp.concatenate([r[...], r[...]], axis=1)
        outs = []
        for h in range(0, len(kinds), 2):
            kind = kinds[h]
            assert kinds[h + 1] == kind
            zp = z[:, h * HEAD_DIM:(h + 2) * HEAD_DIM]
            if kind == "q":
                zp = _rope_head_pair(zp, two(qn_ref), two(cos_ref), two(sin_ref), ones_ref[...], perm_ref[...],
                                     LOG2_E * HEAD_DIM ** -0.5)
            elif kind == "k":
                zp = _rope_head_pair(zp, two(kn_ref), two(cos_ref), two(sin_ref), ones_ref[...], perm_ref[...], 1.0)
            outs.append(zp)
        return jnp.concatenate(outs, axis=1)

    for lo, hi, kind in tile_kinds:
        @pl.when((j >= lo) & (j < hi))
        def _(kind=kind):
            if kind == "zh":
                zh_ref[...] = z
            elif kind == "gate":
                gate_ref[...] = z.astype(gate_ref.dtype)
            else:
                qkv_ref[...] = qkv_tile(kind).astype(qkv_ref.dtype)


def _norm_proj(x, gain, w, q_gain, k_gain, cos, sin, seq):
    n, d = x.shape
    attn_w = N_Q_HEADS * HEAD_DIM
    kv_w = N_KV_HEADS * HEAD_DIM
    hg5 = 5 * HG_HEADS * HG_D
    qkv_w = attn_w + 2 * kv_w
    assert w.shape[1] == qkv_w + hg5 + 2 * d
    tm = _tile(seq, 1024)
    tn = 1024
    while attn_w % tn or (2 * kv_w) % tn or hg5 % tn or (2 * d) % tn:
        tn //= 2
    assert tn % HEAD_DIM == 0
    heads = ["q"] * N_Q_HEADS + ["k"] * N_KV_HEADS + ["v"] * N_KV_HEADS
    hpt = tn // HEAD_DIM
    n_qkv, n_zh, n_gate = qkv_w // tn, hg5 // tn, 2 * d // tn
    tile_kinds = []
    for t in range(n_qkv):
        kind = tuple(heads[t * hpt:(t + 1) * hpt])
        if tile_kinds and tile_kinds[-1][2] == kind:
            tile_kinds[-1] = (tile_kinds[-1][0], t + 1, kind)
        else:
            tile_kinds.append((t, t + 1, kind))
    tile_kinds += [(n_qkv, n_qkv + n_zh, "zh"), (n_qkv + n_zh, n_qkv + n_zh + n_gate, "gate")]
    n_pos = seq // tm
    r = jnp.arange(2 * HEAD_DIM)
    same_head = (r[:, None] // HEAD_DIM) == (r[None, :] // HEAD_DIM)
    ones_blk = same_head.astype(BF16)
    partner_of = jnp.where((r % (2 * ROPE_HALF)) < ROPE_HALF, r + ROPE_HALF, r - ROPE_HALF)
    perm_blk = (r[:, None] == partner_of[None, :]).astype(BF16)
    const_spec = pl.BlockSpec((2 * HEAD_DIM, 2 * HEAD_DIM), lambda i, j: (0, 0))
    return pl.pallas_call(
        functools.partial(_proj_kernel, tile_kinds=tile_kinds),
        grid=(n // tm, n_qkv + n_zh + n_gate),
        in_specs=[
            pl.BlockSpec((tm, d), lambda i, j: (i, 0)),
            pl.BlockSpec((1, d), lambda i, j: (0, 0)),
            pl.BlockSpec((d, tn), lambda i, j: (0, j)),
            pl.BlockSpec((1, HEAD_DIM), lambda i, j: (0, 0)),
            pl.BlockSpec((1, HEAD_DIM), lambda i, j: (0, 0)),
            pl.BlockSpec((tm, HEAD_DIM), lambda i, j: (i % n_pos, 0)),
            pl.BlockSpec((tm, HEAD_DIM), lambda i, j: (i % n_pos, 0)),
            const_spec,
            const_spec,
        ],
        out_specs=[
            pl.BlockSpec((tm, tn), lambda i, j: (i, jnp.minimum(j, n_qkv - 1))),
            pl.BlockSpec((tm, tn), lambda i, j: (i, jnp.clip(j - n_qkv, 0, n_zh - 1))),
            pl.BlockSpec((tm, tn), lambda i, j: (i, jnp.clip(j - n_qkv - n_zh, 0, n_gate - 1))),
        ],
        out_shape=[
            jax.ShapeDtypeStruct((n, qkv_w), BF16),
            jax.ShapeDtypeStruct((n, hg5), F32),
            jax.ShapeDtypeStruct((n, 2 * d), BF16),
        ],
        scratch_shapes=[pltpu.VMEM((tm, d), BF16)],
        compiler_params=_params("parallel", "arbitrary"),
        name="norm_proj",
    )(x, gain, w, q_gain, k_gain, cos, sin, ones_blk, perm_blk)


def _attn_kernel(bound_ref, q_ref, k_ref, v_ref, wf_ref, wb_in_ref, o_ref, wb_ref, acc_scr, m_scr, *, tk, group):
    del wb_in_ref
    tq = q_ref.shape[0]
    seq = k_ref.shape[0]
    n_chunks = seq // tk
    slab = wf_ref.shape[0] // n_chunks
    q = jnp.concatenate([q_ref[:, g * HEAD_DIM:(g + 1) * HEAD_DIM] for g in range(group)], axis=0)
    ones = jnp.ones((tk, HEAD_DIM), BF16)
    bound = bound_ref[0]
    acc_scr[...] = jnp.zeros(acc_scr.shape, F32)

    def chunk(c):
        rows = pl.ds(pl.multiple_of(c * tk, tk), tk)
        s = _dot_nt(q, k_ref[rows, :])
        return s, jnp.concatenate([v_ref[rows, :], ones], axis=1)

    @pl.when(bound <= ATTN_FIXED_SHIFT_LIMIT)
    def _():
        def body(c, carry):
            s, v1 = chunk(c)
            acc_scr[...] += _dot(jnp.exp2(s - bound).astype(BF16), v1)
            wrows = pl.ds(pl.multiple_of(c * slab, slab), slab)
            wb_ref[wrows, :] = wf_ref[wrows, :].astype(BF16)
            return carry
        lax.fori_loop(0, n_chunks, body, 0, unroll=True)

    @pl.when(bound > ATTN_FIXED_SHIFT_LIMIT)
    def _():
        wb_ref[...] = wf_ref[...].astype(BF16)
        m_scr[...] = jnp.full(m_scr.shape, -1e30, F32)

        def body(c, carry):
            s, v1 = chunk(c)
            m_prev = m_scr[...]
            m_new = jnp.maximum(m_prev, jnp.max(s, axis=1, keepdims=True))
            alpha = jnp.exp2(m_prev - m_new)
            p = jnp.exp2(s - jnp.tile(m_new, (1, tk // LANES)))
            acc_scr[...] = jnp.tile(alpha, (1, 2)) * acc_scr[...] + _dot(p.astype(BF16), v1)
            m_scr[...] = m_new
            return carry
        lax.fori_loop(0, seq // tk, body, 0)

    acc = acc_scr[...]
    o = acc[:, :HEAD_DIM] / acc[:, HEAD_DIM:]
    for g in range(group):
        o_ref[:, g * HEAD_DIM:(g + 1) * HEAD_DIM] = o[g * tq:(g + 1) * tq].astype(o_ref.dtype)


def _attention(qkv, score_bound, batch, seq, w_f32, col_lo, col_hi, w_bf16_prev):
    group = N_Q_HEADS // N_KV_HEADS
    gw = group * HEAD_DIM
    tq = _tile(seq, 512)
    tk = _tile(seq, 512)
    nq = seq // tq
    n_steps = batch * N_KV_HEADS * nq
    w_rows, w_cols = w_f32.shape
    slab = w_rows // n_steps
    width = col_hi - col_lo
    assert slab * n_steps == w_rows and slab % (16 * (seq // tk)) == 0, (w_rows, n_steps, seq // tk)
    assert col_lo % width == 0 and width % LANES == 0
    wmap = lambda b, n, i: ((b * N_KV_HEADS + n) * nq + i, col_lo // width)
    if w_bf16_prev is None:
        w_bf16_prev = jnp.zeros((8, LANES), BF16)
        aliases = {}
    else:
        aliases = {5: 1}
    w_cols_block = width
    return pl.pallas_call(
        functools.partial(_attn_kernel, tk=tk, group=group),
        grid=(batch, N_KV_HEADS, nq),
        in_specs=[
            pl.BlockSpec(memory_space=pltpu.SMEM),
            pl.BlockSpec((tq, gw), lambda b, n, i: (b * nq + i, n)),
            pl.BlockSpec((seq, HEAD_DIM), lambda b, n, i: (b, N_Q_HEADS + n)),
            pl.BlockSpec((seq, HEAD_DIM), lambda b, n, i: (b, N_Q_HEADS + N_KV_HEADS + n)),
            pl.BlockSpec((slab, w_cols_block), wmap),
            pl.BlockSpec(memory_space=pl.ANY),
        ],
        out_specs=[pl.BlockSpec((tq, gw), lambda b, n, i: (b * nq + i, n)),
                   pl.BlockSpec((slab, w_cols_block), wmap)],
        out_shape=[jax.ShapeDtypeStruct((batch * seq, N_Q_HEADS * HEAD_DIM), BF16),
                   jax.ShapeDtypeStruct((w_rows, w_cols), BF16)],
        scratch_shapes=[
            pltpu.VMEM((group * tq, 2 * HEAD_DIM), F32),
            pltpu.VMEM((group * tq, LANES), F32),
        ],
        input_output_aliases=aliases,
        compiler_params=_params("parallel", "parallel", "arbitrary"),
        name="gqa_attention",
    )(score_bound, qkv, qkv, qkv, w_f32, w_bf16_prev)


def _hgrn_span(i, q_scr, b_scr, k_scr, zi_ref, vt_scr, o_scr, s_scr, tri, *, span, reverse, pairwise):
    C = HG_CHUNK
    nc = span // C
    r0 = pl.multiple_of(i * span, span)
    rows = pl.ds(r0, span)
    q = q_scr[rows, :]
    b = b_scr[rows, :]
    k = k_scr[rows, :]
    v = zi_ref[rows, :]
    vb = v.astype(BF16)
    end_row = 0 if reverse else C - 1
    ends = [b[c * C + end_row:c * C + end_row + 1, :] for c in range(nc)]
    b_end = jnp.concatenate([jnp.broadcast_to(e, (C, HG_D)) for e in ends], axis=0)
    qtb = (q * jnp.exp(b)).astype(BF16)
    kp = k * jnp.exp(b_end - b)

    if pairwise:
        rid = lax.broadcasted_iota(jnp.int32, (span, HG_D), 0)

        def pair(s, acc):
            cs = (s // C) * C
            if reverse:
                m = (rid <= s) & (rid >= cs)
            else:
                m = (rid >= s) & (rid < cs + C)
            w = jnp.where(m, jnp.exp(jnp.minimum(b - b_scr[pl.ds(r0 + s, 1), :], 0.0)), 0.0)
            r = jnp.sum(q * w * k_scr[pl.ds(r0 + s, 1), :], axis=1, keepdims=True)
            return acc + r * zi_ref[pl.ds(r0 + s, 1), :]

        o_intra = lax.fori_loop(0, span, pair, jnp.zeros((span, HG_D), F32))
    else:
        ktb = (k * jnp.exp(-b)).astype(BF16)
        a = jnp.where(tri, _dot_nt(qtb, ktb), 0.0)
        o_intra = _dot(a.astype(BF16), vb)

    chunk_of_row = lax.broadcasted_iota(jnp.int32, (span, HG_D), 0) // C
    kp_blocks = jnp.concatenate([jnp.where(chunk_of_row == c, kp, 0.0) for c in range(nc)], axis=1)
    upd = _dot(vt_scr[:, rows], kp_blocks.astype(BF16))

    st = s_scr[...]
    states = [None] * nc
    for c in (reversed(range(nc)) if reverse else range(nc)):
        states[c] = st.astype(BF16)
        st = st * jnp.exp(ends[c]) + upd[:, c * HG_D:(c + 1) * HG_D]
    s_scr[...] = st
    o_inter = _dot_nt(qtb, jnp.concatenate(states, axis=0))
    o_scr[rows, :] = o_intra + jnp.concatenate(
        [o_inter[c * C:(c + 1) * C, c * HG_D:(c + 1) * HG_D] for c in range(nc)], axis=0)


def _hgrn_kernel(zq_ref, zf_ref, zb_ref, zi_ref, zo_ref, lbf_ref, lbb_ref, on_ref, out_ref,
                 of_scr, ob_scr, sf_scr, sb_scr, q_scr, bf_scr, bb_scr, kf_scr, kb_scr, vt_scr, *, span):
    seq = zq_ref.shape[0]
    n_span = seq // span
    C = HG_CHUNK
    nc = span // C
    sf_scr[...] = jnp.zeros(sf_scr.shape, F32)
    sb_scr[...] = jnp.zeros(sb_scr.shape, F32)
    r = lax.broadcasted_iota(jnp.int32, (span, span), 0)
    c = lax.broadcasted_iota(jnp.int32, (span, span), 1)
    same = (r // C) == (c // C)
    tri_f = same & (c <= r)
    tri_b = same & (c >= r)
    tri_f16 = tri_f.astype(BF16)
    tri_b16 = tri_b.astype(BF16)

    def prepare(i, min_end):
        rows = pl.ds(pl.multiple_of(i * span, span), span)
        qh = zq_ref[rows, :]
        q_scr[rows, :] = qh * _sigmoid(qh)
        vt_scr[:, rows] = zi_ref[rows, :].T.astype(BF16)
        for z_ref, lb_ref, tri, b_scr, k_scr, end_row in (
                (zf_ref, lbf_ref, tri_f16, bf_scr, kf_scr, C - 1), (zb_ref, lbb_ref, tri_b16, bb_scr, kb_scr, 0)):
            lb = lb_ref[...]
            f = lb + (1.0 - lb) * _sigmoid(z_ref[rows, :])
            b = _dot_exact_lhs(tri, jnp.log(f), pieces=2)
            b_scr[rows, :] = b
            k_scr[rows, :] = 1.0 - f
            for cc in range(nc):
                min_end = jnp.minimum(min_end, b[cc * C + end_row:cc * C + end_row + 1, :])
        return min_end

    min_end = lax.fori_loop(0, n_span, prepare, jnp.zeros((1, HG_D), F32), unroll=2)
    safe = jnp.min(min_end) > HG_SAFE_LOG_DECAY

    def scan(pairwise):
        def body(i, carry):
            _hgrn_span(i, q_scr, bf_scr, kf_scr, zi_ref, vt_scr, of_scr, sf_scr, tri_f,
                       span=span, reverse=False, pairwise=pairwise)
            _hgrn_span(n_span - 1 - i, q_scr, bb_scr, kb_scr, zi_ref, vt_scr, ob_scr, sb_scr, tri_b,
                       span=span, reverse=True, pairwise=pairwise)
            return carry
        lax.fori_loop(0, n_span, body, 0, unroll=1 if pairwise else 2)

    @pl.when(safe)
    def _():
        scan(False)

    @pl.when(jnp.logical_not(safe))
    def _():
        scan(True)

    def finish(i, carry):
        rows = pl.ds(pl.multiple_of(i * span, span), span)
        o = _rmsnorm(of_scr[rows, :] + ob_scr[rows, :], on_ref[...])
        og = zo_ref[rows, :]
        out_ref[rows, :] = (o * (og * _sigmoid(og))).astype(out_ref.dtype)
        return carry

    lax.fori_loop(0, n_span, finish, 0)


def _hgrn(zh, lb_f, lb_b, out_norm, batch, seq):
    span = _tile(seq, 256)
    assert span % HG_CHUNK == 0
    zspec = lambda grp: pl.BlockSpec((seq, HG_D), lambda b, h: (b, grp * HG_HEADS + h))
    hspec = pl.BlockSpec((1, HG_D), lambda b, h: (0, h))
    seq_buf = pltpu.VMEM((seq, HG_D), F32)
    return pl.pallas_call(
        functools.partial(_hgrn_kernel, span=span),
        grid=(batch, HG_HEADS),
        in_specs=[zspec(0), zspec(1), zspec(2), zspec(3), zspec(4), hspec, hspec, hspec],
        out_specs=pl.BlockSpec((seq, HG_D), lambda b, h: (b, h)),
        out_shape=jax.ShapeDtypeStruct((batch * seq, HG_HEADS * HG_D), BF16),
        scratch_shapes=[
            seq_buf, seq_buf,
            pltpu.VMEM((HG_D, HG_D), F32), pltpu.VMEM((HG_D, HG_D), F32),
            seq_buf, seq_buf, seq_buf, seq_buf, seq_buf,
            pltpu.VMEM((HG_D, seq), BF16),
        ],
        compiler_params=_params("parallel", "parallel"),
        name="hgrn2",
    )(zh, zh, zh, zh, zh, lb_f, lb_b, out_norm)


def _merge_kernel(a_ref, h_ref, ga_ref, gb_ref, wa_ref, wh_ref, o_ref):
    ya = _dot(a_ref[...], wa_ref[...])
    yb = _dot(h_ref[...], wh_ref[...])
    merged = _sigmoid(ga_ref[...].astype(F32)) * ya + _sigmoid(gb_ref[...].astype(F32)) * yb
    o_ref[...] = merged.astype(o_ref.dtype)


def _merge(attn, hg, gates, w_up_attn, w_up_hgrn):
    n, wa = attn.shape
    wh = hg.shape[1]
    d = w_up_attn.shape[1]
    tm = _tile(n, 1024)
    tn = _tile(d, 512)
    nj = d // tn
    return pl.pallas_call(
        _merge_kernel,
        grid=(n // tm, nj),
        in_specs=[
            pl.BlockSpec((tm, wa), lambda i, j: (i, 0)),
            pl.BlockSpec((tm, wh), lambda i, j: (i, 0)),
            pl.BlockSpec((tm, tn), lambda i, j: (i, j)),
            pl.BlockSpec((tm, tn), lambda i, j: (i, nj + j)),
            pl.BlockSpec((wa, tn), lambda i, j: (0, j)),
            pl.BlockSpec((wh, tn), lambda i, j: (0, j)),
        ],
        out_specs=pl.BlockSpec((tm, tn), lambda i, j: (i, j)),
        out_shape=jax.ShapeDtypeStruct((n, d), BF16),
        compiler_params=_params("parallel", "arbitrary"),
        name="gated_merge",
    )(attn, hg, gates, gates, w_up_attn, w_up_hgrn)


def _pack_bf16_pairs(h):
    half = h.shape[1] // 2
    lo = pltpu.bitcast(h[:, :half].astype(BF16).astype(F32), jnp.uint32)
    hi = pltpu.bitcast(h[:, half:].astype(BF16).astype(F32), jnp.uint32)
    return (hi & jnp.uint32(0xFFFF0000)) | (lo >> 16)


def _unpack_bf16_pairs(u):
    lo = pltpu.bitcast(u << 16, F32).astype(BF16)
    hi = pltpu.bitcast(u & jnp.uint32(0xFFFF0000), F32).astype(BF16)
    return lo, hi


def _outproj_router_kernel(x_ref, m_ref, w_ref, g_ref, wr_ref, br_ref, hp_in_ref,
                           x1_ref, hp_ref, idx_ref, gate_ref):
    del hp_in_ref
    x1 = x_ref[...] + _dot(m_ref[...], w_ref[...])
    x1_ref[...] = x1
    h = _rmsnorm(x1, g_ref[...])
    hp_ref[...] = _pack_bf16_pairs(h)
    h_hi, h_mid, _ = _split3(h)
    w_hi, w_mid, _ = _split3(wr_ref[...])
    lg = (_dot_nt(w_hi, h_hi) + _dot_nt(w_hi, h_mid) + _dot_nt(w_mid, h_hi)) + br_ref[...]
    n_exp, tm = lg.shape
    eid = lax.broadcasted_iota(jnp.int32, (n_exp, tm), 0)
    vals = []
    for kk in range(TOP_K):
        m = jnp.max(lg, axis=0, keepdims=True)
        sel = jnp.min(jnp.where(lg == m, eid, n_exp), axis=0, keepdims=True)
        idx_ref[kk:kk + 1, :] = sel
        vals.append(m)
        lg = jnp.where(eid == sel, -jnp.inf, lg)
    ex = [jnp.exp(vv - vals[0]) for vv in vals]
    den = ex[0]
    for e in ex[1:]:
        den = den + e
    for kk in range(TOP_K):
        gate_ref[kk:kk + 1, :] = ex[kk] / den


def _outproj_router(x, merged, w_out, ffn_gain, w_router_t, b_router, hp_prev, row_off, n_total):
    n, d = x.shape
    tm = _tile(n, 512)
    assert row_off % tm == 0
    n_exp = w_router_t.shape[0]
    if hp_prev is None:
        hp_prev = jnp.zeros((8, LANES), jnp.uint32)
        aliases = {}
    else:
        aliases = {6: 1}
    return pl.pallas_call(
        _outproj_router_kernel,
        grid=(n // tm,),
        in_specs=[
            pl.BlockSpec((tm, d), lambda i: (i, 0)),
            pl.BlockSpec((tm, d), lambda i: (i, 0)),
            pl.BlockSpec((d, d), lambda i: (0, 0)),
            pl.BlockSpec((1, d), lambda i: (0, 0)),
            pl.BlockSpec((n_exp, d), lambda i: (0, 0)),
            pl.BlockSpec((n_exp, 1), lambda i: (0, 0)),
            pl.BlockSpec(memory_space=pl.ANY),
        ],
        out_specs=[
            pl.BlockSpec((tm, d), lambda i: (i, 0)),
            pl.BlockSpec((tm, d // 2), lambda i: (row_off // tm + i, 0)),
            pl.BlockSpec((TOP_K, tm), lambda i: (0, i)),
            pl.BlockSpec((TOP_K, tm), lambda i: (0, i)),
        ],
        out_shape=[
            jax.ShapeDtypeStruct((n, d), F32),
            jax.ShapeDtypeStruct((n_total, d // 2), jnp.uint32),
            jax.ShapeDtypeStruct((TOP_K, n), jnp.int32),
            jax.ShapeDtypeStruct((TOP_K, n), F32),
        ],
        input_output_aliases=aliases,
        compiler_params=_params("parallel"),
        name="outproj_router",
    )(x, merged, w_out, ffn_gain, w_router_t, b_router, hp_prev)


def _route_kernel(idx_ref, dest_ref, cnt_ref, cnt_scr, base_scr, *, row_block):
    phase = pl.program_id(0)
    i = pl.program_id(1)
    n_exp = cnt_scr.shape[0]
    tt = idx_ref.shape[1]
    eid = lax.broadcasted_iota(jnp.int32, (n_exp, tt), 0)
    onehot = [(eid == idx_ref[kk:kk + 1, :]) for kk in range(TOP_K)]

    @pl.when((phase == 0) & (i == 0))
    def _():
        cnt_scr[...] = jnp.zeros(cnt_scr.shape, F32)

    @pl.when(phase == 0)
    def _():
        tot = onehot[0].astype(F32)
        for oh in onehot[1:]:
            tot = tot + oh.astype(F32)
        cnt_scr[...] = cnt_scr[...] + jnp.sum(tot, axis=1, keepdims=True)
        cnt_ref[...] = cnt_scr[...]

    @pl.when((phase == 1) & (i == 0))
    def _():
        cnt = cnt_scr[...].astype(jnp.int32)
        padded = ((cnt + (row_block - 1)) // row_block * row_block).astype(F32)
        er = lax.broadcasted_iota(jnp.int32, (n_exp, n_exp), 0)
        ec = lax.broadcasted_iota(jnp.int32, (n_exp, n_exp), 1)
        base_scr[...] = _dot_exact_lhs((ec < er).astype(BF16), padded)

    @pl.when(phase == 1)
    def _():
        tr = lax.broadcasted_iota(jnp.int32, (tt, tt), 0)
        tc = lax.broadcasted_iota(jnp.int32, (tt, tt), 1)
        before = (tr < tc).astype(BF16)
        run = base_scr[...][:, :1]
        for kk in range(TOP_K):
            oh = onehot[kk].astype(F32)
            rank = _dot(oh.astype(BF16), before) + run
            dest_ref[kk:kk + 1, :] = jnp.sum(oh * rank, axis=0, keepdims=True).astype(jnp.int32)
            run = run + jnp.sum(oh, axis=1, keepdims=True)
        base_scr[...] = jnp.broadcast_to(run, base_scr.shape)


def _route(idx, row_block):
    n = idx.shape[1]
    tt = _tile(n, 512)
    return pl.pallas_call(
        functools.partial(_route_kernel, row_block=row_block),
        grid=(2, n // tt),
        in_specs=[pl.BlockSpec((TOP_K, tt), lambda p, i: (0, i))],
        out_specs=[
            pl.BlockSpec((TOP_K, tt), lambda p, i: (0, i * p)),
            pl.BlockSpec((N_EXPERTS, LANES), lambda p, i: (0, 0)),
        ],
        out_shape=[
            jax.ShapeDtypeStruct((TOP_K, n), jnp.int32),
            jax.ShapeDtypeStruct((N_EXPERTS, LANES), F32),
        ],
        scratch_shapes=[pltpu.VMEM((N_EXPERTS, LANES), F32), pltpu.VMEM((N_EXPERTS, LANES), F32)],
        compiler_params=_params("arbitrary", "arbitrary"),
        name="route_offsets",
    )(idx)


def _dispatch_kernel(seg_ref, dest_ref, h_ref, xs_ref, zero_scr, sem, zsem, *, row_block):
    tt = dest_ref.shape[1]
    n_exp = seg_ref.shape[1]

    @pl.when(pl.program_id(0) == 0)
    def _():
        zero_scr[...] = jnp.zeros(zero_scr.shape, zero_scr.dtype)

        def zero_copy(e):
            start = pl.multiple_of(seg_ref[1, e] - row_block, row_block)
            return pltpu.make_async_copy(zero_scr, xs_ref.at[pl.ds(start, row_block)], zsem)

        for e in range(n_exp):
            @pl.when(seg_ref[1, e] > seg_ref[0, e])
            def _(e=e):
                zero_copy(e).start()
        for e in range(n_exp):
            @pl.when(seg_ref[1, e] > seg_ref[0, e])
            def _(e=e):
                zero_copy(e).wait()

    def start(g, carry):
        for u in range(8):
            for kk in range(TOP_K):
                pltpu.make_async_copy(h_ref.at[g, pl.ds(u, 1), :],
                                      xs_ref.at[pl.ds(dest_ref[kk, g * 8 + u], 1)], sem).start()
        return carry

    lax.fori_loop(0, tt // 8, start, 0)
    pltpu.make_async_copy(xs_ref.at[pl.ds(0, TOP_K * tt)], xs_ref.at[pl.ds(0, TOP_K * tt)], sem).wait()


def _dispatch(seg, dest, hp, n_rows, row_block):
    n, w = hp.shape
    tt = _tile(n, 512)
    grid_spec = pltpu.PrefetchScalarGridSpec(
        num_scalar_prefetch=1,
        grid=(n // tt,),
        in_specs=[
            pl.BlockSpec((TOP_K, tt), lambda i, s: (0, i), memory_space=pltpu.SMEM),
            pl.BlockSpec((tt // 8, 8, w), lambda i, s: (i, 0, 0)),
        ],
        out_specs=pl.BlockSpec(memory_space=pl.ANY),
        scratch_shapes=[pltpu.VMEM((row_block, w), hp.dtype), pltpu.SemaphoreType.DMA(()),
                        pltpu.SemaphoreType.DMA(())],
    )
    return pl.pallas_call(
        functools.partial(_dispatch_kernel, row_block=row_block),
        grid_spec=grid_spec,
        out_shape=jax.ShapeDtypeStruct((n_rows, w), hp.dtype),
        compiler_params=_params("arbitrary"),
        name="dispatch_rows",
    )(seg, dest, hp.reshape(n // 8, 8, w))


def _expert_kernel(meta_ref, xs_ref, wg_ref, wl_ref, bg_ref, bl_ref, wd_ref, bd_ref, ys_ref,
                   x_scr, y_ref):
    i = pl.program_id(0)
    j = pl.program_id(1)

    @pl.when(i < meta_ref[0])
    def _():
        @pl.when(j == 0)
        def _():
            half = xs_ref.shape[1]
            lo, hi = _unpack_bf16_pairs(xs_ref[...])
            x_scr[:, :half] = lo
            x_scr[:, half:] = hi
            y_ref[...] = jnp.broadcast_to(bd_ref[...], y_ref.shape)

        x = x_scr[...]
        glu = _dot(x, wg_ref[...]) + bg_ref[...]
        lin = _dot(x, wl_ref[...]) + bl_ref[...]
        glu = jnp.minimum(glu, SWIGLU_LIMIT)
        lin = jnp.clip(lin, -SWIGLU_LIMIT, SWIGLU_LIMIT)
        act = glu * _sigmoid(SWIGLU_ALPHA * glu) * (lin + 1.0)
        y_ref[...] = y_ref[...] + _dot(act.astype(BF16), wd_ref[...].astype(BF16))

        @pl.when(j == pl.num_programs(1) - 1)
        def _():
            ys_ref[...] = _pack_bf16_pairs(y_ref[...])


def _experts(meta, xs, w_gu, b_gu, w_dn, b_dn, row_block):
    n_rows, half = xs.shape
    d = 2 * half
    d_ff = w_dn.shape[1]
    tf = _tile(d_ff, 1024)
    nf = d_ff // tf
    n_blocks = n_rows // row_block

    def jj(i, j, m):
        return jnp.where(i < m[0], j, nf - 1)

    grid_spec = pltpu.PrefetchScalarGridSpec(
        num_scalar_prefetch=1,
        grid=(n_blocks, nf),
        in_specs=[
            pl.BlockSpec((row_block, half), lambda i, j, m: (i, 0)),
            pl.BlockSpec((None, d, tf), lambda i, j, m: (m[1 + i], 0, jj(i, j, m))),
            pl.BlockSpec((None, d, tf), lambda i, j, m: (m[1 + i], 0, nf + jj(i, j, m))),
            pl.BlockSpec((None, 1, tf), lambda i, j, m: (m[1 + i], 0, jj(i, j, m))),
            pl.BlockSpec((None, 1, tf), lambda i, j, m: (m[1 + i], 0, nf + jj(i, j, m))),
            pl.BlockSpec((None, tf, d), lambda i, j, m: (m[1 + i], jj(i, j, m), 0)),
            pl.BlockSpec((None, 1, d), lambda i, j, m: (m[1 + i], 0, 0)),
        ],
        out_specs=pl.BlockSpec((row_block, half), lambda i, j, m: (i, 0)),
        scratch_shapes=[pltpu.VMEM((row_block, d), BF16), pltpu.VMEM((row_block, d), F32)],
    )
    return pl.pallas_call(
        _expert_kernel,
        grid_spec=grid_spec,
        out_shape=jax.ShapeDtypeStruct((n_rows, half), jnp.uint32),
        compiler_params=_params("arbitrary", "arbitrary"),
        name="expert_swiglu",
    )(meta, xs, w_gu, w_gu, b_gu, b_gu, w_dn, b_dn)


def _combine_kernel(dest_ref, dest_next_ref, gate_ref, x1_ref, fn_ref, ys_ref, o_ref, buf, sem):
    tt = dest_ref.shape[1]
    i = pl.program_id(0)
    slot = i % 2

    def gather(d_ref, s):
        def start(g, carry):
            for u in range(8):
                for kk in range(TOP_K):
                    pltpu.make_async_copy(ys_ref.at[pl.ds(d_ref[kk, g * 8 + u], 1)],
                                          buf.at[s, kk * (tt // 8) + g, pl.ds(u, 1), :], sem.at[s]).start()
            return carry
        lax.fori_loop(0, tt // 8, start, 0)

    @pl.when(i == 0)
    def _():
        gather(dest_ref, 0)

    @pl.when(i + 1 < pl.num_programs(0))
    def _():
        gather(dest_next_ref, 1 - slot)

    pltpu.make_async_copy(ys_ref.at[pl.ds(0, TOP_K * tt)], ys_ref.at[pl.ds(0, TOP_K * tt)], sem.at[slot]).wait()
    half = buf.shape[3]
    gates = gate_ref[...]
    acc_lo = x1_ref[:, :half]
    acc_hi = x1_ref[:, half:]
    for kk in range(TOP_K):
        u = buf[slot, pl.ds(kk * (tt // 8), tt // 8)].reshape(tt, half)
        g = gates[:, kk:kk + 1]
        acc_lo = acc_lo + pltpu.bitcast(u << 16, F32) * g
        acc_hi = acc_hi + pltpu.bitcast(u & jnp.uint32(0xFFFF0000), F32) * g
    ms = (jnp.sum(acc_lo * acc_lo, axis=-1, keepdims=True)
          + jnp.sum(acc_hi * acc_hi, axis=-1, keepdims=True)) / (2 * half)
    inv = lax.rsqrt(ms + NORM_EPS)
    o_ref[:, :half] = acc_lo * inv * fn_ref[:, :half]
    o_ref[:, half:] = acc_hi * inv * fn_ref[:, half:]


def _combine(dest, gates_t, x1, final_gain, ys, row_off):
    n, d = x1.shape
    tt = _tile(n, 256)
    assert row_off % tt == 0
    off = row_off // tt
    nt = n // tt
    return pl.pallas_call(
        _combine_kernel,
        grid=(nt,),
        in_specs=[
            pl.BlockSpec((TOP_K, tt), lambda i: (0, off + i), memory_space=pltpu.SMEM),
            pl.BlockSpec((TOP_K, tt), lambda i: (0, off + jnp.minimum(i + 1, nt - 1)), memory_space=pltpu.SMEM),
            pl.BlockSpec((tt, TOP_K), lambda i: (off + i, 0)),
            pl.BlockSpec((tt, d), lambda i: (i, 0)),
            pl.BlockSpec((1, d), lambda i: (0, 0)),
            pl.BlockSpec(memory_space=pl.ANY),
        ],
        out_specs=pl.BlockSpec((tt, d), lambda i: (i, 0)),
        out_shape=jax.ShapeDtypeStruct((n, d), F32),
        scratch_shapes=[pltpu.VMEM((2, TOP_K * tt // 8, 8, d // 2), jnp.uint32), pltpu.SemaphoreType.DMA((2,))],
        compiler_params=_params("arbitrary"),
        name="combine_rows",
    )(dest, dest, gates_t, x1, final_gain, ys)


def _rope_tables(seq_len):
    rows = seq_len // GRID_W
    row = jnp.repeat(jnp.arange(rows, dtype=F32), GRID_W)
    col = jnp.tile(jnp.arange(GRID_W, dtype=F32), rows)
    freqs = ROPE_THETA ** (-jnp.arange(ROPE_HALF, dtype=F32) / ROPE_HALF)
    ang_r = row[:, None] * freqs[None, :]
    ang_c = col[:, None] * freqs[None, :]
    cos = jnp.concatenate([jnp.cos(ang_r), jnp.cos(ang_r), jnp.cos(ang_c), jnp.cos(ang_c)], axis=1)
    sin = jnp.concatenate([-jnp.sin(ang_r), jnp.sin(ang_r), -jnp.sin(ang_c), jnp.sin(ang_c)], axis=1)
    return cos, sin


def kernel(x_prompt, x_sample, mix_norm, w_in, q_norm, k_norm, hg_lb_logits, hg_out_norm, w_up_attn,
           w_up_hgrn, w_out, ffn_norm, w_router, b_router, w_gate_up, b_gate_up, w_down, b_down, final_norm):
    assert mix_norm.shape[0] == 1, "single trunk layer"
    d = x_prompt.shape[-1]
    hg_w = HG_HEADS * HG_D
    n_exp = w_router.shape[-1]
    row_block = MOE_ROW_BLOCK
    streams = [(x.reshape(-1, d), x.shape[0], x.shape[1]) for x in (x_prompt, x_sample)]
    n_total = sum(x.shape[0] for x, _, _ in streams)

    lb = jnp.cumsum(jax.nn.softmax(hg_lb_logits.astype(F32), axis=1), axis=1)[:, 0]
    w_in_b = w_in[0].astype(BF16)
    w_ua_b = w_up_attn[0].astype(BF16)
    w_uh_b = w_up_hgrn[0].astype(BF16)
    w_out_b = w_out[0].astype(BF16)
    w_r_t = w_router[0].T
    mix_g = mix_norm[0].reshape(1, d)
    score_bound = (1.02 * LOG2_E * HEAD_DIM ** 0.5 * jnp.max(jnp.abs(q_norm[0])) * jnp.max(jnp.abs(k_norm[0])))
    score_bound = score_bound.astype(F32).reshape(1)

    w_gu2d = w_gate_up[0].reshape(-1, w_gate_up.shape[-1])
    quarter = w_gu2d.shape[1] // 4
    cast_cols = [(3 * quarter, 4 * quarter), (0, 3 * quarter)]
    w_gu_b = None

    x1s, idxs, gate_ts = [], [], []
    hp = None
    row_off = 0
    for (x, batch, seq), (col_lo, col_hi) in zip(streams, cast_cols):
        cos, sin = _rope_tables(seq)
        qkv, zh, gates = _norm_proj(x, mix_g, w_in_b, q_norm[0].reshape(1, HEAD_DIM),
                                    k_norm[0].reshape(1, HEAD_DIM), cos, sin, seq)
        attn, w_gu_b = _attention(qkv, score_bound, batch, seq, w_gu2d, col_lo, col_hi, w_gu_b)
        hg = _hgrn(zh, lb[0:1], lb[1:2], hg_out_norm[0].reshape(1, hg_w), batch, seq)
        merged = _merge(attn, hg, gates, w_ua_b, w_uh_b)
        x1, hp, idx, gate = _outproj_router(x, merged, w_out_b, ffn_norm[0].reshape(1, d), w_r_t,
                                            b_router[0].reshape(n_exp, 1), hp, row_off, n_total)
        x1s.append(x1)
        idxs.append(idx)
        gate_ts.append(gate.T)
        row_off += x.shape[0]

    idx = jnp.concatenate(idxs, axis=1)
    gate_t = jnp.concatenate(gate_ts, axis=0)
    dest, counts = _route(idx, row_block)
    cnt = counts[:, 0].astype(jnp.int32)
    padded = (cnt + row_block - 1) // row_block * row_block
    pad_end = jnp.cumsum(padded)
    n_rows = n_total * TOP_K + n_exp * row_block
    n_blocks = n_rows // row_block
    blk_start = jnp.arange(n_blocks, dtype=jnp.int32) * row_block
    blk_e = jnp.minimum(jnp.sum(pad_end[None, :] <= blk_start[:, None], axis=1), n_exp - 1).astype(jnp.int32)
    meta = jnp.concatenate([(pad_end[-1:] // row_block).astype(jnp.int32), blk_e])
    seg = jnp.stack([pad_end - padded, pad_end]).astype(jnp.int32)

    xs = _dispatch(seg, dest, hp, n_rows, row_block)
    ys = _experts(meta, xs, w_gu_b.reshape(w_gate_up.shape[1:]), b_gate_up[0].reshape(n_exp, 1, -1),
                  w_down[0], b_down[0].reshape(n_exp, 1, d), row_block)

    outs = []
    row_off = 0
    for (x, batch, seq), x1 in zip(streams, x1s):
        out = _combine(dest, gate_t, x1, final_norm.reshape(1, d), ys, row_off)
        outs.append(out.reshape(batch, seq, d))
        row_off += x.shape[0]
    return tuple(outs)
```

```python
import functools

import jax
import jax.numpy as jnp
from jax import lax
from jax.experimental import pallas as pl
from jax.experimental.pallas import tpu as pltpu

GRID_W = 64
HEAD_DIM = 128
N_Q_HEADS = 16
N_KV_HEADS = 4
ROPE_THETA = 10000.0
ROPE_HALF = HEAD_DIM // 4
HG_HEADS = 8
HG_D = 128
HG_CHUNK = 64
N_EXPERTS = 32
TOP_K = 4
SWIGLU_LIMIT = 7.0
SWIGLU_ALPHA = 1.702
NORM_EPS = 1e-5

HG_SAFE_LOG_DECAY = -60.0

LOG2_E = 1.4426950408889634
ATTN_FIXED_SHIFT_LIMIT = 60.0

V7X_VMEM_BYTES = 64 * 1024 * 1024
VMEM_LIMIT_BYTES = V7X_VMEM_BYTES - 8 * 1024 * 1024
LANES = 128
MOE_ROW_BLOCK = 512

BF16 = jnp.bfloat16
F32 = jnp.float32


def _params(*sem):
    return pltpu.CompilerParams(dimension_semantics=sem, vmem_limit_bytes=VMEM_LIMIT_BYTES)


def _tile(n, pref):
    t = min(n, pref)
    while n % t:
        t //= 2
    return t


def _sigmoid(x):
    return 1.0 / (1.0 + jnp.exp(-x))


def _rmsnorm(x, g):
    return x * lax.rsqrt(jnp.mean(x * x, axis=-1, keepdims=True) + NORM_EPS) * g


def _dot(a, b):
    return jnp.dot(a, b, preferred_element_type=F32)


def _dot_nt(a, b):
    return lax.dot_general(a, b, (((1,), (1,)), ((), ())), preferred_element_type=F32)


def _split3(x):
    hi = x.astype(BF16)
    r = x - hi.astype(F32)
    mid = r.astype(BF16)
    lo = (r - mid.astype(F32)).astype(BF16)
    return hi, mid, lo


def _dot_exact_lhs(m_bf16, x, pieces=3):
    parts = _split3(x)[:pieces]
    acc = _dot(m_bf16, parts[0])
    for p in parts[1:]:
        acc = acc + _dot(m_bf16, p)
    return acc


def _rope_head_pair(zp, gain, cos, sin, ones_blk, perm_blk, scale):
    ss = _dot((zp * zp).astype(BF16), ones_blk)
    y = zp * lax.rsqrt(ss * (1.0 / HEAD_DIM) + NORM_EPS) * gain
    y_hi = y.astype(BF16)
    y_lo = (y - y_hi.astype(F32)).astype(BF16)
    partner = _dot(y_hi, perm_blk) + _dot(y_lo, perm_blk)
    return (y * cos + partner * sin) * scale


def _proj_kernel(x_ref, g_ref, w_ref, qn_ref, kn_ref, cos_ref, sin_ref, ones_ref, perm_ref,
                 qkv_ref, zh_ref, gate_ref, h_scr, *, tile_kinds):
    j = pl.program_id(1)

    @pl.when(j == 0)
    def _():
        h_scr[...] = _rmsnorm(x_ref[...], g_ref[...]).astype(BF16)

    z = _dot(h_scr[...], w_ref[...])

    def qkv_tile(kinds):
        two = lambda r: jnp.concatenate([r[...], r[...]], axis=1)
        outs = []
        for h in range(0, len(kinds), 2):
            kind = kinds[h]
            assert kinds[h + 1] == kind
            zp = z[:, h * HEAD_DIM:(h + 2) * HEAD_DIM]
            if kind == "q":
                zp = _rope_head_pair(zp, two(qn_ref), two(cos_ref), two(sin_ref), ones_ref[...], perm_ref[...],
                                     LOG2_E * HEAD_DIM ** -0.5)
            elif kind == "k":
                zp = _rope_head_pair(zp, two(kn_ref), two(cos_ref), two(sin_ref), ones_ref[...], perm_ref[...], 1.0)
            outs.append(zp)
        return jnp.concatenate(outs, axis=1)

    for lo, hi, kind in tile_kinds:
        @pl.when((j >= lo) & (j < hi))
        def _(kind=kind):
            if kind == "zh":
                zh_ref[...] = z
            elif kind == "gate":
                gate_ref[...] = z.astype(gate_ref.dtype)
            else:
                qkv_ref[...] = qkv_tile(kind).astype(qkv_ref.dtype)


def _norm_proj(x, gain, w, q_gain, k_gain, cos, sin, seq):
    n, d = x.shape
    attn_w = N_Q_HEADS * HEAD_DIM
    kv_w = N_KV_HEADS * HEAD_DIM
    hg5 = 5 * HG_HEADS * HG_D
    qkv_w = attn_w + 2 * kv_w
    assert w.shape[1] == qkv_w + hg5 + 2 * d
    tm = _tile(seq, 1024)
    tn = 1024
    while attn_w % tn or (2 * kv_w) % tn or hg5 % tn or (2 * d) % tn:
        tn //= 2
    assert tn % HEAD_DIM == 0
    heads = ["q"] * N_Q_HEADS + ["k"] * N_KV_HEADS + ["v"] * N_KV_HEADS
    hpt = tn // HEAD_DIM
    n_qkv, n_zh, n_gate = qkv_w // tn, hg5 // tn, 2 * d // tn
    tile_kinds = []
    for t in range(n_qkv):
        kind = tuple(heads[t * hpt:(t + 1) * hpt])
        if tile_kinds and tile_kinds[-1][2] == kind:
            tile_kinds[-1] = (tile_kinds[-1][0], t + 1, kind)
        else:
            tile_kinds.append((t, t + 1, kind))
    tile_kinds += [(n_qkv, n_qkv + n_zh, "zh"), (n_qkv + n_zh, n_qkv + n_zh + n_gate, "gate")]
    n_pos = seq // tm
    r = jnp.arange(2 * HEAD_DIM)
    same_head = (r[:, None] // HEAD_DIM) == (r[None, :] // HEAD_DIM)
    ones_blk = same_head.astype(BF16)
    partner_of = jnp.where((r % (2 * ROPE_HALF)) < ROPE_HALF, r + ROPE_HALF, r - ROPE_HALF)
    perm_blk = (r[:, None] == partner_of[None, :]).astype(BF16)
    const_spec = pl.BlockSpec((2 * HEAD_DIM, 2 * HEAD_DIM), lambda i, j: (0, 0))
    return pl.pallas_call(
        functools.partial(_proj_kernel, tile_kinds=tile_kinds),
        grid=(n // tm, n_qkv + n_zh + n_gate),
        in_specs=[
            pl.BlockSpec((tm, d), lambda i, j: (i, 0)),
            pl.BlockSpec((1, d), lambda i, j: (0, 0)),
            pl.BlockSpec((d, tn), lambda i, j: (0, j)),
            pl.BlockSpec((1, HEAD_DIM), lambda i, j: (0, 0)),
            pl.BlockSpec((1, HEAD_DIM), lambda i, j: (0, 0)),
            pl.BlockSpec((tm, HEAD_DIM), lambda i, j: (i % n_pos, 0)),
            pl.BlockSpec((tm, HEAD_DIM), lambda i, j: (i % n_pos, 0)),
            const_spec,
            const_spec,
        ],
        out_specs=[
            pl.BlockSpec((tm, tn), lambda i, j: (i, jnp.minimum(j, n_qkv - 1))),
            pl.BlockSpec((tm, tn), lambda i, j: (i, jnp.clip(j - n_qkv, 0, n_zh - 1))),
            pl.BlockSpec((tm, tn), lambda i, j: (i, jnp.clip(j - n_qkv - n_zh, 0, n_gate - 1))),
        ],
        out_shape=[
            jax.ShapeDtypeStruct((n, qkv_w), BF16),
            jax.ShapeDtypeStruct((n, hg5), F32),
            jax.ShapeDtypeStruct((n, 2 * d), BF16),
        ],
        scratch_shapes=[pltpu.VMEM((tm, d), BF16)],
        compiler_params=_params("parallel", "arbitrary"),
        name="norm_proj",
    )(x, gain, w, q_gain, k_gain, cos, sin, ones_blk, perm_blk)


def _attn_kernel(bound_ref, q_ref, k_ref, v_ref, *rest, tk, group):
    if len(rest) == 5:
        wf_ref, o_ref, wb_ref, acc_scr, m_scr = rest
    else:
        wf_ref = wb_ref = None
        o_ref, acc_scr, m_scr = rest
    tq = q_ref.shape[0]
    seq = k_ref.shape[0]
    n_chunks = seq // tk
    slab = None if wf_ref is None else wf_ref.shape[0] // n_chunks
    q = jnp.concatenate([q_ref[:, g * HEAD_DIM:(g + 1) * HEAD_DIM] for g in range(group)], axis=0)
    ones = jnp.ones((tk, HEAD_DIM), BF16)
    bound = bound_ref[0]
    acc_scr[...] = jnp.zeros(acc_scr.shape, F32)

    def chunk(c):
        rows = pl.ds(pl.multiple_of(c * tk, tk), tk)
        s = _dot_nt(q, k_ref[rows, :])
        return s, jnp.concatenate([v_ref[rows, :], ones], axis=1)

    @pl.when(bound <= ATTN_FIXED_SHIFT_LIMIT)
    def _():
        def body(c, carry):
            s, v1 = chunk(c)
            acc_scr[...] += _dot(jnp.exp2(s - bound).astype(BF16), v1)
            if wf_ref is not None:
                wrows = pl.ds(pl.multiple_of(c * slab, slab), slab)
                wb_ref[wrows, :] = wf_ref[wrows, :].astype(BF16)
            return carry
        lax.fori_loop(0, n_chunks, body, 0, unroll=True)

    @pl.when(bound > ATTN_FIXED_SHIFT_LIMIT)
    def _():
        if wf_ref is not None:
            wb_ref[...] = wf_ref[...].astype(BF16)
        m_scr[...] = jnp.full(m_scr.shape, -1e30, F32)

        def body(c, carry):
            s, v1 = chunk(c)
            m_prev = m_scr[...]
            m_new = jnp.maximum(m_prev, jnp.max(s, axis=1, keepdims=True))
            alpha = jnp.exp2(m_prev - m_new)
            p = jnp.exp2(s - jnp.tile(m_new, (1, tk // LANES)))
            acc_scr[...] = jnp.tile(alpha, (1, 2)) * acc_scr[...] + _dot(p.astype(BF16), v1)
            m_scr[...] = m_new
            return carry
        lax.fori_loop(0, seq // tk, body, 0)

    acc = acc_scr[...]
    o = acc[:, :HEAD_DIM] / acc[:, HEAD_DIM:]
    for g in range(group):
        o_ref[:, g * HEAD_DIM:(g + 1) * HEAD_DIM] = o[g * tq:(g + 1) * tq].astype(o_ref.dtype)


def _attention(qkv, score_bound, batch, seq, w_f32=None):
    group = N_Q_HEADS // N_KV_HEADS
    gw = group * HEAD_DIM
    tq = _tile(seq, 512)
    tk = _tile(seq, 512)
    nq = seq // tq
    in_specs = [
        pl.BlockSpec(memory_space=pltpu.SMEM),
        pl.BlockSpec((tq, gw), lambda b, n, i: (b * nq + i, n)),
        pl.BlockSpec((seq, HEAD_DIM), lambda b, n, i: (b, N_Q_HEADS + n)),
        pl.BlockSpec((seq, HEAD_DIM), lambda b, n, i: (b, N_Q_HEADS + N_KV_HEADS + n)),
    ]
    out_specs = [pl.BlockSpec((tq, gw), lambda b, n, i: (b * nq + i, n))]
    out_shape = [jax.ShapeDtypeStruct((batch * seq, N_Q_HEADS * HEAD_DIM), BF16)]
    args = [score_bound, qkv, qkv, qkv]
    if w_f32 is not None:
        n_steps = batch * N_KV_HEADS * nq
        w_rows, w_cols = w_f32.shape
        slab = w_rows // n_steps
        assert slab * n_steps == w_rows and slab % (16 * (seq // tk)) == 0, (w_rows, n_steps, seq // tk)
        wspec = pl.BlockSpec((slab, w_cols), lambda b, n, i: ((b * N_KV_HEADS + n) * nq + i, 0))
        in_specs.append(wspec)
        out_specs.append(wspec)
        out_shape.append(jax.ShapeDtypeStruct((w_rows, w_cols), BF16))
        args.append(w_f32)
    return pl.pallas_call(
        functools.partial(_attn_kernel, tk=tk, group=group),
        grid=(batch, N_KV_HEADS, nq),
        in_specs=in_specs,
        out_specs=out_specs,
        out_shape=out_shape,
        scratch_shapes=[
            pltpu.VMEM((group * tq, 2 * HEAD_DIM), F32),
            pltpu.VMEM((group * tq, LANES), F32),
        ],
        compiler_params=_params("parallel", "parallel", "arbitrary"),
        name="gqa_attention",
    )(*args)


def _hgrn_span(i, q_scr, b_scr, k_scr, zi_ref, vt_scr, o_scr, s_scr, tri, *, span, reverse, pairwise):
    C = HG_CHUNK
    nc = span // C
    r0 = pl.multiple_of(i * span, span)
    rows = pl.ds(r0, span)
    q = q_scr[rows, :]
    b = b_scr[rows, :]
    k = k_scr[rows, :]
    v = zi_ref[rows, :]
    vb = v.astype(BF16)
    end_row = 0 if reverse else C - 1
    ends = [b[c * C + end_row:c * C + end_row + 1, :] for c in range(nc)]
    b_end = jnp.concatenate([jnp.broadcast_to(e, (C, HG_D)) for e in ends], axis=0)
    qtb = (q * jnp.exp(b)).astype(BF16)
    kp = k * jnp.exp(b_end - b)

    if pairwise:
        rid = lax.broadcasted_iota(jnp.int32, (span, HG_D), 0)

        def pair(s, acc):
            cs = (s // C) * C
            if reverse:
                m = (rid <= s) & (rid >= cs)
            else:
                m = (rid >= s) & (rid < cs + C)
            w = jnp.where(m, jnp.exp(jnp.minimum(b - b_scr[pl.ds(r0 + s, 1), :], 0.0)), 0.0)
            r = jnp.sum(q * w * k_scr[pl.ds(r0 + s, 1), :], axis=1, keepdims=True)
            return acc + r * zi_ref[pl.ds(r0 + s, 1), :]

        o_intra = lax.fori_loop(0, span, pair, jnp.zeros((span, HG_D), F32))
    else:
        ktb = (k * jnp.exp(-b)).astype(BF16)
        a = jnp.where(tri, _dot_nt(qtb, ktb), 0.0)
        o_intra = _dot(a.astype(BF16), vb)

    chunk_of_row = lax.broadcasted_iota(jnp.int32, (span, HG_D), 0) // C
    kp_blocks = jnp.concatenate([jnp.where(chunk_of_row == c, kp, 0.0) for c in range(nc)], axis=1)
    upd = _dot(vt_scr[:, rows], kp_blocks.astype(BF16))

    st = s_scr[...]
    states = [None] * nc
    for c in (reversed(range(nc)) if reverse else range(nc)):
        states[c] = st.astype(BF16)
        st = st * jnp.exp(ends[c]) + upd[:, c * HG_D:(c + 1) * HG_D]
    s_scr[...] = st
    o_inter = _dot_nt(qtb, jnp.concatenate(states, axis=0))
    o_scr[rows, :] = o_intra + jnp.concatenate(
        [o_inter[c * C:(c + 1) * C, c * HG_D:(c + 1) * HG_D] for c in range(nc)], axis=0)


def _hgrn_kernel(zq_ref, zf_ref, zb_ref, zi_ref, zo_ref, lbf_ref, lbb_ref, on_ref, out_ref,
                 of_scr, ob_scr, sf_scr, sb_scr, q_scr, bf_scr, bb_scr, kf_scr, kb_scr, vt_scr, *, span):
    seq = zq_ref.shape[0]
    n_span = seq // span
    C = HG_CHUNK
    nc = span // C
    sf_scr[...] = jnp.zeros(sf_scr.shape, F32)
    sb_scr[...] = jnp.zeros(sb_scr.shape, F32)
    r = lax.broadcasted_iota(jnp.int32, (span, span), 0)
    c = lax.broadcasted_iota(jnp.int32, (span, span), 1)
    same = (r // C) == (c // C)
    tri_f = same & (c <= r)
    tri_b = same & (c >= r)
    tri_f16 = tri_f.astype(BF16)
    tri_b16 = tri_b.astype(BF16)

    def prepare(i, min_end):
        rows = pl.ds(pl.multiple_of(i * span, span), span)
        qh = zq_ref[rows, :]
        q_scr[rows, :] = qh * _sigmoid(qh)
        vt_scr[:, rows] = zi_ref[rows, :].T.astype(BF16)
        for z_ref, lb_ref, tri, b_scr, k_scr, end_row in (
                (zf_ref, lbf_ref, tri_f16, bf_scr, kf_scr, C - 1), (zb_ref, lbb_ref, tri_b16, bb_scr, kb_scr, 0)):
            lb = lb_ref[...]
            f = lb + (1.0 - lb) * _sigmoid(z_ref[rows, :])
            b = _dot_exact_lhs(tri, jnp.log(f), pieces=2)
            b_scr[rows, :] = b
            k_scr[rows, :] = 1.0 - f
            for cc in range(nc):
                min_end = jnp.minimum(min_end, b[cc * C + end_row:cc * C + end_row + 1, :])
        return min_end

    min_end = lax.fori_loop(0, n_span, prepare, jnp.zeros((1, HG_D), F32), unroll=2)
    safe = jnp.min(min_end) > HG_SAFE_LOG_DECAY

    def scan(pairwise):
        def body(i, carry):
            _hgrn_span(i, q_scr, bf_scr, kf_scr, zi_ref, vt_scr, of_scr, sf_scr, tri_f,
                       span=span, reverse=False, pairwise=pairwise)
            _hgrn_span(n_span - 1 - i, q_scr, bb_scr, kb_scr, zi_ref, vt_scr, ob_scr, sb_scr, tri_b,
                       span=span, reverse=True, pairwise=pairwise)
            return carry
        lax.fori_loop(0, n_span, body, 0, unroll=1 if pairwise else 4)

    @pl.when(safe)
    def _():
        scan(False)

    @pl.when(jnp.logical_not(safe))
    def _():
        scan(True)

    def finish(i, carry):
        rows = pl.ds(pl.multiple_of(i * span, span), span)
        o = _rmsnorm(of_scr[rows, :] + ob_scr[rows, :], on_ref[...])
        og = zo_ref[rows, :]
        out_ref[rows, :] = (o * (og * _sigmoid(og))).astype(out_ref.dtype)
        return carry

    lax.fori_loop(0, n_span, finish, 0)


def _hgrn(zh, lb_f, lb_b, out_norm, batch, seq):
    span = _tile(seq, 256)
    assert span % HG_CHUNK == 0
    zspec = lambda grp: pl.BlockSpec((seq, HG_D), lambda b, h: (b, grp * HG_HEADS + h))
    hspec = pl.BlockSpec((1, HG_D), lambda b, h: (0, h))
    seq_buf = pltpu.VMEM((seq, HG_D), F32)
    return pl.pallas_call(
        functools.partial(_hgrn_kernel, span=span),
        grid=(batch, HG_HEADS),
        in_specs=[zspec(0), zspec(1), zspec(2), zspec(3), zspec(4), hspec, hspec, hspec],
        out_specs=pl.BlockSpec((seq, HG_D), lambda b, h: (b, h)),
        out_shape=jax.ShapeDtypeStruct((batch * seq, HG_HEADS * HG_D), BF16),
        scratch_shapes=[
            seq_buf, seq_buf,
            pltpu.VMEM((HG_D, HG_D), F32), pltpu.VMEM((HG_D, HG_D), F32),
            seq_buf, seq_buf, seq_buf, seq_buf, seq_buf,
            pltpu.VMEM((HG_D, seq), BF16),
        ],
        compiler_params=_params("parallel", "parallel"),
        name="hgrn2",
    )(zh, zh, zh, zh, zh, lb_f, lb_b, out_norm)


def _merge_kernel(a_ref, h_ref, ga_ref, gb_ref, wa_ref, wh_ref, o_ref):
    ya = _dot(a_ref[...], wa_ref[...])
    yb = _dot(h_ref[...], wh_ref[...])
    merged = _sigmoid(ga_ref[...].astype(F32)) * ya + _sigmoid(gb_ref[...].astype(F32)) * yb
    o_ref[...] = merged.astype(o_ref.dtype)


def _merge(attn, hg, gates, w_up_attn, w_up_hgrn):
    n, wa = attn.shape
    wh = hg.shape[1]
    d = w_up_attn.shape[1]
    tm = _tile(n, 1024)
    tn = _tile(d, 512)
    nj = d // tn
    return pl.pallas_call(
        _merge_kernel,
        grid=(n // tm, nj),
        in_specs=[
            pl.BlockSpec((tm, wa), lambda i, j: (i, 0)),
            pl.BlockSpec((tm, wh), lambda i, j: (i, 0)),
            pl.BlockSpec((tm, tn), lambda i, j: (i, j)),
            pl.BlockSpec((tm, tn), lambda i, j: (i, nj + j)),
            pl.BlockSpec((wa, tn), lambda i, j: (0, j)),
            pl.BlockSpec((wh, tn), lambda i, j: (0, j)),
        ],
        out_specs=pl.BlockSpec((tm, tn), lambda i, j: (i, j)),
        out_shape=jax.ShapeDtypeStruct((n, d), BF16),
        compiler_params=_params("parallel", "arbitrary"),
        name="gated_merge",
    )(attn, hg, gates, gates, w_up_attn, w_up_hgrn)


def _pack_bf16_pairs(h):
    half = h.shape[1] // 2
    lo = pltpu.bitcast(h[:, :half].astype(BF16).astype(F32), jnp.uint32)
    hi = pltpu.bitcast(h[:, half:].astype(BF16).astype(F32), jnp.uint32)
    return (hi & jnp.uint32(0xFFFF0000)) | (lo >> 16)


def _unpack_bf16_pairs(u):
    lo = pltpu.bitcast(u << 16, F32).astype(BF16)
    hi = pltpu.bitcast(u & jnp.uint32(0xFFFF0000), F32).astype(BF16)
    return lo, hi


def _outproj_router_kernel(x_ref, m_ref, w_ref, g_ref, wr_ref, br_ref, hp_in_ref,
                           x1_ref, hp_ref, idx_ref, gate_ref):
    del hp_in_ref
    x1 = x_ref[...] + _dot(m_ref[...], w_ref[...])
    x1_ref[...] = x1
    h = _rmsnorm(x1, g_ref[...])
    hp_ref[...] = _pack_bf16_pairs(h)
    h_hi, h_mid, _ = _split3(h)
    w_hi, w_mid, _ = _split3(wr_ref[...])
    lg = (_dot_nt(w_hi, h_hi) + _dot_nt(w_hi, h_mid) + _dot_nt(w_mid, h_hi)) + br_ref[...]
    n_exp, tm = lg.shape
    eid = lax.broadcasted_iota(jnp.int32, (n_exp, tm), 0)
    vals = []
    for kk in range(TOP_K):
        m = jnp.max(lg, axis=0, keepdims=True)
        sel = jnp.min(jnp.where(lg == m, eid, n_exp), axis=0, keepdims=True)
        idx_ref[kk:kk + 1, :] = sel
        vals.append(m)
        lg = jnp.where(eid == sel, -jnp.inf, lg)
    ex = [jnp.exp(vv - vals[0]) for vv in vals]
    den = ex[0]
    for e in ex[1:]:
        den = den + e
    for kk in range(TOP_K):
        gate_ref[kk:kk + 1, :] = ex[kk] / den


def _outproj_router(x, merged, w_out, ffn_gain, w_router_t, b_router, hp_prev, row_off, n_total):
    n, d = x.shape
    tm = _tile(n, 512)
    assert row_off % tm == 0
    n_exp = w_router_t.shape[0]
    if hp_prev is None:
        hp_prev = jnp.zeros((8, LANES), jnp.uint32)
        aliases = {}
    else:
        aliases = {6: 1}
    return pl.pallas_call(
        _outproj_router_kernel,
        grid=(n // tm,),
        in_specs=[
            pl.BlockSpec((tm, d), lambda i: (i, 0)),
            pl.BlockSpec((tm, d), lambda i: (i, 0)),
            pl.BlockSpec((d, d), lambda i: (0, 0)),
            pl.BlockSpec((1, d), lambda i: (0, 0)),
            pl.BlockSpec((n_exp, d), lambda i: (0, 0)),
            pl.BlockSpec((n_exp, 1), lambda i: (0, 0)),
            pl.BlockSpec(memory_space=pl.ANY),
        ],
        out_specs=[
            pl.BlockSpec((tm, d), lambda i: (i, 0)),
            pl.BlockSpec((tm, d // 2), lambda i: (row_off // tm + i, 0)),
            pl.BlockSpec((TOP_K, tm), lambda i: (0, i)),
            pl.BlockSpec((TOP_K, tm), lambda i: (0, i)),
        ],
        out_shape=[
            jax.ShapeDtypeStruct((n, d), F32),
            jax.ShapeDtypeStruct((n_total, d // 2), jnp.uint32),
            jax.ShapeDtypeStruct((TOP_K, n), jnp.int32),
            jax.ShapeDtypeStruct((TOP_K, n), F32),
        ],
        input_output_aliases=aliases,
        compiler_params=_params("parallel"),
        name="outproj_router",
    )(x, merged, w_out, ffn_gain, w_router_t, b_router, hp_prev)


def _route_kernel(idx_ref, dest_ref, cnt_ref, cnt_scr, base_scr, *, row_block):
    phase = pl.program_id(0)
    i = pl.program_id(1)
    n_exp = cnt_scr.shape[0]
    tt = idx_ref.shape[1]
    eid = lax.broadcasted_iota(jnp.int32, (n_exp, tt), 0)
    onehot = [(eid == idx_ref[kk:kk + 1, :]) for kk in range(TOP_K)]

    @pl.when((phase == 0) & (i == 0))
    def _():
        cnt_scr[...] = jnp.zeros(cnt_scr.shape, F32)

    @pl.when(phase == 0)
    def _():
        tot = onehot[0].astype(F32)
        for oh in onehot[1:]:
            tot = tot + oh.astype(F32)
        cnt_scr[...] = cnt_scr[...] + jnp.sum(tot, axis=1, keepdims=True)
        cnt_ref[...] = cnt_scr[...]

    @pl.when((phase == 1) & (i == 0))
    def _():
        cnt = cnt_scr[...].astype(jnp.int32)
        padded = ((cnt + (row_block - 1)) // row_block * row_block).astype(F32)
        er = lax.broadcasted_iota(jnp.int32, (n_exp, n_exp), 0)
        ec = lax.broadcasted_iota(jnp.int32, (n_exp, n_exp), 1)
        base_scr[...] = _dot_exact_lhs((ec < er).astype(BF16), padded)

    @pl.when(phase == 1)
    def _():
        tr = lax.broadcasted_iota(jnp.int32, (tt, tt), 0)
        tc = lax.broadcasted_iota(jnp.int32, (tt, tt), 1)
        before = (tr < tc).astype(BF16)
        run = base_scr[...][:, :1]
        for kk in range(TOP_K):
            oh = onehot[kk].astype(F32)
            rank = _dot(oh.astype(BF16), before) + run
            dest_ref[kk:kk + 1, :] = jnp.sum(oh * rank, axis=0, keepdims=True).astype(jnp.int32)
            run = run + jnp.sum(oh, axis=1, keepdims=True)
        base_scr[...] = jnp.broadcast_to(run, base_scr.shape)


def _route(idx, row_block):
    n = idx.shape[1]
    tt = _tile(n, 512)
    return pl.pallas_call(
        functools.partial(_route_kernel, row_block=row_block),
        grid=(2, n // tt),
        in_specs=[pl.BlockSpec((TOP_K, tt), lambda p, i: (0, i))],
        out_specs=[
            pl.BlockSpec((TOP_K, tt), lambda p, i: (0, i * p)),
            pl.BlockSpec((N_EXPERTS, LANES), lambda p, i: (0, 0)),
        ],
        out_shape=[
            jax.ShapeDtypeStruct((TOP_K, n), jnp.int32),
            jax.ShapeDtypeStruct((N_EXPERTS, LANES), F32),
        ],
        scratch_shapes=[pltpu.VMEM((N_EXPERTS, LANES), F32), pltpu.VMEM((N_EXPERTS, LANES), F32)],
        compiler_params=_params("arbitrary", "arbitrary"),
        name="route_offsets",
    )(idx)


def _dispatch_kernel(seg_ref, dest_ref, h_ref, xs_ref, zero_scr, sem, zsem, *, row_block):
    tt = dest_ref.shape[1]
    n_exp = seg_ref.shape[1]

    @pl.when(pl.program_id(0) == 0)
    def _():
        zero_scr[...] = jnp.zeros(zero_scr.shape, zero_scr.dtype)

        def zero_copy(e):
            start = pl.multiple_of(seg_ref[1, e] - row_block, row_block)
            return pltpu.make_async_copy(zero_scr, xs_ref.at[pl.ds(start, row_block)], zsem)

        for e in range(n_exp):
            @pl.when(seg_ref[1, e] > seg_ref[0, e])
            def _(e=e):
                zero_copy(e).start()
        for e in range(n_exp):
            @pl.when(seg_ref[1, e] > seg_ref[0, e])
            def _(e=e):
                zero_copy(e).wait()

    def start(g, carry):
        for u in range(8):
            for kk in range(TOP_K):
                pltpu.make_async_copy(h_ref.at[g, pl.ds(u, 1), :],
                                      xs_ref.at[pl.ds(dest_ref[kk, g * 8 + u], 1)], sem).start()
        return carry

    lax.fori_loop(0, tt // 8, start, 0)
    pltpu.make_async_copy(xs_ref.at[pl.ds(0, TOP_K * tt)], xs_ref.at[pl.ds(0, TOP_K * tt)], sem).wait()


def _dispatch(seg, dest, hp, n_rows, row_block):
    n, w = hp.shape
    tt = _tile(n, 512)
    grid_spec = pltpu.PrefetchScalarGridSpec(
        num_scalar_prefetch=1,
        grid=(n // tt,),
        in_specs=[
            pl.BlockSpec((TOP_K, tt), lambda i, s: (0, i), memory_space=pltpu.SMEM),
            pl.BlockSpec((tt // 8, 8, w), lambda i, s: (i, 0, 0)),
        ],
        out_specs=pl.BlockSpec(memory_space=pl.ANY),
        scratch_shapes=[pltpu.VMEM((row_block, w), hp.dtype), pltpu.SemaphoreType.DMA(()),
                        pltpu.SemaphoreType.DMA(())],
    )
    return pl.pallas_call(
        functools.partial(_dispatch_kernel, row_block=row_block),
        grid_spec=grid_spec,
        out_shape=jax.ShapeDtypeStruct((n_rows, w), hp.dtype),
        compiler_params=_params("arbitrary"),
        name="dispatch_rows",
    )(seg, dest, hp.reshape(n // 8, 8, w))


def _expert_kernel(meta_ref, xs_ref, wg_ref, wl_ref, bg_ref, bl_ref, wd_ref, bd_ref, ys_ref,
                   x_scr, y_ref):
    i = pl.program_id(0)
    j = pl.program_id(1)

    @pl.when(i < meta_ref[0])
    def _():
        @pl.when(j == 0)
        def _():
            half = xs_ref.shape[1]
            lo, hi = _unpack_bf16_pairs(xs_ref[...])
            x_scr[:, :half] = lo
            x_scr[:, half:] = hi
            y_ref[...] = jnp.broadcast_to(bd_ref[...], y_ref.shape)

        x = x_scr[...]
        glu = _dot(x, wg_ref[...]) + bg_ref[...]
        lin = _dot(x, wl_ref[...]) + bl_ref[...]
        glu = jnp.minimum(glu, SWIGLU_LIMIT)
        lin = jnp.clip(lin, -SWIGLU_LIMIT, SWIGLU_LIMIT)
        act = glu * _sigmoid(SWIGLU_ALPHA * glu) * (lin + 1.0)
        y_ref[...] = y_ref[...] + _dot(act.astype(BF16), wd_ref[...].astype(BF16))

        @pl.when(j == pl.num_programs(1) - 1)
        def _():
            ys_ref[...] = _pack_bf16_pairs(y_ref[...])


def _experts(meta, xs, w_gu, b_gu, w_dn, b_dn, row_block):
    n_rows, half = xs.shape
    d = 2 * half
    d_ff = w_dn.shape[1]
    tf = _tile(d_ff, 1024)
    nf = d_ff // tf
    n_blocks = n_rows // row_block

    def jj(i, j, m):
        return jnp.where(i < m[0], j, nf - 1)

    grid_spec = pltpu.PrefetchScalarGridSpec(
        num_scalar_prefetch=1,
        grid=(n_blocks, nf),
        in_specs=[
            pl.BlockSpec((row_block, half), lambda i, j, m: (i, 0)),
            pl.BlockSpec((None, d, tf), lambda i, j, m: (m[1 + i], 0, jj(i, j, m))),
            pl.BlockSpec((None, d, tf), lambda i, j, m: (m[1 + i], 0, nf + jj(i, j, m))),
            pl.BlockSpec((None, 1, tf), lambda i, j, m: (m[1 + i], 0, jj(i, j, m))),
            pl.BlockSpec((None, 1, tf), lambda i, j, m: (m[1 + i], 0, nf + jj(i, j, m))),
            pl.BlockSpec((None, tf, d), lambda i, j, m: (m[1 + i], jj(i, j, m), 0)),
            pl.BlockSpec((None, 1, d), lambda i, j, m: (m[1 + i], 0, 0)),
        ],
        out_specs=pl.BlockSpec((row_block, half), lambda i, j, m: (i, 0)),
        scratch_shapes=[pltpu.VMEM((row_block, d), BF16), pltpu.VMEM((row_block, d), F32)],
    )
    return pl.pallas_call(
        _expert_kernel,
        grid_spec=grid_spec,
        out_shape=jax.ShapeDtypeStruct((n_rows, half), jnp.uint32),
        compiler_params=_params("arbitrary", "arbitrary"),
        name="expert_swiglu",
    )(meta, xs, w_gu, w_gu, b_gu, b_gu, w_dn, b_dn)


def _combine_kernel(dest_ref, dest_next_ref, gate_ref, x1_ref, fn_ref, ys_ref, o_ref, buf, sem):
    tt = dest_ref.shape[1]
    i = pl.program_id(0)
    slot = i % 2

    def gather(d_ref, s):
        def start(g, carry):
            for u in range(8):
                for kk in range(TOP_K):
                    pltpu.make_async_copy(ys_ref.at[pl.ds(d_ref[kk, g * 8 + u], 1)],
                                          buf.at[s, kk * (tt // 8) + g, pl.ds(u, 1), :], sem.at[s]).start()
            return carry
        lax.fori_loop(0, tt // 8, start, 0)

    @pl.when(i == 0)
    def _():
        gather(dest_ref, 0)

    @pl.when(i + 1 < pl.num_programs(0))
    def _():
        gather(dest_next_ref, 1 - slot)

    pltpu.make_async_copy(ys_ref.at[pl.ds(0, TOP_K * tt)], ys_ref.at[pl.ds(0, TOP_K * tt)], sem.at[slot]).wait()
    half = buf.shape[3]
    gates = gate_ref[...]
    acc_lo = x1_ref[:, :half]
    acc_hi = x1_ref[:, half:]
    for kk in range(TOP_K):
        u = buf[slot, pl.ds(kk * (tt // 8), tt // 8)].reshape(tt, half)
        g = gates[:, kk:kk + 1]
        acc_lo = acc_lo + pltpu.bitcast(u << 16, F32) * g
        acc_hi = acc_hi + pltpu.bitcast(u & jnp.uint32(0xFFFF0000), F32) * g
    ms = (jnp.sum(acc_lo * acc_lo, axis=-1, keepdims=True)
          + jnp.sum(acc_hi * acc_hi, axis=-1, keepdims=True)) / (2 * half)
    inv = lax.rsqrt(ms + NORM_EPS)
    o_ref[:, :half] = acc_lo * inv * fn_ref[:, :half]
    o_ref[:, half:] = acc_hi * inv * fn_ref[:, half:]


def _combine(dest, gates_t, x1, final_gain, ys, row_off):
    n, d = x1.shape
    tt = _tile(n, 256)
    assert row_off % tt == 0
    off = row_off // tt
    nt = n // tt
    return pl.pallas_call(
        _combine_kernel,
        grid=(nt,),
        in_specs=[
            pl.BlockSpec((TOP_K, tt), lambda i: (0, off + i), memory_space=pltpu.SMEM),
            pl.BlockSpec((TOP_K, tt), lambda i: (0, off + jnp.minimum(i + 1, nt - 1)), memory_space=pltpu.SMEM),
            pl.BlockSpec((tt, TOP_K), lambda i: (off + i, 0)),
            pl.BlockSpec((tt, d), lambda i: (i, 0)),
            pl.BlockSpec((1, d), lambda i: (0, 0)),
            pl.BlockSpec(memory_space=pl.ANY),
        ],
        out_specs=pl.BlockSpec((tt, d), lambda i: (i, 0)),
        out_shape=jax.ShapeDtypeStruct((n, d), F32),
        scratch_shapes=[pltpu.VMEM((2, TOP_K * tt // 8, 8, d // 2), jnp.uint32), pltpu.SemaphoreType.DMA((2,))],
        compiler_params=_params("arbitrary"),
        name="combine_rows",
    )(dest, dest, gates_t, x1, final_gain, ys)


def _rope_tables(seq_len):
    rows = seq_len // GRID_W
    row = jnp.repeat(jnp.arange(rows, dtype=F32), GRID_W)
    col = jnp.tile(jnp.arange(GRID_W, dtype=F32), rows)
    freqs = ROPE_THETA ** (-jnp.arange(ROPE_HALF, dtype=F32) / ROPE_HALF)
    ang_r = row[:, None] * freqs[None, :]
    ang_c = col[:, None] * freqs[None, :]
    cos = jnp.concatenate([jnp.cos(ang_r), jnp.cos(ang_r), jnp.cos(ang_c), jnp.cos(ang_c)], axis=1)
    sin = jnp.concatenate([-jnp.sin(ang_r), jnp.sin(ang_r), -jnp.sin(ang_c), jnp.sin(ang_c)], axis=1)
    return cos, sin


def kernel(x_prompt, x_sample, mix_norm, w_in, q_norm, k_norm, hg_lb_logits, hg_out_norm, w_up_attn,
           w_up_hgrn, w_out, ffn_norm, w_router, b_router, w_gate_up, b_gate_up, w_down, b_down, final_norm):
    assert mix_norm.shape[0] == 1, "single trunk layer"
    d = x_prompt.shape[-1]
    hg_w = HG_HEADS * HG_D
    n_exp = w_router.shape[-1]
    row_block = MOE_ROW_BLOCK
    streams = [(x.reshape(-1, d), x.shape[0], x.shape[1]) for x in (x_prompt, x_sample)]
    n_total = sum(x.shape[0] for x, _, _ in streams)

    lb = jnp.cumsum(jax.nn.softmax(hg_lb_logits.astype(F32), axis=1), axis=1)[:, 0]
    w_in_b = w_in[0].astype(BF16)
    w_ua_b = w_up_attn[0].astype(BF16)
    w_uh_b = w_up_hgrn[0].astype(BF16)
    w_out_b = w_out[0].astype(BF16)
    w_r_t = w_router[0].T
    mix_g = mix_norm[0].reshape(1, d)
    score_bound = (1.02 * LOG2_E * HEAD_DIM ** 0.5 * jnp.max(jnp.abs(q_norm[0])) * jnp.max(jnp.abs(k_norm[0])))
    score_bound = score_bound.astype(F32).reshape(1)

    w_gu2d = w_gate_up[0].reshape(-1, w_gate_up.shape[-1])
    cast_host = max(range(len(streams)), key=lambda s: streams[s][0].shape[0])
    w_gu_b = None

    x1s, idxs, gate_ts = [], [], []
    hp = None
    row_off = 0
    for s, (x, batch, seq) in enumerate(streams):
        cos, sin = _rope_tables(seq)
        qkv, zh, gates = _norm_proj(x, mix_g, w_in_b, q_norm[0].reshape(1, HEAD_DIM),
                                    k_norm[0].reshape(1, HEAD_DIM), cos, sin, seq)
        if s == cast_host:
            attn, w_gu_b = _attention(qkv, score_bound, batch, seq, w_gu2d)
        else:
            attn, = _attention(qkv, score_bound, batch, seq)
        hg = _hgrn(zh, lb[0:1], lb[1:2], hg_out_norm[0].reshape(1, hg_w), batch, seq)
        merged = _merge(attn, hg, gates, w_ua_b, w_uh_b)
        x1, hp, idx, gate = _outproj_router(x, merged, w_out_b, ffn_norm[0].reshape(1, d), w_r_t,
                                            b_router[0].reshape(n_exp, 1), hp, row_off, n_total)
        x1s.append(x1)
        idxs.append(idx)
        gate_ts.append(gate.T)
        row_off += x.shape[0]

    idx = jnp.concatenate(idxs, axis=1)
    gate_t = jnp.concatenate(gate_ts, axis=0)
    dest, counts = _route(idx, row_block)
    cnt = counts[:, 0].astype(jnp.int32)
    padded = (cnt + row_block - 1) // row_block * row_block
    pad_end = jnp.cumsum(padded)
    n_rows = n_total * TOP_K + n_exp * row_block
    n_blocks = n_rows // row_block
    blk_start = jnp.arange(n_blocks, dtype=jnp.int32) * row_block
    blk_e = jnp.minimum(jnp.sum(pad_end[None, :] <= blk_start[:, None], axis=1), n_exp - 1).astype(jnp.int32)
    meta = jnp.concatenate([(pad_end[-1:] // row_block).astype(jnp.int32), blk_e])
    seg = jnp.stack([pad_end - padded, pad_end]).astype(jnp.int32)

    xs = _dispatch(seg, dest, hp, n_rows, row_block)
    ys = _experts(meta, xs, w_gu_b.reshape(w_gate_up.shape[1:]), b_gate_up[0].reshape(n_exp, 1, -1),
                  w_down[0], b_down[0].reshape(n_exp, 1, d), row_block)

    outs = []
    row_off = 0
    for (x, batch, seq), x1 in zip(streams, x1s):
        out = _combine(dest, gate_t, x1, final_norm.reshape(1, d), ys, row_off)
        outs.append(out.reshape(batch, seq, d))
        row_off += x.shape[0]
    return tuple(outs)
```

```python
import functools

import jax
import jax.numpy as jnp
from jax import lax
from jax.experimental import pallas as pl
from jax.experimental.pallas import tpu as pltpu

GRID_W = 64
HEAD_DIM = 128
N_Q_HEADS = 16
N_KV_HEADS = 4
ROPE_THETA = 10000.0
ROPE_HALF = HEAD_DIM // 4
HG_HEADS = 8
HG_D = 128
HG_CHUNK = 64
N_EXPERTS = 32
TOP_K = 4
SWIGLU_LIMIT = 7.0
SWIGLU_ALPHA = 1.702
NORM_EPS = 1e-5

HG_SAFE_LOG_DECAY = -60.0

LOG2_E = 1.4426950408889634
ATTN_FIXED_SHIFT_LIMIT = 60.0

V7X_VMEM_BYTES = 64 * 1024 * 1024
VMEM_LIMIT_BYTES = V7X_VMEM_BYTES - 8 * 1024 * 1024
LANES = 128
MOE_ROW_BLOCK = 512

BF16 = jnp.bfloat16
F32 = jnp.float32


def _params(*sem):
    return pltpu.CompilerParams(dimension_semantics=sem, vmem_limit_bytes=VMEM_LIMIT_BYTES)


def _tile(n, pref):
    t = min(n, pref)
    while n % t:
        t //= 2
    return t


def _sigmoid(x):
    return 1.0 / (1.0 + jnp.exp(-x))


def _rmsnorm(x, g):
    return x * lax.rsqrt(jnp.mean(x * x, axis=-1, keepdims=True) + NORM_EPS) * g


def _dot(a, b):
    return jnp.dot(a, b, preferred_element_type=F32)


def _dot_nt(a, b):
    return lax.dot_general(a, b, (((1,), (1,)), ((), ())), preferred_element_type=F32)


def _split3(x):
    hi = x.astype(BF16)
    r = x - hi.astype(F32)
    mid = r.astype(BF16)
    lo = (r - mid.astype(F32)).astype(BF16)
    return hi, mid, lo


def _dot_exact_lhs(m_bf16, x, pieces=3):
    parts = _split3(x)[:pieces]
    acc = _dot(m_bf16, parts[0])
    for p in parts[1:]:
        acc = acc + _dot(m_bf16, p)
    return acc


def _rope_head_pair(zp, gain, cos, sin, ones_blk, perm_blk, scale):
    ss = _dot((zp * zp).astype(BF16), ones_blk)
    y = zp * lax.rsqrt(ss * (1.0 / HEAD_DIM) + NORM_EPS) * gain
    y_hi = y.astype(BF16)
    y_lo = (y - y_hi.astype(F32)).astype(BF16)
    partner = _dot(y_hi, perm_blk) + _dot(y_lo, perm_blk)
    return (y * cos + partner * sin) * scale


def _proj_kernel(x_ref, g_ref, w_ref, qn_ref, kn_ref, cos_ref, sin_ref, ones_ref, perm_ref,
                 qkv_ref, zh_ref, gate_ref, h_scr, *, tile_kinds):
    j = pl.program_id(1)

    @pl.when(j == 0)
    def _():
        h_scr[...] = _rmsnorm(x_ref[...], g_ref[...]).astype(BF16)

    z = _dot(h_scr[...], w_ref[...])

    def qkv_tile(kinds):
        two = lambda r: jnp.concatenate([r[...], r[...]], axis=1)
        outs = []
        for h in range(0, len(kinds), 2):
            kind = kinds[h]
            assert kinds[h + 1] == kind
            zp = z[:, h * HEAD_DIM:(h + 2) * HEAD_DIM]
            if kind == "q":
                zp = _rope_head_pair(zp, two(qn_ref), two(cos_ref), two(sin_ref), ones_ref[...], perm_ref[...],
                                     LOG2_E * HEAD_DIM ** -0.5)
            elif kind == "k":
                zp = _rope_head_pair(zp, two(kn_ref), two(cos_ref), two(sin_ref), ones_ref[...], perm_ref[...], 1.0)
            outs.append(zp)
        return jnp.concatenate(outs, axis=1)

    for lo, hi, kind in tile_kinds:
        @pl.when((j >= lo) & (j < hi))
        def _(kind=kind):
            if kind == "zh":
                zh_ref[...] = z
            elif kind == "gate":
                gate_ref[...] = z.astype(gate_ref.dtype)
            else:
                qkv_ref[...] = qkv_tile(kind).astype(qkv_ref.dtype)


def _norm_proj(x, gain, w, q_gain, k_gain, cos, sin, seq):
    n, d = x.shape
    attn_w = N_Q_HEADS * HEAD_DIM
    kv_w = N_KV_HEADS * HEAD_DIM
    hg5 = 5 * HG_HEADS * HG_D
    qkv_w = attn_w + 2 * kv_w
    assert w.shape[1] == qkv_w + hg5 + 2 * d
    tm = _tile(seq, 1024)
    tn = 1024
    while attn_w % tn or (2 * kv_w) % tn or hg5 % tn or (2 * d) % tn:
        tn //= 2
    assert tn % HEAD_DIM == 0
    heads = ["q"] * N_Q_HEADS + ["k"] * N_KV_HEADS + ["v"] * N_KV_HEADS
    hpt = tn // HEAD_DIM
    n_qkv, n_zh, n_gate = qkv_w // tn, hg5 // tn, 2 * d // tn
    tile_kinds = []
    for t in range(n_qkv):
        kind = tuple(heads[t * hpt:(t + 1) * hpt])
        if tile_kinds and tile_kinds[-1][2] == kind:
            tile_kinds[-1] = (tile_kinds[-1][0], t + 1, kind)
        else:
            tile_kinds.append((t, t + 1, kind))
    tile_kinds += [(n_qkv, n_qkv + n_zh, "zh"), (n_qkv + n_zh, n_qkv + n_zh + n_gate, "gate")]
    n_pos = seq // tm
    r = jnp.arange(2 * HEAD_DIM)
    same_head = (r[:, None] // HEAD_DIM) == (r[None, :] // HEAD_DIM)
    ones_blk = same_head.astype(BF16)
    partner_of = jnp.where((r % (2 * ROPE_HALF)) < ROPE_HALF, r + ROPE_HALF, r - ROPE_HALF)
    perm_blk = (r[:, None] == partner_of[None, :]).astype(BF16)
    const_spec = pl.BlockSpec((2 * HEAD_DIM, 2 * HEAD_DIM), lambda i, j: (0, 0))
    return pl.pallas_call(
        functools.partial(_proj_kernel, tile_kinds=tile_kinds),
        grid=(n // tm, n_qkv + n_zh + n_gate),
        in_specs=[
            pl.BlockSpec((tm, d), lambda i, j: (i, 0)),
            pl.BlockSpec((1, d), lambda i, j: (0, 0)),
            pl.BlockSpec((d, tn), lambda i, j: (0, j)),
            pl.BlockSpec((1, HEAD_DIM), lambda i, j: (0, 0)),
            pl.BlockSpec((1, HEAD_DIM), lambda i, j: (0, 0)),
            pl.BlockSpec((tm, HEAD_DIM), lambda i, j: (i % n_pos, 0)),
            pl.BlockSpec((tm, HEAD_DIM), lambda i, j: (i % n_pos, 0)),
            const_spec,
            const_spec,
        ],
        out_specs=[
            pl.BlockSpec((tm, tn), lambda i, j: (i, jnp.minimum(j, n_qkv - 1))),
            pl.BlockSpec((tm, tn), lambda i, j: (i, jnp.clip(j - n_qkv, 0, n_zh - 1))),
            pl.BlockSpec((tm, tn), lambda i, j: (i, jnp.clip(j - n_qkv - n_zh, 0, n_gate - 1))),
        ],
        out_shape=[
            jax.ShapeDtypeStruct((n, qkv_w), BF16),
            jax.ShapeDtypeStruct((n, hg5), F32),
            jax.ShapeDtypeStruct((n, 2 * d), BF16),
        ],
        scratch_shapes=[pltpu.VMEM((tm, d), BF16)],
        compiler_params=_params("parallel", "arbitrary"),
        name="norm_proj",
    )(x, gain, w, q_gain, k_gain, cos, sin, ones_blk, perm_blk)


def _attn_kernel(bound_ref, q_ref, k_ref, v_ref, *rest, tk, group):
    if len(rest) == 5:
        wf_ref, o_ref, wb_ref, acc_scr, m_scr = rest
    else:
        wf_ref = wb_ref = None
        o_ref, acc_scr, m_scr = rest
    tq = q_ref.shape[0]
    seq = k_ref.shape[0]
    n_chunks = seq // tk
    slab = None if wf_ref is None else wf_ref.shape[0] // n_chunks
    q = jnp.concatenate([q_ref[:, g * HEAD_DIM:(g + 1) * HEAD_DIM] for g in range(group)], axis=0)
    ones = jnp.ones((tk, HEAD_DIM), BF16)
    bound = bound_ref[0]
    acc_scr[...] = jnp.zeros(acc_scr.shape, F32)

    def chunk(c):
        rows = pl.ds(pl.multiple_of(c * tk, tk), tk)
        s = _dot_nt(q, k_ref[rows, :])
        return s, jnp.concatenate([v_ref[rows, :], ones], axis=1)

    @pl.when(bound <= ATTN_FIXED_SHIFT_LIMIT)
    def _():
        def body(c, carry):
            s, v1 = chunk(c)
            acc_scr[...] += _dot(jnp.exp2(s - bound).astype(BF16), v1)
            if wf_ref is not None:
                wrows = pl.ds(pl.multiple_of(c * slab, slab), slab)
                wb_ref[wrows, :] = wf_ref[wrows, :].astype(BF16)
            return carry
        lax.fori_loop(0, n_chunks, body, 0, unroll=True)

    @pl.when(bound > ATTN_FIXED_SHIFT_LIMIT)
    def _():
        if wf_ref is not None:
            wb_ref[...] = wf_ref[...].astype(BF16)
        m_scr[...] = jnp.full(m_scr.shape, -1e30, F32)

        def body(c, carry):
            s, v1 = chunk(c)
            m_prev = m_scr[...]
            m_new = jnp.maximum(m_prev, jnp.max(s, axis=1, keepdims=True))
            alpha = jnp.exp2(m_prev - m_new)
            p = jnp.exp2(s - jnp.tile(m_new, (1, tk // LANES)))
            acc_scr[...] = jnp.tile(alpha, (1, 2)) * acc_scr[...] + _dot(p.astype(BF16), v1)
            m_scr[...] = m_new
            return carry
        lax.fori_loop(0, seq // tk, body, 0)

    acc = acc_scr[...]
    o = acc[:, :HEAD_DIM] / acc[:, HEAD_DIM:]
    for g in range(group):
        o_ref[:, g * HEAD_DIM:(g + 1) * HEAD_DIM] = o[g * tq:(g + 1) * tq].astype(o_ref.dtype)


def _attention(qkv, score_bound, batch, seq, w_f32=None):
    group = N_Q_HEADS // N_KV_HEADS
    gw = group * HEAD_DIM
    tq = _tile(seq, 512)
    tk = _tile(seq, 512)
    nq = seq // tq
    in_specs = [
        pl.BlockSpec(memory_space=pltpu.SMEM),
        pl.BlockSpec((tq, gw), lambda b, n, i: (b * nq + i, n)),
        pl.BlockSpec((seq, HEAD_DIM), lambda b, n, i: (b, N_Q_HEADS + n)),
        pl.BlockSpec((seq, HEAD_DIM), lambda b, n, i: (b, N_Q_HEADS + N_KV_HEADS + n)),
    ]
    out_specs = [pl.BlockSpec((tq, gw), lambda b, n, i: (b * nq + i, n))]
    out_shape = [jax.ShapeDtypeStruct((batch * seq, N_Q_HEADS * HEAD_DIM), BF16)]
    args = [score_bound, qkv, qkv, qkv]
    if w_f32 is not None:
        n_steps = batch * N_KV_HEADS * nq
        w_rows, w_cols = w_f32.shape
        slab = w_rows // n_steps
        assert slab * n_steps == w_rows and slab % (16 * (seq // tk)) == 0, (w_rows, n_steps, seq // tk)
        wspec = pl.BlockSpec((slab, w_cols), lambda b, n, i: ((b * N_KV_HEADS + n) * nq + i, 0))
        in_specs.append(wspec)
        out_specs.append(wspec)
        out_shape.append(jax.ShapeDtypeStruct((w_rows, w_cols), BF16))
        args.append(w_f32)
    return pl.pallas_call(
        functools.partial(_attn_kernel, tk=tk, group=group),
        grid=(batch, N_KV_HEADS, nq),
        in_specs=in_specs,
        out_specs=out_specs,
        out_shape=out_shape,
        scratch_shapes=[
            pltpu.VMEM((group * tq, 2 * HEAD_DIM), F32),
            pltpu.VMEM((group * tq, LANES), F32),
        ],
        compiler_params=_params("parallel", "parallel", "arbitrary"),
        name="gqa_attention",
    )(*args)


def _hgrn_span(i, q_scr, b_scr, k_scr, zi_ref, vt_scr, o_scr, s_scr, tri, *, span, reverse, pairwise):
    C = HG_CHUNK
    nc = span // C
    r0 = pl.multiple_of(i * span, span)
    rows = pl.ds(r0, span)
    q = q_scr[rows, :]
    b = b_scr[rows, :]
    k = k_scr[rows, :]
    v = zi_ref[rows, :]
    vb = v.astype(BF16)
    end_row = 0 if reverse else C - 1
    ends = [b[c * C + end_row:c * C + end_row + 1, :] for c in range(nc)]
    b_end = jnp.concatenate([jnp.broadcast_to(e, (C, HG_D)) for e in ends], axis=0)
    qtb = (q * jnp.exp(b)).astype(BF16)
    kp = k * jnp.exp(b_end - b)

    if pairwise:
        rid = lax.broadcasted_iota(jnp.int32, (span, HG_D), 0)

        def pair(s, acc):
            cs = (s // C) * C
            if reverse:
                m = (rid <= s) & (rid >= cs)
            else:
                m = (rid >= s) & (rid < cs + C)
            w = jnp.where(m, jnp.exp(jnp.minimum(b - b_scr[pl.ds(r0 + s, 1), :], 0.0)), 0.0)
            r = jnp.sum(q * w * k_scr[pl.ds(r0 + s, 1), :], axis=1, keepdims=True)
            return acc + r * zi_ref[pl.ds(r0 + s, 1), :]

        o_intra = lax.fori_loop(0, span, pair, jnp.zeros((span, HG_D), F32))
    else:
        ktb = (k * jnp.exp(-b)).astype(BF16)
        a = jnp.where(tri, _dot_nt(qtb, ktb), 0.0)
        o_intra = _dot(a.astype(BF16), vb)

    chunk_of_row = lax.broadcasted_iota(jnp.int32, (span, HG_D), 0) // C
    kp_blocks = jnp.concatenate([jnp.where(chunk_of_row == c, kp, 0.0) for c in range(nc)], axis=1)
    upd = _dot(vt_scr[:, rows], kp_blocks.astype(BF16))

    st = s_scr[...]
    states = [None] * nc
    for c in (reversed(range(nc)) if reverse else range(nc)):
        states[c] = st.astype(BF16)
        st = st * jnp.exp(ends[c]) + upd[:, c * HG_D:(c + 1) * HG_D]
    s_scr[...] = st
    o_inter = _dot_nt(qtb, jnp.concatenate(states, axis=0))
    o_scr[rows, :] = o_intra + jnp.concatenate(
        [o_inter[c * C:(c + 1) * C, c * HG_D:(c + 1) * HG_D] for c in range(nc)], axis=0)


def _hgrn_kernel(zq_ref, zf_ref, zb_ref, zi_ref, zo_ref, lbf_ref, lbb_ref, on_ref, out_ref,
                 of_scr, ob_scr, sf_scr, sb_scr, q_scr, bf_scr, bb_scr, kf_scr, kb_scr, vt_scr, *, span):
    seq = zq_ref.shape[0]
    n_span = seq // span
    C = HG_CHUNK
    nc = span // C
    sf_scr[...] = jnp.zeros(sf_scr.shape, F32)
    sb_scr[...] = jnp.zeros(sb_scr.shape, F32)
    r = lax.broadcasted_iota(jnp.int32, (span, span), 0)
    c = lax.broadcasted_iota(jnp.int32, (span, span), 1)
    same = (r // C) == (c // C)
    tri_f = same & (c <= r)
    tri_b = same & (c >= r)
    tri_f16 = tri_f.astype(BF16)
    tri_b16 = tri_b.astype(BF16)

    def prepare(i, min_end):
        rows = pl.ds(pl.multiple_of(i * span, span), span)
        qh = zq_ref[rows, :]
        q_scr[rows, :] = qh * _sigmoid(qh)
        vt_scr[:, rows] = zi_ref[rows, :].T.astype(BF16)
        for z_ref, lb_ref, tri, b_scr, k_scr, end_row in (
                (zf_ref, lbf_ref, tri_f16, bf_scr, kf_scr, C - 1), (zb_ref, lbb_ref, tri_b16, bb_scr, kb_scr, 0)):
            lb = lb_ref[...]
            f = lb + (1.0 - lb) * _sigmoid(z_ref[rows, :])
            b = _dot_exact_lhs(tri, jnp.log(f), pieces=2)
            b_scr[rows, :] = b
            k_scr[rows, :] = 1.0 - f
            for cc in range(nc):
                min_end = jnp.minimum(min_end, b[cc * C + end_row:cc * C + end_row + 1, :])
        return min_end

    min_end = lax.fori_loop(0, n_span, prepare, jnp.zeros((1, HG_D), F32), unroll=2)
    safe = jnp.min(min_end) > HG_SAFE_LOG_DECAY

    def scan(pairwise):
        def body(i, carry):
            _hgrn_span(i, q_scr, bf_scr, kf_scr, zi_ref, vt_scr, of_scr, sf_scr, tri_f,
                       span=span, reverse=False, pairwise=pairwise)
            _hgrn_span(n_span - 1 - i, q_scr, bb_scr, kb_scr, zi_ref, vt_scr, ob_scr, sb_scr, tri_b,
                       span=span, reverse=True, pairwise=pairwise)
            return carry
        lax.fori_loop(0, n_span, body, 0, unroll=1 if pairwise else 4)

    @pl.when(safe)
    def _():
        scan(False)

    @pl.when(jnp.logical_not(safe))
    def _():
        scan(True)

    def finish(i, carry):
        rows = pl.ds(pl.multiple_of(i * span, span), span)
        o = _rmsnorm(of_scr[rows, :] + ob_scr[rows, :], on_ref[...])
        og = zo_ref[rows, :]
        out_ref[rows, :] = (o * (og * _sigmoid(og))).astype(out_ref.dtype)
        return carry

    lax.fori_loop(0, n_span, finish, 0)


def _hgrn(zh, lb_f, lb_b, out_norm, batch, seq):
    span = _tile(seq, 256)
    assert span % HG_CHUNK == 0
    zspec = lambda grp: pl.BlockSpec((seq, HG_D), lambda b, h: (b, grp * HG_HEADS + h))
    hspec = pl.BlockSpec((1, HG_D), lambda b, h: (0, h))
    seq_buf = pltpu.VMEM((seq, HG_D), F32)
    return pl.pallas_call(
        functools.partial(_hgrn_kernel, span=span),
        grid=(batch, HG_HEADS),
        in_specs=[zspec(0), zspec(1), zspec(2), zspec(3), zspec(4), hspec, hspec, hspec],
        out_specs=pl.BlockSpec((seq, HG_D), lambda b, h: (b, h)),
        out_shape=jax.ShapeDtypeStruct((batch * seq, HG_HEADS * HG_D), BF16),
        scratch_shapes=[
            seq_buf, seq_buf,
            pltpu.VMEM((HG_D, HG_D), F32), pltpu.VMEM((HG_D, HG_D), F32),
            seq_buf, seq_buf, seq_buf, seq_buf, seq_buf,
            pltpu.VMEM((HG_D, seq), BF16),
        ],
        compiler_params=_params("parallel", "parallel"),
        name="hgrn2",
    )(zh, zh, zh, zh, zh, lb_f, lb_b, out_norm)


def _merge_kernel(a_ref, h_ref, ga_ref, gb_ref, wa_ref, wh_ref, o_ref):
    ya = _dot(a_ref[...], wa_ref[...])
    yb = _dot(h_ref[...], wh_ref[...])
    merged = _sigmoid(ga_ref[...].astype(F32)) * ya + _sigmoid(gb_ref[...].astype(F32)) * yb
    o_ref[...] = merged.astype(o_ref.dtype)


def _merge(attn, hg, gates, w_up_attn, w_up_hgrn):
    n, wa = attn.shape
    wh = hg.shape[1]
    d = w_up_attn.shape[1]
    tm = _tile(n, 1024)
    tn = _tile(d, 512)
    nj = d // tn
    return pl.pallas_call(
        _merge_kernel,
        grid=(n // tm, nj),
        in_specs=[
            pl.BlockSpec((tm, wa), lambda i, j: (i, 0)),
            pl.BlockSpec((tm, wh), lambda i, j: (i, 0)),
            pl.BlockSpec((tm, tn), lambda i, j: (i, j)),
            pl.BlockSpec((tm, tn), lambda i, j: (i, nj + j)),
            pl.BlockSpec((wa, tn), lambda i, j: (0, j)),
            pl.BlockSpec((wh, tn), lambda i, j: (0, j)),
        ],
        out_specs=pl.BlockSpec((tm, tn), lambda i, j: (i, j)),
        out_shape=jax.ShapeDtypeStruct((n, d), BF16),
        compiler_params=_params("parallel", "arbitrary"),
        name="gated_merge",
    )(attn, hg, gates, gates, w_up_attn, w_up_hgrn)


def _pack_bf16_pairs(h):
    half = h.shape[1] // 2
    lo = pltpu.bitcast(h[:, :half].astype(BF16).astype(F32), jnp.uint32)
    hi = pltpu.bitcast(h[:, half:].astype(BF16).astype(F32), jnp.uint32)
    return (hi & jnp.uint32(0xFFFF0000)) | (lo >> 16)


def _unpack_bf16_pairs(u):
    lo = pltpu.bitcast(u << 16, F32).astype(BF16)
    hi = pltpu.bitcast(u & jnp.uint32(0xFFFF0000), F32).astype(BF16)
    return lo, hi


def _outproj_router_kernel(x_ref, m_ref, w_ref, g_ref, wr_ref, br_ref, hp_in_ref,
                           x1_ref, hp_ref, idx_ref, gate_ref):
    del hp_in_ref
    x1 = x_ref[...] + _dot(m_ref[...], w_ref[...])
    x1_ref[...] = x1
    h = _rmsnorm(x1, g_ref[...])
    hp_ref[...] = _pack_bf16_pairs(h)
    h_hi, h_mid, _ = _split3(h)
    w_hi, w_mid, _ = _split3(wr_ref[...])
    lg = (_dot_nt(w_hi, h_hi) + _dot_nt(w_hi, h_mid) + _dot_nt(w_mid, h_hi)) + br_ref[...]
    n_exp, tm = lg.shape
    eid = lax.broadcasted_iota(jnp.int32, (n_exp, tm), 0)
    vals = []
    for kk in range(TOP_K):
        m = jnp.max(lg, axis=0, keepdims=True)
        sel = jnp.min(jnp.where(lg == m, eid, n_exp), axis=0, keepdims=True)
        idx_ref[kk:kk + 1, :] = sel
        vals.append(m)
        lg = jnp.where(eid == sel, -jnp.inf, lg)
    ex = [jnp.exp(vv - vals[0]) for vv in vals]
    den = ex[0]
    for e in ex[1:]:
        den = den + e
    for kk in range(TOP_K):
        gate_ref[kk:kk + 1, :] = ex[kk] / den


def _outproj_router(x, merged, w_out, ffn_gain, w_router_t, b_router, hp_prev, row_off, n_total):
    n, d = x.shape
    tm = _tile(n, 512)
    assert row_off % tm == 0
    n_exp = w_router_t.shape[0]
    if hp_prev is None:
        hp_prev = jnp.zeros((8, LANES), jnp.uint32)
        aliases = {}
    else:
        aliases = {6: 1}
    return pl.pallas_call(
        _outproj_router_kernel,
        grid=(n // tm,),
        in_specs=[
            pl.BlockSpec((tm, d), lambda i: (i, 0)),
            pl.BlockSpec((tm, d), lambda i: (i, 0)),
            pl.BlockSpec((d, d), lambda i: (0, 0)),
            pl.BlockSpec((1, d), lambda i: (0, 0)),
            pl.BlockSpec((n_exp, d), lambda i: (0, 0)),
            pl.BlockSpec((n_exp, 1), lambda i: (0, 0)),
            pl.BlockSpec(memory_space=pl.ANY),
        ],
        out_specs=[
            pl.BlockSpec((tm, d), lambda i: (i, 0)),
            pl.BlockSpec((tm, d // 2), lambda i: (row_off // tm + i, 0)),
            pl.BlockSpec((TOP_K, tm), lambda i: (0, i)),
            pl.BlockSpec((TOP_K, tm), lambda i: (0, i)),
        ],
        out_shape=[
            jax.ShapeDtypeStruct((n, d), F32),
            jax.ShapeDtypeStruct((n_total, d // 2), jnp.uint32),
            jax.ShapeDtypeStruct((TOP_K, n), jnp.int32),
            jax.ShapeDtypeStruct((TOP_K, n), F32),
        ],
        input_output_aliases=aliases,
        compiler_params=_params("parallel"),
        name="outproj_router",
    )(x, merged, w_out, ffn_gain, w_router_t, b_router, hp_prev)


def _route_kernel(idx_ref, dest_ref, cnt_ref, cnt_scr, base_scr, *, row_block):
    phase = pl.program_id(0)
    i = pl.program_id(1)
    n_exp = cnt_scr.shape[0]
    tt = idx_ref.shape[1]
    eid = lax.broadcasted_iota(jnp.int32, (n_exp, tt), 0)
    onehot = [(eid == idx_ref[kk:kk + 1, :]) for kk in range(TOP_K)]

    @pl.when((phase == 0) & (i == 0))
    def _():
        cnt_scr[...] = jnp.zeros(cnt_scr.shape, F32)

    @pl.when(phase == 0)
    def _():
        tot = onehot[0].astype(F32)
        for oh in onehot[1:]:
            tot = tot + oh.astype(F32)
        cnt_scr[...] = cnt_scr[...] + jnp.sum(tot, axis=1, keepdims=True)
        cnt_ref[...] = cnt_scr[...]

    @pl.when((phase == 1) & (i == 0))
    def _():
        cnt = cnt_scr[...].astype(jnp.int32)
        padded = ((cnt + (row_block - 1)) // row_block * row_block).astype(F32)
        er = lax.broadcasted_iota(jnp.int32, (n_exp, n_exp), 0)
        ec = lax.broadcasted_iota(jnp.int32, (n_exp, n_exp), 1)
        base_scr[...] = _dot_exact_lhs((ec < er).astype(BF16), padded)

    @pl.when(phase == 1)
    def _():
        tr = lax.broadcasted_iota(jnp.int32, (tt, tt), 0)
        tc = lax.broadcasted_iota(jnp.int32, (tt, tt), 1)
        before = (tr < tc).astype(BF16)
        run = base_scr[...][:, :1]
        for kk in range(TOP_K):
            oh = onehot[kk].astype(F32)
            rank = _dot(oh.astype(BF16), before) + run
            dest_ref[kk:kk + 1, :] = jnp.sum(oh * rank, axis=0, keepdims=True).astype(jnp.int32)
            run = run + jnp.sum(oh, axis=1, keepdims=True)
        base_scr[...] = jnp.broadcast_to(run, base_scr.shape)


def _route(idx, row_block):
    n = idx.shape[1]
    tt = _tile(n, 512)
    return pl.pallas_call(
        functools.partial(_route_kernel, row_block=row_block),
        grid=(2, n // tt),
        in_specs=[pl.BlockSpec((TOP_K, tt), lambda p, i: (0, i))],
        out_specs=[
            pl.BlockSpec((TOP_K, tt), lambda p, i: (0, i * p)),
            pl.BlockSpec((N_EXPERTS, LANES), lambda p, i: (0, 0)),
        ],
        out_shape=[
            jax.ShapeDtypeStruct((TOP_K, n), jnp.int32),
            jax.ShapeDtypeStruct((N_EXPERTS, LANES), F32),
        ],
        scratch_shapes=[pltpu.VMEM((N_EXPERTS, LANES), F32), pltpu.VMEM((N_EXPERTS, LANES), F32)],
        compiler_params=_params("arbitrary", "arbitrary"),
        name="route_offsets",
    )(idx)


def _dispatch_kernel(seg_ref, dest_ref, h_ref, xs_ref, zero_scr, sem, zsem, *, row_block):
    tt = dest_ref.shape[1]
    n_exp = seg_ref.shape[1]

    @pl.when(pl.program_id(0) == 0)
    def _():
        zero_scr[...] = jnp.zeros(zero_scr.shape, zero_scr.dtype)

        def zero_copy(e):
            start = pl.multiple_of(seg_ref[1, e] - row_block, row_block)
            return pltpu.make_async_copy(zero_scr, xs_ref.at[pl.ds(start, row_block)], zsem)

        for e in range(n_exp):
            @pl.when(seg_ref[1, e] > seg_ref[0, e])
            def _(e=e):
                zero_copy(e).start()
        for e in range(n_exp):
            @pl.when(seg_ref[1, e] > seg_ref[0, e])
            def _(e=e):
                zero_copy(e).wait()

    def start(g, carry):
        for u in range(8):
            for kk in range(TOP_K):
                pltpu.make_async_copy(h_ref.at[g, pl.ds(u, 1), :],
                                      xs_ref.at[pl.ds(dest_ref[kk, g * 8 + u], 1)], sem).start()
        return carry

    lax.fori_loop(0, tt // 8, start, 0)
    pltpu.make_async_copy(xs_ref.at[pl.ds(0, TOP_K * tt)], xs_ref.at[pl.ds(0, TOP_K * tt)], sem).wait()


def _dispatch(seg, dest, hp, n_rows, row_block):
    n, w = hp.shape
    tt = _tile(n, 1024)
    grid_spec = pltpu.PrefetchScalarGridSpec(
        num_scalar_prefetch=1,
        grid=(n // tt,),
        in_specs=[
            pl.BlockSpec((TOP_K, tt), lambda i, s: (0, i), memory_space=pltpu.SMEM),
            pl.BlockSpec((tt // 8, 8, w), lambda i, s: (i, 0, 0)),
        ],
        out_specs=pl.BlockSpec(memory_space=pl.ANY),
        scratch_shapes=[pltpu.VMEM((row_block, w), hp.dtype), pltpu.SemaphoreType.DMA(()),
                        pltpu.SemaphoreType.DMA(())],
    )
    return pl.pallas_call(
        functools.partial(_dispatch_kernel, row_block=row_block),
        grid_spec=grid_spec,
        out_shape=jax.ShapeDtypeStruct((n_rows, w), hp.dtype),
        compiler_params=_params("arbitrary"),
        name="dispatch_rows",
    )(seg, dest, hp.reshape(n // 8, 8, w))


def _expert_kernel(meta_ref, xs_ref, wg_ref, wl_ref, bg_ref, bl_ref, wd_ref, bd_ref, ys_ref,
                   x_scr, y_ref):
    i = pl.program_id(0)
    j = pl.program_id(1)
    n_blocks = pl.num_programs(0)
    row_block = xs_ref.shape[0]
    valid = meta_ref[1 + n_blocks + i]

    def block(m):
        rows = slice(0, m)

        @pl.when(j == 0)
        def _():
            half = xs_ref.shape[1]
            lo, hi = _unpack_bf16_pairs(xs_ref[rows, :])
            x_scr[rows, :half] = lo
            x_scr[rows, half:] = hi
            y_ref[rows, :] = jnp.broadcast_to(bd_ref[...], (m, y_ref.shape[1]))

        x = x_scr[rows, :]
        glu = _dot(x, wg_ref[...]) + bg_ref[...]
        lin = _dot(x, wl_ref[...]) + bl_ref[...]
        glu = jnp.minimum(glu, SWIGLU_LIMIT)
        lin = jnp.clip(lin, -SWIGLU_LIMIT, SWIGLU_LIMIT)
        act = glu * _sigmoid(SWIGLU_ALPHA * glu) * (lin + 1.0)
        y_ref[rows, :] = y_ref[rows, :] + _dot(act.astype(BF16), wd_ref[...].astype(BF16))

        @pl.when(j == pl.num_programs(1) - 1)
        def _():
            ys_ref[rows, :] = _pack_bf16_pairs(y_ref[rows, :])

    @pl.when(valid > row_block // 2)
    def _():
        block(row_block)

    @pl.when((valid > 0) & (valid <= row_block // 2))
    def _():
        block(row_block // 2)


def _experts(meta, xs, w_gu, b_gu, w_dn, b_dn, row_block):
    n_rows, half = xs.shape
    d = 2 * half
    d_ff = w_dn.shape[1]
    tf = _tile(d_ff, 1024)
    nf = d_ff // tf
    n_blocks = n_rows // row_block

    def jj(i, j, m):
        return jnp.where(i < m[0], j, nf - 1)

    grid_spec = pltpu.PrefetchScalarGridSpec(
        num_scalar_prefetch=1,
        grid=(n_blocks, nf),
        in_specs=[
            pl.BlockSpec((row_block, half), lambda i, j, m: (i, 0)),
            pl.BlockSpec((None, d, tf), lambda i, j, m: (m[1 + i], 0, jj(i, j, m))),
            pl.BlockSpec((None, d, tf), lambda i, j, m: (m[1 + i], 0, nf + jj(i, j, m))),
            pl.BlockSpec((None, 1, tf), lambda i, j, m: (m[1 + i], 0, jj(i, j, m))),
            pl.BlockSpec((None, 1, tf), lambda i, j, m: (m[1 + i], 0, nf + jj(i, j, m))),
            pl.BlockSpec((None, tf, d), lambda i, j, m: (m[1 + i], jj(i, j, m), 0)),
            pl.BlockSpec((None, 1, d), lambda i, j, m: (m[1 + i], 0, 0)),
        ],
        out_specs=pl.BlockSpec((row_block, half), lambda i, j, m: (i, 0)),
        scratch_shapes=[pltpu.VMEM((row_block, d), BF16), pltpu.VMEM((row_block, d), F32)],
    )
    return pl.pallas_call(
        _expert_kernel,
        grid_spec=grid_spec,
        out_shape=jax.ShapeDtypeStruct((n_rows, half), jnp.uint32),
        compiler_params=_params("arbitrary", "arbitrary"),
        name="expert_swiglu",
    )(meta, xs, w_gu, w_gu, b_gu, b_gu, w_dn, b_dn)


def _combine_kernel(dest_ref, dest_next_ref, gate_ref, x1_ref, fn_ref, ys_ref, o_ref, buf, sem):
    tt = dest_ref.shape[1]
    i = pl.program_id(0)
    slot = i % 2

    def gather(d_ref, s):
        def start(g, carry):
            for u in range(8):
                for kk in range(TOP_K):
                    pltpu.make_async_copy(ys_ref.at[pl.ds(d_ref[kk, g * 8 + u], 1)],
                                          buf.at[s, kk * (tt // 8) + g, pl.ds(u, 1), :], sem.at[s]).start()
            return carry
        lax.fori_loop(0, tt // 8, start, 0)

    @pl.when(i == 0)
    def _():
        gather(dest_ref, 0)

    @pl.when(i + 1 < pl.num_programs(0))
    def _():
        gather(dest_next_ref, 1 - slot)

    pltpu.make_async_copy(ys_ref.at[pl.ds(0, TOP_K * tt)], ys_ref.at[pl.ds(0, TOP_K * tt)], sem.at[slot]).wait()
    half = buf.shape[3]
    gates = gate_ref[...]
    acc_lo = x1_ref[:, :half]
    acc_hi = x1_ref[:, half:]
    for kk in range(TOP_K):
        u = buf[slot, pl.ds(kk * (tt // 8), tt // 8)].reshape(tt, half)
        g = gates[:, kk:kk + 1]
        acc_lo = acc_lo + pltpu.bitcast(u << 16, F32) * g
        acc_hi = acc_hi + pltpu.bitcast(u & jnp.uint32(0xFFFF0000), F32) * g
    ms = (jnp.sum(acc_lo * acc_lo, axis=-1, keepdims=True)
          + jnp.sum(acc_hi * acc_hi, axis=-1, keepdims=True)) / (2 * half)
    inv = lax.rsqrt(ms + NORM_EPS)
    o_ref[:, :half] = acc_lo * inv * fn_ref[:, :half]
    o_ref[:, half:] = acc_hi * inv * fn_ref[:, half:]


def _combine(dest, gates_t, x1, final_gain, ys, row_off):
    n, d = x1.shape
    tt = _tile(n, 256)
    assert row_off % tt == 0
    off = row_off // tt
    nt = n // tt
    return pl.pallas_call(
        _combine_kernel,
        grid=(nt,),
        in_specs=[
            pl.BlockSpec((TOP_K, tt), lambda i: (0, off + i), memory_space=pltpu.SMEM),
            pl.BlockSpec((TOP_K, tt), lambda i: (0, off + jnp.minimum(i + 1, nt - 1)), memory_space=pltpu.SMEM),
            pl.BlockSpec((tt, TOP_K), lambda i: (off + i, 0)),
            pl.BlockSpec((tt, d), lambda i: (i, 0)),
            pl.BlockSpec((1, d), lambda i: (0, 0)),
            pl.BlockSpec(memory_space=pl.ANY),
        ],
        out_specs=pl.BlockSpec((tt, d), lambda i: (i, 0)),
        out_shape=jax.ShapeDtypeStruct((n, d), F32),
        scratch_shapes=[pltpu.VMEM((2, TOP_K * tt // 8, 8, d // 2), jnp.uint32), pltpu.SemaphoreType.DMA((2,))],
        compiler_params=_params("arbitrary"),
        name="combine_rows",
    )(dest, dest, gates_t, x1, final_gain, ys)


def _rope_tables(seq_len):
    rows = seq_len // GRID_W
    row = jnp.repeat(jnp.arange(rows, dtype=F32), GRID_W)
    col = jnp.tile(jnp.arange(GRID_W, dtype=F32), rows)
    freqs = ROPE_THETA ** (-jnp.arange(ROPE_HALF, dtype=F32) / ROPE_HALF)
    ang_r = row[:, None] * freqs[None, :]
    ang_c = col[:, None] * freqs[None, :]
    cos = jnp.concatenate([jnp.cos(ang_r), jnp.cos(ang_r), jnp.cos(ang_c), jnp.cos(ang_c)], axis=1)
    sin = jnp.concatenate([-jnp.sin(ang_r), jnp.sin(ang_r), -jnp.sin(ang_c), jnp.sin(ang_c)], axis=1)
    return cos, sin


def kernel(x_prompt, x_sample, mix_norm, w_in, q_norm, k_norm, hg_lb_logits, hg_out_norm, w_up_attn,
           w_up_hgrn, w_out, ffn_norm, w_router, b_router, w_gate_up, b_gate_up, w_down, b_down, final_norm):
    assert mix_norm.shape[0] == 1, "single trunk layer"
    d = x_prompt.shape[-1]
    hg_w = HG_HEADS * HG_D
    n_exp = w_router.shape[-1]
    row_block = MOE_ROW_BLOCK
    streams = [(x.reshape(-1, d), x.shape[0], x.shape[1]) for x in (x_prompt, x_sample)]
    n_total = sum(x.shape[0] for x, _, _ in streams)

    lb = jnp.cumsum(jax.nn.softmax(hg_lb_logits.astype(F32), axis=1), axis=1)[:, 0]
    w_in_b = w_in[0].astype(BF16)
    w_ua_b = w_up_attn[0].astype(BF16)
    w_uh_b = w_up_hgrn[0].astype(BF16)
    w_out_b = w_out[0].astype(BF16)
    w_r_t = w_router[0].T
    mix_g = mix_norm[0].reshape(1, d)
    score_bound = (1.02 * LOG2_E * HEAD_DIM ** 0.5 * jnp.max(jnp.abs(q_norm[0])) * jnp.max(jnp.abs(k_norm[0])))
    score_bound = score_bound.astype(F32).reshape(1)

    w_gu2d = w_gate_up[0].reshape(-1, w_gate_up.shape[-1])
    cast_host = max(range(len(streams)), key=lambda s: streams[s][0].shape[0])
    w_gu_b = None

    x1s, idxs, gate_ts = [], [], []
    hp = None
    row_off = 0
    for s, (x, batch, seq) in enumerate(streams):
        cos, sin = _rope_tables(seq)
        qkv, zh, gates = _norm_proj(x, mix_g, w_in_b, q_norm[0].reshape(1, HEAD_DIM),
                                    k_norm[0].reshape(1, HEAD_DIM), cos, sin, seq)
        if s == cast_host:
            attn, w_gu_b = _attention(qkv, score_bound, batch, seq, w_gu2d)
        else:
            attn, = _attention(qkv, score_bound, batch, seq)
        hg = _hgrn(zh, lb[0:1], lb[1:2], hg_out_norm[0].reshape(1, hg_w), batch, seq)
        merged = _merge(attn, hg, gates, w_ua_b, w_uh_b)
        x1, hp, idx, gate = _outproj_router(x, merged, w_out_b, ffn_norm[0].reshape(1, d), w_r_t,
                                            b_router[0].reshape(n_exp, 1), hp, row_off, n_total)
        x1s.append(x1)
        idxs.append(idx)
        gate_ts.append(gate.T)
        row_off += x.shape[0]

    idx = jnp.concatenate(idxs, axis=1)
    gate_t = jnp.concatenate(gate_ts, axis=0)
    dest, counts = _route(idx, row_block)
    cnt = counts[:, 0].astype(jnp.int32)
    padded = (cnt + row_block - 1) // row_block * row_block
    pad_end = jnp.cumsum(padded)
    n_rows = n_total * TOP_K + n_exp * row_block
    n_blocks = n_rows // row_block
    blk_start = jnp.arange(n_blocks, dtype=jnp.int32) * row_block
    blk_e = jnp.minimum(jnp.sum(pad_end[None, :] <= blk_start[:, None], axis=1), n_exp - 1).astype(jnp.int32)
    seg_start = pad_end - padded
    blk_valid = jnp.clip(cnt[blk_e] - (blk_start - seg_start[blk_e]), 0, row_block)
    blk_valid = jnp.where(blk_start < pad_end[-1], blk_valid, 0).astype(jnp.int32)
    meta = jnp.concatenate([(pad_end[-1:] // row_block).astype(jnp.int32), blk_e, blk_valid])
    seg = jnp.stack([seg_start, pad_end]).astype(jnp.int32)

    xs = _dispatch(seg, dest, hp, n_rows, row_block)
    ys = _experts(meta, xs, w_gu_b.reshape(w_gate_up.shape[1:]), b_gate_up[0].reshape(n_exp, 1, -1),
                  w_down[0], b_down[0].reshape(n_exp, 1, d), row_block)

    outs = []
    row_off = 0
    for (x, batch, seq), x1 in zip(streams, x1s):
        out = _combine(dest, gate_t, x1, final_norm.reshape(1, d), ys, row_off)
        outs.append(out.reshape(batch, seq, d))
        row_off += x.shape[0]
    return tuple(outs)
```

```python
import functools

import jax
import jax.numpy as jnp
from jax import lax
from jax.experimental import pallas as pl
from jax.experimental.pallas import tpu as pltpu

GRID_W = 64
HEAD_DIM = 128
N_Q_HEADS = 16
N_KV_HEADS = 4
ROPE_THETA = 10000.0
ROPE_HALF = HEAD_DIM // 4
HG_HEADS = 8
HG_D = 128
HG_CHUNK = 64
N_EXPERTS = 32
TOP_K = 4
SWIGLU_LIMIT = 7.0
SWIGLU_ALPHA = 1.702
NORM_EPS = 1e-5

HG_SAFE_LOG_DECAY = -60.0

LOG2_E = 1.4426950408889634
ATTN_FIXED_SHIFT_LIMIT = 60.0

V7X_VMEM_BYTES = 64 * 1024 * 1024
VMEM_LIMIT_BYTES = V7X_VMEM_BYTES - 8 * 1024 * 1024
LANES = 128
MOE_ROW_BLOCK = 512

BF16 = jnp.bfloat16
F32 = jnp.float32


def _params(*sem):
    return pltpu.CompilerParams(dimension_semantics=sem, vmem_limit_bytes=VMEM_LIMIT_BYTES)


def _tile(n, pref):
    t = min(n, pref)
    while n % t:
        t //= 2
    return t


def _sigmoid(x):
    return 1.0 / (1.0 + jnp.exp(-x))


def _rmsnorm(x, g):
    return x * lax.rsqrt(jnp.mean(x * x, axis=-1, keepdims=True) + NORM_EPS) * g


def _dot(a, b):
    return jnp.dot(a, b, preferred_element_type=F32)


def _dot_nt(a, b):
    return lax.dot_general(a, b, (((1,), (1,)), ((), ())), preferred_element_type=F32)


def _split3(x):
    hi = x.astype(BF16)
    r = x - hi.astype(F32)
    mid = r.astype(BF16)
    lo = (r - mid.astype(F32)).astype(BF16)
    return hi, mid, lo


def _dot_exact_lhs(m_bf16, x, pieces=3):
    parts = _split3(x)[:pieces]
    acc = _dot(m_bf16, parts[0])
    for p in parts[1:]:
        acc = acc + _dot(m_bf16, p)
    return acc


def _rope_head_pair(zp, gain, cos, sin, ones_blk, perm_blk, scale):
    ss = _dot((zp * zp).astype(BF16), ones_blk)
    y = zp * lax.rsqrt(ss * (1.0 / HEAD_DIM) + NORM_EPS) * gain
    y_hi = y.astype(BF16)
    y_lo = (y - y_hi.astype(F32)).astype(BF16)
    partner = _dot(y_hi, perm_blk) + _dot(y_lo, perm_blk)
    return (y * cos + partner * sin) * scale


def _proj_kernel(x_ref, g_ref, w_ref, qn_ref, kn_ref, cos_ref, sin_ref, ones_ref, perm_ref,
                 qkv_ref, zh_ref, gate_ref, h_scr, *, tile_kinds):
    j = pl.program_id(1)

    @pl.when(j == 0)
    def _():
        h_scr[...] = _rmsnorm(x_ref[...], g_ref[...]).astype(BF16)

    z = _dot(h_scr[...], w_ref[...])

    def qkv_tile(kinds):
        two = lambda r: jnp.concatenate([r[...], r[...]], axis=1)
        outs = []
        for h in range(0, len(kinds), 2):
            kind = kinds[h]
            assert kinds[h + 1] == kind
            zp = z[:, h * HEAD_DIM:(h + 2) * HEAD_DIM]
            if kind == "q":
                zp = _rope_head_pair(zp, two(qn_ref), two(cos_ref), two(sin_ref), ones_ref[...], perm_ref[...],
                                     LOG2_E * HEAD_DIM ** -0.5)
            elif kind == "k":
                zp = _rope_head_pair(zp, two(kn_ref), two(cos_ref), two(sin_ref), ones_ref[...], perm_ref[...], 1.0)
            outs.append(zp)
        return jnp.concatenate(outs, axis=1)

    for lo, hi, kind in tile_kinds:
        @pl.when((j >= lo) & (j < hi))
        def _(kind=kind):
            if kind == "zh":
                zh_ref[...] = z
            elif kind == "gate":
                gate_ref[...] = z.astype(gate_ref.dtype)
            else:
                qkv_ref[...] = qkv_tile(kind).astype(qkv_ref.dtype)


def _norm_proj(x, gain, w, q_gain, k_gain, cos, sin, seq):
    n, d = x.shape
    attn_w = N_Q_HEADS * HEAD_DIM
    kv_w = N_KV_HEADS * HEAD_DIM
    hg5 = 5 * HG_HEADS * HG_D
    qkv_w = attn_w + 2 * kv_w
    assert w.shape[1] == qkv_w + hg5 + 2 * d
    tm = _tile(seq, 1024)
    tn = 1024
    while attn_w % tn or (2 * kv_w) % tn or hg5 % tn or (2 * d) % tn:
        tn //= 2
    assert tn % HEAD_DIM == 0
    heads = ["q"] * N_Q_HEADS + ["k"] * N_KV_HEADS + ["v"] * N_KV_HEADS
    hpt = tn // HEAD_DIM
    n_qkv, n_zh, n_gate = qkv_w // tn, hg5 // tn, 2 * d // tn
    tile_kinds = []
    for t in range(n_qkv):
        kind = tuple(heads[t * hpt:(t + 1) * hpt])
        if tile_kinds and tile_kinds[-1][2] == kind:
            tile_kinds[-1] = (tile_kinds[-1][0], t + 1, kind)
        else:
            tile_kinds.append((t, t + 1, kind))
    tile_kinds += [(n_qkv, n_qkv + n_zh, "zh"), (n_qkv + n_zh, n_qkv + n_zh + n_gate, "gate")]
    n_pos = seq // tm
    r = jnp.arange(2 * HEAD_DIM)
    same_head = (r[:, None] // HEAD_DIM) == (r[None, :] // HEAD_DIM)
    ones_blk = same_head.astype(BF16)
    partner_of = jnp.where((r % (2 * ROPE_HALF)) < ROPE_HALF, r + ROPE_HALF, r - ROPE_HALF)
    perm_blk = (r[:, None] == partner_of[None, :]).astype(BF16)
    const_spec = pl.BlockSpec((2 * HEAD_DIM, 2 * HEAD_DIM), lambda i, j: (0, 0))
    return pl.pallas_call(
        functools.partial(_proj_kernel, tile_kinds=tile_kinds),
        grid=(n // tm, n_qkv + n_zh + n_gate),
        in_specs=[
            pl.BlockSpec((tm, d), lambda i, j: (i, 0)),
            pl.BlockSpec((1, d), lambda i, j: (0, 0)),
            pl.BlockSpec((d, tn), lambda i, j: (0, j)),
            pl.BlockSpec((1, HEAD_DIM), lambda i, j: (0, 0)),
            pl.BlockSpec((1, HEAD_DIM), lambda i, j: (0, 0)),
            pl.BlockSpec((tm, HEAD_DIM), lambda i, j: (i % n_pos, 0)),
            pl.BlockSpec((tm, HEAD_DIM), lambda i, j: (i % n_pos, 0)),
            const_spec,
            const_spec,
        ],
        out_specs=[
            pl.BlockSpec((tm, tn), lambda i, j: (i, jnp.minimum(j, n_qkv - 1))),
            pl.BlockSpec((tm, tn), lambda i, j: (i, jnp.clip(j - n_qkv, 0, n_zh - 1))),
            pl.BlockSpec((tm, tn), lambda i, j: (i, jnp.clip(j - n_qkv - n_zh, 0, n_gate - 1))),
        ],
        out_shape=[
            jax.ShapeDtypeStruct((n, qkv_w), BF16),
            jax.ShapeDtypeStruct((n, hg5), F32),
            jax.ShapeDtypeStruct((n, 2 * d), BF16),
        ],
        scratch_shapes=[pltpu.VMEM((tm, d), BF16)],
        compiler_params=_params("parallel", "arbitrary"),
        name="norm_proj",
    )(x, gain, w, q_gain, k_gain, cos, sin, ones_blk, perm_blk)


def _attn_kernel(bound_ref, q_ref, k_ref, v_ref, *rest, tk, group):
    if len(rest) == 5:
        wf_ref, o_ref, wb_ref, acc_scr, m_scr = rest
    else:
        wf_ref = wb_ref = None
        o_ref, acc_scr, m_scr = rest
    tq = q_ref.shape[0]
    seq = k_ref.shape[0]
    n_chunks = seq // tk
    slab = None if wf_ref is None else wf_ref.shape[0] // n_chunks
    q = jnp.concatenate([q_ref[:, g * HEAD_DIM:(g + 1) * HEAD_DIM] for g in range(group)], axis=0)
    ones = jnp.ones((tk, HEAD_DIM), BF16)
    bound = bound_ref[0]
    acc_scr[...] = jnp.zeros(acc_scr.shape, F32)

    def chunk(c):
        rows = pl.ds(pl.multiple_of(c * tk, tk), tk)
        s = _dot_nt(q, k_ref[rows, :])
        return s, jnp.concatenate([v_ref[rows, :], ones], axis=1)

    @pl.when(bound <= ATTN_FIXED_SHIFT_LIMIT)
    def _():
        def body(c, carry):
            s, v1 = chunk(c)
            acc_scr[...] += _dot(jnp.exp2(s - bound).astype(BF16), v1)
            if wf_ref is not None:
                wrows = pl.ds(pl.multiple_of(c * slab, slab), slab)
                wb_ref[wrows, :] = wf_ref[wrows, :].astype(BF16)
            return carry
        lax.fori_loop(0, n_chunks, body, 0, unroll=True)

    @pl.when(bound > ATTN_FIXED_SHIFT_LIMIT)
    def _():
        if wf_ref is not None:
            wb_ref[...] = wf_ref[...].astype(BF16)
        m_scr[...] = jnp.full(m_scr.shape, -1e30, F32)

        def body(c, carry):
            s, v1 = chunk(c)
            m_prev = m_scr[...]
            m_new = jnp.maximum(m_prev, jnp.max(s, axis=1, keepdims=True))
            alpha = jnp.exp2(m_prev - m_new)
            p = jnp.exp2(s - jnp.tile(m_new, (1, tk // LANES)))
            acc_scr[...] = jnp.tile(alpha, (1, 2)) * acc_scr[...] + _dot(p.astype(BF16), v1)
            m_scr[...] = m_new
            return carry
        lax.fori_loop(0, seq // tk, body, 0)

    acc = acc_scr[...]
    o = acc[:, :HEAD_DIM] / acc[:, HEAD_DIM:]
    for g in range(group):
        o_ref[:, g * HEAD_DIM:(g + 1) * HEAD_DIM] = o[g * tq:(g + 1) * tq].astype(o_ref.dtype)


def _attention(qkv, score_bound, batch, seq, w_f32=None):
    group = N_Q_HEADS // N_KV_HEADS
    gw = group * HEAD_DIM
    tq = _tile(seq, 512)
    tk = _tile(seq, 512)
    nq = seq // tq
    in_specs = [
        pl.BlockSpec(memory_space=pltpu.SMEM),
        pl.BlockSpec((tq, gw), lambda b, n, i: (b * nq + i, n)),
        pl.BlockSpec((seq, HEAD_DIM), lambda b, n, i: (b, N_Q_HEADS + n)),
        pl.BlockSpec((seq, HEAD_DIM), lambda b, n, i: (b, N_Q_HEADS + N_KV_HEADS + n)),
    ]
    out_specs = [pl.BlockSpec((tq, gw), lambda b, n, i: (b * nq + i, n))]
    out_shape = [jax.ShapeDtypeStruct((batch * seq, N_Q_HEADS * HEAD_DIM), BF16)]
    args = [score_bound, qkv, qkv, qkv]
    if w_f32 is not None:
        n_steps = batch * N_KV_HEADS * nq
        w_rows, w_cols = w_f32.shape
        slab = w_rows // n_steps
        assert slab * n_steps == w_rows and slab % (16 * (seq // tk)) == 0, (w_rows, n_steps, seq // tk)
        wspec = pl.BlockSpec((slab, w_cols), lambda b, n, i: ((b * N_KV_HEADS + n) * nq + i, 0))
        in_specs.append(wspec)
        out_specs.append(wspec)
        out_shape.append(jax.ShapeDtypeStruct((w_rows, w_cols), BF16))
        args.append(w_f32)
    return pl.pallas_call(
        functools.partial(_attn_kernel, tk=tk, group=group),
        grid=(batch, N_KV_HEADS, nq),
        in_specs=in_specs,
        out_specs=out_specs,
        out_shape=out_shape,
        scratch_shapes=[
            pltpu.VMEM((group * tq, 2 * HEAD_DIM), F32),
            pltpu.VMEM((group * tq, LANES), F32),
        ],
        compiler_params=_params("parallel", "parallel", "arbitrary"),
        name="gqa_attention",
    )(*args)


def _hgrn_span(i, q_scr, b_scr, k_scr, zi_ref, vt_scr, o_scr, s_scr, tri, *, span, reverse, pairwise):
    C = HG_CHUNK
    nc = span // C
    r0 = pl.multiple_of(i * span, span)
    rows = pl.ds(r0, span)
    q = q_scr[rows, :]
    b = b_scr[rows, :]
    k = k_scr[rows, :]
    v = zi_ref[rows, :]
    vb = v.astype(BF16)
    end_row = 0 if reverse else C - 1
    ends = [b[c * C + end_row:c * C + end_row + 1, :] for c in range(nc)]
    b_end = jnp.concatenate([jnp.broadcast_to(e, (C, HG_D)) for e in ends], axis=0)
    qtb = (q * jnp.exp(b)).astype(BF16)
    kp = k * jnp.exp(b_end - b)

    if pairwise:
        rid = lax.broadcasted_iota(jnp.int32, (span, HG_D), 0)

        def pair(s, acc):
            cs = (s // C) * C
            if reverse:
                m = (rid <= s) & (rid >= cs)
            else:
                m = (rid >= s) & (rid < cs + C)
            w = jnp.where(m, jnp.exp(jnp.minimum(b - b_scr[pl.ds(r0 + s, 1), :], 0.0)), 0.0)
            r = jnp.sum(q * w * k_scr[pl.ds(r0 + s, 1), :], axis=1, keepdims=True)
            return acc + r * zi_ref[pl.ds(r0 + s, 1), :]

        o_intra = lax.fori_loop(0, span, pair, jnp.zeros((span, HG_D), F32))
    else:
        ktb = (k * jnp.exp(-b)).astype(BF16)
        a = jnp.where(tri, _dot_nt(qtb, ktb), 0.0)
        o_intra = _dot(a.astype(BF16), vb)

    chunk_of_row = lax.broadcasted_iota(jnp.int32, (span, HG_D), 0) // C
    kp_blocks = jnp.concatenate([jnp.where(chunk_of_row == c, kp, 0.0) for c in range(nc)], axis=1)
    upd = _dot(vt_scr[:, rows], kp_blocks.astype(BF16))

    st = s_scr[...]
    states = [None] * nc
    for c in (reversed(range(nc)) if reverse else range(nc)):
        states[c] = st.astype(BF16)
        st = st * jnp.exp(ends[c]) + upd[:, c * HG_D:(c + 1) * HG_D]
    s_scr[...] = st
    o_inter = _dot_nt(qtb, jnp.concatenate(states, axis=0))
    o_scr[rows, :] = o_intra + jnp.concatenate(
        [o_inter[c * C:(c + 1) * C, c * HG_D:(c + 1) * HG_D] for c in range(nc)], axis=0)


def _hgrn_kernel(zq_ref, zf_ref, zb_ref, zi_ref, zo_ref, lbf_ref, lbb_ref, on_ref, out_ref,
                 of_scr, ob_scr, sf_scr, sb_scr, q_scr, bf_scr, bb_scr, kf_scr, kb_scr, vt_scr, *, span):
    seq = zq_ref.shape[0]
    n_span = seq // span
    C = HG_CHUNK
    nc = span // C
    sf_scr[...] = jnp.zeros(sf_scr.shape, F32)
    sb_scr[...] = jnp.zeros(sb_scr.shape, F32)
    r = lax.broadcasted_iota(jnp.int32, (span, span), 0)
    c = lax.broadcasted_iota(jnp.int32, (span, span), 1)
    same = (r // C) == (c // C)
    tri_f = same & (c <= r)
    tri_b = same & (c >= r)
    tri_f16 = tri_f.astype(BF16)
    tri_b16 = tri_b.astype(BF16)

    def prepare(i, min_end):
        rows = pl.ds(pl.multiple_of(i * span, span), span)
        qh = zq_ref[rows, :]
        q_scr[rows, :] = qh * _sigmoid(qh)
        vt_scr[:, rows] = zi_ref[rows, :].T.astype(BF16)
        for z_ref, lb_ref, tri, b_scr, k_scr, end_row in (
                (zf_ref, lbf_ref, tri_f16, bf_scr, kf_scr, C - 1), (zb_ref, lbb_ref, tri_b16, bb_scr, kb_scr, 0)):
            lb = lb_ref[...]
            f = lb + (1.0 - lb) * _sigmoid(z_ref[rows, :])
            b = _dot_exact_lhs(tri, jnp.log(f), pieces=2)
            b_scr[rows, :] = b
            k_scr[rows, :] = 1.0 - f
            for cc in range(nc):
                min_end = jnp.minimum(min_end, b[cc * C + end_row:cc * C + end_row + 1, :])
        return min_end

    min_end = lax.fori_loop(0, n_span, prepare, jnp.zeros((1, HG_D), F32), unroll=2)
    safe = jnp.min(min_end) > HG_SAFE_LOG_DECAY

    def scan(pairwise):
        def body(i, carry):
            _hgrn_span(i, q_scr, bf_scr, kf_scr, zi_ref, vt_scr, of_scr, sf_scr, tri_f,
                       span=span, reverse=False, pairwise=pairwise)
            _hgrn_span(n_span - 1 - i, q_scr, bb_scr, kb_scr, zi_ref, vt_scr, ob_scr, sb_scr, tri_b,
                       span=span, reverse=True, pairwise=pairwise)
            return carry
        lax.fori_loop(0, n_span, body, 0, unroll=1 if pairwise else 4)

    @pl.when(safe)
    def _():
        scan(False)

    @pl.when(jnp.logical_not(safe))
    def _():
        scan(True)

    def finish(i, carry):
        rows = pl.ds(pl.multiple_of(i * span, span), span)
        o = _rmsnorm(of_scr[rows, :] + ob_scr[rows, :], on_ref[...])
        og = zo_ref[rows, :]
        out_ref[rows, :] = (o * (og * _sigmoid(og))).astype(out_ref.dtype)
        return carry

    lax.fori_loop(0, n_span, finish, 0)


def _hgrn(zh, lb_f, lb_b, out_norm, batch, seq):
    span = _tile(seq, 256)
    assert span % HG_CHUNK == 0
    zspec = lambda grp: pl.BlockSpec((seq, HG_D), lambda b, h: (b, grp * HG_HEADS + h))
    hspec = pl.BlockSpec((1, HG_D), lambda b, h: (0, h))
    seq_buf = pltpu.VMEM((seq, HG_D), F32)
    return pl.pallas_call(
        functools.partial(_hgrn_kernel, span=span),
        grid=(batch, HG_HEADS),
        in_specs=[zspec(0), zspec(1), zspec(2), zspec(3), zspec(4), hspec, hspec, hspec],
        out_specs=pl.BlockSpec((seq, HG_D), lambda b, h: (b, h)),
        out_shape=jax.ShapeDtypeStruct((batch * seq, HG_HEADS * HG_D), BF16),
        scratch_shapes=[
            seq_buf, seq_buf,
            pltpu.VMEM((HG_D, HG_D), F32), pltpu.VMEM((HG_D, HG_D), F32),
            seq_buf, seq_buf, seq_buf, seq_buf, seq_buf,
            pltpu.VMEM((HG_D, seq), BF16),
        ],
        compiler_params=_params("parallel", "parallel"),
        name="hgrn2",
    )(zh, zh, zh, zh, zh, lb_f, lb_b, out_norm)


def _merge_kernel(a_ref, h_ref, ga_ref, gb_ref, wa_ref, wh_ref, o_ref):
    ya = _dot(a_ref[...], wa_ref[...])
    yb = _dot(h_ref[...], wh_ref[...])
    merged = _sigmoid(ga_ref[...].astype(F32)) * ya + _sigmoid(gb_ref[...].astype(F32)) * yb
    o_ref[...] = merged.astype(o_ref.dtype)


def _merge(attn, hg, gates, w_up_attn, w_up_hgrn):
    n, wa = attn.shape
    wh = hg.shape[1]
    d = w_up_attn.shape[1]
    tm = _tile(n, 1024)
    tn = _tile(d, 512)
    nj = d // tn
    return pl.pallas_call(
        _merge_kernel,
        grid=(n // tm, nj),
        in_specs=[
            pl.BlockSpec((tm, wa), lambda i, j: (i, 0)),
            pl.BlockSpec((tm, wh), lambda i, j: (i, 0)),
            pl.BlockSpec((tm, tn), lambda i, j: (i, j)),
            pl.BlockSpec((tm, tn), lambda i, j: (i, nj + j)),
            pl.BlockSpec((wa, tn), lambda i, j: (0, j)),
            pl.BlockSpec((wh, tn), lambda i, j: (0, j)),
        ],
        out_specs=pl.BlockSpec((tm, tn), lambda i, j: (i, j)),
        out_shape=jax.ShapeDtypeStruct((n, d), BF16),
        compiler_params=_params("parallel", "arbitrary"),
        name="gated_merge",
    )(attn, hg, gates, gates, w_up_attn, w_up_hgrn)


def _pack_bf16_pairs(h):
    half = h.shape[1] // 2
    lo = pltpu.bitcast(h[:, :half].astype(BF16).astype(F32), jnp.uint32)
    hi = pltpu.bitcast(h[:, half:].astype(BF16).astype(F32), jnp.uint32)
    return (hi & jnp.uint32(0xFFFF0000)) | (lo >> 16)


def _unpack_bf16_pairs(u):
    lo = pltpu.bitcast(u << 16, F32).astype(BF16)
    hi = pltpu.bitcast(u & jnp.uint32(0xFFFF0000), F32).astype(BF16)
    return lo, hi


def _outproj_router_kernel(x_ref, m_ref, w_ref, g_ref, wr_ref, br_ref, hp_in_ref,
                           x1_ref, hp_ref, idx_ref, gate_ref):
    del hp_in_ref
    x1 = x_ref[...] + _dot(m_ref[...], w_ref[...])
    x1_ref[...] = x1
    h = _rmsnorm(x1, g_ref[...])
    hp_ref[...] = _pack_bf16_pairs(h)
    h_hi, h_mid, _ = _split3(h)
    w_hi, w_mid, _ = _split3(wr_ref[...])
    lg = (_dot_nt(w_hi, h_hi) + _dot_nt(w_hi, h_mid) + _dot_nt(w_mid, h_hi)) + br_ref[...]
    n_exp, tm = lg.shape
    eid = lax.broadcasted_iota(jnp.int32, (n_exp, tm), 0)
    vals = []
    for kk in range(TOP_K):
        m = jnp.max(lg, axis=0, keepdims=True)
        sel = jnp.min(jnp.where(lg == m, eid, n_exp), axis=0, keepdims=True)
        idx_ref[kk:kk + 1, :] = sel
        vals.append(m)
        lg = jnp.where(eid == sel, -jnp.inf, lg)
    ex = [jnp.exp(vv - vals[0]) for vv in vals]
    den = ex[0]
    for e in ex[1:]:
        den = den + e
    for kk in range(TOP_K):
        gate_ref[kk:kk + 1, :] = ex[kk] / den


def _outproj_router(x, merged, w_out, ffn_gain, w_router_t, b_router, hp_prev, row_off, n_total):
    n, d = x.shape
    tm = _tile(n, 512)
    assert row_off % tm == 0
    n_exp = w_router_t.shape[0]
    if hp_prev is None:
        hp_prev = jnp.zeros((8, LANES), jnp.uint32)
        aliases = {}
    else:
        aliases = {6: 1}
    return pl.pallas_call(
        _outproj_router_kernel,
        grid=(n // tm,),
        in_specs=[
            pl.BlockSpec((tm, d), lambda i: (i, 0)),
            pl.BlockSpec((tm, d), lambda i: (i, 0)),
            pl.BlockSpec((d, d), lambda i: (0, 0)),
            pl.BlockSpec((1, d), lambda i: (0, 0)),
            pl.BlockSpec((n_exp, d), lambda i: (0, 0)),
            pl.BlockSpec((n_exp, 1), lambda i: (0, 0)),
            pl.BlockSpec(memory_space=pl.ANY),
        ],
        out_specs=[
            pl.BlockSpec((tm, d), lambda i: (i, 0)),
            pl.BlockSpec((tm, d // 2), lambda i: (row_off // tm + i, 0)),
            pl.BlockSpec((TOP_K, tm), lambda i: (0, i)),
            pl.BlockSpec((TOP_K, tm), lambda i: (0, i)),
        ],
        out_shape=[
            jax.ShapeDtypeStruct((n, d), F32),
            jax.ShapeDtypeStruct((n_total, d // 2), jnp.uint32),
            jax.ShapeDtypeStruct((TOP_K, n), jnp.int32),
            jax.ShapeDtypeStruct((TOP_K, n), F32),
        ],
        input_output_aliases=aliases,
        compiler_params=_params("parallel"),
        name="outproj_router",
    )(x, merged, w_out, ffn_gain, w_router_t, b_router, hp_prev)


def _route_kernel(idx_ref, dest_ref, cnt_ref, cnt_scr, base_scr, *, row_block):
    phase = pl.program_id(0)
    i = pl.program_id(1)
    n_exp = cnt_scr.shape[0]
    tt = idx_ref.shape[1]
    eid = lax.broadcasted_iota(jnp.int32, (n_exp, tt), 0)
    onehot = [(eid == idx_ref[kk:kk + 1, :]) for kk in range(TOP_K)]

    @pl.when((phase == 0) & (i == 0))
    def _():
        cnt_scr[...] = jnp.zeros(cnt_scr.shape, F32)

    @pl.when(phase == 0)
    def _():
        tot = onehot[0].astype(F32)
        for oh in onehot[1:]:
            tot = tot + oh.astype(F32)
        cnt_scr[...] = cnt_scr[...] + jnp.sum(tot, axis=1, keepdims=True)
        cnt_ref[...] = cnt_scr[...]

    @pl.when((phase == 1) & (i == 0))
    def _():
        cnt = cnt_scr[...].astype(jnp.int32)
        padded = ((cnt + (row_block - 1)) // row_block * row_block).astype(F32)
        er = lax.broadcasted_iota(jnp.int32, (n_exp, n_exp), 0)
        ec = lax.broadcasted_iota(jnp.int32, (n_exp, n_exp), 1)
        base_scr[...] = _dot_exact_lhs((ec < er).astype(BF16), padded)

    @pl.when(phase == 1)
    def _():
        tr = lax.broadcasted_iota(jnp.int32, (tt, tt), 0)
        tc = lax.broadcasted_iota(jnp.int32, (tt, tt), 1)
        before = (tr < tc).astype(BF16)
        run = base_scr[...][:, :1]
        for kk in range(TOP_K):
            oh = onehot[kk].astype(F32)
            rank = _dot(oh.astype(BF16), before) + run
            dest_ref[kk:kk + 1, :] = jnp.sum(oh * rank, axis=0, keepdims=True).astype(jnp.int32)
            run = run + jnp.sum(oh, axis=1, keepdims=True)
        base_scr[...] = jnp.broadcast_to(run, base_scr.shape)


def _route(idx, row_block):
    n = idx.shape[1]
    tt = _tile(n, 512)
    return pl.pallas_call(
        functools.partial(_route_kernel, row_block=row_block),
        grid=(2, n // tt),
        in_specs=[pl.BlockSpec((TOP_K, tt), lambda p, i: (0, i))],
        out_specs=[
            pl.BlockSpec((TOP_K, tt), lambda p, i: (0, i * p)),
            pl.BlockSpec((N_EXPERTS, LANES), lambda p, i: (0, 0)),
        ],
        out_shape=[
            jax.ShapeDtypeStruct((TOP_K, n), jnp.int32),
            jax.ShapeDtypeStruct((N_EXPERTS, LANES), F32),
        ],
        scratch_shapes=[pltpu.VMEM((N_EXPERTS, LANES), F32), pltpu.VMEM((N_EXPERTS, LANES), F32)],
        compiler_params=_params("arbitrary", "arbitrary"),
        name="route_offsets",
    )(idx)


def _dispatch_kernel(seg_ref, dest_ref, h_ref, xs_ref, zero_scr, sem, zsem, *, row_block):
    tt = dest_ref.shape[1]
    n_exp = seg_ref.shape[1]

    @pl.when(pl.program_id(0) == 0)
    def _():
        zero_scr[...] = jnp.zeros(zero_scr.shape, zero_scr.dtype)

        def zero_copy(e):
            start = pl.multiple_of(seg_ref[1, e] - row_block, row_block)
            return pltpu.make_async_copy(zero_scr, xs_ref.at[pl.ds(start, row_block)], zsem)

        for e in range(n_exp):
            @pl.when(seg_ref[1, e] > seg_ref[0, e])
            def _(e=e):
                zero_copy(e).start()
        for e in range(n_exp):
            @pl.when(seg_ref[1, e] > seg_ref[0, e])
            def _(e=e):
                zero_copy(e).wait()

    def start(g, carry):
        for u in range(8):
            for kk in range(TOP_K):
                pltpu.make_async_copy(h_ref.at[g, pl.ds(u, 1), :],
                                      xs_ref.at[pl.ds(dest_ref[kk, g * 8 + u], 1)], sem).start()
        return carry

    lax.fori_loop(0, tt // 8, start, 0)
    pltpu.make_async_copy(xs_ref.at[pl.ds(0, TOP_K * tt)], xs_ref.at[pl.ds(0, TOP_K * tt)], sem).wait()


def _dispatch(seg, dest, hp, n_rows, row_block):
    n, w = hp.shape
    tt = _tile(n, 1024)
    grid_spec = pltpu.PrefetchScalarGridSpec(
        num_scalar_prefetch=1,
        grid=(n // tt,),
        in_specs=[
            pl.BlockSpec((TOP_K, tt), lambda i, s: (0, i), memory_space=pltpu.SMEM),
            pl.BlockSpec((tt // 8, 8, w), lambda i, s: (i, 0, 0)),
        ],
        out_specs=pl.BlockSpec(memory_space=pl.ANY),
        scratch_shapes=[pltpu.VMEM((row_block, w), hp.dtype), pltpu.SemaphoreType.DMA(()),
                        pltpu.SemaphoreType.DMA(())],
    )
    return pl.pallas_call(
        functools.partial(_dispatch_kernel, row_block=row_block),
        grid_spec=grid_spec,
        out_shape=jax.ShapeDtypeStruct((n_rows, w), hp.dtype),
        compiler_params=_params("arbitrary"),
        name="dispatch_rows",
    )(seg, dest, hp.reshape(n // 8, 8, w))


def _expert_kernel(meta_ref, xs_ref, wg_ref, wl_ref, bg_ref, bl_ref, wd_ref, bd_ref, ys_ref,
                   x_scr, y_ref):
    i = pl.program_id(0)
    j = pl.program_id(1)
    n_blocks = pl.num_programs(0)
    row_block = xs_ref.shape[0]
    valid = meta_ref[1 + n_blocks + i]

    def block(m):
        rows = slice(0, m)

        @pl.when(j == 0)
        def _():
            half = xs_ref.shape[1]
            lo, hi = _unpack_bf16_pairs(xs_ref[rows, :])
            x_scr[rows, :half] = lo
            x_scr[rows, half:] = hi
            y_ref[rows, :] = jnp.broadcast_to(bd_ref[...], (m, y_ref.shape[1]))

        x = x_scr[rows, :]
        glu = _dot(x, wg_ref[...]) + bg_ref[...]
        lin = _dot(x, wl_ref[...]) + bl_ref[...]
        glu = jnp.minimum(glu, SWIGLU_LIMIT)
        lin = jnp.clip(lin, -SWIGLU_LIMIT, SWIGLU_LIMIT)
        act = glu * _sigmoid(SWIGLU_ALPHA * glu) * (lin + 1.0)
        y_ref[rows, :] = y_ref[rows, :] + _dot(act.astype(BF16), wd_ref[...].astype(BF16))

        @pl.when(j == pl.num_programs(1) - 1)
        def _():
            ys_ref[rows, :] = _pack_bf16_pairs(y_ref[rows, :])

    @pl.when(valid > row_block // 2)
    def _():
        block(row_block)

    @pl.when((valid > 0) & (valid <= row_block // 2))
    def _():
        block(row_block // 2)


def _experts(meta, xs, w_gu, b_gu, w_dn, b_dn, row_block):
    n_rows, half = xs.shape
    d = 2 * half
    d_ff = w_dn.shape[1]
    tf = _tile(d_ff, 1024)
    nf = d_ff // tf
    n_blocks = n_rows // row_block

    def jj(i, j, m):
        return jnp.where(i < m[0], j, nf - 1)

    grid_spec = pltpu.PrefetchScalarGridSpec(
        num_scalar_prefetch=1,
        grid=(n_blocks, nf),
        in_specs=[
            pl.BlockSpec((row_block, half), lambda i, j, m: (i, 0)),
            pl.BlockSpec((None, d, tf), lambda i, j, m: (m[1 + i], 0, jj(i, j, m))),
            pl.BlockSpec((None, d, tf), lambda i, j, m: (m[1 + i], 0, nf + jj(i, j, m))),
            pl.BlockSpec((None, 1, tf), lambda i, j, m: (m[1 + i], 0, jj(i, j, m))),
            pl.BlockSpec((None, 1, tf), lambda i, j, m: (m[1 + i], 0, nf + jj(i, j, m))),
            pl.BlockSpec((None, tf, d), lambda i, j, m: (m[1 + i], jj(i, j, m), 0)),
            pl.BlockSpec((None, 1, d), lambda i, j, m: (m[1 + i], 0, 0)),
        ],
        out_specs=pl.BlockSpec((row_block, half), lambda i, j, m: (i, 0)),
        scratch_shapes=[pltpu.VMEM((row_block, d), BF16), pltpu.VMEM((row_block, d), F32)],
    )
    return pl.pallas_call(
        _expert_kernel,
        grid_spec=grid_spec,
        out_shape=jax.ShapeDtypeStruct((n_rows, half), jnp.uint32),
        compiler_params=_params("arbitrary", "arbitrary"),
        name="expert_swiglu",
    )(meta, xs, w_gu, w_gu, b_gu, b_gu, w_dn, b_dn)


def _combine_kernel(dest_ref, dest_next_ref, gate_ref, x1_ref, fn_ref, ys_ref, o_ref, buf, sem):
    tt = dest_ref.shape[1]
    groups = tt // 8
    i = pl.program_id(0)
    slot = i % 2
    half = buf.shape[3]

    def start_group(d_ref, s, g):
        for u in range(8):
            for kk in range(TOP_K):
                pltpu.make_async_copy(ys_ref.at[pl.ds(d_ref[kk, g * 8 + u], 1)],
                                      buf.at[s, kk * groups + g, pl.ds(u, 1), :], sem.at[s]).start()

    def sum_group(g):
        rows = pl.ds(pl.multiple_of(g * 8, 8), 8)
        gates = gate_ref[rows, :]
        acc_lo = x1_ref[rows, :half]
        acc_hi = x1_ref[rows, half:]
        for kk in range(TOP_K):
            u = buf[slot, kk * groups + g]
            gk = gates[:, kk:kk + 1]
            acc_lo = acc_lo + pltpu.bitcast(u << 16, F32) * gk
            acc_hi = acc_hi + pltpu.bitcast(u & jnp.uint32(0xFFFF0000), F32) * gk
        ms = (jnp.sum(acc_lo * acc_lo, axis=-1, keepdims=True)
              + jnp.sum(acc_hi * acc_hi, axis=-1, keepdims=True)) / (2 * half)
        inv = lax.rsqrt(ms + NORM_EPS)
        o_ref[rows, :half] = acc_lo * inv * fn_ref[:, :half]
        o_ref[rows, half:] = acc_hi * inv * fn_ref[:, half:]

    @pl.when(i == 0)
    def _():
        def first(g, carry):
            start_group(dest_ref, 0, g)
            return carry
        lax.fori_loop(0, groups, first, 0)

    pltpu.make_async_copy(ys_ref.at[pl.ds(0, TOP_K * tt)], ys_ref.at[pl.ds(0, TOP_K * tt)], sem.at[slot]).wait()
    is_last = i + 1 == pl.num_programs(0)

    @pl.when(jnp.logical_not(is_last))
    def _():
        def both(g, carry):
            start_group(dest_next_ref, 1 - slot, g)
            sum_group(g)
            return carry
        lax.fori_loop(0, groups, both, 0, unroll=8)

    @pl.when(is_last)
    def _():
        def only_sum(g, carry):
            sum_group(g)
            return carry
        lax.fori_loop(0, groups, only_sum, 0, unroll=4)


def _combine(dest, gates_t, x1, final_gain, ys, row_off):
    n, d = x1.shape
    tt = _tile(n, 256)
    assert row_off % tt == 0
    off = row_off // tt
    nt = n // tt
    return pl.pallas_call(
        _combine_kernel,
        grid=(nt,),
        in_specs=[
            pl.BlockSpec((TOP_K, tt), lambda i: (0, off + i), memory_space=pltpu.SMEM),
            pl.BlockSpec((TOP_K, tt), lambda i: (0, off + jnp.minimum(i + 1, nt - 1)), memory_space=pltpu.SMEM),
            pl.BlockSpec((tt, TOP_K), lambda i: (off + i, 0)),
            pl.BlockSpec((tt, d), lambda i: (i, 0)),
            pl.BlockSpec((1, d), lambda i: (0, 0)),
            pl.BlockSpec(memory_space=pl.ANY),
        ],
        out_specs=pl.BlockSpec((tt, d), lambda i: (i, 0)),
        out_shape=jax.ShapeDtypeStruct((n, d), F32),
        scratch_shapes=[pltpu.VMEM((2, TOP_K * tt // 8, 8, d // 2), jnp.uint32), pltpu.SemaphoreType.DMA((2,))],
        compiler_params=_params("arbitrary"),
        name="combine_rows",
    )(dest, dest, gates_t, x1, final_gain, ys)


def _rope_tables(seq_len):
    rows = seq_len // GRID_W
    row = jnp.repeat(jnp.arange(rows, dtype=F32), GRID_W)
    col = jnp.tile(jnp.arange(GRID_W, dtype=F32), rows)
    freqs = ROPE_THETA ** (-jnp.arange(ROPE_HALF, dtype=F32) / ROPE_HALF)
    ang_r = row[:, None] * freqs[None, :]
    ang_c = col[:, None] * freqs[None, :]
    cos = jnp.concatenate([jnp.cos(ang_r), jnp.cos(ang_r), jnp.cos(ang_c), jnp.cos(ang_c)], axis=1)
    sin = jnp.concatenate([-jnp.sin(ang_r), jnp.sin(ang_r), -jnp.sin(ang_c), jnp.sin(ang_c)], axis=1)
    return cos, sin


def kernel(x_prompt, x_sample, mix_norm, w_in, q_norm, k_norm, hg_lb_logits, hg_out_norm, w_up_attn,
           w_up_hgrn, w_out, ffn_norm, w_router, b_router, w_gate_up, b_gate_up, w_down, b_down, final_norm):
    assert mix_norm.shape[0] == 1, "single trunk layer"
    d = x_prompt.shape[-1]
    hg_w = HG_HEADS * HG_D
    n_exp = w_router.shape[-1]
    row_block = MOE_ROW_BLOCK
    streams = [(x.reshape(-1, d), x.shape[0], x.shape[1]) for x in (x_prompt, x_sample)]
    n_total = sum(x.shape[0] for x, _, _ in streams)

    lb = jnp.cumsum(jax.nn.softmax(hg_lb_logits.astype(F32), axis=1), axis=1)[:, 0]
    w_in_b = w_in[0].astype(BF16)
    w_ua_b = w_up_attn[0].astype(BF16)
    w_uh_b = w_up_hgrn[0].astype(BF16)
    w_out_b = w_out[0].astype(BF16)
    w_r_t = w_router[0].T
    mix_g = mix_norm[0].reshape(1, d)
    score_bound = (1.02 * LOG2_E * HEAD_DIM ** 0.5 * jnp.max(jnp.abs(q_norm[0])) * jnp.max(jnp.abs(k_norm[0])))
    score_bound = score_bound.astype(F32).reshape(1)

    w_gu2d = w_gate_up[0].reshape(-1, w_gate_up.shape[-1])
    cast_host = max(range(len(streams)), key=lambda s: streams[s][0].shape[0])
    w_gu_b = None

    x1s, idxs, gate_ts = [], [], []
    hp = None
    row_off = 0
    for s, (x, batch, seq) in enumerate(streams):
        cos, sin = _rope_tables(seq)
        qkv, zh, gates = _norm_proj(x, mix_g, w_in_b, q_norm[0].reshape(1, HEAD_DIM),
                                    k_norm[0].reshape(1, HEAD_DIM), cos, sin, seq)
        if s == cast_host:
            attn, w_gu_b = _attention(qkv, score_bound, batch, seq, w_gu2d)
        else:
            attn, = _attention(qkv, score_bound, batch, seq)
        hg = _hgrn(zh, lb[0:1], lb[1:2], hg_out_norm[0].reshape(1, hg_w), batch, seq)
        merged = _merge(attn, hg, gates, w_ua_b, w_uh_b)
        x1, hp, idx, gate = _outproj_router(x, merged, w_out_b, ffn_norm[0].reshape(1, d), w_r_t,
                                            b_router[0].reshape(n_exp, 1), hp, row_off, n_total)
        x1s.append(x1)
        idxs.append(idx)
        gate_ts.append(gate.T)
        row_off += x.shape[0]

    idx = jnp.concatenate(idxs, axis=1)
    gate_t = jnp.concatenate(gate_ts, axis=0)
    dest, counts = _route(idx, row_block)
    cnt = counts[:, 0].astype(jnp.int32)
    padded = (cnt + row_block - 1) // row_block * row_block
    pad_end = jnp.cumsum(padded)
    n_rows = n_total * TOP_K + n_exp * row_block
    n_blocks = n_rows // row_block
    blk_start = jnp.arange(n_blocks, dtype=jnp.int32) * row_block
    blk_e = jnp.minimum(jnp.sum(pad_end[None, :] <= blk_start[:, None], axis=1), n_exp - 1).astype(jnp.int32)
    seg_start = pad_end - padded
    blk_valid = jnp.clip(cnt[blk_e] - (blk_start - seg_start[blk_e]), 0, row_block)
    blk_valid = jnp.where(blk_start < pad_end[-1], blk_valid, 0).astype(jnp.int32)
    meta = jnp.concatenate([(pad_end[-1:] // row_block).astype(jnp.int32), blk_e, blk_valid])
    seg = jnp.stack([seg_start, pad_end]).astype(jnp.int32)

    xs = _dispatch(seg, dest, hp, n_rows, row_block)
    ys = _experts(meta, xs, w_gu_b.reshape(w_gate_up.shape[1:]), b_gate_up[0].reshape(n_exp, 1, -1),
                  w_down[0], b_down[0].reshape(n_exp, 1, d), row_block)

    outs = []
    row_off = 0
    for (x, batch, seq), x1 in zip(streams, x1s):
        out = _combine(dest, gate_t, x1, final_norm.reshape(1, d), ys, row_off)
        outs.append(out.reshape(batch, seq, d))
        row_off += x.shape[0]
    return tuple(outs)
```

```python
import functools

import jax
import jax.numpy as jnp
from jax import lax
from jax.experimental import pallas as pl
from jax.experimental.pallas import tpu as pltpu

GRID_W = 64
HEAD_DIM = 128
N_Q_HEADS = 16
N_KV_HEADS = 4
ROPE_THETA = 10000.0
ROPE_HALF = HEAD_DIM // 4
HG_HEADS = 8
HG_D = 128
HG_CHUNK = 64
N_EXPERTS = 32
TOP_K = 4
SWIGLU_LIMIT = 7.0
SWIGLU_ALPHA = 1.702
NORM_EPS = 1e-5

HG_SAFE_LOG_DECAY = -60.0

LOG2_E = 1.4426950408889634
ATTN_FIXED_SHIFT_LIMIT = 60.0

V7X_VMEM_BYTES = 64 * 1024 * 1024
VMEM_LIMIT_BYTES = V7X_VMEM_BYTES - 8 * 1024 * 1024
LANES = 128
MOE_ROW_BLOCK = 512
DMA_PRIORITIES = 2

BF16 = jnp.bfloat16
F32 = jnp.float32


def _params(*sem):
    return pltpu.CompilerParams(dimension_semantics=sem, vmem_limit_bytes=VMEM_LIMIT_BYTES)


def _tile(n, pref):
    t = min(n, pref)
    while n % t:
        t //= 2
    return t


def _sigmoid(x):
    return 1.0 / (1.0 + jnp.exp(-x))


def _rmsnorm(x, g):
    return x * lax.rsqrt(jnp.mean(x * x, axis=-1, keepdims=True) + NORM_EPS) * g


def _dot(a, b):
    return jnp.dot(a, b, preferred_element_type=F32)


def _dot_nt(a, b):
    return lax.dot_general(a, b, (((1,), (1,)), ((), ())), preferred_element_type=F32)


def _split3(x):
    hi = x.astype(BF16)
    r = x - hi.astype(F32)
    mid = r.astype(BF16)
    lo = (r - mid.astype(F32)).astype(BF16)
    return hi, mid, lo


def _dot_exact_lhs(m_bf16, x, pieces=3):
    parts = _split3(x)[:pieces]
    acc = _dot(m_bf16, parts[0])
    for p in parts[1:]:
        acc = acc + _dot(m_bf16, p)
    return acc


def _rope_head_pair(zp, gain, cos, sin, ones_blk, perm_blk, scale):
    ss = _dot((zp * zp).astype(BF16), ones_blk)
    y = zp * lax.rsqrt(ss * (1.0 / HEAD_DIM) + NORM_EPS) * gain
    y_hi = y.astype(BF16)
    y_lo = (y - y_hi.astype(F32)).astype(BF16)
    partner = _dot(y_hi, perm_blk) + _dot(y_lo, perm_blk)
    return (y * cos + partner * sin) * scale


def _proj_kernel(x_ref, g_ref, w_ref, qn_ref, kn_ref, cos_ref, sin_ref, ones_ref, perm_ref,
                 qkv_ref, zh_ref, gate_ref, h_scr, *, tile_kinds):
    j = pl.program_id(1)

    @pl.when(j == 0)
    def _():
        h_scr[...] = _rmsnorm(x_ref[...], g_ref[...]).astype(BF16)

    z = _dot(h_scr[...], w_ref[...])

    def qkv_tile(kinds):
        two = lambda r: jnp.concatenate([r[...], r[...]], axis=1)
        outs = []
        for h in range(0, len(kinds), 2):
            kind = kinds[h]
            assert kinds[h + 1] == kind
            zp = z[:, h * HEAD_DIM:(h + 2) * HEAD_DIM]
            if kind == "q":
                zp = _rope_head_pair(zp, two(qn_ref), two(cos_ref), two(sin_ref), ones_ref[...], perm_ref[...],
                                     LOG2_E * HEAD_DIM ** -0.5)
            elif kind == "k":
                zp = _rope_head_pair(zp, two(kn_ref), two(cos_ref), two(sin_ref), ones_ref[...], perm_ref[...], 1.0)
            outs.append(zp)
        return jnp.concatenate(outs, axis=1)

    for lo, hi, kind in tile_kinds:
        @pl.when((j >= lo) & (j < hi))
        def _(kind=kind):
            if kind == "zh":
                zh_ref[...] = z
            elif kind == "gate":
                gate_ref[...] = z.astype(gate_ref.dtype)
            else:
                qkv_ref[...] = qkv_tile(kind).astype(qkv_ref.dtype)


def _norm_proj(x, gain, w, q_gain, k_gain, cos, sin, seq):
    n, d = x.shape
    attn_w = N_Q_HEADS * HEAD_DIM
    kv_w = N_KV_HEADS * HEAD_DIM
    hg5 = 5 * HG_HEADS * HG_D
    qkv_w = attn_w + 2 * kv_w
    assert w.shape[1] == qkv_w + hg5 + 2 * d
    tm = _tile(seq, 1024)
    tn = 1024
    while attn_w % tn or (2 * kv_w) % tn or hg5 % tn or (2 * d) % tn:
        tn //= 2
    assert tn % HEAD_DIM == 0
    heads = ["q"] * N_Q_HEADS + ["k"] * N_KV_HEADS + ["v"] * N_KV_HEADS
    hpt = tn // HEAD_DIM
    n_qkv, n_zh, n_gate = qkv_w // tn, hg5 // tn, 2 * d // tn
    tile_kinds = []
    for t in range(n_qkv):
        kind = tuple(heads[t * hpt:(t + 1) * hpt])
        if tile_kinds and tile_kinds[-1][2] == kind:
            tile_kinds[-1] = (tile_kinds[-1][0], t + 1, kind)
        else:
            tile_kinds.append((t, t + 1, kind))
    tile_kinds += [(n_qkv, n_qkv + n_zh, "zh"), (n_qkv + n_zh, n_qkv + n_zh + n_gate, "gate")]
    n_pos = seq // tm
    r = jnp.arange(2 * HEAD_DIM)
    same_head = (r[:, None] // HEAD_DIM) == (r[None, :] // HEAD_DIM)
    ones_blk = same_head.astype(BF16)
    partner_of = jnp.where((r % (2 * ROPE_HALF)) < ROPE_HALF, r + ROPE_HALF, r - ROPE_HALF)
    perm_blk = (r[:, None] == partner_of[None, :]).astype(BF16)
    const_spec = pl.BlockSpec((2 * HEAD_DIM, 2 * HEAD_DIM), lambda i, j: (0, 0))
    return pl.pallas_call(
        functools.partial(_proj_kernel, tile_kinds=tile_kinds),
        grid=(n // tm, n_qkv + n_zh + n_gate),
        in_specs=[
            pl.BlockSpec((tm, d), lambda i, j: (i, 0)),
            pl.BlockSpec((1, d), lambda i, j: (0, 0)),
            pl.BlockSpec((d, tn), lambda i, j: (0, j)),
            pl.BlockSpec((1, HEAD_DIM), lambda i, j: (0, 0)),
            pl.BlockSpec((1, HEAD_DIM), lambda i, j: (0, 0)),
            pl.BlockSpec((tm, HEAD_DIM), lambda i, j: (i % n_pos, 0)),
            pl.BlockSpec((tm, HEAD_DIM), lambda i, j: (i % n_pos, 0)),
            const_spec,
            const_spec,
        ],
        out_specs=[
            pl.BlockSpec((tm, tn), lambda i, j: (i, jnp.minimum(j, n_qkv - 1))),
            pl.BlockSpec((tm, tn), lambda i, j: (i, jnp.clip(j - n_qkv, 0, n_zh - 1))),
            pl.BlockSpec((tm, tn), lambda i, j: (i, jnp.clip(j - n_qkv - n_zh, 0, n_gate - 1))),
        ],
        out_shape=[
            jax.ShapeDtypeStruct((n, qkv_w), BF16),
            jax.ShapeDtypeStruct((n, hg5), F32),
            jax.ShapeDtypeStruct((n, 2 * d), BF16),
        ],
        scratch_shapes=[pltpu.VMEM((tm, d), BF16)],
        compiler_params=_params("parallel", "arbitrary"),
        name="norm_proj",
    )(x, gain, w, q_gain, k_gain, cos, sin, ones_blk, perm_blk)


def _attn_kernel(bound_ref, q_ref, k_ref, v_ref, *rest, tk, group):
    if len(rest) == 5:
        wf_ref, o_ref, wb_ref, acc_scr, m_scr = rest
    else:
        wf_ref = wb_ref = None
        o_ref, acc_scr, m_scr = rest
    tq = q_ref.shape[0]
    seq = k_ref.shape[0]
    n_chunks = seq // tk
    slab = None if wf_ref is None else wf_ref.shape[0] // n_chunks
    q = jnp.concatenate([q_ref[:, g * HEAD_DIM:(g + 1) * HEAD_DIM] for g in range(group)], axis=0)
    ones = jnp.ones((tk, HEAD_DIM), BF16)
    bound = bound_ref[0]
    acc_scr[...] = jnp.zeros(acc_scr.shape, F32)

    def chunk(c):
        rows = pl.ds(pl.multiple_of(c * tk, tk), tk)
        s = _dot_nt(q, k_ref[rows, :])
        return s, jnp.concatenate([v_ref[rows, :], ones], axis=1)

    @pl.when(bound <= ATTN_FIXED_SHIFT_LIMIT)
    def _():
        def body(c, carry):
            s, v1 = chunk(c)
            acc_scr[...] += _dot(jnp.exp2(s - bound).astype(BF16), v1)
            if wf_ref is not None:
                wrows = pl.ds(pl.multiple_of(c * slab, slab), slab)
                wb_ref[wrows, :] = wf_ref[wrows, :].astype(BF16)
            return carry
        lax.fori_loop(0, n_chunks, body, 0, unroll=True)

    @pl.when(bound > ATTN_FIXED_SHIFT_LIMIT)
    def _():
        if wf_ref is not None:
            wb_ref[...] = wf_ref[...].astype(BF16)
        m_scr[...] = jnp.full(m_scr.shape, -1e30, F32)

        def body(c, carry):
            s, v1 = chunk(c)
            m_prev = m_scr[...]
            m_new = jnp.maximum(m_prev, jnp.max(s, axis=1, keepdims=True))
            alpha = jnp.exp2(m_prev - m_new)
            p = jnp.exp2(s - jnp.tile(m_new, (1, tk // LANES)))
            acc_scr[...] = jnp.tile(alpha, (1, 2)) * acc_scr[...] + _dot(p.astype(BF16), v1)
            m_scr[...] = m_new
            return carry
        lax.fori_loop(0, seq // tk, body, 0)

    acc = acc_scr[...]
    o = acc[:, :HEAD_DIM] / acc[:, HEAD_DIM:]
    for g in range(group):
        o_ref[:, g * HEAD_DIM:(g + 1) * HEAD_DIM] = o[g * tq:(g + 1) * tq].astype(o_ref.dtype)


def _attention(qkv, score_bound, batch, seq, w_f32=None):
    group = N_Q_HEADS // N_KV_HEADS
    gw = group * HEAD_DIM
    tq = _tile(seq, 512)
    tk = _tile(seq, 512)
    nq = seq // tq
    in_specs = [
        pl.BlockSpec(memory_space=pltpu.SMEM),
        pl.BlockSpec((tq, gw), lambda b, n, i: (b * nq + i, n)),
        pl.BlockSpec((seq, HEAD_DIM), lambda b, n, i: (b, N_Q_HEADS + n)),
        pl.BlockSpec((seq, HEAD_DIM), lambda b, n, i: (b, N_Q_HEADS + N_KV_HEADS + n)),
    ]
    out_specs = [pl.BlockSpec((tq, gw), lambda b, n, i: (b * nq + i, n))]
    out_shape = [jax.ShapeDtypeStruct((batch * seq, N_Q_HEADS * HEAD_DIM), BF16)]
    args = [score_bound, qkv, qkv, qkv]
    if w_f32 is not None:
        n_steps = batch * N_KV_HEADS * nq
        w_rows, w_cols = w_f32.shape
        slab = w_rows // n_steps
        assert slab * n_steps == w_rows and slab % (16 * (seq // tk)) == 0, (w_rows, n_steps, seq // tk)
        wspec = pl.BlockSpec((slab, w_cols), lambda b, n, i: ((b * N_KV_HEADS + n) * nq + i, 0))
        in_specs.append(wspec)
        out_specs.append(wspec)
        out_shape.append(jax.ShapeDtypeStruct((w_rows, w_cols), BF16))
        args.append(w_f32)
    return pl.pallas_call(
        functools.partial(_attn_kernel, tk=tk, group=group),
        grid=(batch, N_KV_HEADS, nq),
        in_specs=in_specs,
        out_specs=out_specs,
        out_shape=out_shape,
        scratch_shapes=[
            pltpu.VMEM((group * tq, 2 * HEAD_DIM), F32),
            pltpu.VMEM((group * tq, LANES), F32),
        ],
        compiler_params=_params("parallel", "parallel", "arbitrary"),
        name="gqa_attention",
    )(*args)


def _hgrn_span(i, q_scr, b_scr, k_scr, zi_ref, vt_scr, o_scr, s_scr, tri, *, span, reverse, pairwise):
    C = HG_CHUNK
    nc = span // C
    r0 = pl.multiple_of(i * span, span)
    rows = pl.ds(r0, span)
    q = q_scr[rows, :]
    b = b_scr[rows, :]
    k = k_scr[rows, :]
    v = zi_ref[rows, :]
    vb = v.astype(BF16)
    end_row = 0 if reverse else C - 1
    ends = [b[c * C + end_row:c * C + end_row + 1, :] for c in range(nc)]
    b_end = jnp.concatenate([jnp.broadcast_to(e, (C, HG_D)) for e in ends], axis=0)
    qtb = (q * jnp.exp(b)).astype(BF16)
    kp = k * jnp.exp(b_end - b)

    if pairwise:
        rid = lax.broadcasted_iota(jnp.int32, (span, HG_D), 0)

        def pair(s, acc):
            cs = (s // C) * C
            if reverse:
                m = (rid <= s) & (rid >= cs)
            else:
                m = (rid >= s) & (rid < cs + C)
            w = jnp.where(m, jnp.exp(jnp.minimum(b - b_scr[pl.ds(r0 + s, 1), :], 0.0)), 0.0)
            r = jnp.sum(q * w * k_scr[pl.ds(r0 + s, 1), :], axis=1, keepdims=True)
            return acc + r * zi_ref[pl.ds(r0 + s, 1), :]

        o_intra = lax.fori_loop(0, span, pair, jnp.zeros((span, HG_D), F32))
    else:
        ktb = (k * jnp.exp(-b)).astype(BF16)
        a = jnp.where(tri, _dot_nt(qtb, ktb), 0.0)
        o_intra = _dot(a.astype(BF16), vb)

    chunk_of_row = lax.broadcasted_iota(jnp.int32, (span, HG_D), 0) // C
    kp_blocks = jnp.concatenate([jnp.where(chunk_of_row == c, kp, 0.0) for c in range(nc)], axis=1)
    upd = _dot(vt_scr[:, rows], kp_blocks.astype(BF16))

    st = s_scr[...]
    states = [None] * nc
    for c in (reversed(range(nc)) if reverse else range(nc)):
        states[c] = st.astype(BF16)
        st = st * jnp.exp(ends[c]) + upd[:, c * HG_D:(c + 1) * HG_D]
    s_scr[...] = st
    o_inter = _dot_nt(qtb, jnp.concatenate(states, axis=0))
    o_scr[rows, :] = o_intra + jnp.concatenate(
        [o_inter[c * C:(c + 1) * C, c * HG_D:(c + 1) * HG_D] for c in range(nc)], axis=0)


def _hgrn_kernel(zq_ref, zf_ref, zb_ref, zi_ref, zo_ref, lbf_ref, lbb_ref, on_ref, out_ref,
                 of_scr, ob_scr, sf_scr, sb_scr, q_scr, bf_scr, bb_scr, kf_scr, kb_scr, vt_scr, *, span):
    seq = zq_ref.shape[0]
    n_span = seq // span
    C = HG_CHUNK
    nc = span // C
    sf_scr[...] = jnp.zeros(sf_scr.shape, F32)
    sb_scr[...] = jnp.zeros(sb_scr.shape, F32)
    r = lax.broadcasted_iota(jnp.int32, (span, span), 0)
    c = lax.broadcasted_iota(jnp.int32, (span, span), 1)
    same = (r // C) == (c // C)
    tri_f = same & (c <= r)
    tri_b = same & (c >= r)
    tri_f16 = tri_f.astype(BF16)
    tri_b16 = tri_b.astype(BF16)

    def prepare(i, min_end):
        rows = pl.ds(pl.multiple_of(i * span, span), span)
        qh = zq_ref[rows, :]
        q_scr[rows, :] = qh * _sigmoid(qh)
        vt_scr[:, rows] = zi_ref[rows, :].T.astype(BF16)
        for z_ref, lb_ref, tri, b_scr, k_scr, end_row in (
                (zf_ref, lbf_ref, tri_f16, bf_scr, kf_scr, C - 1), (zb_ref, lbb_ref, tri_b16, bb_scr, kb_scr, 0)):
            lb = lb_ref[...]
            f = lb + (1.0 - lb) * _sigmoid(z_ref[rows, :])
            b = _dot_exact_lhs(tri, jnp.log(f), pieces=2)
            b_scr[rows, :] = b
            k_scr[rows, :] = 1.0 - f
            for cc in range(nc):
                min_end = jnp.minimum(min_end, b[cc * C + end_row:cc * C + end_row + 1, :])
        return min_end

    min_end = lax.fori_loop(0, n_span, prepare, jnp.zeros((1, HG_D), F32), unroll=2)
    safe = jnp.min(min_end) > HG_SAFE_LOG_DECAY

    def scan(pairwise):
        def body(i, carry):
            _hgrn_span(i, q_scr, bf_scr, kf_scr, zi_ref, vt_scr, of_scr, sf_scr, tri_f,
                       span=span, reverse=False, pairwise=pairwise)
            _hgrn_span(n_span - 1 - i, q_scr, bb_scr, kb_scr, zi_ref, vt_scr, ob_scr, sb_scr, tri_b,
                       span=span, reverse=True, pairwise=pairwise)
            return carry
        lax.fori_loop(0, n_span, body, 0, unroll=1 if pairwise else 4)

    @pl.when(safe)
    def _():
        scan(False)

    @pl.when(jnp.logical_not(safe))
    def _():
        scan(True)

    def finish(i, carry):
        rows = pl.ds(pl.multiple_of(i * span, span), span)
        o = _rmsnorm(of_scr[rows, :] + ob_scr[rows, :], on_ref[...])
        og = zo_ref[rows, :]
        out_ref[rows, :] = (o * (og * _sigmoid(og))).astype(out_ref.dtype)
        return carry

    lax.fori_loop(0, n_span, finish, 0)


def _hgrn(zh, lb_f, lb_b, out_norm, batch, seq):
    span = _tile(seq, 256)
    assert span % HG_CHUNK == 0
    zspec = lambda grp: pl.BlockSpec((seq, HG_D), lambda b, h: (b, grp * HG_HEADS + h))
    hspec = pl.BlockSpec((1, HG_D), lambda b, h: (0, h))
    seq_buf = pltpu.VMEM((seq, HG_D), F32)
    return pl.pallas_call(
        functools.partial(_hgrn_kernel, span=span),
        grid=(batch, HG_HEADS),
        in_specs=[zspec(0), zspec(1), zspec(2), zspec(3), zspec(4), hspec, hspec, hspec],
        out_specs=pl.BlockSpec((seq, HG_D), lambda b, h: (b, h)),
        out_shape=jax.ShapeDtypeStruct((batch * seq, HG_HEADS * HG_D), BF16),
        scratch_shapes=[
            seq_buf, seq_buf,
            pltpu.VMEM((HG_D, HG_D), F32), pltpu.VMEM((HG_D, HG_D), F32),
            seq_buf, seq_buf, seq_buf, seq_buf, seq_buf,
            pltpu.VMEM((HG_D, seq), BF16),
        ],
        compiler_params=_params("parallel", "parallel"),
        name="hgrn2",
    )(zh, zh, zh, zh, zh, lb_f, lb_b, out_norm)


def _merge_kernel(a_ref, h_ref, ga_ref, gb_ref, wa_ref, wh_ref, o_ref):
    ya = _dot(a_ref[...], wa_ref[...])
    yb = _dot(h_ref[...], wh_ref[...])
    merged = _sigmoid(ga_ref[...].astype(F32)) * ya + _sigmoid(gb_ref[...].astype(F32)) * yb
    o_ref[...] = merged.astype(o_ref.dtype)


def _merge(attn, hg, gates, w_up_attn, w_up_hgrn):
    n, wa = attn.shape
    wh = hg.shape[1]
    d = w_up_attn.shape[1]
    tm = _tile(n, 1024)
    tn = _tile(d, 512)
    nj = d // tn
    return pl.pallas_call(
        _merge_kernel,
        grid=(n // tm, nj),
        in_specs=[
            pl.BlockSpec((tm, wa), lambda i, j: (i, 0)),
            pl.BlockSpec((tm, wh), lambda i, j: (i, 0)),
            pl.BlockSpec((tm, tn), lambda i, j: (i, j)),
            pl.BlockSpec((tm, tn), lambda i, j: (i, nj + j)),
            pl.BlockSpec((wa, tn), lambda i, j: (0, j)),
            pl.BlockSpec((wh, tn), lambda i, j: (0, j)),
        ],
        out_specs=pl.BlockSpec((tm, tn), lambda i, j: (i, j)),
        out_shape=jax.ShapeDtypeStruct((n, d), BF16),
        compiler_params=_params("parallel", "arbitrary"),
        name="gated_merge",
    )(attn, hg, gates, gates, w_up_attn, w_up_hgrn)


def _pack_bf16_pairs(h):
    half = h.shape[1] // 2
    lo = pltpu.bitcast(h[:, :half].astype(BF16).astype(F32), jnp.uint32)
    hi = pltpu.bitcast(h[:, half:].astype(BF16).astype(F32), jnp.uint32)
    return (hi & jnp.uint32(0xFFFF0000)) | (lo >> 16)


def _unpack_bf16_pairs(u):
    lo = pltpu.bitcast(u << 16, F32).astype(BF16)
    hi = pltpu.bitcast(u & jnp.uint32(0xFFFF0000), F32).astype(BF16)
    return lo, hi


def _outproj_router_kernel(x_ref, m_ref, w_ref, g_ref, wr_ref, br_ref, hp_in_ref,
                           x1_ref, hp_ref, idx_ref, gate_ref):
    del hp_in_ref
    x1 = x_ref[...] + _dot(m_ref[...], w_ref[...])
    x1_ref[...] = x1
    h = _rmsnorm(x1, g_ref[...])
    hp_ref[...] = _pack_bf16_pairs(h)
    h_hi, h_mid, _ = _split3(h)
    w_hi, w_mid, _ = _split3(wr_ref[...])
    lg = (_dot_nt(w_hi, h_hi) + _dot_nt(w_hi, h_mid) + _dot_nt(w_mid, h_hi)) + br_ref[...]
    n_exp, tm = lg.shape
    eid = lax.broadcasted_iota(jnp.int32, (n_exp, tm), 0)
    vals = []
    for kk in range(TOP_K):
        m = jnp.max(lg, axis=0, keepdims=True)
        sel = jnp.min(jnp.where(lg == m, eid, n_exp), axis=0, keepdims=True)
        idx_ref[kk:kk + 1, :] = sel
        vals.append(m)
        lg = jnp.where(eid == sel, -jnp.inf, lg)
    ex = [jnp.exp(vv - vals[0]) for vv in vals]
    den = ex[0]
    for e in ex[1:]:
        den = den + e
    for kk in range(TOP_K):
        gate_ref[kk:kk + 1, :] = ex[kk] / den


def _outproj_router(x, merged, w_out, ffn_gain, w_router_t, b_router, hp_prev, row_off, n_total):
    n, d = x.shape
    tm = _tile(n, 512)
    assert row_off % tm == 0
    n_exp = w_router_t.shape[0]
    if hp_prev is None:
        hp_prev = jnp.zeros((8, LANES), jnp.uint32)
        aliases = {}
    else:
        aliases = {6: 1}
    return pl.pallas_call(
        _outproj_router_kernel,
        grid=(n // tm,),
        in_specs=[
            pl.BlockSpec((tm, d), lambda i: (i, 0)),
            pl.BlockSpec((tm, d), lambda i: (i, 0)),
            pl.BlockSpec((d, d), lambda i: (0, 0)),
            pl.BlockSpec((1, d), lambda i: (0, 0)),
            pl.BlockSpec((n_exp, d), lambda i: (0, 0)),
            pl.BlockSpec((n_exp, 1), lambda i: (0, 0)),
            pl.BlockSpec(memory_space=pl.ANY),
        ],
        out_specs=[
            pl.BlockSpec((tm, d), lambda i: (i, 0)),
            pl.BlockSpec((tm, d // 2), lambda i: (row_off // tm + i, 0)),
            pl.BlockSpec((TOP_K, tm), lambda i: (0, i)),
            pl.BlockSpec((TOP_K, tm), lambda i: (0, i)),
        ],
        out_shape=[
            jax.ShapeDtypeStruct((n, d), F32),
            jax.ShapeDtypeStruct((n_total, d // 2), jnp.uint32),
            jax.ShapeDtypeStruct((TOP_K, n), jnp.int32),
            jax.ShapeDtypeStruct((TOP_K, n), F32),
        ],
        input_output_aliases=aliases,
        compiler_params=_params("parallel"),
        name="outproj_router",
    )(x, merged, w_out, ffn_gain, w_router_t, b_router, hp_prev)


def _route_kernel(idx_ref, dest_ref, cnt_ref, cnt_scr, base_scr, *, row_block):
    phase = pl.program_id(0)
    i = pl.program_id(1)
    n_exp = cnt_scr.shape[0]
    tt = idx_ref.shape[1]
    eid = lax.broadcasted_iota(jnp.int32, (n_exp, tt), 0)
    onehot = [(eid == idx_ref[kk:kk + 1, :]) for kk in range(TOP_K)]

    @pl.when((phase == 0) & (i == 0))
    def _():
        cnt_scr[...] = jnp.zeros(cnt_scr.shape, F32)

    @pl.when(phase == 0)
    def _():
        tot = onehot[0].astype(F32)
        for oh in onehot[1:]:
            tot = tot + oh.astype(F32)
        cnt_scr[...] = cnt_scr[...] + jnp.sum(tot, axis=1, keepdims=True)
        cnt_ref[...] = cnt_scr[...]

    @pl.when((phase == 1) & (i == 0))
    def _():
        cnt = cnt_scr[...].astype(jnp.int32)
        padded = ((cnt + (row_block - 1)) // row_block * row_block).astype(F32)
        er = lax.broadcasted_iota(jnp.int32, (n_exp, n_exp), 0)
        ec = lax.broadcasted_iota(jnp.int32, (n_exp, n_exp), 1)
        base_scr[...] = _dot_exact_lhs((ec < er).astype(BF16), padded)

    @pl.when(phase == 1)
    def _():
        tr = lax.broadcasted_iota(jnp.int32, (tt, tt), 0)
        tc = lax.broadcasted_iota(jnp.int32, (tt, tt), 1)
        before = (tr < tc).astype(BF16)
        run = base_scr[...][:, :1]
        for kk in range(TOP_K):
            oh = onehot[kk].astype(F32)
            rank = _dot(oh.astype(BF16), before) + run
            dest_ref[kk:kk + 1, :] = jnp.sum(oh * rank, axis=0, keepdims=True).astype(jnp.int32)
            run = run + jnp.sum(oh, axis=1, keepdims=True)
        base_scr[...] = jnp.broadcast_to(run, base_scr.shape)


def _route(idx, row_block):
    n = idx.shape[1]
    tt = _tile(n, 512)
    return pl.pallas_call(
        functools.partial(_route_kernel, row_block=row_block),
        grid=(2, n // tt),
        in_specs=[pl.BlockSpec((TOP_K, tt), lambda p, i: (0, i))],
        out_specs=[
            pl.BlockSpec((TOP_K, tt), lambda p, i: (0, i * p)),
            pl.BlockSpec((N_EXPERTS, LANES), lambda p, i: (0, 0)),
        ],
        out_shape=[
            jax.ShapeDtypeStruct((TOP_K, n), jnp.int32),
            jax.ShapeDtypeStruct((N_EXPERTS, LANES), F32),
        ],
        scratch_shapes=[pltpu.VMEM((N_EXPERTS, LANES), F32), pltpu.VMEM((N_EXPERTS, LANES), F32)],
        compiler_params=_params("arbitrary", "arbitrary"),
        name="route_offsets",
    )(idx)


def _dispatch_kernel(seg_ref, dest_ref, h_ref, xs_ref, zero_scr, sem, zsem, *, row_block):
    tt = dest_ref.shape[1]
    n_exp = seg_ref.shape[1]

    @pl.when(pl.program_id(0) == 0)
    def _():
        zero_scr[...] = jnp.zeros(zero_scr.shape, zero_scr.dtype)

        def zero_copy(e):
            start = pl.multiple_of(seg_ref[1, e] - row_block, row_block)
            return pltpu.make_async_copy(zero_scr, xs_ref.at[pl.ds(start, row_block)], zsem)

        for e in range(n_exp):
            @pl.when(seg_ref[1, e] > seg_ref[0, e])
            def _(e=e):
                zero_copy(e).start()
        for e in range(n_exp):
            @pl.when(seg_ref[1, e] > seg_ref[0, e])
            def _(e=e):
                zero_copy(e).wait()

    def start(g, carry):
        for u in range(8):
            for kk in range(TOP_K):
                pltpu.make_async_copy(h_ref.at[g, pl.ds(u, 1), :],
                                      xs_ref.at[pl.ds(dest_ref[kk, g * 8 + u], 1)],
                                      sem).start(priority=(u * TOP_K + kk) % DMA_PRIORITIES)
        return carry

    lax.fori_loop(0, tt // 8, start, 0)
    pltpu.make_async_copy(xs_ref.at[pl.ds(0, TOP_K * tt)], xs_ref.at[pl.ds(0, TOP_K * tt)], sem).wait()


def _dispatch(seg, dest, hp, n_rows, row_block):
    n, w = hp.shape
    tt = _tile(n, 1024)
    grid_spec = pltpu.PrefetchScalarGridSpec(
        num_scalar_prefetch=1,
        grid=(n // tt,),
        in_specs=[
            pl.BlockSpec((TOP_K, tt), lambda i, s: (0, i), memory_space=pltpu.SMEM),
            pl.BlockSpec((tt // 8, 8, w), lambda i, s: (i, 0, 0)),
        ],
        out_specs=pl.BlockSpec(memory_space=pl.ANY),
        scratch_shapes=[pltpu.VMEM((row_block, w), hp.dtype), pltpu.SemaphoreType.DMA(()),
                        pltpu.SemaphoreType.DMA(())],
    )
    return pl.pallas_call(
        functools.partial(_dispatch_kernel, row_block=row_block),
        grid_spec=grid_spec,
        out_shape=jax.ShapeDtypeStruct((n_rows, w), hp.dtype),
        compiler_params=_params("arbitrary"),
        name="dispatch_rows",
    )(seg, dest, hp.reshape(n // 8, 8, w))


def _expert_kernel(meta_ref, xs_ref, wg_ref, wl_ref, bg_ref, bl_ref, wd_ref, bd_ref, ys_ref,
                   x_scr, y_ref):
    i = pl.program_id(0)
    j = pl.program_id(1)
    n_blocks = pl.num_programs(0)
    row_block = xs_ref.shape[0]
    valid = meta_ref[1 + n_blocks + i]

    def block(m):
        rows = slice(0, m)

        @pl.when(j == 0)
        def _():
            half = xs_ref.shape[1]
            lo, hi = _unpack_bf16_pairs(xs_ref[rows, :])
            x_scr[rows, :half] = lo
            x_scr[rows, half:] = hi
            y_ref[rows, :] = jnp.broadcast_to(bd_ref[...], (m, y_ref.shape[1]))

        x = x_scr[rows, :]
        glu = _dot(x, wg_ref[...]) + bg_ref[...]
        lin = _dot(x, wl_ref[...]) + bl_ref[...]
        glu = jnp.minimum(glu, SWIGLU_LIMIT)
        lin = jnp.clip(lin, -SWIGLU_LIMIT, SWIGLU_LIMIT)
        act = glu * _sigmoid(SWIGLU_ALPHA * glu) * (lin + 1.0)
        y_ref[rows, :] = y_ref[rows, :] + _dot(act.astype(BF16), wd_ref[...].astype(BF16))

        @pl.when(j == pl.num_programs(1) - 1)
        def _():
            ys_ref[rows, :] = _pack_bf16_pairs(y_ref[rows, :])

    @pl.when(valid > row_block // 2)
    def _():
        block(row_block)

    @pl.when((valid > 0) & (valid <= row_block // 2))
    def _():
        block(row_block // 2)


def _experts(meta, xs, w_gu, b_gu, w_dn, b_dn, row_block):
    n_rows, half = xs.shape
    d = 2 * half
    d_ff = w_dn.shape[1]
    tf = _tile(d_ff, 1024)
    nf = d_ff // tf
    n_blocks = n_rows // row_block

    def jj(i, j, m):
        return jnp.where(i < m[0], j, nf - 1)

    grid_spec = pltpu.PrefetchScalarGridSpec(
        num_scalar_prefetch=1,
        grid=(n_blocks, nf),
        in_specs=[
            pl.BlockSpec((row_block, half), lambda i, j, m: (i, 0)),
            pl.BlockSpec((None, d, tf), lambda i, j, m: (m[1 + i], 0, jj(i, j, m))),
            pl.BlockSpec((None, d, tf), lambda i, j, m: (m[1 + i], 0, nf + jj(i, j, m))),
            pl.BlockSpec((None, 1, tf), lambda i, j, m: (m[1 + i], 0, jj(i, j, m))),
            pl.BlockSpec((None, 1, tf), lambda i, j, m: (m[1 + i], 0, nf + jj(i, j, m))),
            pl.BlockSpec((None, tf, d), lambda i, j, m: (m[1 + i], jj(i, j, m), 0)),
            pl.BlockSpec((None, 1, d), lambda i, j, m: (m[1 + i], 0, 0)),
        ],
        out_specs=pl.BlockSpec((row_block, half), lambda i, j, m: (i, 0)),
        scratch_shapes=[pltpu.VMEM((row_block, d), BF16), pltpu.VMEM((row_block, d), F32)],
    )
    return pl.pallas_call(
        _expert_kernel,
        grid_spec=grid_spec,
        out_shape=jax.ShapeDtypeStruct((n_rows, half), jnp.uint32),
        compiler_params=_params("arbitrary", "arbitrary"),
        name="expert_swiglu",
    )(meta, xs, w_gu, w_gu, b_gu, b_gu, w_dn, b_dn)


def _combine_kernel(dest_ref, dest_next_ref, gate_ref, x1_ref, fn_ref, ys_ref, o_ref, buf, sem):
    tt = dest_ref.shape[1]
    i = pl.program_id(0)
    slot = i % 2

    def gather(d_ref, s):
        def start(g, carry):
            for u in range(8):
                for kk in range(TOP_K):
                    pltpu.make_async_copy(ys_ref.at[pl.ds(d_ref[kk, g * 8 + u], 1)],
                                          buf.at[s, kk * (tt // 8) + g, pl.ds(u, 1), :],
                                          sem.at[s]).start(priority=(u * TOP_K + kk) % DMA_PRIORITIES)
            return carry
        lax.fori_loop(0, tt // 8, start, 0)

    @pl.when(i == 0)
    def _():
        gather(dest_ref, 0)

    @pl.when(i + 1 < pl.num_programs(0))
    def _():
        gather(dest_next_ref, 1 - slot)

    pltpu.make_async_copy(ys_ref.at[pl.ds(0, TOP_K * tt)], ys_ref.at[pl.ds(0, TOP_K * tt)], sem.at[slot]).wait()
    half = buf.shape[3]
    gates = gate_ref[...]
    acc_lo = x1_ref[:, :half]
    acc_hi = x1_ref[:, half:]
    for kk in range(TOP_K):
        u = buf[slot, pl.ds(kk * (tt // 8), tt // 8)].reshape(tt, half)
        g = gates[:, kk:kk + 1]
        acc_lo = acc_lo + pltpu.bitcast(u << 16, F32) * g
        acc_hi = acc_hi + pltpu.bitcast(u & jnp.uint32(0xFFFF0000), F32) * g
    ms = (jnp.sum(acc_lo * acc_lo, axis=-1, keepdims=True)
          + jnp.sum(acc_hi * acc_hi, axis=-1, keepdims=True)) / (2 * half)
    inv = lax.rsqrt(ms + NORM_EPS)
    o_ref[:, :half] = acc_lo * inv * fn_ref[:, :half]
    o_ref[:, half:] = acc_hi * inv * fn_ref[:, half:]


def _combine(dest, gates_t, x1, final_gain, ys, row_off):
    n, d = x1.shape
    tt = _tile(n, 256)
    assert row_off % tt == 0
    off = row_off // tt
    nt = n // tt
    return pl.pallas_call(
        _combine_kernel,
        grid=(nt,),
        in_specs=[
            pl.BlockSpec((TOP_K, tt), lambda i: (0, off + i), memory_space=pltpu.SMEM),
            pl.BlockSpec((TOP_K, tt), lambda i: (0, off + jnp.minimum(i + 1, nt - 1)), memory_space=pltpu.SMEM),
            pl.BlockSpec((tt, TOP_K), lambda i: (off + i, 0)),
            pl.BlockSpec((tt, d), lambda i: (i, 0)),
            pl.BlockSpec((1, d), lambda i: (0, 0)),
            pl.BlockSpec(memory_space=pl.ANY),
        ],
        out_specs=pl.BlockSpec((tt, d), lambda i: (i, 0)),
        out_shape=jax.ShapeDtypeStruct((n, d), F32),
        scratch_shapes=[pltpu.VMEM((2, TOP_K * tt // 8, 8, d // 2), jnp.uint32), pltpu.SemaphoreType.DMA((2,))],
        compiler_params=_params("arbitrary"),
        name="combine_rows",
    )(dest, dest, gates_t, x1, final_gain, ys)


def _rope_tables(seq_len):
    rows = seq_len // GRID_W
    row = jnp.repeat(jnp.arange(rows, dtype=F32), GRID_W)
    col = jnp.tile(jnp.arange(GRID_W, dtype=F32), rows)
    freqs = ROPE_THETA ** (-jnp.arange(ROPE_HALF, dtype=F32) / ROPE_HALF)
    ang_r = row[:, None] * freqs[None, :]
    ang_c = col[:, None] * freqs[None, :]
    cos = jnp.concatenate([jnp.cos(ang_r), jnp.cos(ang_r), jnp.cos(ang_c), jnp.cos(ang_c)], axis=1)
    sin = jnp.concatenate([-jnp.sin(ang_r), jnp.sin(ang_r), -jnp.sin(ang_c), jnp.sin(ang_c)], axis=1)
    return cos, sin


def kernel(x_prompt, x_sample, mix_norm, w_in, q_norm, k_norm, hg_lb_logits, hg_out_norm, w_up_attn,
           w_up_hgrn, w_out, ffn_norm, w_router, b_router, w_gate_up, b_gate_up, w_down, b_down, final_norm):
    assert mix_norm.shape[0] == 1, "single trunk layer"
    d = x_prompt.shape[-1]
    hg_w = HG_HEADS * HG_D
    n_exp = w_router.shape[-1]
    row_block = MOE_ROW_BLOCK
    streams = [(x.reshape(-1, d), x.shape[0], x.shape[1]) for x in (x_prompt, x_sample)]
    n_total = sum(x.shape[0] for x, _, _ in streams)

    lb = jnp.cumsum(jax.nn.softmax(hg_lb_logits.astype(F32), axis=1), axis=1)[:, 0]
    w_in_b = w_in[0].astype(BF16)
    w_ua_b = w_up_attn[0].astype(BF16)
    w_uh_b = w_up_hgrn[0].astype(BF16)
    w_out_b = w_out[0].astype(BF16)
    w_r_t = w_router[0].T
    mix_g = mix_norm[0].reshape(1, d)
    score_bound = (1.02 * LOG2_E * HEAD_DIM ** 0.5 * jnp.max(jnp.abs(q_norm[0])) * jnp.max(jnp.abs(k_norm[0])))
    score_bound = score_bound.astype(F32).reshape(1)

    w_gu2d = w_gate_up[0].reshape(-1, w_gate_up.shape[-1])
    cast_host = max(range(len(streams)), key=lambda s: streams[s][0].shape[0])
    w_gu_b = None

    x1s, idxs, gate_ts = [], [], []
    hp = None
    row_off = 0
    for s, (x, batch, seq) in enumerate(streams):
        cos, sin = _rope_tables(seq)
        qkv, zh, gates = _norm_proj(x, mix_g, w_in_b, q_norm[0].reshape(1, HEAD_DIM),
                                    k_norm[0].reshape(1, HEAD_DIM), cos, sin, seq)
        if s == cast_host:
            attn, w_gu_b = _attention(qkv, score_bound, batch, seq, w_gu2d)
        else:
            attn, = _attention(qkv, score_bound, batch, seq)
        hg = _hgrn(zh, lb[0:1], lb[1:2], hg_out_norm[0].reshape(1, hg_w), batch, seq)
        merged = _merge(attn, hg, gates, w_ua_b, w_uh_b)
        x1, hp, idx, gate = _outproj_router(x, merged, w_out_b, ffn_norm[0].reshape(1, d), w_r_t,
                                            b_router[0].reshape(n_exp, 1), hp, row_off, n_total)
        x1s.append(x1)
        idxs.append(idx)
        gate_ts.append(gate.T)
        row_off += x.shape[0]

    idx = jnp.concatenate(idxs, axis=1)
    gate_t = jnp.concatenate(gate_ts, axis=0)
    dest, counts = _route(idx, row_block)
    cnt = counts[:, 0].astype(jnp.int32)
    padded = (cnt + row_block - 1) // row_block * row_block
    pad_end = jnp.cumsum(padded)
    n_rows = n_total * TOP_K + n_exp * row_block
    n_blocks = n_rows // row_block
    blk_start = jnp.arange(n_blocks, dtype=jnp.int32) * row_block
    blk_e = jnp.minimum(jnp.sum(pad_end[None, :] <= blk_start[:, None], axis=1), n_exp - 1).astype(jnp.int32)
    seg_start = pad_end - padded
    blk_valid = jnp.clip(cnt[blk_e] - (blk_start - seg_start[blk_e]), 0, row_block)
    blk_valid = jnp.where(blk_start < pad_end[-1], blk_valid, 0).astype(jnp.int32)
    meta = jnp.concatenate([(pad_end[-1:] // row_block).astype(jnp.int32), blk_e, blk_valid])
    seg = jnp.stack([seg_start, pad_end]).astype(jnp.int32)

    xs = _dispatch(seg, dest, hp, n_rows, row_block)
    ys = _experts(meta, xs, w_gu_b.reshape(w_gate_up.shape[1:]), b_gate_up[0].reshape(n_exp, 1, -1),
                  w_down[0], b_down[0].reshape(n_exp, 1, d), row_block)

    outs = []
    row_off = 0
    for (x, batch, seq), x1 in zip(streams, x1s):
        out = _combine(dest, gate_t, x1, final_norm.reshape(1, d), ys, row_off)
        outs.append(out.reshape(batch, seq, d))
        row_off += x.shape[0]
    return tuple(outs)
```

```python
import functools

import jax
import jax.numpy as jnp
from jax import lax
from jax.experimental import pallas as pl
from jax.experimental.pallas import tpu as pltpu

GRID_W = 64
HEAD_DIM = 128
N_Q_HEADS = 16
N_KV_HEADS = 4
ROPE_THETA = 10000.0
ROPE_HALF = HEAD_DIM // 4
HG_HEADS = 8
HG_D = 128
HG_CHUNK = 64
N_EXPERTS = 32
TOP_K = 4
SWIGLU_LIMIT = 7.0
SWIGLU_ALPHA = 1.702
NORM_EPS = 1e-5

HG_SAFE_LOG_DECAY = -60.0

LOG2_E = 1.4426950408889634
ATTN_FIXED_SHIFT_LIMIT = 60.0

V7X_VMEM_BYTES = 64 * 1024 * 1024
VMEM_LIMIT_BYTES = V7X_VMEM_BYTES - 8 * 1024 * 1024
LANES = 128
MOE_ROW_BLOCK = 512
DMA_PRIORITIES = 2

BF16 = jnp.bfloat16
F32 = jnp.float32


def _params(*sem):
    return pltpu.CompilerParams(dimension_semantics=sem, vmem_limit_bytes=VMEM_LIMIT_BYTES)


def _tile(n, pref):
    t = min(n, pref)
    while n % t:
        t //= 2
    return t


def _sigmoid(x):
    return 1.0 / (1.0 + jnp.exp(-x))


def _rmsnorm(x, g):
    return x * lax.rsqrt(jnp.mean(x * x, axis=-1, keepdims=True) + NORM_EPS) * g


def _dot(a, b):
    return jnp.dot(a, b, preferred_element_type=F32)


def _dot_nt(a, b):
    return lax.dot_general(a, b, (((1,), (1,)), ((), ())), preferred_element_type=F32)


def _split3(x):
    hi = x.astype(BF16)
    r = x - hi.astype(F32)
    mid = r.astype(BF16)
    lo = (r - mid.astype(F32)).astype(BF16)
    return hi, mid, lo


def _dot_exact_lhs(m_bf16, x, pieces=3):
    parts = _split3(x)[:pieces]
    acc = _dot(m_bf16, parts[0])
    for p in parts[1:]:
        acc = acc + _dot(m_bf16, p)
    return acc


def _rope_head_pair(zp, gain, cos, sin, ones_blk, perm_blk, scale):
    ms = _dot((zp * zp).astype(BF16), ones_blk)
    y = zp * lax.rsqrt(ms + NORM_EPS) * (gain * scale)
    y_hi = y.astype(BF16)
    y_lo = (y - y_hi.astype(F32)).astype(BF16)
    partner = _dot(y_hi, perm_blk) + _dot(y_lo, perm_blk)
    return y * cos + partner * sin


def _proj_kernel(x_ref, g_ref, w_ref, qn_ref, kn_ref, cos_ref, sin_ref, ones_ref, perm_ref,
                 qkv_ref, zh_ref, gate_ref, h_scr, *, tile_kinds):
    j = pl.program_id(1)

    @pl.when(j == 0)
    def _():
        h_scr[...] = _rmsnorm(x_ref[...], g_ref[...]).astype(BF16)

    z = _dot(h_scr[...], w_ref[...])

    def qkv_tile(kinds):
        two = lambda r: jnp.concatenate([r[...], r[...]], axis=1)
        outs = []
        for h in range(0, len(kinds), 2):
            kind = kinds[h]
            assert kinds[h + 1] == kind
            zp = z[:, h * HEAD_DIM:(h + 2) * HEAD_DIM]
            if kind == "q":
                zp = _rope_head_pair(zp, two(qn_ref), two(cos_ref), two(sin_ref), ones_ref[...], perm_ref[...],
                                     LOG2_E * HEAD_DIM ** -0.5)
            elif kind == "k":
                zp = _rope_head_pair(zp, two(kn_ref), two(cos_ref), two(sin_ref), ones_ref[...], perm_ref[...], 1.0)
            outs.append(zp)
        return jnp.concatenate(outs, axis=1)

    for lo, hi, kind in tile_kinds:
        @pl.when((j >= lo) & (j < hi))
        def _(kind=kind):
            if kind == "zh":
                zh_ref[...] = z
            elif kind == "gate":
                gate_ref[...] = z.astype(gate_ref.dtype)
            else:
                qkv_ref[...] = qkv_tile(kind).astype(qkv_ref.dtype)


def _norm_proj(x, gain, w, q_gain, k_gain, cos, sin, seq):
    n, d = x.shape
    attn_w = N_Q_HEADS * HEAD_DIM
    kv_w = N_KV_HEADS * HEAD_DIM
    hg5 = 5 * HG_HEADS * HG_D
    qkv_w = attn_w + 2 * kv_w
    assert w.shape[1] == qkv_w + hg5 + 2 * d
    tm = _tile(seq, 1024)
    tn = 1024
    while attn_w % tn or (2 * kv_w) % tn or hg5 % tn or (2 * d) % tn:
        tn //= 2
    assert tn % HEAD_DIM == 0
    heads = ["q"] * N_Q_HEADS + ["k"] * N_KV_HEADS + ["v"] * N_KV_HEADS
    hpt = tn // HEAD_DIM
    n_qkv, n_zh, n_gate = qkv_w // tn, hg5 // tn, 2 * d // tn
    tile_kinds = []
    for t in range(n_qkv):
        kind = tuple(heads[t * hpt:(t + 1) * hpt])
        if tile_kinds and tile_kinds[-1][2] == kind:
            tile_kinds[-1] = (tile_kinds[-1][0], t + 1, kind)
        else:
            tile_kinds.append((t, t + 1, kind))
    tile_kinds += [(n_qkv, n_qkv + n_zh, "zh"), (n_qkv + n_zh, n_qkv + n_zh + n_gate, "gate")]
    n_pos = seq // tm
    r = jnp.arange(2 * HEAD_DIM)
    same_head = (r[:, None] // HEAD_DIM) == (r[None, :] // HEAD_DIM)
    ones_blk = (same_head.astype(F32) / HEAD_DIM).astype(BF16)
    partner_of = jnp.where((r % (2 * ROPE_HALF)) < ROPE_HALF, r + ROPE_HALF, r - ROPE_HALF)
    perm_blk = (r[:, None] == partner_of[None, :]).astype(BF16)
    const_spec = pl.BlockSpec((2 * HEAD_DIM, 2 * HEAD_DIM), lambda i, j: (0, 0))
    return pl.pallas_call(
        functools.partial(_proj_kernel, tile_kinds=tile_kinds),
        grid=(n // tm, n_qkv + n_zh + n_gate),
        in_specs=[
            pl.BlockSpec((tm, d), lambda i, j: (i, 0)),
            pl.BlockSpec((1, d), lambda i, j: (0, 0)),
            pl.BlockSpec((d, tn), lambda i, j: (0, j)),
            pl.BlockSpec((1, HEAD_DIM), lambda i, j: (0, 0)),
            pl.BlockSpec((1, HEAD_DIM), lambda i, j: (0, 0)),
            pl.BlockSpec((tm, HEAD_DIM), lambda i, j: (i % n_pos, 0)),
            pl.BlockSpec((tm, HEAD_DIM), lambda i, j: (i % n_pos, 0)),
            const_spec,
            const_spec,
        ],
        out_specs=[
            pl.BlockSpec((tm, tn), lambda i, j: (i, jnp.minimum(j, n_qkv - 1))),
            pl.BlockSpec((tm, tn), lambda i, j: (i, jnp.clip(j - n_qkv, 0, n_zh - 1))),
            pl.BlockSpec((tm, tn), lambda i, j: (i, jnp.clip(j - n_qkv - n_zh, 0, n_gate - 1))),
        ],
        out_shape=[
            jax.ShapeDtypeStruct((n, qkv_w), BF16),
            jax.ShapeDtypeStruct((n, hg5), F32),
            jax.ShapeDtypeStruct((n, 2 * d), BF16),
        ],
        scratch_shapes=[pltpu.VMEM((tm, d), BF16)],
        compiler_params=_params("parallel", "arbitrary"),
        name="norm_proj",
    )(x, gain, w, q_gain, k_gain, cos, sin, ones_blk, perm_blk)


def _attn_kernel(bound_ref, q_ref, k_ref, v_ref, *rest, tk, group):
    if len(rest) == 5:
        wf_ref, o_ref, wb_ref, acc_scr, m_scr = rest
    else:
        wf_ref = wb_ref = None
        o_ref, acc_scr, m_scr = rest
    tq = q_ref.shape[0]
    seq = k_ref.shape[0]
    n_chunks = seq // tk
    slab = None if wf_ref is None else wf_ref.shape[0] // n_chunks
    q = jnp.concatenate([q_ref[:, g * HEAD_DIM:(g + 1) * HEAD_DIM] for g in range(group)], axis=0)
    ones = jnp.ones((tk, HEAD_DIM), BF16)
    bound = bound_ref[0]
    acc_scr[...] = jnp.zeros(acc_scr.shape, F32)

    def chunk(c):
        rows = pl.ds(pl.multiple_of(c * tk, tk), tk)
        s = _dot_nt(q, k_ref[rows, :])
        return s, jnp.concatenate([v_ref[rows, :], ones], axis=1)

    @pl.when(bound <= ATTN_FIXED_SHIFT_LIMIT)
    def _():
        def body(c, carry):
            s, v1 = chunk(c)
            acc_scr[...] += _dot(jnp.exp2(s - bound).astype(BF16), v1)
            if wf_ref is not None:
                wrows = pl.ds(pl.multiple_of(c * slab, slab), slab)
                wb_ref[wrows, :] = wf_ref[wrows, :].astype(BF16)
            return carry
        lax.fori_loop(0, n_chunks, body, 0, unroll=True)

    @pl.when(bound > ATTN_FIXED_SHIFT_LIMIT)
    def _():
        if wf_ref is not None:
            wb_ref[...] = wf_ref[...].astype(BF16)
        m_scr[...] = jnp.full(m_scr.shape, -1e30, F32)

        def body(c, carry):
            s, v1 = chunk(c)
            m_prev = m_scr[...]
            m_new = jnp.maximum(m_prev, jnp.max(s, axis=1, keepdims=True))
            alpha = jnp.exp2(m_prev - m_new)
            p = jnp.exp2(s - jnp.tile(m_new, (1, tk // LANES)))
            acc_scr[...] = jnp.tile(alpha, (1, 2)) * acc_scr[...] + _dot(p.astype(BF16), v1)
            m_scr[...] = m_new
            return carry
        lax.fori_loop(0, seq // tk, body, 0)

    acc = acc_scr[...]
    o = acc[:, :HEAD_DIM] / acc[:, HEAD_DIM:]
    for g in range(group):
        o_ref[:, g * HEAD_DIM:(g + 1) * HEAD_DIM] = o[g * tq:(g + 1) * tq].astype(o_ref.dtype)


def _attention(qkv, score_bound, batch, seq, w_f32=None):
    group = N_Q_HEADS // N_KV_HEADS
    gw = group * HEAD_DIM
    tq = _tile(seq, 512)
    tk = _tile(seq, 512)
    nq = seq // tq
    in_specs = [
        pl.BlockSpec(memory_space=pltpu.SMEM),
        pl.BlockSpec((tq, gw), lambda b, n, i: (b * nq + i, n)),
        pl.BlockSpec((seq, HEAD_DIM), lambda b, n, i: (b, N_Q_HEADS + n)),
        pl.BlockSpec((seq, HEAD_DIM), lambda b, n, i: (b, N_Q_HEADS + N_KV_HEADS + n)),
    ]
    out_specs = [pl.BlockSpec((tq, gw), lambda b, n, i: (b * nq + i, n))]
    out_shape = [jax.ShapeDtypeStruct((batch * seq, N_Q_HEADS * HEAD_DIM), BF16)]
    args = [score_bound, qkv, qkv, qkv]
    if w_f32 is not None:
        n_steps = batch * N_KV_HEADS * nq
        w_rows, w_cols = w_f32.shape
        slab = w_rows // n_steps
        assert slab * n_steps == w_rows and slab % (16 * (seq // tk)) == 0, (w_rows, n_steps, seq // tk)
        wspec = pl.BlockSpec((slab, w_cols), lambda b, n, i: ((b * N_KV_HEADS + n) * nq + i, 0))
        in_specs.append(wspec)
        out_specs.append(wspec)
        out_shape.append(jax.ShapeDtypeStruct((w_rows, w_cols), BF16))
        args.append(w_f32)
    return pl.pallas_call(
        functools.partial(_attn_kernel, tk=tk, group=group),
        grid=(batch, N_KV_HEADS, nq),
        in_specs=in_specs,
        out_specs=out_specs,
        out_shape=out_shape,
        scratch_shapes=[
            pltpu.VMEM((group * tq, 2 * HEAD_DIM), F32),
            pltpu.VMEM((group * tq, LANES), F32),
        ],
        compiler_params=_params("parallel", "parallel", "arbitrary"),
        name="gqa_attention",
    )(*args)


def _hgrn_span(i, q_scr, b_scr, k_scr, zi_ref, vt_scr, o_scr, s_scr, tri, *, span, reverse, pairwise):
    C = HG_CHUNK
    nc = span // C
    r0 = pl.multiple_of(i * span, span)
    rows = pl.ds(r0, span)
    q = q_scr[rows, :]
    b = b_scr[rows, :]
    k = k_scr[rows, :]
    v = zi_ref[rows, :]
    vb = v.astype(BF16)
    end_row = 0 if reverse else C - 1
    ends = [b[c * C + end_row:c * C + end_row + 1, :] for c in range(nc)]
    b_end = jnp.concatenate([jnp.broadcast_to(e, (C, HG_D)) for e in ends], axis=0)
    qtb = (q * jnp.exp(b)).astype(BF16)
    kp = k * jnp.exp(b_end - b)

    if pairwise:
        rid = lax.broadcasted_iota(jnp.int32, (span, HG_D), 0)

        def pair(s, acc):
            cs = (s // C) * C
            if reverse:
                m = (rid <= s) & (rid >= cs)
            else:
                m = (rid >= s) & (rid < cs + C)
            w = jnp.where(m, jnp.exp(jnp.minimum(b - b_scr[pl.ds(r0 + s, 1), :], 0.0)), 0.0)
            r = jnp.sum(q * w * k_scr[pl.ds(r0 + s, 1), :], axis=1, keepdims=True)
            return acc + r * zi_ref[pl.ds(r0 + s, 1), :]

        o_intra = lax.fori_loop(0, span, pair, jnp.zeros((span, HG_D), F32))
    else:
        ktb = (k * jnp.exp(-b)).astype(BF16)
        a = jnp.where(tri, _dot_nt(qtb, ktb), 0.0)
        o_intra = _dot(a.astype(BF16), vb)

    chunk_of_row = lax.broadcasted_iota(jnp.int32, (span, HG_D), 0) // C
    kp_blocks = jnp.concatenate([jnp.where(chunk_of_row == c, kp, 0.0) for c in range(nc)], axis=1)
    upd = _dot(vt_scr[:, rows], kp_blocks.astype(BF16))

    st = s_scr[...]
    states = [None] * nc
    for c in (reversed(range(nc)) if reverse else range(nc)):
        states[c] = st.astype(BF16)
        st = st * jnp.exp(ends[c]) + upd[:, c * HG_D:(c + 1) * HG_D]
    s_scr[...] = st
    o_inter = _dot_nt(qtb, jnp.concatenate(states, axis=0))
    o_scr[rows, :] = o_intra + jnp.concatenate(
        [o_inter[c * C:(c + 1) * C, c * HG_D:(c + 1) * HG_D] for c in range(nc)], axis=0)


def _hgrn_kernel(zq_ref, zf_ref, zb_ref, zi_ref, zo_ref, lbf_ref, lbb_ref, on_ref, out_ref,
                 of_scr, ob_scr, sf_scr, sb_scr, q_scr, bf_scr, bb_scr, kf_scr, kb_scr, vt_scr, *, span):
    seq = zq_ref.shape[0]
    n_span = seq // span
    C = HG_CHUNK
    nc = span // C
    sf_scr[...] = jnp.zeros(sf_scr.shape, F32)
    sb_scr[...] = jnp.zeros(sb_scr.shape, F32)
    r = lax.broadcasted_iota(jnp.int32, (span, span), 0)
    c = lax.broadcasted_iota(jnp.int32, (span, span), 1)
    same = (r // C) == (c // C)
    tri_f = same & (c <= r)
    tri_b = same & (c >= r)
    tri_f16 = tri_f.astype(BF16)
    tri_b16 = tri_b.astype(BF16)

    def prepare(i, min_end):
        rows = pl.ds(pl.multiple_of(i * span, span), span)
        qh = zq_ref[rows, :]
        q_scr[rows, :] = qh * _sigmoid(qh)
        vt_scr[:, rows] = zi_ref[rows, :].T.astype(BF16)
        for z_ref, lb_ref, tri, b_scr, k_scr, end_row in (
                (zf_ref, lbf_ref, tri_f16, bf_scr, kf_scr, C - 1), (zb_ref, lbb_ref, tri_b16, bb_scr, kb_scr, 0)):
            lb = lb_ref[...]
            f = lb + (1.0 - lb) * _sigmoid(z_ref[rows, :])
            b = _dot_exact_lhs(tri, jnp.log(f), pieces=2)
            b_scr[rows, :] = b
            k_scr[rows, :] = 1.0 - f
            for cc in range(nc):
                min_end = jnp.minimum(min_end, b[cc * C + end_row:cc * C + end_row + 1, :])
        return min_end

    min_end = lax.fori_loop(0, n_span, prepare, jnp.zeros((1, HG_D), F32), unroll=2)
    safe = jnp.min(min_end) > HG_SAFE_LOG_DECAY

    def scan(pairwise):
        def body(i, carry):
            _hgrn_span(i, q_scr, bf_scr, kf_scr, zi_ref, vt_scr, of_scr, sf_scr, tri_f,
                       span=span, reverse=False, pairwise=pairwise)
            _hgrn_span(n_span - 1 - i, q_scr, bb_scr, kb_scr, zi_ref, vt_scr, ob_scr, sb_scr, tri_b,
                       span=span, reverse=True, pairwise=pairwise)
            return carry
        lax.fori_loop(0, n_span, body, 0, unroll=1 if pairwise else 4)

    @pl.when(safe)
    def _():
        scan(False)

    @pl.when(jnp.logical_not(safe))
    def _():
        scan(True)

    def finish(i, carry):
        rows = pl.ds(pl.multiple_of(i * span, span), span)
        o = _rmsnorm(of_scr[rows, :] + ob_scr[rows, :], on_ref[...])
        og = zo_ref[rows, :]
        out_ref[rows, :] = (o * (og * _sigmoid(og))).astype(out_ref.dtype)
        return carry

    lax.fori_loop(0, n_span, finish, 0)


def _hgrn(zh, lb_f, lb_b, out_norm, batch, seq):
    span = _tile(seq, 256)
    assert span % HG_CHUNK == 0
    zspec = lambda grp: pl.BlockSpec((seq, HG_D), lambda b, h: (b, grp * HG_HEADS + h))
    hspec = pl.BlockSpec((1, HG_D), lambda b, h: (0, h))
    seq_buf = pltpu.VMEM((seq, HG_D), F32)
    return pl.pallas_call(
        functools.partial(_hgrn_kernel, span=span),
        grid=(batch, HG_HEADS),
        in_specs=[zspec(0), zspec(1), zspec(2), zspec(3), zspec(4), hspec, hspec, hspec],
        out_specs=pl.BlockSpec((seq, HG_D), lambda b, h: (b, h)),
        out_shape=jax.ShapeDtypeStruct((batch * seq, HG_HEADS * HG_D), BF16),
        scratch_shapes=[
            seq_buf, seq_buf,
            pltpu.VMEM((HG_D, HG_D), F32), pltpu.VMEM((HG_D, HG_D), F32),
            seq_buf, seq_buf, seq_buf, seq_buf, seq_buf,
            pltpu.VMEM((HG_D, seq), BF16),
        ],
        compiler_params=_params("parallel", "parallel"),
        name="hgrn2",
    )(zh, zh, zh, zh, zh, lb_f, lb_b, out_norm)


def _merge_kernel(a_ref, h_ref, ga_ref, gb_ref, wa_ref, wh_ref, o_ref):
    ya = _dot(a_ref[...], wa_ref[...])
    yb = _dot(h_ref[...], wh_ref[...])
    merged = _sigmoid(ga_ref[...].astype(F32)) * ya + _sigmoid(gb_ref[...].astype(F32)) * yb
    o_ref[...] = merged.astype(o_ref.dtype)


def _merge(attn, hg, gates, w_up_attn, w_up_hgrn):
    n, wa = attn.shape
    wh = hg.shape[1]
    d = w_up_attn.shape[1]
    tm = _tile(n, 1024)
    tn = _tile(d, 512)
    nj = d // tn
    return pl.pallas_call(
        _merge_kernel,
        grid=(n // tm, nj),
        in_specs=[
            pl.BlockSpec((tm, wa), lambda i, j: (i, 0)),
            pl.BlockSpec((tm, wh), lambda i, j: (i, 0)),
            pl.BlockSpec((tm, tn), lambda i, j: (i, j)),
            pl.BlockSpec((tm, tn), lambda i, j: (i, nj + j)),
            pl.BlockSpec((wa, tn), lambda i, j: (0, j)),
            pl.BlockSpec((wh, tn), lambda i, j: (0, j)),
        ],
        out_specs=pl.BlockSpec((tm, tn), lambda i, j: (i, j)),
        out_shape=jax.ShapeDtypeStruct((n, d), BF16),
        compiler_params=_params("parallel", "arbitrary"),
        name="gated_merge",
    )(attn, hg, gates, gates, w_up_attn, w_up_hgrn)


def _pack_bf16_pairs(h):
    half = h.shape[1] // 2
    lo = pltpu.bitcast(h[:, :half].astype(BF16).astype(F32), jnp.uint32)
    hi = pltpu.bitcast(h[:, half:].astype(BF16).astype(F32), jnp.uint32)
    return (hi & jnp.uint32(0xFFFF0000)) | (lo >> 16)


def _unpack_bf16_pairs(u):
    lo = pltpu.bitcast(u << 16, F32).astype(BF16)
    hi = pltpu.bitcast(u & jnp.uint32(0xFFFF0000), F32).astype(BF16)
    return lo, hi


def _outproj_router_kernel(x_ref, m_ref, w_ref, g_ref, wr_ref, br_ref, hp_in_ref,
                           x1_ref, hp_ref, idx_ref, gate_ref):
    del hp_in_ref
    x1 = x_ref[...] + _dot(m_ref[...], w_ref[...])
    x1_ref[...] = x1
    h = _rmsnorm(x1, g_ref[...])
    hp_ref[...] = _pack_bf16_pairs(h)
    h_hi, h_mid, _ = _split3(h)
    w_hi, w_mid, _ = _split3(wr_ref[...])
    lg = (_dot_nt(w_hi, h_hi) + _dot_nt(w_hi, h_mid) + _dot_nt(w_mid, h_hi)) + br_ref[...]
    n_exp, tm = lg.shape
    eid = lax.broadcasted_iota(jnp.int32, (n_exp, tm), 0)
    vals = []
    for kk in range(TOP_K):
        m = jnp.max(lg, axis=0, keepdims=True)
        sel = jnp.min(jnp.where(lg == m, eid, n_exp), axis=0, keepdims=True)
        idx_ref[kk:kk + 1, :] = sel
        vals.append(m)
        lg = jnp.where(eid == sel, -jnp.inf, lg)
    ex = [jnp.exp(vv - vals[0]) for vv in vals]
    den = ex[0]
    for e in ex[1:]:
        den = den + e
    for kk in range(TOP_K):
        gate_ref[kk:kk + 1, :] = ex[kk] / den


def _outproj_router(x, merged, w_out, ffn_gain, w_router_t, b_router, hp_prev, row_off, n_total):
    n, d = x.shape
    tm = _tile(n, 512)
    assert row_off % tm == 0
    n_exp = w_router_t.shape[0]
    if hp_prev is None:
        hp_prev = jnp.zeros((8, LANES), jnp.uint32)
        aliases = {}
    else:
        aliases = {6: 1}
    return pl.pallas_call(
        _outproj_router_kernel,
        grid=(n // tm,),
        in_specs=[
            pl.BlockSpec((tm, d), lambda i: (i, 0)),
            pl.BlockSpec((tm, d), lambda i: (i, 0)),
            pl.BlockSpec((d, d), lambda i: (0, 0)),
            pl.BlockSpec((1, d), lambda i: (0, 0)),
            pl.BlockSpec((n_exp, d), lambda i: (0, 0)),
            pl.BlockSpec((n_exp, 1), lambda i: (0, 0)),
            pl.BlockSpec(memory_space=pl.ANY),
        ],
        out_specs=[
            pl.BlockSpec((tm, d), lambda i: (i, 0)),
            pl.BlockSpec((tm, d // 2), lambda i: (row_off // tm + i, 0)),
            pl.BlockSpec((TOP_K, tm), lambda i: (0, i)),
            pl.BlockSpec((TOP_K, tm), lambda i: (0, i)),
        ],
        out_shape=[
            jax.ShapeDtypeStruct((n, d), F32),
            jax.ShapeDtypeStruct((n_total, d // 2), jnp.uint32),
            jax.ShapeDtypeStruct((TOP_K, n), jnp.int32),
            jax.ShapeDtypeStruct((TOP_K, n), F32),
        ],
        input_output_aliases=aliases,
        compiler_params=_params("parallel"),
        name="outproj_router",
    )(x, merged, w_out, ffn_gain, w_router_t, b_router, hp_prev)


def _route_kernel(idx_ref, dest_ref, cnt_ref, cnt_scr, base_scr, *, row_block):
    phase = pl.program_id(0)
    i = pl.program_id(1)
    n_exp = cnt_scr.shape[0]
    tt = idx_ref.shape[1]
    eid = lax.broadcasted_iota(jnp.int32, (n_exp, tt), 0)
    onehot = [(eid == idx_ref[kk:kk + 1, :]) for kk in range(TOP_K)]

    @pl.when((phase == 0) & (i == 0))
    def _():
        cnt_scr[...] = jnp.zeros(cnt_scr.shape, F32)

    @pl.when(phase == 0)
    def _():
        tot = onehot[0].astype(F32)
        for oh in onehot[1:]:
            tot = tot + oh.astype(F32)
        cnt_scr[...] = cnt_scr[...] + jnp.sum(tot, axis=1, keepdims=True)
        cnt_ref[...] = cnt_scr[...]

    @pl.when((phase == 1) & (i == 0))
    def _():
        cnt = cnt_scr[...].astype(jnp.int32)
        padded = ((cnt + (row_block - 1)) // row_block * row_block).astype(F32)
        er = lax.broadcasted_iota(jnp.int32, (n_exp, n_exp), 0)
        ec = lax.broadcasted_iota(jnp.int32, (n_exp, n_exp), 1)
        base_scr[...] = _dot_exact_lhs((ec < er).astype(BF16), padded)

    @pl.when(phase == 1)
    def _():
        tr = lax.broadcasted_iota(jnp.int32, (tt, tt), 0)
        tc = lax.broadcasted_iota(jnp.int32, (tt, tt), 1)
        before = (tr < tc).astype(BF16)
        run = base_scr[...][:, :1]
        for kk in range(TOP_K):
            oh = onehot[kk].astype(F32)
            rank = _dot(oh.astype(BF16), before) + run
            dest_ref[kk:kk + 1, :] = jnp.sum(oh * rank, axis=0, keepdims=True).astype(jnp.int32)
            run = run + jnp.sum(oh, axis=1, keepdims=True)
        base_scr[...] = jnp.broadcast_to(run, base_scr.shape)


def _route(idx, row_block):
    n = idx.shape[1]
    tt = _tile(n, 512)
    return pl.pallas_call(
        functools.partial(_route_kernel, row_block=row_block),
        grid=(2, n // tt),
        in_specs=[pl.BlockSpec((TOP_K, tt), lambda p, i: (0, i))],
        out_specs=[
            pl.BlockSpec((TOP_K, tt), lambda p, i: (0, i * p)),
            pl.BlockSpec((N_EXPERTS, LANES), lambda p, i: (0, 0)),
        ],
        out_shape=[
            jax.ShapeDtypeStruct((TOP_K, n), jnp.int32),
            jax.ShapeDtypeStruct((N_EXPERTS, LANES), F32),
        ],
        scratch_shapes=[pltpu.VMEM((N_EXPERTS, LANES), F32), pltpu.VMEM((N_EXPERTS, LANES), F32)],
        compiler_params=_params("arbitrary", "arbitrary"),
        name="route_offsets",
    )(idx)


def _dispatch_kernel(seg_ref, dest_ref, h_ref, xs_ref, zero_scr, sem, zsem, *, row_block):
    tt = dest_ref.shape[1]
    n_exp = seg_ref.shape[1]

    @pl.when(pl.program_id(0) == 0)
    def _():
        zero_scr[...] = jnp.zeros(zero_scr.shape, zero_scr.dtype)

        def zero_copy(e):
            start = pl.multiple_of(seg_ref[1, e] - row_block, row_block)
            return pltpu.make_async_copy(zero_scr, xs_ref.at[pl.ds(start, row_block)], zsem)

        for e in range(n_exp):
            @pl.when(seg_ref[1, e] > seg_ref[0, e])
            def _(e=e):
                zero_copy(e).start()
        for e in range(n_exp):
            @pl.when(seg_ref[1, e] > seg_ref[0, e])
            def _(e=e):
                zero_copy(e).wait()

    def start(g, carry):
        for u in range(8):
            for kk in range(TOP_K):
                pltpu.make_async_copy(h_ref.at[g, pl.ds(u, 1), :],
                                      xs_ref.at[pl.ds(dest_ref[kk, g * 8 + u], 1)],
                                      sem).start(priority=(u * TOP_K + kk) % DMA_PRIORITIES)
        return carry

    lax.fori_loop(0, tt // 8, start, 0)
    pltpu.make_async_copy(xs_ref.at[pl.ds(0, TOP_K * tt)], xs_ref.at[pl.ds(0, TOP_K * tt)], sem).wait()


def _dispatch(seg, dest, hp, n_rows, row_block):
    n, w = hp.shape
    tt = _tile(n, 1024)
    grid_spec = pltpu.PrefetchScalarGridSpec(
        num_scalar_prefetch=1,
        grid=(n // tt,),
        in_specs=[
            pl.BlockSpec((TOP_K, tt), lambda i, s: (0, i), memory_space=pltpu.SMEM),
            pl.BlockSpec((tt // 8, 8, w), lambda i, s: (i, 0, 0)),
        ],
        out_specs=pl.BlockSpec(memory_space=pl.ANY),
        scratch_shapes=[pltpu.VMEM((row_block, w), hp.dtype), pltpu.SemaphoreType.DMA(()),
                        pltpu.SemaphoreType.DMA(())],
    )
    return pl.pallas_call(
        functools.partial(_dispatch_kernel, row_block=row_block),
        grid_spec=grid_spec,
        out_shape=jax.ShapeDtypeStruct((n_rows, w), hp.dtype),
        compiler_params=_params("arbitrary"),
        name="dispatch_rows",
    )(seg, dest, hp.reshape(n // 8, 8, w))


def _expert_kernel(meta_ref, xs_ref, wg_ref, wl_ref, bg_ref, bl_ref, wd_ref, bd_ref, ys_ref,
                   x_scr, y_ref):
    i = pl.program_id(0)
    j = pl.program_id(1)
    n_blocks = pl.num_programs(0)
    row_block = xs_ref.shape[0]
    valid = meta_ref[1 + n_blocks + i]

    def block(m):
        rows = slice(0, m)

        @pl.when(j == 0)
        def _():
            half = xs_ref.shape[1]
            lo, hi = _unpack_bf16_pairs(xs_ref[rows, :])
            x_scr[rows, :half] = lo
            x_scr[rows, half:] = hi
            y_ref[rows, :] = jnp.broadcast_to(bd_ref[...], (m, y_ref.shape[1]))

        x = x_scr[rows, :]
        glu = _dot(x, wg_ref[...]) + bg_ref[...]
        lin = _dot(x, wl_ref[...]) + bl_ref[...]
        glu = jnp.minimum(glu, SWIGLU_LIMIT)
        lin = jnp.clip(lin, -SWIGLU_LIMIT, SWIGLU_LIMIT)
        act = glu * _sigmoid(SWIGLU_ALPHA * glu) * (lin + 1.0)
        y_ref[rows, :] = y_ref[rows, :] + _dot(act.astype(BF16), wd_ref[...].astype(BF16))

        @pl.when(j == pl.num_programs(1) - 1)
        def _():
            ys_ref[rows, :] = _pack_bf16_pairs(y_ref[rows, :])

    @pl.when(valid > row_block // 2)
    def _():
        block(row_block)

    @pl.when((valid > 0) & (valid <= row_block // 2))
    def _():
        block(row_block // 2)


def _experts(meta, xs, w_gu, b_gu, w_dn, b_dn, row_block):
    n_rows, half = xs.shape
    d = 2 * half
    d_ff = w_dn.shape[1]
    tf = _tile(d_ff, 1024)
    nf = d_ff // tf
    n_blocks = n_rows // row_block

    def jj(i, j, m):
        return jnp.where(i < m[0], j, nf - 1)

    grid_spec = pltpu.PrefetchScalarGridSpec(
        num_scalar_prefetch=1,
        grid=(n_blocks, nf),
        in_specs=[
            pl.BlockSpec((row_block, half), lambda i, j, m: (i, 0)),
            pl.BlockSpec((None, d, tf), lambda i, j, m: (m[1 + i], 0, jj(i, j, m))),
            pl.BlockSpec((None, d, tf), lambda i, j, m: (m[1 + i], 0, nf + jj(i, j, m))),
            pl.BlockSpec((None, 1, tf), lambda i, j, m: (m[1 + i], 0, jj(i, j, m))),
            pl.BlockSpec((None, 1, tf), lambda i, j, m: (m[1 + i], 0, nf + jj(i, j, m))),
            pl.BlockSpec((None, tf, d), lambda i, j, m: (m[1 + i], jj(i, j, m), 0)),
            pl.BlockSpec((None, 1, d), lambda i, j, m: (m[1 + i], 0, 0)),
        ],
        out_specs=pl.BlockSpec((row_block, half), lambda i, j, m: (i, 0)),
        scratch_shapes=[pltpu.VMEM((row_block, d), BF16), pltpu.VMEM((row_block, d), F32)],
    )
    return pl.pallas_call(
        _expert_kernel,
        grid_spec=grid_spec,
        out_shape=jax.ShapeDtypeStruct((n_rows, half), jnp.uint32),
        compiler_params=_params("arbitrary", "arbitrary"),
        name="expert_swiglu",
    )(meta, xs, w_gu, w_gu, b_gu, b_gu, w_dn, b_dn)


def _combine_kernel(dest_ref, dest_next_ref, gate_ref, x1_ref, fn_ref, ys_ref, o_ref, buf, sem):
    tt = dest_ref.shape[1]
    groups = tt // 8
    i = pl.program_id(0)
    slot = i % 2
    half = buf.shape[3]

    def start_group(d_ref, s, g):
        for u in range(8):
            for kk in range(TOP_K):
                pltpu.make_async_copy(ys_ref.at[pl.ds(d_ref[kk, g * 8 + u], 1)],
                                      buf.at[s, kk * groups + g, pl.ds(u, 1), :],
                                      sem.at[s]).start(priority=(u * TOP_K + kk) % DMA_PRIORITIES)

    def sum_group(g):
        rows = pl.ds(pl.multiple_of(g * 8, 8), 8)
        gates = gate_ref[rows, :]
        acc_lo = x1_ref[rows, :half]
        acc_hi = x1_ref[rows, half:]
        for kk in range(TOP_K):
            u = buf[slot, kk * groups + g]
            gk = gates[:, kk:kk + 1]
            acc_lo = acc_lo + pltpu.bitcast(u << 16, F32) * gk
            acc_hi = acc_hi + pltpu.bitcast(u & jnp.uint32(0xFFFF0000), F32) * gk
        ms = (jnp.sum(acc_lo * acc_lo, axis=-1, keepdims=True)
              + jnp.sum(acc_hi * acc_hi, axis=-1, keepdims=True)) / (2 * half)
        inv = lax.rsqrt(ms + NORM_EPS)
        o_ref[rows, :half] = acc_lo * inv * fn_ref[:, :half]
        o_ref[rows, half:] = acc_hi * inv * fn_ref[:, half:]

    @pl.when(i == 0)
    def _():
        def first(g, carry):
            start_group(dest_ref, 0, g)
            return carry
        lax.fori_loop(0, groups, first, 0)

    pltpu.make_async_copy(ys_ref.at[pl.ds(0, TOP_K * tt)], ys_ref.at[pl.ds(0, TOP_K * tt)], sem.at[slot]).wait()
    is_last = i + 1 == pl.num_programs(0)

    @pl.when(jnp.logical_not(is_last))
    def _():
        def both(g, carry):
            start_group(dest_next_ref, 1 - slot, g)
            sum_group(g)
            return carry
        lax.fori_loop(0, groups, both, 0, unroll=8)

    @pl.when(is_last)
    def _():
        def only_sum(g, carry):
            sum_group(g)
            return carry
        lax.fori_loop(0, groups, only_sum, 0, unroll=4)


def _combine(dest, gates_t, x1, final_gain, ys, row_off):
    n, d = x1.shape
    tt = _tile(n, 256)
    assert row_off % tt == 0
    off = row_off // tt
    nt = n // tt
    return pl.pallas_call(
        _combine_kernel,
        grid=(nt,),
        in_specs=[
            pl.BlockSpec((TOP_K, tt), lambda i: (0, off + i), memory_space=pltpu.SMEM),
            pl.BlockSpec((TOP_K, tt), lambda i: (0, off + jnp.minimum(i + 1, nt - 1)), memory_space=pltpu.SMEM),
            pl.BlockSpec((tt, TOP_K), lambda i: (off + i, 0)),
            pl.BlockSpec((tt, d), lambda i: (i, 0)),
            pl.BlockSpec((1, d), lambda i: (0, 0)),
            pl.BlockSpec(memory_space=pl.ANY),
        ],
        out_specs=pl.BlockSpec((tt, d), lambda i: (i, 0)),
        out_shape=jax.ShapeDtypeStruct((n, d), F32),
        scratch_shapes=[pltpu.VMEM((2, TOP_K * tt // 8, 8, d // 2), jnp.uint32), pltpu.SemaphoreType.DMA((2,))],
        compiler_params=_params("arbitrary"),
        name="combine_rows",
    )(dest, dest, gates_t, x1, final_gain, ys)


def _rope_tables(seq_len):
    rows = seq_len // GRID_W
    row = jnp.repeat(jnp.arange(rows, dtype=F32), GRID_W)
    col = jnp.tile(jnp.arange(GRID_W, dtype=F32), rows)
    freqs = ROPE_THETA ** (-jnp.arange(ROPE_HALF, dtype=F32) / ROPE_HALF)
    ang_r = row[:, None] * freqs[None, :]
    ang_c = col[:, None] * freqs[None, :]
    cos = jnp.concatenate([jnp.cos(ang_r), jnp.cos(ang_r), jnp.cos(ang_c), jnp.cos(ang_c)], axis=1)
    sin = jnp.concatenate([-jnp.sin(ang_r), jnp.sin(ang_r), -jnp.sin(ang_c), jnp.sin(ang_c)], axis=1)
    return cos, sin


def kernel(x_prompt, x_sample, mix_norm, w_in, q_norm, k_norm, hg_lb_logits, hg_out_norm, w_up_attn,
           w_up_hgrn, w_out, ffn_norm, w_router, b_router, w_gate_up, b_gate_up, w_down, b_down, final_norm):
    assert mix_norm.shape[0] == 1, "single trunk layer"
    d = x_prompt.shape[-1]
    hg_w = HG_HEADS * HG_D
    n_exp = w_router.shape[-1]
    row_block = MOE_ROW_BLOCK
    streams = [(x.reshape(-1, d), x.shape[0], x.shape[1]) for x in (x_prompt, x_sample)]
    n_total = sum(x.shape[0] for x, _, _ in streams)

    lb = jnp.cumsum(jax.nn.softmax(hg_lb_logits.astype(F32), axis=1), axis=1)[:, 0]
    w_in_b = w_in[0].astype(BF16)
    w_ua_b = w_up_attn[0].astype(BF16)
    w_uh_b = w_up_hgrn[0].astype(BF16)
    w_out_b = w_out[0].astype(BF16)
    w_r_t = w_router[0].T
    mix_g = mix_norm[0].reshape(1, d)
    score_bound = (1.02 * LOG2_E * HEAD_DIM ** 0.5 * jnp.max(jnp.abs(q_norm[0])) * jnp.max(jnp.abs(k_norm[0])))
    score_bound = score_bound.astype(F32).reshape(1)

    w_gu2d = w_gate_up[0].reshape(-1, w_gate_up.shape[-1])
    cast_host = max(range(len(streams)), key=lambda s: streams[s][0].shape[0])
    w_gu_b = None

    x1s, idxs, gate_ts = [], [], []
    hp = None
    row_off = 0
    for s, (x, batch, seq) in enumerate(streams):
        cos, sin = _rope_tables(seq)
        qkv, zh, gates = _norm_proj(x, mix_g, w_in_b, q_norm[0].reshape(1, HEAD_DIM),
                                    k_norm[0].reshape(1, HEAD_DIM), cos, sin, seq)
        if s == cast_host:
            attn, w_gu_b = _attention(qkv, score_bound, batch, seq, w_gu2d)
        else:
            attn, = _attention(qkv, score_bound, batch, seq)
        hg = _hgrn(zh, lb[0:1], lb[1:2], hg_out_norm[0].reshape(1, hg_w), batch, seq)
        merged = _merge(attn, hg, gates, w_ua_b, w_uh_b)
        x1, hp, idx, gate = _outproj_router(x, merged, w_out_b, ffn_norm[0].reshape(1, d), w_r_t,
                                            b_router[0].reshape(n_exp, 1), hp, row_off, n_total)
        x1s.append(x1)
        idxs.append(idx)
        gate_ts.append(gate.T)
        row_off += x.shape[0]

    idx = jnp.concatenate(idxs, axis=1)
    gate_t = jnp.concatenate(gate_ts, axis=0)
    dest, counts = _route(idx, row_block)
    cnt = counts[:, 0].astype(jnp.int32)
    padded = (cnt + row_block - 1) // row_block * row_block
    pad_end = jnp.cumsum(padded)
    n_rows = n_total * TOP_K + n_exp * row_block
    n_blocks = n_rows // row_block
    blk_start = jnp.arange(n_blocks, dtype=jnp.int32) * row_block
    blk_e = jnp.minimum(jnp.sum(pad_end[None, :] <= blk_start[:, None], axis=1), n_exp - 1).astype(jnp.int32)
    seg_start = pad_end - padded
    blk_valid = jnp.clip(cnt[blk_e] - (blk_start - seg_start[blk_e]), 0, row_block)
    blk_valid = jnp.where(blk_start < pad_end[-1], blk_valid, 0).astype(jnp.int32)
    meta = jnp.concatenate([(pad_end[-1:] // row_block).astype(jnp.int32), blk_e, blk_valid])
    seg = jnp.stack([seg_start, pad_end]).astype(jnp.int32)

    xs = _dispatch(seg, dest, hp, n_rows, row_block)
    ys = _experts(meta, xs, w_gu_b.reshape(w_gate_up.shape[1:]), b_gate_up[0].reshape(n_exp, 1, -1),
                  w_down[0], b_down[0].reshape(n_exp, 1, d), row_block)

    outs = []
    row_off = 0
    for (x, batch, seq), x1 in zip(streams, x1s):
        out = _combine(dest, gate_t, x1, final_norm.reshape(1, d), ys, row_off)
        outs.append(out.reshape(batch, seq, d))
        row_off += x.shape[0]
    return tuple(outs)
```

```python
import functools

import jax
import jax.numpy as jnp
from jax import lax
from jax.experimental import pallas as pl
from jax.experimental.pallas import tpu as pltpu

GRID_W = 64
HEAD_DIM = 128
N_Q_HEADS = 16
N_KV_HEADS = 4
ROPE_THETA = 10000.0
ROPE_HALF = HEAD_DIM // 4
HG_HEADS = 8
HG_D = 128
HG_CHUNK = 64
N_EXPERTS = 32
TOP_K = 4
SWIGLU_LIMIT = 7.0
SWIGLU_ALPHA = 1.702
NORM_EPS = 1e-5

HG_SAFE_LOG_DECAY = -60.0

LOG2_E = 1.4426950408889634
ATTN_FIXED_SHIFT_LIMIT = 60.0

V7X_VMEM_BYTES = 64 * 1024 * 1024
VMEM_LIMIT_BYTES = V7X_VMEM_BYTES - 8 * 1024 * 1024
LANES = 128
MOE_ROW_BLOCK = 512
DMA_PRIORITIES = 2

BF16 = jnp.bfloat16
F32 = jnp.float32


def _params(*sem):
    return pltpu.CompilerParams(dimension_semantics=sem, vmem_limit_bytes=VMEM_LIMIT_BYTES)


def _tile(n, pref):
    t = min(n, pref)
    while n % t:
        t //= 2
    return t


def _sigmoid(x):
    return 1.0 / (1.0 + jnp.exp(-x))


def _rmsnorm(x, g):
    return x * lax.rsqrt(jnp.mean(x * x, axis=-1, keepdims=True) + NORM_EPS) * g


def _dot(a, b):
    return jnp.dot(a, b, preferred_element_type=F32)


def _dot_nt(a, b):
    return lax.dot_general(a, b, (((1,), (1,)), ((), ())), preferred_element_type=F32)


def _split3(x):
    hi = x.astype(BF16)
    r = x - hi.astype(F32)
    mid = r.astype(BF16)
    lo = (r - mid.astype(F32)).astype(BF16)
    return hi, mid, lo


def _dot_exact_lhs(m_bf16, x, pieces=3):
    parts = _split3(x)[:pieces]
    acc = _dot(m_bf16, parts[0])
    for p in parts[1:]:
        acc = acc + _dot(m_bf16, p)
    return acc


def _rope_head_pair(zp, gain, cos, sin, ones_blk, perm_blk, scale):
    ms = _dot((zp * zp).astype(BF16), ones_blk)
    y = zp * lax.rsqrt(ms + NORM_EPS) * (gain * scale)
    y_hi = y.astype(BF16)
    y_lo = (y - y_hi.astype(F32)).astype(BF16)
    partner = _dot(y_hi, perm_blk) + _dot(y_lo, perm_blk)
    return y * cos + partner * sin


def _proj_kernel(x_ref, g_ref, w_ref, qn_ref, kn_ref, cos_ref, sin_ref, ones_ref, perm_ref,
                 qkv_ref, zh_ref, gate_ref, h_scr, *, tile_kinds):
    j = pl.program_id(1)

    @pl.when(j == 0)
    def _():
        h_scr[...] = _rmsnorm(x_ref[...], g_ref[...]).astype(BF16)

    z = _dot(h_scr[...], w_ref[...])

    def qkv_tile(kinds):
        two = lambda r: jnp.concatenate([r[...], r[...]], axis=1)
        outs = []
        for h in range(0, len(kinds), 2):
            kind = kinds[h]
            assert kinds[h + 1] == kind
            zp = z[:, h * HEAD_DIM:(h + 2) * HEAD_DIM]
            if kind == "q":
                zp = _rope_head_pair(zp, two(qn_ref), two(cos_ref), two(sin_ref), ones_ref[...], perm_ref[...],
                                     LOG2_E * HEAD_DIM ** -0.5)
            elif kind == "k":
                zp = _rope_head_pair(zp, two(kn_ref), two(cos_ref), two(sin_ref), ones_ref[...], perm_ref[...], 1.0)
            outs.append(zp)
        return jnp.concatenate(outs, axis=1)

    for lo, hi, kind in tile_kinds:
        @pl.when((j >= lo) & (j < hi))
        def _(kind=kind):
            if kind == "zh":
                zh_ref[...] = z
            elif kind == "gate":
                gate_ref[...] = z.astype(gate_ref.dtype)
            else:
                qkv_ref[...] = qkv_tile(kind).astype(qkv_ref.dtype)


def _norm_proj(x, gain, w, q_gain, k_gain, cos, sin, seq):
    n, d = x.shape
    attn_w = N_Q_HEADS * HEAD_DIM
    kv_w = N_KV_HEADS * HEAD_DIM
    hg5 = 5 * HG_HEADS * HG_D
    qkv_w = attn_w + 2 * kv_w
    assert w.shape[1] == qkv_w + hg5 + 2 * d
    tm = _tile(seq, 1024)
    tn = 1024
    while attn_w % tn or (2 * kv_w) % tn or hg5 % tn or (2 * d) % tn:
        tn //= 2
    assert tn % HEAD_DIM == 0
    heads = ["q"] * N_Q_HEADS + ["k"] * N_KV_HEADS + ["v"] * N_KV_HEADS
    hpt = tn // HEAD_DIM
    n_qkv, n_zh, n_gate = qkv_w // tn, hg5 // tn, 2 * d // tn
    tile_kinds = []
    for t in range(n_qkv):
        kind = tuple(heads[t * hpt:(t + 1) * hpt])
        if tile_kinds and tile_kinds[-1][2] == kind:
            tile_kinds[-1] = (tile_kinds[-1][0], t + 1, kind)
        else:
            tile_kinds.append((t, t + 1, kind))
    tile_kinds += [(n_qkv, n_qkv + n_zh, "zh"), (n_qkv + n_zh, n_qkv + n_zh + n_gate, "gate")]
    n_pos = seq // tm
    r = jnp.arange(2 * HEAD_DIM)
    same_head = (r[:, None] // HEAD_DIM) == (r[None, :] // HEAD_DIM)
    ones_blk = (same_head.astype(F32) / HEAD_DIM).astype(BF16)
    partner_of = jnp.where((r % (2 * ROPE_HALF)) < ROPE_HALF, r + ROPE_HALF, r - ROPE_HALF)
    perm_blk = (r[:, None] == partner_of[None, :]).astype(BF16)
    const_spec = pl.BlockSpec((2 * HEAD_DIM, 2 * HEAD_DIM), lambda i, j: (0, 0))
    return pl.pallas_call(
        functools.partial(_proj_kernel, tile_kinds=tile_kinds),
        grid=(n // tm, n_qkv + n_zh + n_gate),
        in_specs=[
            pl.BlockSpec((tm, d), lambda i, j: (i, 0)),
            pl.BlockSpec((1, d), lambda i, j: (0, 0)),
            pl.BlockSpec((d, tn), lambda i, j: (0, j)),
            pl.BlockSpec((1, HEAD_DIM), lambda i, j: (0, 0)),
            pl.BlockSpec((1, HEAD_DIM), lambda i, j: (0, 0)),
            pl.BlockSpec((tm, HEAD_DIM), lambda i, j: (i % n_pos, 0)),
            pl.BlockSpec((tm, HEAD_DIM), lambda i, j: (i % n_pos, 0)),
            const_spec,
            const_spec,
        ],
        out_specs=[
            pl.BlockSpec((tm, tn), lambda i, j: (i, jnp.minimum(j, n_qkv - 1))),
            pl.BlockSpec((tm, tn), lambda i, j: (i, jnp.clip(j - n_qkv, 0, n_zh - 1))),
            pl.BlockSpec((tm, tn), lambda i, j: (i, jnp.clip(j - n_qkv - n_zh, 0, n_gate - 1))),
        ],
        out_shape=[
            jax.ShapeDtypeStruct((n, qkv_w), BF16),
            jax.ShapeDtypeStruct((n, hg5), F32),
            jax.ShapeDtypeStruct((n, 2 * d), BF16),
        ],
        scratch_shapes=[pltpu.VMEM((tm, d), BF16)],
        compiler_params=_params("parallel", "arbitrary"),
        name="norm_proj",
    )(x, gain, w, q_gain, k_gain, cos, sin, ones_blk, perm_blk)


def _attn_kernel(bound_ref, q_ref, k_ref, v_ref, *rest, tk, group):
    if len(rest) == 5:
        wf_ref, o_ref, wb_ref, acc_scr, m_scr = rest
    else:
        wf_ref = wb_ref = None
        o_ref, acc_scr, m_scr = rest
    tq = q_ref.shape[0]
    seq = k_ref.shape[0]
    n_chunks = seq // tk
    slab = None if wf_ref is None else wf_ref.shape[0] // n_chunks
    q = jnp.concatenate([q_ref[:, g * HEAD_DIM:(g + 1) * HEAD_DIM] for g in range(group)], axis=0)
    ones = jnp.ones((tk, HEAD_DIM), BF16)
    bound = bound_ref[0]
    acc_scr[...] = jnp.zeros(acc_scr.shape, F32)

    def chunk(c):
        rows = pl.ds(pl.multiple_of(c * tk, tk), tk)
        s = _dot_nt(q, k_ref[rows, :])
        return s, jnp.concatenate([v_ref[rows, :], ones], axis=1)

    @pl.when(bound <= ATTN_FIXED_SHIFT_LIMIT)
    def _():
        def body(c, carry):
            s, v1 = chunk(c)
            acc_scr[...] += _dot(jnp.exp2(s - bound).astype(BF16), v1)
            if wf_ref is not None:
                wrows = pl.ds(pl.multiple_of(c * slab, slab), slab)
                wb_ref[wrows, :] = wf_ref[wrows, :].astype(BF16)
            return carry
        lax.fori_loop(0, n_chunks, body, 0, unroll=True)

    @pl.when(bound > ATTN_FIXED_SHIFT_LIMIT)
    def _():
        if wf_ref is not None:
            wb_ref[...] = wf_ref[...].astype(BF16)
        m_scr[...] = jnp.full(m_scr.shape, -1e30, F32)

        def body(c, carry):
            s, v1 = chunk(c)
            m_prev = m_scr[...]
            m_new = jnp.maximum(m_prev, jnp.max(s, axis=1, keepdims=True))
            alpha = jnp.exp2(m_prev - m_new)
            p = jnp.exp2(s - jnp.tile(m_new, (1, tk // LANES)))
            acc_scr[...] = jnp.tile(alpha, (1, 2)) * acc_scr[...] + _dot(p.astype(BF16), v1)
            m_scr[...] = m_new
            return carry
        lax.fori_loop(0, seq // tk, body, 0)

    acc = acc_scr[...]
    o = acc[:, :HEAD_DIM] / acc[:, HEAD_DIM:]
    for g in range(group):
        o_ref[:, g * HEAD_DIM:(g + 1) * HEAD_DIM] = o[g * tq:(g + 1) * tq].astype(o_ref.dtype)


def _attention(qkv, score_bound, batch, seq, w_f32=None):
    group = N_Q_HEADS // N_KV_HEADS
    gw = group * HEAD_DIM
    tq = _tile(seq, 512)
    tk = _tile(seq, 512)
    nq = seq // tq
    in_specs = [
        pl.BlockSpec(memory_space=pltpu.SMEM),
        pl.BlockSpec((tq, gw), lambda b, n, i: (b * nq + i, n)),
        pl.BlockSpec((seq, HEAD_DIM), lambda b, n, i: (b, N_Q_HEADS + n)),
        pl.BlockSpec((seq, HEAD_DIM), lambda b, n, i: (b, N_Q_HEADS + N_KV_HEADS + n)),
    ]
    out_specs = [pl.BlockSpec((tq, gw), lambda b, n, i: (b * nq + i, n))]
    out_shape = [jax.ShapeDtypeStruct((batch * seq, N_Q_HEADS * HEAD_DIM), BF16)]
    args = [score_bound, qkv, qkv, qkv]
    if w_f32 is not None:
        n_steps = batch * N_KV_HEADS * nq
        w_rows, w_cols = w_f32.shape
        slab = w_rows // n_steps
        assert slab * n_steps == w_rows and slab % (16 * (seq // tk)) == 0, (w_rows, n_steps, seq // tk)
        wspec = pl.BlockSpec((slab, w_cols), lambda b, n, i: ((b * N_KV_HEADS + n) * nq + i, 0))
        in_specs.append(wspec)
        out_specs.append(wspec)
        out_shape.append(jax.ShapeDtypeStruct((w_rows, w_cols), BF16))
        args.append(w_f32)
    return pl.pallas_call(
        functools.partial(_attn_kernel, tk=tk, group=group),
        grid=(batch, N_KV_HEADS, nq),
        in_specs=in_specs,
        out_specs=out_specs,
        out_shape=out_shape,
        scratch_shapes=[
            pltpu.VMEM((group * tq, 2 * HEAD_DIM), F32),
            pltpu.VMEM((group * tq, LANES), F32),
        ],
        compiler_params=_params("parallel", "parallel", "arbitrary"),
        name="gqa_attention",
    )(*args)


def _hgrn_span(i, q_scr, b_scr, k_scr, zi_ref, vt_scr, o_scr, s_scr, tri, *, span, reverse, pairwise):
    C = HG_CHUNK
    nc = span // C
    r0 = pl.multiple_of(i * span, span)
    rows = pl.ds(r0, span)
    q = q_scr[rows, :]
    b = b_scr[rows, :]
    k = k_scr[rows, :]
    v = zi_ref[rows, :]
    vb = v.astype(BF16)
    end_row = 0 if reverse else C - 1
    ends = [b[c * C + end_row:c * C + end_row + 1, :] for c in range(nc)]
    b_end = jnp.concatenate([jnp.broadcast_to(e, (C, HG_D)) for e in ends], axis=0)
    qtb = (q * jnp.exp(b)).astype(BF16)
    kp = k * jnp.exp(b_end - b)

    if pairwise:
        rid = lax.broadcasted_iota(jnp.int32, (span, HG_D), 0)

        def pair(s, acc):
            cs = (s // C) * C
            if reverse:
                m = (rid <= s) & (rid >= cs)
            else:
                m = (rid >= s) & (rid < cs + C)
            w = jnp.where(m, jnp.exp(jnp.minimum(b - b_scr[pl.ds(r0 + s, 1), :], 0.0)), 0.0)
            r = jnp.sum(q * w * k_scr[pl.ds(r0 + s, 1), :], axis=1, keepdims=True)
            return acc + r * zi_ref[pl.ds(r0 + s, 1), :]

        o_intra = lax.fori_loop(0, span, pair, jnp.zeros((span, HG_D), F32))
    else:
        ktb = (k * jnp.exp(-b)).astype(BF16)
        a = jnp.where(tri, _dot_nt(qtb, ktb), 0.0)
        o_intra = _dot(a.astype(BF16), vb)

    chunk_of_row = lax.broadcasted_iota(jnp.int32, (span, HG_D), 0) // C
    kp_blocks = jnp.concatenate([jnp.where(chunk_of_row == c, kp, 0.0) for c in range(nc)], axis=1)
    upd = _dot(vt_scr[:, rows], kp_blocks.astype(BF16))

    st = s_scr[...]
    states = [None] * nc
    for c in (reversed(range(nc)) if reverse else range(nc)):
        states[c] = st.astype(BF16)
        st = st * jnp.exp(ends[c]) + upd[:, c * HG_D:(c + 1) * HG_D]
    s_scr[...] = st
    o_inter = _dot_nt(qtb, jnp.concatenate(states, axis=0))
    o_scr[rows, :] = o_intra + jnp.concatenate(
        [o_inter[c * C:(c + 1) * C, c * HG_D:(c + 1) * HG_D] for c in range(nc)], axis=0)


def _hgrn_kernel(zq_ref, zf_ref, zb_ref, zi_ref, zo_ref, lbf_ref, lbb_ref, on_ref, out_ref,
                 of_scr, ob_scr, sf_scr, sb_scr, q_scr, bf_scr, bb_scr, kf_scr, kb_scr, vt_scr, *, span):
    seq = zq_ref.shape[0]
    n_span = seq // span
    C = HG_CHUNK
    nc = span // C
    sf_scr[...] = jnp.zeros(sf_scr.shape, F32)
    sb_scr[...] = jnp.zeros(sb_scr.shape, F32)
    r = lax.broadcasted_iota(jnp.int32, (span, span), 0)
    c = lax.broadcasted_iota(jnp.int32, (span, span), 1)
    same = (r // C) == (c // C)
    tri_f = same & (c <= r)
    tri_b = same & (c >= r)
    tri_f16 = tri_f.astype(BF16)
    tri_b16 = tri_b.astype(BF16)

    def prepare(i, min_end):
        rows = pl.ds(pl.multiple_of(i * span, span), span)
        qh = zq_ref[rows, :]
        q_scr[rows, :] = qh * _sigmoid(qh)
        vt_scr[:, rows] = zi_ref[rows, :].T.astype(BF16)
        for z_ref, lb_ref, tri, b_scr, k_scr, end_row in (
                (zf_ref, lbf_ref, tri_f16, bf_scr, kf_scr, C - 1), (zb_ref, lbb_ref, tri_b16, bb_scr, kb_scr, 0)):
            lb = lb_ref[...]
            f = lb + (1.0 - lb) * _sigmoid(z_ref[rows, :])
            b = _dot_exact_lhs(tri, jnp.log(f), pieces=2)
            b_scr[rows, :] = b
            k_scr[rows, :] = 1.0 - f
            for cc in range(nc):
                min_end = jnp.minimum(min_end, b[cc * C + end_row:cc * C + end_row + 1, :])
        return min_end

    min_end = lax.fori_loop(0, n_span, prepare, jnp.zeros((1, HG_D), F32), unroll=2)
    safe = jnp.min(min_end) > HG_SAFE_LOG_DECAY

    def scan(pairwise):
        def body(i, carry):
            _hgrn_span(i, q_scr, bf_scr, kf_scr, zi_ref, vt_scr, of_scr, sf_scr, tri_f,
                       span=span, reverse=False, pairwise=pairwise)
            _hgrn_span(n_span - 1 - i, q_scr, bb_scr, kb_scr, zi_ref, vt_scr, ob_scr, sb_scr, tri_b,
                       span=span, reverse=True, pairwise=pairwise)
            return carry
        lax.fori_loop(0, n_span, body, 0, unroll=1 if pairwise else 4)

    @pl.when(safe)
    def _():
        scan(False)

    @pl.when(jnp.logical_not(safe))
    def _():
        scan(True)

    def finish(i, carry):
        rows = pl.ds(pl.multiple_of(i * span, span), span)
        o = _rmsnorm(of_scr[rows, :] + ob_scr[rows, :], on_ref[...])
        og = zo_ref[rows, :]
        out_ref[rows, :] = (o * (og * _sigmoid(og))).astype(out_ref.dtype)
        return carry

    lax.fori_loop(0, n_span, finish, 0)


def _hgrn(zh, lb_f, lb_b, out_norm, batch, seq):
    span = _tile(seq, 256)
    assert span % HG_CHUNK == 0
    zspec = lambda grp: pl.BlockSpec((seq, HG_D), lambda b, h: (b, grp * HG_HEADS + h))
    hspec = pl.BlockSpec((1, HG_D), lambda b, h: (0, h))
    seq_buf = pltpu.VMEM((seq, HG_D), F32)
    return pl.pallas_call(
        functools.partial(_hgrn_kernel, span=span),
        grid=(batch, HG_HEADS),
        in_specs=[zspec(0), zspec(1), zspec(2), zspec(3), zspec(4), hspec, hspec, hspec],
        out_specs=pl.BlockSpec((seq, HG_D), lambda b, h: (b, h)),
        out_shape=jax.ShapeDtypeStruct((batch * seq, HG_HEADS * HG_D), BF16),
        scratch_shapes=[
            seq_buf, seq_buf,
            pltpu.VMEM((HG_D, HG_D), F32), pltpu.VMEM((HG_D, HG_D), F32),
            seq_buf, seq_buf, seq_buf, seq_buf, seq_buf,
            pltpu.VMEM((HG_D, seq), BF16),
        ],
        compiler_params=_params("parallel", "parallel"),
        name="hgrn2",
    )(zh, zh, zh, zh, zh, lb_f, lb_b, out_norm)


def _merge_kernel(a_ref, h_ref, ga_ref, gb_ref, wa_ref, wh_ref, o_ref):
    ya = _dot(a_ref[...], wa_ref[...])
    yb = _dot(h_ref[...], wh_ref[...])
    merged = _sigmoid(ga_ref[...].astype(F32)) * ya + _sigmoid(gb_ref[...].astype(F32)) * yb
    o_ref[...] = merged.astype(o_ref.dtype)


def _merge(attn, hg, gates, w_up_attn, w_up_hgrn):
    n, wa = attn.shape
    wh = hg.shape[1]
    d = w_up_attn.shape[1]
    tm = _tile(n, 1024)
    tn = _tile(d, 512)
    nj = d // tn
    return pl.pallas_call(
        _merge_kernel,
        grid=(n // tm, nj),
        in_specs=[
            pl.BlockSpec((tm, wa), lambda i, j: (i, 0)),
            pl.BlockSpec((tm, wh), lambda i, j: (i, 0)),
            pl.BlockSpec((tm, tn), lambda i, j: (i, j)),
            pl.BlockSpec((tm, tn), lambda i, j: (i, nj + j)),
            pl.BlockSpec((wa, tn), lambda i, j: (0, j)),
            pl.BlockSpec((wh, tn), lambda i, j: (0, j)),
        ],
        out_specs=pl.BlockSpec((tm, tn), lambda i, j: (i, j)),
        out_shape=jax.ShapeDtypeStruct((n, d), BF16),
        compiler_params=_params("parallel", "arbitrary"),
        name="gated_merge",
    )(attn, hg, gates, gates, w_up_attn, w_up_hgrn)


def _pack_bf16_pairs(h):
    half = h.shape[1] // 2
    lo = pltpu.bitcast(h[:, :half].astype(BF16).astype(F32), jnp.uint32)
    hi = pltpu.bitcast(h[:, half:].astype(BF16).astype(F32), jnp.uint32)
    return (hi & jnp.uint32(0xFFFF0000)) | (lo >> 16)


def _unpack_bf16_pairs(u):
    lo = pltpu.bitcast(u << 16, F32).astype(BF16)
    hi = pltpu.bitcast(u & jnp.uint32(0xFFFF0000), F32).astype(BF16)
    return lo, hi


def _outproj_router_kernel(x_ref, m_ref, w_ref, g_ref, wr_ref, br_ref, hp_in_ref,
                           x1_ref, hp_ref, idx_ref, gate_ref):
    del hp_in_ref
    x1 = x_ref[...] + _dot(m_ref[...], w_ref[...])
    x1_ref[...] = x1
    h = _rmsnorm(x1, g_ref[...])
    hp_ref[...] = _pack_bf16_pairs(h)
    h_hi, h_mid, _ = _split3(h)
    w_hi, w_mid, _ = _split3(wr_ref[...])
    lg = (_dot_nt(w_hi, h_hi) + _dot_nt(w_hi, h_mid) + _dot_nt(w_mid, h_hi)) + br_ref[...]
    n_exp, tm = lg.shape
    eid = lax.broadcasted_iota(jnp.int32, (n_exp, tm), 0)
    vals = []
    for kk in range(TOP_K):
        m = jnp.max(lg, axis=0, keepdims=True)
        sel = jnp.min(jnp.where(lg == m, eid, n_exp), axis=0, keepdims=True)
        idx_ref[kk:kk + 1, :] = sel
        vals.append(m)
        lg = jnp.where(eid == sel, -jnp.inf, lg)
    ex = [jnp.exp(vv - vals[0]) for vv in vals]
    den = ex[0]
    for e in ex[1:]:
        den = den + e
    for kk in range(TOP_K):
        gate_ref[kk:kk + 1, :] = ex[kk] / den


def _outproj_router(x, merged, w_out, ffn_gain, w_router_t, b_router, hp_prev, row_off, n_total):
    n, d = x.shape
    tm = _tile(n, 512)
    assert row_off % tm == 0
    n_exp = w_router_t.shape[0]
    if hp_prev is None:
        hp_prev = jnp.zeros((8, LANES), jnp.uint32)
        aliases = {}
    else:
        aliases = {6: 1}
    return pl.pallas_call(
        _outproj_router_kernel,
        grid=(n // tm,),
        in_specs=[
            pl.BlockSpec((tm, d), lambda i: (i, 0)),
            pl.BlockSpec((tm, d), lambda i: (i, 0)),
            pl.BlockSpec((d, d), lambda i: (0, 0)),
            pl.BlockSpec((1, d), lambda i: (0, 0)),
            pl.BlockSpec((n_exp, d), lambda i: (0, 0)),
            pl.BlockSpec((n_exp, 1), lambda i: (0, 0)),
            pl.BlockSpec(memory_space=pl.ANY),
        ],
        out_specs=[
            pl.BlockSpec((tm, d), lambda i: (i, 0)),
            pl.BlockSpec((tm, d // 2), lambda i: (row_off // tm + i, 0)),
            pl.BlockSpec((TOP_K, tm), lambda i: (0, i)),
            pl.BlockSpec((TOP_K, tm), lambda i: (0, i)),
        ],
        out_shape=[
            jax.ShapeDtypeStruct((n, d), F32),
            jax.ShapeDtypeStruct((n_total, d // 2), jnp.uint32),
            jax.ShapeDtypeStruct((TOP_K, n), jnp.int32),
            jax.ShapeDtypeStruct((TOP_K, n), F32),
        ],
        input_output_aliases=aliases,
        compiler_params=_params("parallel"),
        name="outproj_router",
    )(x, merged, w_out, ffn_gain, w_router_t, b_router, hp_prev)


def _route_kernel(idx_ref, dest_ref, cnt_ref, cnt_scr, base_scr, *, row_block):
    phase = pl.program_id(0)
    i = pl.program_id(1)
    n_exp = cnt_scr.shape[0]
    tt = idx_ref.shape[1]
    eid = lax.broadcasted_iota(jnp.int32, (n_exp, tt), 0)
    onehot = [(eid == idx_ref[kk:kk + 1, :]) for kk in range(TOP_K)]

    @pl.when((phase == 0) & (i == 0))
    def _():
        cnt_scr[...] = jnp.zeros(cnt_scr.shape, F32)

    @pl.when(phase == 0)
    def _():
        tot = onehot[0].astype(F32)
        for oh in onehot[1:]:
            tot = tot + oh.astype(F32)
        cnt_scr[...] = cnt_scr[...] + jnp.sum(tot, axis=1, keepdims=True)
        cnt_ref[...] = cnt_scr[...]

    @pl.when((phase == 1) & (i == 0))
    def _():
        cnt = cnt_scr[...].astype(jnp.int32)
        padded = ((cnt + (row_block - 1)) // row_block * row_block).astype(F32)
        er = lax.broadcasted_iota(jnp.int32, (n_exp, n_exp), 0)
        ec = lax.broadcasted_iota(jnp.int32, (n_exp, n_exp), 1)
        base_scr[...] = _dot_exact_lhs((ec < er).astype(BF16), padded)

    @pl.when(phase == 1)
    def _():
        tr = lax.broadcasted_iota(jnp.int32, (tt, tt), 0)
        tc = lax.broadcasted_iota(jnp.int32, (tt, tt), 1)
        before = (tr < tc).astype(BF16)
        run = base_scr[...][:, :1]
        for kk in range(TOP_K):
            oh = onehot[kk].astype(F32)
            rank = _dot(oh.astype(BF16), before) + run
            dest_ref[kk:kk + 1, :] = jnp.sum(oh * rank, axis=0, keepdims=True).astype(jnp.int32)
            run = run + jnp.sum(oh, axis=1, keepdims=True)
        base_scr[...] = jnp.broadcast_to(run, base_scr.shape)


def _route(idx, row_block):
    n = idx.shape[1]
    tt = _tile(n, 512)
    return pl.pallas_call(
        functools.partial(_route_kernel, row_block=row_block),
        grid=(2, n // tt),
        in_specs=[pl.BlockSpec((TOP_K, tt), lambda p, i: (0, i))],
        out_specs=[
            pl.BlockSpec((TOP_K, tt), lambda p, i: (0, i * p)),
            pl.BlockSpec((N_EXPERTS, LANES), lambda p, i: (0, 0)),
        ],
        out_shape=[
            jax.ShapeDtypeStruct((TOP_K, n), jnp.int32),
            jax.ShapeDtypeStruct((N_EXPERTS, LANES), F32),
        ],
        scratch_shapes=[pltpu.VMEM((N_EXPERTS, LANES), F32), pltpu.VMEM((N_EXPERTS, LANES), F32)],
        compiler_params=_params("arbitrary", "arbitrary"),
        name="route_offsets",
    )(idx)


def _dispatch_kernel(seg_ref, dest_ref, h_ref, xs_ref, zero_scr, sem, zsem, *, row_block):
    tt = dest_ref.shape[1]
    n_exp = seg_ref.shape[1]

    @pl.when(pl.program_id(0) == 0)
    def _():
        zero_scr[...] = jnp.zeros(zero_scr.shape, zero_scr.dtype)

        def zero_copy(e):
            start = pl.multiple_of(seg_ref[1, e] - row_block, row_block)
            return pltpu.make_async_copy(zero_scr, xs_ref.at[pl.ds(start, row_block)], zsem)

        for e in range(n_exp):
            @pl.when(seg_ref[1, e] > seg_ref[0, e])
            def _(e=e):
                zero_copy(e).start()
        for e in range(n_exp):
            @pl.when(seg_ref[1, e] > seg_ref[0, e])
            def _(e=e):
                zero_copy(e).wait()

    def start(g, carry):
        for u in range(8):
            for kk in range(TOP_K):
                pltpu.make_async_copy(h_ref.at[g, pl.ds(u, 1), :],
                                      xs_ref.at[pl.ds(dest_ref[kk, g * 8 + u], 1)],
                                      sem).start(priority=(u * TOP_K + kk) % DMA_PRIORITIES)
        return carry

    lax.fori_loop(0, tt // 8, start, 0)
    pltpu.make_async_copy(xs_ref.at[pl.ds(0, TOP_K * tt)], xs_ref.at[pl.ds(0, TOP_K * tt)], sem).wait()


def _dispatch(seg, dest, hp, n_rows, row_block):
    n, w = hp.shape
    tt = _tile(n, 1024)
    grid_spec = pltpu.PrefetchScalarGridSpec(
        num_scalar_prefetch=1,
        grid=(n // tt,),
        in_specs=[
            pl.BlockSpec((TOP_K, tt), lambda i, s: (0, i), memory_space=pltpu.SMEM),
            pl.BlockSpec((tt // 8, 8, w), lambda i, s: (i, 0, 0)),
        ],
        out_specs=pl.BlockSpec(memory_space=pl.ANY),
        scratch_shapes=[pltpu.VMEM((row_block, w), hp.dtype), pltpu.SemaphoreType.DMA(()),
                        pltpu.SemaphoreType.DMA(())],
    )
    return pl.pallas_call(
        functools.partial(_dispatch_kernel, row_block=row_block),
        grid_spec=grid_spec,
        out_shape=jax.ShapeDtypeStruct((n_rows, w), hp.dtype),
        compiler_params=_params("arbitrary"),
        name="dispatch_rows",
    )(seg, dest, hp.reshape(n // 8, 8, w))


def _expert_kernel(meta_ref, xs_ref, wg_ref, wl_ref, bg_ref, bl_ref, wd_ref, bd_ref, ys_ref,
                   x_scr, y_ref):
    i = pl.program_id(0)
    j = pl.program_id(1)
    n_blocks = pl.num_programs(0)
    row_block = xs_ref.shape[0]
    valid = meta_ref[1 + n_blocks + i]

    def block(m):
        rows = slice(0, m)

        @pl.when(j == 0)
        def _():
            half = xs_ref.shape[1]
            lo, hi = _unpack_bf16_pairs(xs_ref[rows, :])
            x_scr[rows, :half] = lo
            x_scr[rows, half:] = hi
            y_ref[rows, :] = jnp.broadcast_to(bd_ref[...], (m, y_ref.shape[1]))

        x = x_scr[rows, :]
        glu = _dot(x, wg_ref[...]) + bg_ref[...]
        lin = _dot(x, wl_ref[...]) + bl_ref[...]
        glu = jnp.minimum(glu, SWIGLU_LIMIT)
        lin = jnp.clip(lin, -SWIGLU_LIMIT, SWIGLU_LIMIT)
        act = glu * _sigmoid(SWIGLU_ALPHA * glu) * (lin + 1.0)
        y_ref[rows, :] = y_ref[rows, :] + _dot(act.astype(BF16), wd_ref[...].astype(BF16))

        @pl.when(j == pl.num_programs(1) - 1)
        def _():
            ys_ref[rows, :] = _pack_bf16_pairs(y_ref[rows, :])

    @pl.when(valid > row_block // 2)
    def _():
        block(row_block)

    @pl.when((valid > 0) & (valid <= row_block // 2))
    def _():
        block(row_block // 2)


def _experts(meta, xs, w_gu, b_gu, w_dn, b_dn, row_block):
    n_rows, half = xs.shape
    d = 2 * half
    d_ff = w_dn.shape[1]
    tf = _tile(d_ff, 1024)
    nf = d_ff // tf
    n_blocks = n_rows // row_block

    def jj(i, j, m):
        return jnp.where(i < m[0], j, nf - 1)

    grid_spec = pltpu.PrefetchScalarGridSpec(
        num_scalar_prefetch=1,
        grid=(n_blocks, nf),
        in_specs=[
            pl.BlockSpec((row_block, half), lambda i, j, m: (i, 0)),
            pl.BlockSpec((None, d, tf), lambda i, j, m: (m[1 + i], 0, jj(i, j, m))),
            pl.BlockSpec((None, d, tf), lambda i, j, m: (m[1 + i], 0, nf + jj(i, j, m))),
            pl.BlockSpec((None, 1, tf), lambda i, j, m: (m[1 + i], 0, jj(i, j, m))),
            pl.BlockSpec((None, 1, tf), lambda i, j, m: (m[1 + i], 0, nf + jj(i, j, m))),
            pl.BlockSpec((None, tf, d), lambda i, j, m: (m[1 + i], jj(i, j, m), 0)),
            pl.BlockSpec((None, 1, d), lambda i, j, m: (m[1 + i], 0, 0)),
        ],
        out_specs=pl.BlockSpec((row_block, half), lambda i, j, m: (i, 0)),
        scratch_shapes=[pltpu.VMEM((row_block, d), BF16), pltpu.VMEM((row_block, d), F32)],
    )
    return pl.pallas_call(
        _expert_kernel,
        grid_spec=grid_spec,
        out_shape=jax.ShapeDtypeStruct((n_rows, half), jnp.uint32),
        compiler_params=_params("arbitrary", "arbitrary"),
        name="expert_swiglu",
    )(meta, xs, w_gu, w_gu, b_gu, b_gu, w_dn, b_dn)


def _combine_kernel(dest_ref, dest_next_ref, gate_ref, x1_ref, fn_ref, ys_ref, o_ref, buf, sem):
    tt = dest_ref.shape[1]
    i = pl.program_id(0)
    slot = i % 2

    def gather(d_ref, s):
        def start(g, carry):
            for u in range(8):
                for kk in range(TOP_K):
                    pltpu.make_async_copy(ys_ref.at[pl.ds(d_ref[kk, g * 8 + u], 1)],
                                          buf.at[s, kk * (tt // 8) + g, pl.ds(u, 1), :],
                                          sem.at[s]).start(priority=(u * TOP_K + kk) % DMA_PRIORITIES)
            return carry
        lax.fori_loop(0, tt // 8, start, 0)

    @pl.when(i == 0)
    def _():
        gather(dest_ref, 0)

    @pl.when(i + 1 < pl.num_programs(0))
    def _():
        gather(dest_next_ref, 1 - slot)

    pltpu.make_async_copy(ys_ref.at[pl.ds(0, TOP_K * tt)], ys_ref.at[pl.ds(0, TOP_K * tt)], sem.at[slot]).wait()
    half = buf.shape[3]
    gates = gate_ref[...]
    acc_lo = x1_ref[:, :half]
    acc_hi = x1_ref[:, half:]
    for kk in range(TOP_K):
        u = buf[slot, pl.ds(kk * (tt // 8), tt // 8)].reshape(tt, half)
        g = gates[:, kk:kk + 1]
        acc_lo = acc_lo + pltpu.bitcast(u << 16, F32) * g
        acc_hi = acc_hi + pltpu.bitcast(u & jnp.uint32(0xFFFF0000), F32) * g
    ms = (jnp.sum(acc_lo * acc_lo, axis=-1, keepdims=True)
          + jnp.sum(acc_hi * acc_hi, axis=-1, keepdims=True)) / (2 * half)
    inv = lax.rsqrt(ms + NORM_EPS)
    o_ref[:, :half] = acc_lo * inv * fn_ref[:, :half]
    o_ref[:, half:] = acc_hi * inv * fn_ref[:, half:]


def _combine(dest, gates_t, x1, final_gain, ys, row_off):
    n, d = x1.shape
    tt = _tile(n, 256)
    assert row_off % tt == 0
    off = row_off // tt
    nt = n // tt
    return pl.pallas_call(
        _combine_kernel,
        grid=(nt,),
        in_specs=[
            pl.BlockSpec((TOP_K, tt), lambda i: (0, off + i), memory_space=pltpu.SMEM),
            pl.BlockSpec((TOP_K, tt), lambda i: (0, off + jnp.minimum(i + 1, nt - 1)), memory_space=pltpu.SMEM),
            pl.BlockSpec((tt, TOP_K), lambda i: (off + i, 0)),
            pl.BlockSpec((tt, d), lambda i: (i, 0)),
            pl.BlockSpec((1, d), lambda i: (0, 0)),
            pl.BlockSpec(memory_space=pl.ANY),
        ],
        out_specs=pl.BlockSpec((tt, d), lambda i: (i, 0)),
        out_shape=jax.ShapeDtypeStruct((n, d), F32),
        scratch_shapes=[pltpu.VMEM((2, TOP_K * tt // 8, 8, d // 2), jnp.uint32), pltpu.SemaphoreType.DMA((2,))],
        compiler_params=_params("arbitrary"),
        name="combine_rows",
    )(dest, dest, gates_t, x1, final_gain, ys)


def _rope_tables(seq_len):
    rows = seq_len // GRID_W
    row = jnp.repeat(jnp.arange(rows, dtype=F32), GRID_W)
    col = jnp.tile(jnp.arange(GRID_W, dtype=F32), rows)
    freqs = ROPE_THETA ** (-jnp.arange(ROPE_HALF, dtype=F32) / ROPE_HALF)
    ang_r = row[:, None] * freqs[None, :]
    ang_c = col[:, None] * freqs[None, :]
    cos = jnp.concatenate([jnp.cos(ang_r), jnp.cos(ang_r), jnp.cos(ang_c), jnp.cos(ang_c)], axis=1)
    sin = jnp.concatenate([-jnp.sin(ang_r), jnp.sin(ang_r), -jnp.sin(ang_c), jnp.sin(ang_c)], axis=1)
    return cos, sin


def kernel(x_prompt, x_sample, mix_norm, w_in, q_norm, k_norm, hg_lb_logits, hg_out_norm, w_up_attn,
           w_up_hgrn, w_out, ffn_norm, w_router, b_router, w_gate_up, b_gate_up, w_down, b_down, final_norm):
    assert mix_norm.shape[0] == 1, "single trunk layer"
    d = x_prompt.shape[-1]
    hg_w = HG_HEADS * HG_D
    n_exp = w_router.shape[-1]
    row_block = MOE_ROW_BLOCK
    streams = [(x.reshape(-1, d), x.shape[0], x.shape[1]) for x in (x_prompt, x_sample)]
    n_total = sum(x.shape[0] for x, _, _ in streams)

    lb = jnp.cumsum(jax.nn.softmax(hg_lb_logits.astype(F32), axis=1), axis=1)[:, 0]
    w_in_b = w_in[0].astype(BF16)
    w_ua_b = w_up_attn[0].astype(BF16)
    w_uh_b = w_up_hgrn[0].astype(BF16)
    w_out_b = w_out[0].astype(BF16)
    w_r_t = w_router[0].T
    mix_g = mix_norm[0].reshape(1, d)
    score_bound = (1.02 * LOG2_E * HEAD_DIM ** 0.5 * jnp.max(jnp.abs(q_norm[0])) * jnp.max(jnp.abs(k_norm[0])))
    score_bound = score_bound.astype(F32).reshape(1)

    w_gu2d = w_gate_up[0].reshape(-1, w_gate_up.shape[-1])
    cast_host = max(range(len(streams)), key=lambda s: streams[s][0].shape[0])
    w_gu_b = None

    x1s, idxs, gate_ts = [], [], []
    hp = None
    row_off = 0
    for s, (x, batch, seq) in enumerate(streams):
        cos, sin = _rope_tables(seq)
        qkv, zh, gates = _norm_proj(x, mix_g, w_in_b, q_norm[0].reshape(1, HEAD_DIM),
                                    k_norm[0].reshape(1, HEAD_DIM), cos, sin, seq)
        if s == cast_host:
            attn, w_gu_b = _attention(qkv, score_bound, batch, seq, w_gu2d)
        else:
            attn, = _attention(qkv, score_bound, batch, seq)
        hg = _hgrn(zh, lb[0:1], lb[1:2], hg_out_norm[0].reshape(1, hg_w), batch, seq)
        merged = _merge(attn, hg, gates, w_ua_b, w_uh_b)
        x1, hp, idx, gate = _outproj_router(x, merged, w_out_b, ffn_norm[0].reshape(1, d), w_r_t,
                                            b_router[0].reshape(n_exp, 1), hp, row_off, n_total)
        x1s.append(x1)
        idxs.append(idx)
        gate_ts.append(gate.T)
        row_off += x.shape[0]

    idx = jnp.concatenate(idxs, axis=1)
    gate_t = jnp.concatenate(gate_ts, axis=0)
    dest, counts = _route(idx, row_block)
    cnt = counts[:, 0].astype(jnp.int32)
    padded = (cnt + row_block - 1) // row_block * row_block
    pad_end = jnp.cumsum(padded)
    n_rows = n_total * TOP_K + n_exp * row_block
    n_blocks = n_rows // row_block
    blk_start = jnp.arange(n_blocks, dtype=jnp.int32) * row_block
    blk_e = jnp.minimum(jnp.sum(pad_end[None, :] <= blk_start[:, None], axis=1), n_exp - 1).astype(jnp.int32)
    seg_start = pad_end - padded
    blk_valid = jnp.clip(cnt[blk_e] - (blk_start - seg_start[blk_e]), 0, row_block)
    blk_valid = jnp.where(blk_start < pad_end[-1], blk_valid, 0).astype(jnp.int32)
    meta = jnp.concatenate([(pad_end[-1:] // row_block).astype(jnp.int32), blk_e, blk_valid])
    seg = jnp.stack([seg_start, pad_end]).astype(jnp.int32)

    xs = _dispatch(seg, dest, hp, n_rows, row_block)
    ys = _experts(meta, xs, w_gu_b.reshape(w_gate_up.shape[1:]), b_gate_up[0].reshape(n_exp, 1, -1),
                  w_down[0], b_down[0].reshape(n_exp, 1, d), row_block)

    outs = []
    row_off = 0
    for (x, batch, seq), x1 in zip(streams, x1s):
        out = _combine(dest, gate_t, x1, final_norm.reshape(1, d), ys, row_off)
        outs.append(out.reshape(batch, seq, d))
        row_off += x.shape[0]
    return tuple(outs)
```

```python
import functools

import jax
import jax.numpy as jnp
from jax import lax
from jax.experimental import pallas as pl
from jax.experimental.pallas import tpu as pltpu

GRID_W = 64
HEAD_DIM = 128
N_Q_HEADS = 16
N_KV_HEADS = 4
ROPE_THETA = 10000.0
ROPE_HALF = HEAD_DIM // 4
HG_HEADS = 8
HG_D = 128
HG_CHUNK = 64
N_EXPERTS = 32
TOP_K = 4
SWIGLU_LIMIT = 7.0
SWIGLU_ALPHA = 1.702
NORM_EPS = 1e-5

HG_SAFE_LOG_DECAY = -60.0

LOG2_E = 1.4426950408889634
ATTN_FIXED_SHIFT_LIMIT = 60.0

V7X_VMEM_BYTES = 64 * 1024 * 1024
VMEM_LIMIT_BYTES = V7X_VMEM_BYTES - 8 * 1024 * 1024
LANES = 128
MOE_ROW_BLOCK = 512
DMA_PRIORITIES = 2

BF16 = jnp.bfloat16
F32 = jnp.float32


def _params(*sem):
    return pltpu.CompilerParams(dimension_semantics=sem, vmem_limit_bytes=VMEM_LIMIT_BYTES)


def _tile(n, pref):
    t = min(n, pref)
    while n % t:
        t //= 2
    return t


def _sigmoid(x):
    return 1.0 / (1.0 + jnp.exp(-x))


def _rmsnorm(x, g):
    return x * lax.rsqrt(jnp.mean(x * x, axis=-1, keepdims=True) + NORM_EPS) * g


def _dot(a, b):
    return jnp.dot(a, b, preferred_element_type=F32)


def _dot_nt(a, b):
    return lax.dot_general(a, b, (((1,), (1,)), ((), ())), preferred_element_type=F32)


def _split3(x):
    hi = x.astype(BF16)
    r = x - hi.astype(F32)
    mid = r.astype(BF16)
    lo = (r - mid.astype(F32)).astype(BF16)
    return hi, mid, lo


def _dot_exact_lhs(m_bf16, x, pieces=3):
    parts = _split3(x)[:pieces]
    acc = _dot(m_bf16, parts[0])
    for p in parts[1:]:
        acc = acc + _dot(m_bf16, p)
    return acc


def _rope_head_pair(zp, gain, cos, sin, ones_blk, perm_blk, scale):
    ms = _dot((zp * zp).astype(BF16), ones_blk)
    y = zp * lax.rsqrt(ms + NORM_EPS) * (gain * scale)
    y_hi = y.astype(BF16)
    y_lo = (y - y_hi.astype(F32)).astype(BF16)
    partner = _dot(y_hi, perm_blk) + _dot(y_lo, perm_blk)
    return y * cos + partner * sin


def _proj_kernel(x_ref, g_ref, w_ref, qn_ref, kn_ref, cos_ref, sin_ref, ones_ref, perm_ref,
                 qkv_ref, zh_ref, gate_ref, h_scr, *, tile_kinds):
    j = pl.program_id(1)

    @pl.when(j == 0)
    def _():
        h_scr[...] = _rmsnorm(x_ref[...], g_ref[...]).astype(BF16)

    z = _dot(h_scr[...], w_ref[...])

    def qkv_tile(kinds):
        two = lambda r: jnp.concatenate([r[...], r[...]], axis=1)
        outs = []
        for h in range(0, len(kinds), 2):
            kind = kinds[h]
            assert kinds[h + 1] == kind
            zp = z[:, h * HEAD_DIM:(h + 2) * HEAD_DIM]
            if kind == "q":
                zp = _rope_head_pair(zp, two(qn_ref), two(cos_ref), two(sin_ref), ones_ref[...], perm_ref[...],
                                     LOG2_E * HEAD_DIM ** -0.5)
            elif kind == "k":
                zp = _rope_head_pair(zp, two(kn_ref), two(cos_ref), two(sin_ref), ones_ref[...], perm_ref[...], 1.0)
            outs.append(zp)
        return jnp.concatenate(outs, axis=1)

    for lo, hi, kind in tile_kinds:
        @pl.when((j >= lo) & (j < hi))
        def _(kind=kind):
            if kind == "zh":
                zh_ref[...] = z
            elif kind == "gate":
                gate_ref[...] = z.astype(gate_ref.dtype)
            else:
                qkv_ref[...] = qkv_tile(kind).astype(qkv_ref.dtype)


def _norm_proj(x, gain, w, q_gain, k_gain, cos, sin, seq):
    n, d = x.shape
    attn_w = N_Q_HEADS * HEAD_DIM
    kv_w = N_KV_HEADS * HEAD_DIM
    hg5 = 5 * HG_HEADS * HG_D
    qkv_w = attn_w + 2 * kv_w
    assert w.shape[1] == qkv_w + hg5 + 2 * d
    tm = _tile(seq, 1024)
    tn = 1024
    while attn_w % tn or (2 * kv_w) % tn or hg5 % tn or (2 * d) % tn:
        tn //= 2
    assert tn % HEAD_DIM == 0
    heads = ["q"] * N_Q_HEADS + ["k"] * N_KV_HEADS + ["v"] * N_KV_HEADS
    hpt = tn // HEAD_DIM
    n_qkv, n_zh, n_gate = qkv_w // tn, hg5 // tn, 2 * d // tn
    tile_kinds = []
    for t in range(n_qkv):
        kind = tuple(heads[t * hpt:(t + 1) * hpt])
        if tile_kinds and tile_kinds[-1][2] == kind:
            tile_kinds[-1] = (tile_kinds[-1][0], t + 1, kind)
        else:
            tile_kinds.append((t, t + 1, kind))
    tile_kinds += [(n_qkv, n_qkv + n_zh, "zh"), (n_qkv + n_zh, n_qkv + n_zh + n_gate, "gate")]
    n_pos = seq // tm
    r = jnp.arange(2 * HEAD_DIM)
    same_head = (r[:, None] // HEAD_DIM) == (r[None, :] // HEAD_DIM)
    ones_blk = (same_head.astype(F32) / HEAD_DIM).astype(BF16)
    partner_of = jnp.where((r % (2 * ROPE_HALF)) < ROPE_HALF, r + ROPE_HALF, r - ROPE_HALF)
    perm_blk = (r[:, None] == partner_of[None, :]).astype(BF16)
    const_spec = pl.BlockSpec((2 * HEAD_DIM, 2 * HEAD_DIM), lambda i, j: (0, 0))
    return pl.pallas_call(
        functools.partial(_proj_kernel, tile_kinds=tile_kinds),
        grid=(n // tm, n_qkv + n_zh + n_gate),
        in_specs=[
            pl.BlockSpec((tm, d), lambda i, j: (i, 0)),
            pl.BlockSpec((1, d), lambda i, j: (0, 0)),
            pl.BlockSpec((d, tn), lambda i, j: (0, j)),
            pl.BlockSpec((1, HEAD_DIM), lambda i, j: (0, 0)),
            pl.BlockSpec((1, HEAD_DIM), lambda i, j: (0, 0)),
            pl.BlockSpec((tm, HEAD_DIM), lambda i, j: (i % n_pos, 0)),
            pl.BlockSpec((tm, HEAD_DIM), lambda i, j: (i % n_pos, 0)),
            const_spec,
            const_spec,
        ],
        out_specs=[
            pl.BlockSpec((tm, tn), lambda i, j: (i, jnp.minimum(j, n_qkv - 1))),
            pl.BlockSpec((tm, tn), lambda i, j: (i, jnp.clip(j - n_qkv, 0, n_zh - 1))),
            pl.BlockSpec((tm, tn), lambda i, j: (i, jnp.clip(j - n_qkv - n_zh, 0, n_gate - 1))),
        ],
        out_shape=[
            jax.ShapeDtypeStruct((n, qkv_w), BF16),
            jax.ShapeDtypeStruct((n, hg5), F32),
            jax.ShapeDtypeStruct((n, 2 * d), BF16),
        ],
        scratch_shapes=[pltpu.VMEM((tm, d), BF16)],
        compiler_params=_params("parallel", "arbitrary"),
        name="norm_proj",
    )(x, gain, w, q_gain, k_gain, cos, sin, ones_blk, perm_blk)


def _attn_kernel(bound_ref, q_ref, k_ref, v_ref, *rest, tk, group):
    if len(rest) == 5:
        wf_ref, o_ref, wb_ref, acc_scr, m_scr = rest
    else:
        wf_ref = wb_ref = None
        o_ref, acc_scr, m_scr = rest
    tq = q_ref.shape[0]
    seq = k_ref.shape[0]
    n_chunks = seq // tk
    slab = None if wf_ref is None else wf_ref.shape[0] // n_chunks
    q = jnp.concatenate([q_ref[:, g * HEAD_DIM:(g + 1) * HEAD_DIM] for g in range(group)], axis=0)
    ones = jnp.ones((tk, HEAD_DIM), BF16)
    bound = bound_ref[0]
    acc_scr[...] = jnp.zeros(acc_scr.shape, F32)

    def chunk(c):
        rows = pl.ds(pl.multiple_of(c * tk, tk), tk)
        s = _dot_nt(q, k_ref[rows, :])
        return s, jnp.concatenate([v_ref[rows, :], ones], axis=1)

    @pl.when(bound <= ATTN_FIXED_SHIFT_LIMIT)
    def _():
        def body(c, carry):
            s, v1 = chunk(c)
            acc_scr[...] += _dot(jnp.exp2(s - bound).astype(BF16), v1)
            if wf_ref is not None:
                wrows = pl.ds(pl.multiple_of(c * slab, slab), slab)
                wb_ref[wrows, :] = wf_ref[wrows, :].astype(BF16)
            return carry
        lax.fori_loop(0, n_chunks, body, 0, unroll=True)

    @pl.when(bound > ATTN_FIXED_SHIFT_LIMIT)
    def _():
        if wf_ref is not None:
            wb_ref[...] = wf_ref[...].astype(BF16)
        m_scr[...] = jnp.full(m_scr.shape, -1e30, F32)

        def body(c, carry):
            s, v1 = chunk(c)
            m_prev = m_scr[...]
            m_new = jnp.maximum(m_prev, jnp.max(s, axis=1, keepdims=True))
            alpha = jnp.exp2(m_prev - m_new)
            p = jnp.exp2(s - jnp.tile(m_new, (1, tk // LANES)))
            acc_scr[...] = jnp.tile(alpha, (1, 2)) * acc_scr[...] + _dot(p.astype(BF16), v1)
            m_scr[...] = m_new
            return carry
        lax.fori_loop(0, seq // tk, body, 0)

    acc = acc_scr[...]
    o = acc[:, :HEAD_DIM] / acc[:, HEAD_DIM:]
    for g in range(group):
        o_ref[:, g * HEAD_DIM:(g + 1) * HEAD_DIM] = o[g * tq:(g + 1) * tq].astype(o_ref.dtype)


def _attention(qkv, score_bound, batch, seq, w_f32=None):
    group = N_Q_HEADS // N_KV_HEADS
    gw = group * HEAD_DIM
    tq = _tile(seq, 512)
    tk = _tile(seq, 512)
    nq = seq // tq
    in_specs = [
        pl.BlockSpec(memory_space=pltpu.SMEM),
        pl.BlockSpec((tq, gw), lambda b, n, i: (b * nq + i, n)),
        pl.BlockSpec((seq, HEAD_DIM), lambda b, n, i: (b, N_Q_HEADS + n)),
        pl.BlockSpec((seq, HEAD_DIM), lambda b, n, i: (b, N_Q_HEADS + N_KV_HEADS + n)),
    ]
    out_specs = [pl.BlockSpec((tq, gw), lambda b, n, i: (b * nq + i, n))]
    out_shape = [jax.ShapeDtypeStruct((batch * seq, N_Q_HEADS * HEAD_DIM), BF16)]
    args = [score_bound, qkv, qkv, qkv]
    if w_f32 is not None:
        n_steps = batch * N_KV_HEADS * nq
        w_rows, w_cols = w_f32.shape
        slab = w_rows // n_steps
        assert slab * n_steps == w_rows and slab % (16 * (seq // tk)) == 0, (w_rows, n_steps, seq // tk)
        wspec = pl.BlockSpec((slab, w_cols), lambda b, n, i: ((b * N_KV_HEADS + n) * nq + i, 0))
        in_specs.append(wspec)
        out_specs.append(wspec)
        out_shape.append(jax.ShapeDtypeStruct((w_rows, w_cols), BF16))
        args.append(w_f32)
    return pl.pallas_call(
        functools.partial(_attn_kernel, tk=tk, group=group),
        grid=(batch, N_KV_HEADS, nq),
        in_specs=in_specs,
        out_specs=out_specs,
        out_shape=out_shape,
        scratch_shapes=[
            pltpu.VMEM((group * tq, 2 * HEAD_DIM), F32),
            pltpu.VMEM((group * tq, LANES), F32),
        ],
        compiler_params=_params("parallel", "parallel", "arbitrary"),
        name="gqa_attention",
    )(*args)


def _hgrn_span(i, q_scr, b_scr, k_scr, zi_ref, vt_scr, o_scr, s_scr, tri, *, span, reverse, pairwise):
    C = HG_CHUNK
    nc = span // C
    r0 = pl.multiple_of(i * span, span)
    rows = pl.ds(r0, span)
    q = q_scr[rows, :]
    b = b_scr[rows, :]
    k = k_scr[rows, :]
    v = zi_ref[rows, :]
    vb = v.astype(BF16)
    end_row = 0 if reverse else C - 1
    ends = [b[c * C + end_row:c * C + end_row + 1, :] for c in range(nc)]
    b_end = jnp.concatenate([jnp.broadcast_to(e, (C, HG_D)) for e in ends], axis=0)
    qtb = (q * jnp.exp(b)).astype(BF16)
    kp = k * jnp.exp(b_end - b)

    if pairwise:
        rid = lax.broadcasted_iota(jnp.int32, (span, HG_D), 0)

        def pair(s, acc):
            cs = (s // C) * C
            if reverse:
                m = (rid <= s) & (rid >= cs)
            else:
                m = (rid >= s) & (rid < cs + C)
            w = jnp.where(m, jnp.exp(jnp.minimum(b - b_scr[pl.ds(r0 + s, 1), :], 0.0)), 0.0)
            r = jnp.sum(q * w * k_scr[pl.ds(r0 + s, 1), :], axis=1, keepdims=True)
            return acc + r * zi_ref[pl.ds(r0 + s, 1), :]

        o_intra = lax.fori_loop(0, span, pair, jnp.zeros((span, HG_D), F32))
    else:
        ktb = (k * jnp.exp(-b)).astype(BF16)
        a = jnp.where(tri, _dot_nt(qtb, ktb), 0.0)
        o_intra = _dot(a.astype(BF16), vb)

    chunk_of_row = lax.broadcasted_iota(jnp.int32, (span, HG_D), 0) // C
    kp_blocks = jnp.concatenate([jnp.where(chunk_of_row == c, kp, 0.0) for c in range(nc)], axis=1)
    upd = _dot(vt_scr[:, rows], kp_blocks.astype(BF16))

    st = s_scr[...]
    states = [None] * nc
    for c in (reversed(range(nc)) if reverse else range(nc)):
        states[c] = st.astype(BF16)
        st = st * jnp.exp(ends[c]) + upd[:, c * HG_D:(c + 1) * HG_D]
    s_scr[...] = st
    o_inter = _dot_nt(qtb, jnp.concatenate(states, axis=0))
    o_scr[rows, :] = o_intra + jnp.concatenate(
        [o_inter[c * C:(c + 1) * C, c * HG_D:(c + 1) * HG_D] for c in range(nc)], axis=0)


def _hgrn_kernel(zq_ref, zf_ref, zb_ref, zi_ref, zo_ref, lbf_ref, lbb_ref, on_ref, out_ref,
                 of_scr, ob_scr, sf_scr, sb_scr, q_scr, bf_scr, bb_scr, kf_scr, kb_scr, vt_scr, *, span):
    seq = zq_ref.shape[0]
    n_span = seq // span
    C = HG_CHUNK
    nc = span // C
    sf_scr[...] = jnp.zeros(sf_scr.shape, F32)
    sb_scr[...] = jnp.zeros(sb_scr.shape, F32)
    r = lax.broadcasted_iota(jnp.int32, (span, span), 0)
    c = lax.broadcasted_iota(jnp.int32, (span, span), 1)
    same = (r // C) == (c // C)
    tri_f = same & (c <= r)
    tri_b = same & (c >= r)
    tri_f16 = tri_f.astype(BF16)
    tri_b16 = tri_b.astype(BF16)

    def prepare(i, min_end):
        rows = pl.ds(pl.multiple_of(i * span, span), span)
        qh = zq_ref[rows, :]
        q_scr[rows, :] = qh * _sigmoid(qh)
        vt_scr[:, rows] = zi_ref[rows, :].T.astype(BF16)
        for z_ref, lb_ref, tri, b_scr, k_scr, end_row in (
                (zf_ref, lbf_ref, tri_f16, bf_scr, kf_scr, C - 1), (zb_ref, lbb_ref, tri_b16, bb_scr, kb_scr, 0)):
            lb = lb_ref[...]
            f = lb + (1.0 - lb) * _sigmoid(z_ref[rows, :])
            b = _dot_exact_lhs(tri, jnp.log(f), pieces=2)
            b_scr[rows, :] = b
            k_scr[rows, :] = 1.0 - f
            for cc in range(nc):
                min_end = jnp.minimum(min_end, b[cc * C + end_row:cc * C + end_row + 1, :])
        return min_end

    min_end = lax.fori_loop(0, n_span, prepare, jnp.zeros((1, HG_D), F32), unroll=4)
    safe = jnp.min(min_end) > HG_SAFE_LOG_DECAY

    def scan(pairwise):
        def body(i, carry):
            _hgrn_span(i, q_scr, bf_scr, kf_scr, zi_ref, vt_scr, of_scr, sf_scr, tri_f,
                       span=span, reverse=False, pairwise=pairwise)
            _hgrn_span(n_span - 1 - i, q_scr, bb_scr, kb_scr, zi_ref, vt_scr, ob_scr, sb_scr, tri_b,
                       span=span, reverse=True, pairwise=pairwise)
            return carry
        lax.fori_loop(0, n_span, body, 0, unroll=1 if pairwise else 4)

    @pl.when(safe)
    def _():
        scan(False)

    @pl.when(jnp.logical_not(safe))
    def _():
        scan(True)

    def finish(i, carry):
        rows = pl.ds(pl.multiple_of(i * span, span), span)
        o = _rmsnorm(of_scr[rows, :] + ob_scr[rows, :], on_ref[...])
        og = zo_ref[rows, :]
        out_ref[rows, :] = (o * (og * _sigmoid(og))).astype(out_ref.dtype)
        return carry

    lax.fori_loop(0, n_span, finish, 0)


def _hgrn(zh, lb_f, lb_b, out_norm, batch, seq):
    span = _tile(seq, 256)
    assert span % HG_CHUNK == 0
    zspec = lambda grp: pl.BlockSpec((seq, HG_D), lambda b, h: (b, grp * HG_HEADS + h))
    hspec = pl.BlockSpec((1, HG_D), lambda b, h: (0, h))
    seq_buf = pltpu.VMEM((seq, HG_D), F32)
    return pl.pallas_call(
        functools.partial(_hgrn_kernel, span=span),
        grid=(batch, HG_HEADS),
        in_specs=[zspec(0), zspec(1), zspec(2), zspec(3), zspec(4), hspec, hspec, hspec],
        out_specs=pl.BlockSpec((seq, HG_D), lambda b, h: (b, h)),
        out_shape=jax.ShapeDtypeStruct((batch * seq, HG_HEADS * HG_D), BF16),
        scratch_shapes=[
            seq_buf, seq_buf,
            pltpu.VMEM((HG_D, HG_D), F32), pltpu.VMEM((HG_D, HG_D), F32),
            seq_buf, seq_buf, seq_buf, seq_buf, seq_buf,
            pltpu.VMEM((HG_D, seq), BF16),
        ],
        compiler_params=_params("parallel", "parallel"),
        name="hgrn2",
    )(zh, zh, zh, zh, zh, lb_f, lb_b, out_norm)


def _merge_kernel(a_ref, h_ref, ga_ref, gb_ref, wa_ref, wh_ref, o_ref):
    ya = _dot(a_ref[...], wa_ref[...])
    yb = _dot(h_ref[...], wh_ref[...])
    merged = _sigmoid(ga_ref[...].astype(F32)) * ya + _sigmoid(gb_ref[...].astype(F32)) * yb
    o_ref[...] = merged.astype(o_ref.dtype)


def _merge(attn, hg, gates, w_up_attn, w_up_hgrn):
    n, wa = attn.shape
    wh = hg.shape[1]
    d = w_up_attn.shape[1]
    tm = _tile(n, 1024)
    tn = _tile(d, 512)
    nj = d // tn
    return pl.pallas_call(
        _merge_kernel,
        grid=(n // tm, nj),
        in_specs=[
            pl.BlockSpec((tm, wa), lambda i, j: (i, 0)),
            pl.BlockSpec((tm, wh), lambda i, j: (i, 0)),
            pl.BlockSpec((tm, tn), lambda i, j: (i, j)),
            pl.BlockSpec((tm, tn), lambda i, j: (i, nj + j)),
            pl.BlockSpec((wa, tn), lambda i, j: (0, j)),
            pl.BlockSpec((wh, tn), lambda i, j: (0, j)),
        ],
        out_specs=pl.BlockSpec((tm, tn), lambda i, j: (i, j)),
        out_shape=jax.ShapeDtypeStruct((n, d), BF16),
        compiler_params=_params("parallel", "arbitrary"),
        name="gated_merge",
    )(attn, hg, gates, gates, w_up_attn, w_up_hgrn)


def _pack_bf16_pairs(h):
    half = h.shape[1] // 2
    lo = pltpu.bitcast(h[:, :half].astype(BF16).astype(F32), jnp.uint32)
    hi = pltpu.bitcast(h[:, half:].astype(BF16).astype(F32), jnp.uint32)
    return (hi & jnp.uint32(0xFFFF0000)) | (lo >> 16)


def _unpack_bf16_pairs(u):
    lo = pltpu.bitcast(u << 16, F32).astype(BF16)
    hi = pltpu.bitcast(u & jnp.uint32(0xFFFF0000), F32).astype(BF16)
    return lo, hi


def _outproj_router_kernel(x_ref, m_ref, w_ref, g_ref, wr_ref, br_ref, hp_in_ref,
                           x1_ref, hp_ref, idx_ref, gate_ref):
    del hp_in_ref
    x1 = x_ref[...] + _dot(m_ref[...], w_ref[...])
    x1_ref[...] = x1
    h = _rmsnorm(x1, g_ref[...])
    hp_ref[...] = _pack_bf16_pairs(h)
    h_hi, h_mid, _ = _split3(h)
    w_hi, w_mid, _ = _split3(wr_ref[...])
    lg = (_dot_nt(w_hi, h_hi) + _dot_nt(w_hi, h_mid) + _dot_nt(w_mid, h_hi)) + br_ref[...]
    n_exp, tm = lg.shape
    eid = lax.broadcasted_iota(jnp.int32, (n_exp, tm), 0)
    vals = []
    for kk in range(TOP_K):
        m = jnp.max(lg, axis=0, keepdims=True)
        sel = jnp.min(jnp.where(lg == m, eid, n_exp), axis=0, keepdims=True)
        idx_ref[kk:kk + 1, :] = sel
        vals.append(m)
        lg = jnp.where(eid == sel, -jnp.inf, lg)
    ex = [jnp.exp(vv - vals[0]) for vv in vals]
    den = ex[0]
    for e in ex[1:]:
        den = den + e
    for kk in range(TOP_K):
        gate_ref[kk:kk + 1, :] = ex[kk] / den


def _outproj_router(x, merged, w_out, ffn_gain, w_router_t, b_router, hp_prev, row_off, n_total):
    n, d = x.shape
    tm = _tile(n, 512)
    assert row_off % tm == 0
    n_exp = w_router_t.shape[0]
    if hp_prev is None:
        hp_prev = jnp.zeros((8, LANES), jnp.uint32)
        aliases = {}
    else:
        aliases = {6: 1}
    return pl.pallas_call(
        _outproj_router_kernel,
        grid=(n // tm,),
        in_specs=[
            pl.BlockSpec((tm, d), lambda i: (i, 0)),
            pl.BlockSpec((tm, d), lambda i: (i, 0)),
            pl.BlockSpec((d, d), lambda i: (0, 0)),
            pl.BlockSpec((1, d), lambda i: (0, 0)),
            pl.BlockSpec((n_exp, d), lambda i: (0, 0)),
            pl.BlockSpec((n_exp, 1), lambda i: (0, 0)),
            pl.BlockSpec(memory_space=pl.ANY),
        ],
        out_specs=[
            pl.BlockSpec((tm, d), lambda i: (i, 0)),
            pl.BlockSpec((tm, d // 2), lambda i: (row_off // tm + i, 0)),
            pl.BlockSpec((TOP_K, tm), lambda i: (0, i)),
            pl.BlockSpec((TOP_K, tm), lambda i: (0, i)),
        ],
        out_shape=[
            jax.ShapeDtypeStruct((n, d), F32),
            jax.ShapeDtypeStruct((n_total, d // 2), jnp.uint32),
            jax.ShapeDtypeStruct((TOP_K, n), jnp.int32),
            jax.ShapeDtypeStruct((TOP_K, n), F32),
        ],
        input_output_aliases=aliases,
        compiler_params=_params("parallel"),
        name="outproj_router",
    )(x, merged, w_out, ffn_gain, w_router_t, b_router, hp_prev)


def _route_kernel(idx_ref, dest_ref, cnt_ref, cnt_scr, base_scr, *, row_block):
    phase = pl.program_id(0)
    i = pl.program_id(1)
    n_exp = cnt_scr.shape[0]
    tt = idx_ref.shape[1]
    eid = lax.broadcasted_iota(jnp.int32, (n_exp, tt), 0)
    onehot = [(eid == idx_ref[kk:kk + 1, :]) for kk in range(TOP_K)]

    @pl.when((phase == 0) & (i == 0))
    def _():
        cnt_scr[...] = jnp.zeros(cnt_scr.shape, F32)

    @pl.when(phase == 0)
    def _():
        tot = onehot[0].astype(F32)
        for oh in onehot[1:]:
            tot = tot + oh.astype(F32)
        cnt_scr[...] = cnt_scr[...] + jnp.sum(tot, axis=1, keepdims=True)
        cnt_ref[...] = cnt_scr[...]

    @pl.when((phase == 1) & (i == 0))
    def _():
        cnt = cnt_scr[...].astype(jnp.int32)
        padded = ((cnt + (row_block - 1)) // row_block * row_block).astype(F32)
        er = lax.broadcasted_iota(jnp.int32, (n_exp, n_exp), 0)
        ec = lax.broadcasted_iota(jnp.int32, (n_exp, n_exp), 1)
        base_scr[...] = _dot_exact_lhs((ec < er).astype(BF16), padded)

    @pl.when(phase == 1)
    def _():
        tr = lax.broadcasted_iota(jnp.int32, (tt, tt), 0)
        tc = lax.broadcasted_iota(jnp.int32, (tt, tt), 1)
        before = (tr < tc).astype(BF16)
        run = base_scr[...][:, :1]
        for kk in range(TOP_K):
            oh = onehot[kk].astype(F32)
            rank = _dot(oh.astype(BF16), before) + run
            dest_ref[kk:kk + 1, :] = jnp.sum(oh * rank, axis=0, keepdims=True).astype(jnp.int32)
            run = run + jnp.sum(oh, axis=1, keepdims=True)
        base_scr[...] = jnp.broadcast_to(run, base_scr.shape)


def _route(idx, row_block):
    n = idx.shape[1]
    tt = _tile(n, 512)
    return pl.pallas_call(
        functools.partial(_route_kernel, row_block=row_block),
        grid=(2, n // tt),
        in_specs=[pl.BlockSpec((TOP_K, tt), lambda p, i: (0, i))],
        out_specs=[
            pl.BlockSpec((TOP_K, tt), lambda p, i: (0, i * p)),
            pl.BlockSpec((N_EXPERTS, LANES), lambda p, i: (0, 0)),
        ],
        out_shape=[
            jax.ShapeDtypeStruct((TOP_K, n), jnp.int32),
            jax.ShapeDtypeStruct((N_EXPERTS, LANES), F32),
        ],
        scratch_shapes=[pltpu.VMEM((N_EXPERTS, LANES), F32), pltpu.VMEM((N_EXPERTS, LANES), F32)],
        compiler_params=_params("arbitrary", "arbitrary"),
        name="route_offsets",
    )(idx)


def _dispatch_kernel(seg_ref, dest_ref, h_ref, xs_ref, zero_scr, sem, zsem, *, row_block):
    tt = dest_ref.shape[1]
    n_exp = seg_ref.shape[1]

    @pl.when(pl.program_id(0) == 0)
    def _():
        zero_scr[...] = jnp.zeros(zero_scr.shape, zero_scr.dtype)

        def zero_copy(e):
            start = pl.multiple_of(seg_ref[1, e] - row_block, row_block)
            return pltpu.make_async_copy(zero_scr, xs_ref.at[pl.ds(start, row_block)], zsem)

        for e in range(n_exp):
            @pl.when(seg_ref[1, e] > seg_ref[0, e])
            def _(e=e):
                zero_copy(e).start()
        for e in range(n_exp):
            @pl.when(seg_ref[1, e] > seg_ref[0, e])
            def _(e=e):
                zero_copy(e).wait()

    def start(g, carry):
        for u in range(8):
            for kk in range(TOP_K):
                pltpu.make_async_copy(h_ref.at[g, pl.ds(u, 1), :],
                                      xs_ref.at[pl.ds(dest_ref[kk, g * 8 + u], 1)],
                                      sem).start(priority=(u * TOP_K + kk) % DMA_PRIORITIES)
        return carry

    lax.fori_loop(0, tt // 8, start, 0)
    pltpu.make_async_copy(xs_ref.at[pl.ds(0, TOP_K * tt)], xs_ref.at[pl.ds(0, TOP_K * tt)], sem).wait()


def _dispatch(seg, dest, hp, n_rows, row_block):
    n, w = hp.shape
    tt = _tile(n, 1024)
    grid_spec = pltpu.PrefetchScalarGridSpec(
        num_scalar_prefetch=1,
        grid=(n // tt,),
        in_specs=[
            pl.BlockSpec((TOP_K, tt), lambda i, s: (0, i), memory_space=pltpu.SMEM),
            pl.BlockSpec((tt // 8, 8, w), lambda i, s: (i, 0, 0)),
        ],
        out_specs=pl.BlockSpec(memory_space=pl.ANY),
        scratch_shapes=[pltpu.VMEM((row_block, w), hp.dtype), pltpu.SemaphoreType.DMA(()),
                        pltpu.SemaphoreType.DMA(())],
    )
    return pl.pallas_call(
        functools.partial(_dispatch_kernel, row_block=row_block),
        grid_spec=grid_spec,
        out_shape=jax.ShapeDtypeStruct((n_rows, w), hp.dtype),
        compiler_params=_params("arbitrary"),
        name="dispatch_rows",
    )(seg, dest, hp.reshape(n // 8, 8, w))


def _expert_kernel(meta_ref, xs_ref, wg_ref, wl_ref, bg_ref, bl_ref, wd_ref, bd_ref, ys_ref,
                   x_scr, y_ref):
    i = pl.program_id(0)
    j = pl.program_id(1)
    n_blocks = pl.num_programs(0)
    row_block = xs_ref.shape[0]
    valid = meta_ref[1 + n_blocks + i]

    def block(m):
        rows = slice(0, m)

        @pl.when(j == 0)
        def _():
            half = xs_ref.shape[1]
            lo, hi = _unpack_bf16_pairs(xs_ref[rows, :])
            x_scr[rows, :half] = lo
            x_scr[rows, half:] = hi
            y_ref[rows, :] = jnp.broadcast_to(bd_ref[...], (m, y_ref.shape[1]))

        x = x_scr[rows, :]
        glu = _dot(x, wg_ref[...]) + bg_ref[...]
        lin = _dot(x, wl_ref[...]) + bl_ref[...]
        glu = jnp.minimum(glu, SWIGLU_LIMIT)
        lin = jnp.clip(lin, -SWIGLU_LIMIT, SWIGLU_LIMIT)
        act = glu * _sigmoid(SWIGLU_ALPHA * glu) * (lin + 1.0)
        y_ref[rows, :] = y_ref[rows, :] + _dot(act.astype(BF16), wd_ref[...].astype(BF16))

        @pl.when(j == pl.num_programs(1) - 1)
        def _():
            ys_ref[rows, :] = _pack_bf16_pairs(y_ref[rows, :])

    @pl.when(valid > row_block // 2)
    def _():
        block(row_block)

    @pl.when((valid > 0) & (valid <= row_block // 2))
    def _():
        block(row_block // 2)


def _experts(meta, xs, w_gu, b_gu, w_dn, b_dn, row_block):
    n_rows, half = xs.shape
    d = 2 * half
    d_ff = w_dn.shape[1]
    tf = _tile(d_ff, 1024)
    nf = d_ff // tf
    n_blocks = n_rows // row_block

    def jj(i, j, m):
        return jnp.where(i < m[0], j, nf - 1)

    grid_spec = pltpu.PrefetchScalarGridSpec(
        num_scalar_prefetch=1,
        grid=(n_blocks, nf),
        in_specs=[
            pl.BlockSpec((row_block, half), lambda i, j, m: (i, 0)),
            pl.BlockSpec((None, d, tf), lambda i, j, m: (m[1 + i], 0, jj(i, j, m))),
            pl.BlockSpec((None, d, tf), lambda i, j, m: (m[1 + i], 0, nf + jj(i, j, m))),
            pl.BlockSpec((None, 1, tf), lambda i, j, m: (m[1 + i], 0, jj(i, j, m))),
            pl.BlockSpec((None, 1, tf), lambda i, j, m: (m[1 + i], 0, nf + jj(i, j, m))),
            pl.BlockSpec((None, tf, d), lambda i, j, m: (m[1 + i], jj(i, j, m), 0)),
            pl.BlockSpec((None, 1, d), lambda i, j, m: (m[1 + i], 0, 0)),
        ],
        out_specs=pl.BlockSpec((row_block, half), lambda i, j, m: (i, 0)),
        scratch_shapes=[pltpu.VMEM((row_block, d), BF16), pltpu.VMEM((row_block, d), F32)],
    )
    return pl.pallas_call(
        _expert_kernel,
        grid_spec=grid_spec,
        out_shape=jax.ShapeDtypeStruct((n_rows, half), jnp.uint32),
        compiler_params=_params("arbitrary", "arbitrary"),
        name="expert_swiglu",
    )(meta, xs, w_gu, w_gu, b_gu, b_gu, w_dn, b_dn)


def _combine_kernel(dest_ref, dest_next_ref, gate_ref, x1_ref, fn_ref, ys_ref, o_ref, buf, sem):
    tt = dest_ref.shape[1]
    i = pl.program_id(0)
    slot = i % 2

    def gather(d_ref, s):
        def start(g, carry):
            for u in range(8):
                for kk in range(TOP_K):
                    pltpu.make_async_copy(ys_ref.at[pl.ds(d_ref[kk, g * 8 + u], 1)],
                                          buf.at[s, kk * (tt // 8) + g, pl.ds(u, 1), :],
                                          sem.at[s]).start(priority=(u * TOP_K + kk) % DMA_PRIORITIES)
            return carry
        lax.fori_loop(0, tt // 8, start, 0)

    @pl.when(i == 0)
    def _():
        gather(dest_ref, 0)

    @pl.when(i + 1 < pl.num_programs(0))
    def _():
        gather(dest_next_ref, 1 - slot)

    pltpu.make_async_copy(ys_ref.at[pl.ds(0, TOP_K * tt)], ys_ref.at[pl.ds(0, TOP_K * tt)], sem.at[slot]).wait()
    half = buf.shape[3]
    gates = gate_ref[...]
    acc_lo = x1_ref[:, :half]
    acc_hi = x1_ref[:, half:]
    for kk in range(TOP_K):
        u = buf[slot, pl.ds(kk * (tt // 8), tt // 8)].reshape(tt, half)
        g = gates[:, kk:kk + 1]
        acc_lo = acc_lo + pltpu.bitcast(u << 16, F32) * g
        acc_hi = acc_hi + pltpu.bitcast(u & jnp.uint32(0xFFFF0000), F32) * g
    ms = (jnp.sum(acc_lo * acc_lo, axis=-1, keepdims=True)
          + jnp.sum(acc_hi * acc_hi, axis=-1, keepdims=True)) / (2 * half)
    inv = lax.rsqrt(ms + NORM_EPS)
    o_ref[:, :half] = acc_lo * inv * fn_ref[:, :half]
    o_ref[:, half:] = acc_hi * inv * fn_ref[:, half:]


def _combine(dest, gates_t, x1, final_gain, ys, row_off):
    n, d = x1.shape
    tt = _tile(n, 256)
    assert row_off % tt == 0
    off = row_off // tt
    nt = n // tt
    return pl.pallas_call(
        _combine_kernel,
        grid=(nt,),
        in_specs=[
            pl.BlockSpec((TOP_K, tt), lambda i: (0, off + i), memory_space=pltpu.SMEM),
            pl.BlockSpec((TOP_K, tt), lambda i: (0, off + jnp.minimum(i + 1, nt - 1)), memory_space=pltpu.SMEM),
            pl.BlockSpec((tt, TOP_K), lambda i: (off + i, 0)),
            pl.BlockSpec((tt, d), lambda i: (i, 0)),
            pl.BlockSpec((1, d), lambda i: (0, 0)),
            pl.BlockSpec(memory_space=pl.ANY),
        ],
        out_specs=pl.BlockSpec((tt, d), lambda i: (i, 0)),
        out_shape=jax.ShapeDtypeStruct((n, d), F32),
        scratch_shapes=[pltpu.VMEM((2, TOP_K * tt // 8, 8, d // 2), jnp.uint32), pltpu.SemaphoreType.DMA((2,))],
        compiler_params=_params("arbitrary"),
        name="combine_rows",
    )(dest, dest, gates_t, x1, final_gain, ys)


def _rope_tables(seq_len):
    rows = seq_len // GRID_W
    row = jnp.repeat(jnp.arange(rows, dtype=F32), GRID_W)
    col = jnp.tile(jnp.arange(GRID_W, dtype=F32), rows)
    freqs = ROPE_THETA ** (-jnp.arange(ROPE_HALF, dtype=F32) / ROPE_HALF)
    ang_r = row[:, None] * freqs[None, :]
    ang_c = col[:, None] * freqs[None, :]
    cos = jnp.concatenate([jnp.cos(ang_r), jnp.cos(ang_r), jnp.cos(ang_c), jnp.cos(ang_c)], axis=1)
    sin = jnp.concatenate([-jnp.sin(ang_r), jnp.sin(ang_r), -jnp.sin(ang_c), jnp.sin(ang_c)], axis=1)
    return cos, sin


def kernel(x_prompt, x_sample, mix_norm, w_in, q_norm, k_norm, hg_lb_logits, hg_out_norm, w_up_attn,
           w_up_hgrn, w_out, ffn_norm, w_router, b_router, w_gate_up, b_gate_up, w_down, b_down, final_norm):
    assert mix_norm.shape[0] == 1, "single trunk layer"
    d = x_prompt.shape[-1]
    hg_w = HG_HEADS * HG_D
    n_exp = w_router.shape[-1]
    row_block = MOE_ROW_BLOCK
    streams = [(x.reshape(-1, d), x.shape[0], x.shape[1]) for x in (x_prompt, x_sample)]
    n_total = sum(x.shape[0] for x, _, _ in streams)

    lb = jnp.cumsum(jax.nn.softmax(hg_lb_logits.astype(F32), axis=1), axis=1)[:, 0]
    w_in_b = w_in[0].astype(BF16)
    w_ua_b = w_up_attn[0].astype(BF16)
    w_uh_b = w_up_hgrn[0].astype(BF16)
    w_out_b = w_out[0].astype(BF16)
    w_r_t = w_router[0].T
    mix_g = mix_norm[0].reshape(1, d)
    score_bound = (1.02 * LOG2_E * HEAD_DIM ** 0.5 * jnp.max(jnp.abs(q_norm[0])) * jnp.max(jnp.abs(k_norm[0])))
    score_bound = score_bound.astype(F32).reshape(1)

    w_gu2d = w_gate_up[0].reshape(-1, w_gate_up.shape[-1])
    cast_host = max(range(len(streams)), key=lambda s: streams[s][0].shape[0])
    w_gu_b = None

    x1s, idxs, gate_ts = [], [], []
    hp = None
    row_off = 0
    for s, (x, batch, seq) in enumerate(streams):
        cos, sin = _rope_tables(seq)
        qkv, zh, gates = _norm_proj(x, mix_g, w_in_b, q_norm[0].reshape(1, HEAD_DIM),
                                    k_norm[0].reshape(1, HEAD_DIM), cos, sin, seq)
        if s == cast_host:
            attn, w_gu_b = _attention(qkv, score_bound, batch, seq, w_gu2d)
        else:
            attn, = _attention(qkv, score_bound, batch, seq)
        hg = _hgrn(zh, lb[0:1], lb[1:2], hg_out_norm[0].reshape(1, hg_w), batch, seq)
        merged = _merge(attn, hg, gates, w_ua_b, w_uh_b)
        x1, hp, idx, gate = _outproj_router(x, merged, w_out_b, ffn_norm[0].reshape(1, d), w_r_t,
                                            b_router[0].reshape(n_exp, 1), hp, row_off, n_total)
        x1s.append(x1)
        idxs.append(idx)
        gate_ts.append(gate.T)
        row_off += x.shape[0]

    idx = jnp.concatenate(idxs, axis=1)
    gate_t = jnp.concatenate(gate_ts, axis=0)
    dest, counts = _route(idx, row_block)
    cnt = counts[:, 0].astype(jnp.int32)
    padded = (cnt + row_block - 1) // row_block * row_block
    pad_end = jnp.cumsum(padded)
    n_rows = n_total * TOP_K + n_exp * row_block
    n_blocks = n_rows // row_block
    blk_start = jnp.arange(n_blocks, dtype=jnp.int32) * row_block
    blk_e = jnp.minimum(jnp.sum(pad_end[None, :] <= blk_start[:, None], axis=1), n_exp - 1).astype(jnp.int32)
    seg_start = pad_end - padded
    blk_valid = jnp.clip(cnt[blk_e] - (blk_start - seg_start[blk_e]), 0, row_block)
    blk_valid = jnp.where(blk_start < pad_end[-1], blk_valid, 0).astype(jnp.int32)
    meta = jnp.concatenate([(pad_end[-1:] // row_block).astype(jnp.int32), blk_e, blk_valid])
    seg = jnp.stack([seg_start, pad_end]).astype(jnp.int32)

    xs = _dispatch(seg, dest, hp, n_rows, row_block)
    ys = _experts(meta, xs, w_gu_b.reshape(w_gate_up.shape[1:]), b_gate_up[0].reshape(n_exp, 1, -1),
                  w_down[0], b_down[0].reshape(n_exp, 1, d), row_block)

    outs = []
    row_off = 0
    for (x, batch, seq), x1 in zip(streams, x1s):
        out = _combine(dest, gate_t, x1, final_norm.reshape(1, d), ys, row_off)
        outs.append(out.reshape(batch, seq, d))
        row_off += x.shape[0]
    return tuple(outs)
```
